```python
import math
import jax
import jax.numpy as jnp
from jax import lax
import numpy as np

D_MODEL = 1024
BATCH = 16
SEQ = 2048
DEPTH = 1

N_HEADS_A = 8
HEAD_DIM_A = 64
WIDTH_A = N_HEADS_A * HEAD_DIM_A
KV_RANK = 256
N_HEADS_IDX = 8
HEAD_DIM_IDX = 64
TOPK_MAX = 256
Q_BLOCK = 128
N_BUCKETS = 32
MAX_DISTANCE = 128
N_HEADS_M = 4
HEAD_DIM_M = 128
WIDTH_M = N_HEADS_M * HEAD_DIM_M
CONV_WIDTH = 4
CHUNK = 64
N_GROUPS = 4
EXPERTS_PER_GROUP = 4
N_EXPERTS = N_GROUPS * EXPERTS_PER_GROUP
TOP_K_EXP = 2
D_EXPERT = 512
LN_EPS = 1e-5
ALPHA = (2.0 * DEPTH) ** 0.25
BETA = (8.0 * DEPTH) ** -0.25

IN_SPLITS = (
    WIDTH_A,
    KV_RANK,
    N_HEADS_IDX * HEAD_DIM_IDX,
    HEAD_DIM_IDX,
    N_HEADS_IDX,
    2 * WIDTH_M,
    WIDTH_M,
    N_HEADS_M,
    N_HEADS_M,
    WIDTH_M,
    D_MODEL,
    D_MODEL,
)
D_IN = sum(IN_SPLITS)

kernel_name = "hybrid_dsa_mlstm_hmoe_deepnorm"


def split_cols(h):
    idx = np.cumsum(IN_SPLITS)[:-1].tolist()
    return jnp.split(h, idx, axis=-1)


def layer_norm(x, g, b):
    xf = x.astype(jnp.float32)
    mu = jnp.mean(xf, axis=-1, keepdims=True)
    var = jnp.mean(jnp.square(xf - mu), axis=-1, keepdims=True)
    return ((xf - mu) * lax.rsqrt(var + LN_EPS) * g + b).astype(x.dtype)


def rms_norm(x, g):
    xf = x.astype(jnp.float32)
    return (xf * lax.rsqrt(jnp.mean(xf * xf, axis=-1, keepdims=True) + LN_EPS) * g).astype(x.dtype)


def t5_bucket(dist):
    max_exact = N_BUCKETS // 2
    d = jnp.maximum(dist, 0)
    ratio = jnp.log(jnp.maximum(d, 1).astype(jnp.float32) / max_exact) / math.log(MAX_DISTANCE / max_exact)
    large = jnp.minimum(max_exact + (ratio * (N_BUCKETS - max_exact)).astype(jnp.int32), N_BUCKETS - 1)
    return jnp.where(d < max_exact, d, large)


def causal_conv(u, w, b):
    out = lax.conv_general_dilated(
        u, w[:, None, :], window_strides=(1,), padding=[(CONV_WIDTH - 1, 0)],
        dimension_numbers=("NWC", "WIO", "NWC"), feature_group_count=u.shape[-1])
    return out + b


def dsa_attention(q, c_kv, q_idx, k_idx, w_idx, w_uk, w_uv, rel_bias):
    B, S = q.shape[0], q.shape[1]
    topk = min(TOPK_MAX, S // 4)
    n_blk = S // Q_BLOCK
    key_pos = jnp.arange(S)
    q_lat = jnp.einsum("bshd,rhd->bshr", q, w_uk) * (HEAD_DIM_A ** -0.5)

    def block(i):
        t0 = i * Q_BLOCK
        qi = lax.dynamic_slice_in_dim(q_idx, t0, Q_BLOCK, axis=1)
        wi = lax.dynamic_slice_in_dim(w_idx, t0, Q_BLOCK, axis=1)
        ql = lax.dynamic_slice_in_dim(q_lat, t0, Q_BLOCK, axis=1)
        qpos = t0 + jnp.arange(Q_BLOCK)
        dots = jnp.einsum("bthd,bsd->bths", qi, k_idx)
        score = jnp.einsum("bth,bths->bts", wi, jax.nn.relu(dots)).astype(jnp.float32)
        causal = key_pos[None, :] <= qpos[:, None]
        score = jnp.where(causal[None], score, -jnp.inf)
        _, sel = lax.top_k(score, topk)
        valid = sel <= qpos[None, :, None]
        c_sel = jax.vmap(lambda c, ix: c[ix])(c_kv, sel)
        logits = jnp.einsum("bthr,btkr->bthk", ql, c_sel).astype(jnp.float32)
        bias = rel_bias[t5_bucket(qpos[None, :, None] - sel)].astype(jnp.float32)
        logits = logits + jnp.moveaxis(bias, -1, 2)
        logits = jnp.where(valid[:, :, None, :], logits, -jnp.inf)
        p = jax.nn.softmax(logits, axis=-1).astype(c_sel.dtype)
        return jnp.einsum("bthk,btkr->bthr", p, c_sel)

    o_lat = lax.map(block, jnp.arange(n_blk))
    o_lat = jnp.moveaxis(o_lat, 0, 1).reshape(B, S, N_HEADS_A, KV_RANK)
    return jnp.einsum("bshr,rhd->bshd", o_lat, w_uv)


def mlstm(q, k, v, i_pre, f_pre):
    B, S, NH, D = q.shape
    NC = S // CHUNK
    f32 = jnp.float32

    def to_chunks(a):
        a = a.astype(f32).reshape((B, NC, CHUNK) + a.shape[2:])
        return jnp.moveaxis(a, 3, 1)

    qc, kc, vc = to_chunks(q), to_chunks(k) * (D ** -0.5), to_chunks(v)
    ic, fc = to_chunks(i_pre), to_chunks(f_pre)
    log_f = jax.nn.log_sigmoid(fc)
    bcum = jnp.cumsum(log_f, axis=-1)
    b_last = bcum[..., -1]
    causal = jnp.tril(jnp.ones((CHUNK, CHUNK), dtype=bool))
    logD = jnp.where(causal, bcum[..., :, None] - bcum[..., None, :] + ic[..., None, :], -jnp.inf)

    log_w_end = b_last[..., None] - bcum + ic
    m_loc = jnp.max(log_w_end, axis=-1)
    w_end = jnp.exp(log_w_end - m_loc[..., None])
    C_loc = jnp.einsum("bncs,bncsk,bncsv->bnckv", w_end, kc, vc)
    n_loc = jnp.einsum("bncs,bncsk->bnck", w_end, kc)

    def step(carry, inp):
        C, n, m = carry
        Cl, nl, ml, bl = inp
        m_new = jnp.maximum(bl + m, ml)
        a = jnp.exp(bl + m - m_new)
        c = jnp.exp(ml - m_new)
        C_new = a[..., None, None] * C + c[..., None, None] * Cl
        n_new = a[..., None] * n + c[..., None] * nl
        return (C_new, n_new, m_new), (C, n, m)

    init = (jnp.zeros((B, NH, D, D), f32), jnp.zeros((B, NH, D), f32), jnp.zeros((B, NH), f32))
    xs = (jnp.moveaxis(C_loc, 2, 0), jnp.moveaxis(n_loc, 2, 0),
          jnp.moveaxis(m_loc, 2, 0), jnp.moveaxis(b_last, 2, 0))
    _, (C_prev, n_prev, m_prev) = lax.scan(step, init, xs)
    C_prev = jnp.moveaxis(C_prev, 0, 2)
    n_prev = jnp.moveaxis(n_prev, 0, 2)
    m_prev = jnp.moveaxis(m_prev, 0, 2)

    log_inter = bcum + m_prev[..., None]
    m_j = jnp.maximum(log_inter, jnp.max(logD, axis=-1))
    w_inter = jnp.exp(log_inter - m_j)
    s = jnp.einsum("bncjd,bncsd->bncjs", qc, kc) * jnp.exp(logD - m_j[..., None])
    num = jnp.einsum("bncjs,bncsv->bncjv", s, vc) + \
        w_inter[..., None] * jnp.einsum("bncjk,bnckv->bncjv", qc, C_prev)
    den = jnp.sum(s, axis=-1) + w_inter * jnp.einsum("bncjk,bnck->bncj", qc, n_prev)
    h = num / jnp.maximum(jnp.abs(den), jnp.exp(-m_j))[..., None]
    h = jnp.transpose(h, (0, 2, 3, 1, 4)).reshape(B, S, NH, D)
    return h.astype(q.dtype)


def token_mixer(x, w_in, conv_w, conv_b, kv_norm_g, w_uk, w_uv, rel_bias, b_i, b_f,
                mh_norm_g, w_up_a, w_up_m, w_out):
    B, S, _ = x.shape
    h = x @ w_in
    (q_a, c_kv, q_idx, k_idx, w_idx, qk_m, v_m, i_m, f_m, o_m, g_a, g_m) = split_cols(h)
    c_kv = rms_norm(c_kv, kv_norm_g)
    o_a = dsa_attention(q_a.reshape(B, S, N_HEADS_A, HEAD_DIM_A), c_kv,
                        q_idx.reshape(B, S, N_HEADS_IDX, HEAD_DIM_IDX), k_idx, w_idx,
                        w_uk, w_uv, rel_bias).reshape(B, S, WIDTH_A)
    qk_m = jax.nn.silu(causal_conv(qk_m, conv_w, conv_b))
    q_m, k_m = jnp.split(qk_m, 2, axis=-1)
    hd = (B, S, N_HEADS_M, HEAD_DIM_M)
    h_m = mlstm(q_m.reshape(hd), k_m.reshape(hd), v_m.reshape(hd), i_m + b_i, f_m + b_f)
    h_m = jax.nn.sigmoid(o_m.reshape(hd)) * rms_norm(h_m, mh_norm_g)
    h_m = h_m.reshape(B, S, WIDTH_M)
    y = jax.nn.sigmoid(g_a) * (o_a @ w_up_a) + jax.nn.sigmoid(g_m) * (h_m @ w_up_m)
    return y @ w_out


def hier_moe(x, w_grp, b_grp, w_rt, b_rt, w_gate, w_up, w_down):
    B, S, D = x.shape
    t = x.reshape(-1, D)
    grp_prob = jax.nn.softmax((t @ w_grp + b_grp).astype(jnp.float32), axis=-1)
    g_w, g_idx = lax.top_k(grp_prob, 1)
    exp_logits = (t @ w_rt + b_rt).astype(jnp.float32).reshape(-1, N_GROUPS, EXPERTS_PER_GROUP)
    in_grp = jnp.take_along_axis(exp_logits, g_idx[:, :, None], axis=1)[:, 0]
    e_logit, e_idx = lax.top_k(in_grp, TOP_K_EXP)
    e_w = jax.nn.softmax(e_logit, axis=-1) * g_w
    glob = g_idx * EXPERTS_PER_GROUP + e_idx
    combine = jnp.sum(jax.nn.one_hot(glob, N_EXPERTS, dtype=jnp.float32) * e_w[..., None], axis=1)
    combine = combine.astype(t.dtype)
    out = jnp.zeros_like(t)
    for e in range(N_EXPERTS):
        hdn = jax.nn.silu(t @ w_gate[e]) * (t @ w_up[e])
        out = out + combine[:, e:e + 1] * (hdn @ w_down[e])
    return out.reshape(B, S, D)


def setup_inputs(seed: int = 0) -> dict:
    key = jax.random.key(seed)
    ks = jax.random.split(key, 26)
    n = lambda k, shape, s: jax.random.normal(k, shape, jnp.float32) * s
    L = DEPTH
    return {
        "x": n(ks[0], (BATCH, SEQ, D_MODEL), 1.0),
        "w_in": n(ks[1], (L, D_MODEL, D_IN), D_MODEL ** -0.5),
        "conv_w": n(ks[2], (L, CONV_WIDTH, 2 * WIDTH_M), CONV_WIDTH ** -0.5),
        "conv_b": n(ks[3], (L, 2 * WIDTH_M), 0.02),
        "kv_norm_g": 1.0 + n(ks[4], (L, KV_RANK), 0.02),
        "w_uk": n(ks[5], (L, KV_RANK, N_HEADS_A, HEAD_DIM_A), KV_RANK ** -0.5),
        "w_uv": n(ks[6], (L, KV_RANK, N_HEADS_A, HEAD_DIM_A), KV_RANK ** -0.5),
        "rel_bias": n(ks[7], (N_BUCKETS, N_HEADS_A), 0.5),
        "b_i": n(ks[8], (L, N_HEADS_M), 0.1),
        "b_f": jnp.linspace(3.0, 6.0, N_HEADS_M, dtype=jnp.float32)[None, :] + n(ks[9], (L, N_HEADS_M), 0.1),
        "mh_norm_g": 1.0 + n(ks[10], (L, N_HEADS_M, HEAD_DIM_M), 0.02),
        "w_up_a": n(ks[11], (L, WIDTH_A, D_MODEL), WIDTH_A ** -0.5),
        "w_up_m": n(ks[12], (L, WIDTH_M, D_MODEL), WIDTH_M ** -0.5),
        "w_out": n(ks[13], (L, D_MODEL, D_MODEL), BETA * D_MODEL ** -0.5),
        "ln1_g": 1.0 + n(ks[14], (L, D_MODEL), 0.02),
        "ln1_b": n(ks[15], (L, D_MODEL), 0.02),
        "w_grp": n(ks[16], (L, D_MODEL, N_GROUPS), D_MODEL ** -0.5),
        "b_grp": n(ks[17], (L, N_GROUPS), 0.01),
        "w_rt": n(ks[18], (L, D_MODEL, N_EXPERTS), D_MODEL ** -0.5),
        "b_rt": n(ks[19], (L, N_EXPERTS), 0.01),
        "w_gate": n(ks[20], (L, N_EXPERTS, D_MODEL, D_EXPERT), D_MODEL ** -0.5),
        "w_up": n(ks[21], (L, N_EXPERTS, D_MODEL, D_EXPERT), D_MODEL ** -0.5),
        "w_down": n(ks[22], (L, N_EXPERTS, D_EXPERT, D_MODEL), BETA * D_EXPERT ** -0.5),
        "ln2_g": 1.0 + n(ks[23], (L, D_MODEL), 0.02),
        "ln2_b": n(ks[24], (L, D_MODEL), 0.02),
    }


def reference(x, w_in, conv_w, conv_b, kv_norm_g, w_uk, w_uv, rel_bias, b_i, b_f,
              mh_norm_g, w_up_a, w_up_m, w_out, ln1_g, ln1_b, w_grp, b_grp, w_rt, b_rt,
              w_gate, w_up, w_down, ln2_g, ln2_b):
    for l in range(DEPTH):
        mix = token_mixer(x, w_in[l], conv_w[l], conv_b[l], kv_norm_g[l], w_uk[l], w_uv[l],
                          rel_bias, b_i[l], b_f[l], mh_norm_g[l], w_up_a[l], w_up_m[l], w_out[l])
        x = layer_norm(ALPHA * x + mix, ln1_g[l], ln1_b[l])
        ffn = hier_moe(x, w_grp[l], b_grp[l], w_rt[l], b_rt[l], w_gate[l], w_up[l], w_down[l])
        x = layer_norm(ALPHA * x + ffn, ln2_g[l], ln2_b[l])
    return x
```

```python
import functools
import math

import jax
import jax.numpy as jnp
import numpy as np
from jax import lax
from jax.experimental import pallas as pl
from jax.experimental.pallas import tpu as pltpu

F32 = jnp.float32
BF16 = jnp.bfloat16

D_MODEL = 1024
N_HEADS_A = 8
HEAD_DIM_A = 64
WIDTH_A = N_HEADS_A * HEAD_DIM_A
KV_RANK = 256
N_HEADS_IDX = 8
HEAD_DIM_IDX = 64
WIDTH_IDX = N_HEADS_IDX * HEAD_DIM_IDX
TOPK_MAX = 256
N_BUCKETS = 32
MAX_DISTANCE = 128
N_HEADS_M = 4
HEAD_DIM_M = 128
WIDTH_M = N_HEADS_M * HEAD_DIM_M
CONV_WIDTH = 4
N_GROUPS = 4
EXPERTS_PER_GROUP = 4
N_EXPERTS = N_GROUPS * EXPERTS_PER_GROUP
D_EXPERT = 512
LN_EPS = 1e-5
DEPTH = 1
ALPHA = (2.0 * DEPTH) ** 0.25

LANES = 128
SUBLANES = 8
VMEM_LIMIT = 56 * 1024 * 1024

SM_KIDX = 0
SM_WIDX = HEAD_DIM_IDX
SM_I = SM_WIDX + N_HEADS_IDX
SM_F = SM_I + N_HEADS_M

Q_TILE = 128
M_CHUNK = 128
BISECT_MAX_ITERS = 48
NEG_INF = float("-inf")


def _cparams(n_grid):
    return pltpu.CompilerParams(dimension_semantics=("arbitrary",) * n_grid,
                                vmem_limit_bytes=VMEM_LIMIT)


def _full(shape):
    nd = len(shape)
    return pl.BlockSpec(shape, lambda *_: (0,) * nd)


def _proj_kernel(x_ref, wqa, wckv, wqi, wsm, wqk, wv, wo, wga, wgm, kvg, smb,
                 qa_o, ckv_o, qi_o, sm_o, qk_o, v_o, o_o, ga_o, gm_o):
    xb = x_ref[...].astype(BF16)

    def mm(w):
        return jnp.dot(xb, w[...], preferred_element_type=F32)

    qa_o[...] = mm(wqa).astype(BF16)
    c = mm(wckv)
    c = c * lax.rsqrt(jnp.mean(c * c, axis=-1, keepdims=True) + LN_EPS) * kvg[...]
    ckv_o[...] = c.astype(BF16)
    qi_o[...] = mm(wqi).astype(BF16)
    sm_o[...] = mm(wsm) + smb[...]
    qk_o[...] = mm(wqk).astype(BF16)
    v_o[...] = mm(wv).astype(BF16)
    o_o[...] = mm(wo).astype(BF16)
    ga_o[...] = mm(wga).astype(BF16)
    gm_o[...] = mm(wgm).astype(BF16)


def _proj(x2, ws, kvg, smb, tm):
    T = x2.shape[0]
    widths = [w.shape[1] for w in ws]
    dts = [BF16, BF16, BF16, F32, BF16, BF16, BF16, BF16, BF16]
    in_specs = [pl.BlockSpec((tm, D_MODEL), lambda i: (i, 0))]
    in_specs += [_full(w.shape) for w in ws]
    in_specs += [_full(kvg.shape), _full(smb.shape)]
    out_specs = [pl.BlockSpec((tm, n), lambda i: (i, 0)) for n in widths]
    out_shape = [jax.ShapeDtypeStruct((T, n), dt) for n, dt in zip(widths, dts)]
    return pl.pallas_call(
        _proj_kernel, grid=(T // tm,), in_specs=in_specs, out_specs=out_specs, out_shape=out_shape,
        compiler_params=_cparams(1), name="proj")(x2, *ws, kvg, smb)


def _dsa_kernel(far_ref, qi_ref, smq_ref, smk_ref, qa_ref, ckv_ref, wuk_ref, wuv_ref, tz_ref,
                oa_ref, sc_ref, am_ref, ql_ref, raw_ref, p_ref, lg_ref, *, seq, topk):
    tq = Q_TILE
    qb = pl.program_id(1)
    t0 = qb * tq

    kidx = smk_ref[:, SM_KIDX:SM_KIDX + HEAD_DIM_IDX].astype(BF16)
    widx = smq_ref[:, SM_WIDX:SM_WIDX + N_HEADS_IDX]
    score = jnp.zeros((tq, seq), F32)
    for h in range(N_HEADS_IDX):
        qh = qi_ref[:, h * HEAD_DIM_IDX:(h + 1) * HEAD_DIM_IDX]
        dots = lax.dot_general(qh, kidx, (((1,), (1,)), ((), ())), preferred_element_type=F32)
        score = score + widx[:, h:h + 1] * jnp.maximum(dots, 0.0)
    row = lax.broadcasted_iota(jnp.int32, (tq, seq), 0) + t0
    col = lax.broadcasted_iota(jnp.int32, (tq, seq), 1)
    causal = col <= row
    score = jnp.where(causal, score, NEG_INF)
    sc_ref[...] = score

    n_vis = (lax.broadcasted_iota(jnp.int32, (tq, 1), 0) + (t0 + 1)).astype(F32)
    k_row = jnp.minimum(n_vis, float(topk))
    row_max = jnp.max(score, axis=-1, keepdims=True)
    row_min = jnp.min(jnp.where(causal, score, jnp.inf), axis=-1, keepdims=True)
    c_max = jnp.sum((score >= row_max).astype(F32), axis=-1, keepdims=True)
    top_tied = c_max >= k_row
    lo0 = jnp.where(top_tied, row_max, row_min)
    cnt_lo0 = jnp.where(top_tied, c_max, n_vis)

    def unresolved(cnt_lo):
        return jnp.max(jnp.where(cnt_lo != k_row, 1.0, 0.0)) > 0.0

    def cond(carry):
        it, lo, hi, cnt_lo, c_hi = carry
        return jnp.logical_and(it < BISECT_MAX_ITERS, unresolved(cnt_lo))

    def body(carry):
        it, lo, hi, cnt_lo, c_hi = carry
        mid = lo * 0.5 + hi * 0.5
        cnt = jnp.sum((sc_ref[...] >= mid).astype(F32), axis=-1, keepdims=True)
        ge = cnt >= k_row
        return (it + 1, jnp.where(ge, mid, lo), jnp.where(ge, hi, mid),
                jnp.where(ge, cnt, cnt_lo), jnp.where(ge, c_hi, cnt))

    _, lo, hi, cnt_lo, c_hi = lax.while_loop(cond, body, (jnp.int32(0), lo0, row_max, cnt_lo0, c_max))

    am_ref[...] = jnp.where(sc_ref[...] >= lo, 0.0, NEG_INF)

    @pl.when(unresolved(cnt_lo))
    def _ties():
        above_ok = c_hi < k_row
        need = k_row - jnp.where(above_ok, c_hi, 0.0)
        upper = (lax.broadcasted_iota(jnp.int32, (LANES, LANES), 0)
                 < lax.broadcasted_iota(jnp.int32, (LANES, LANES), 1)).astype(BF16)
        carry = jnp.zeros((tq, 1), F32)
        for j in range(seq // LANES):
            s = sc_ref[:, j * LANES:(j + 1) * LANES]
            above = jnp.logical_and(s >= hi, above_ok)
            tie = jnp.logical_and(s >= lo, jnp.logical_not(above))
            tie_f = tie.astype(F32)
            rank = jnp.dot(tie_f.astype(BF16), upper, preferred_element_type=F32) + carry
            carry = carry + jnp.sum(tie_f, axis=-1, keepdims=True)
            sel = jnp.logical_or(above, jnp.logical_and(tie, rank < need))
            am_ref[:, j * LANES:(j + 1) * LANES] = jnp.where(sel, 0.0, NEG_INF)

    for h in range(N_HEADS_A):
        qh = qa_ref[:, h * HEAD_DIM_A:(h + 1) * HEAD_DIM_A]
        ql = jnp.dot(qh, wuk_ref[h], preferred_element_type=F32) * (HEAD_DIM_A ** -0.5)
        ql_ref[h * tq:(h + 1) * tq, :] = ql.astype(BF16)

    ckv = ckv_ref[...]
    raw_ref[...] = lax.dot_general(ql_ref[...], ckv, (((1,), (1,)), ((), ())),
                                   preferred_element_type=F32)

    win0 = pl.multiple_of(jnp.maximum(qb - 1, 0) * tq, LANES)
    for h in range(N_HEADS_A):
        lg_ref[...] = raw_ref[h * tq:(h + 1) * tq, :] + far_ref[h] + am_ref[...]
        lg_ref[:, pl.ds(win0, 2 * tq)] += tz_ref[0, h]
        lg = lg_ref[...]
        m = jnp.max(lg, axis=-1, keepdims=True)
        p = jnp.exp(lg - m)
        inv = 1.0 / jnp.sum(p, axis=-1, keepdims=True)
        p_ref[h * tq:(h + 1) * tq, :] = (p * inv).astype(BF16)

    o_lat = jnp.dot(p_ref[...], ckv, preferred_element_type=F32)
    outs = []
    for h in range(N_HEADS_A):
        oh = o_lat[h * tq:(h + 1) * tq, :].astype(BF16)
        outs.append(jnp.dot(oh, wuv_ref[h], preferred_element_type=F32))
    oa_ref[...] = jnp.concatenate(outs, axis=-1).astype(BF16)


def _dsa(far, qi, sm, qa, ckv, wuk_t, wuv_t, tz, batch, seq):
    tq = Q_TILE
    nq = seq // tq
    topk = min(TOPK_MAX, seq // 4)
    T = batch * seq
    kern = functools.partial(_dsa_kernel, seq=seq, topk=topk)
    blk_q = lambda n: pl.BlockSpec((tq, n), lambda b, q: (b * nq + q, 0))
    blk_s = lambda n: pl.BlockSpec((seq, n), lambda b, q: (b, 0))
    in_specs = [
        pl.BlockSpec(memory_space=pltpu.SMEM),
        blk_q(WIDTH_IDX), blk_q(LANES), blk_s(LANES), blk_q(WIDTH_A), blk_s(KV_RANK),
        _full(wuk_t.shape), _full(wuv_t.shape),
        pl.BlockSpec((1, N_HEADS_A, tq, 2 * tq), lambda b, q: (jnp.minimum(q, 1), 0, 0, 0)),
    ]
    scratch = [
        pltpu.VMEM((tq, seq), F32),
        pltpu.VMEM((tq, seq), F32),
        pltpu.VMEM((N_HEADS_A * tq, KV_RANK), BF16),
        pltpu.VMEM((N_HEADS_A * tq, seq), F32),
        pltpu.VMEM((N_HEADS_A * tq, seq), BF16),
        pltpu.VMEM((tq, seq), F32),
    ]
    return pl.pallas_call(
        kern, grid=(batch, nq), in_specs=in_specs,
        out_specs=pl.BlockSpec((tq, WIDTH_A), lambda b, q: (b * nq + q, 0)),
        out_shape=jax.ShapeDtypeStruct((T, WIDTH_A), BF16),
        scratch_shapes=scratch, compiler_params=_cparams(2), name="dsa")(
            far, qi, sm, sm, qa, ckv, wuk_t, wuv_t, tz)


def _t5_bucket(dist):
    max_exact = N_BUCKETS // 2
    d = jnp.maximum(dist, 0)
    ratio = jnp.log(jnp.maximum(d, 1).astype(F32) / max_exact) / math.log(MAX_DISTANCE / max_exact)
    large = jnp.minimum(max_exact + (ratio * (N_BUCKETS - max_exact)).astype(jnp.int32), N_BUCKETS - 1)
    return jnp.where(d < max_exact, d, large)


def _bias_tables(rel_bias):
    tq = Q_TILE
    assert int(_np_bucket(tq)) == N_BUCKETS - 1
    far = rel_bias[N_BUCKETS - 1].astype(F32)
    t = jnp.arange(tq)[:, None]
    u = jnp.arange(2 * tq)[None, :]
    tabs = []
    for shift in (0, tq):
        d = t + shift - u
        b = rel_bias[_t5_bucket(d)].astype(F32) - far
        b = jnp.where((d >= 0)[..., None], b, 0.0)
        tabs.append(jnp.moveaxis(b, -1, 0))
    return far, jnp.stack(tabs)


def _np_bucket(d):
    max_exact = N_BUCKETS // 2
    ratio = np.log(np.float32(max(d, 1)) / np.float32(max_exact)) / math.log(MAX_DISTANCE / max_exact)
    return min(max_exact + int(ratio * (N_BUCKETS - max_exact)), N_BUCKETS - 1) if d >= max_exact else d


def _mlstm_kernel(qk_ref, halo_ref, v_ref, sm_ref, o_ref, cw_ref, cb_ref, g_ref,
                  out_ref, c_ref, m_ref):
    L = M_CHUNK
    dm = HEAD_DIM_M
    c_idx = pl.program_id(1)

    @pl.when(c_idx == 0)
    def _init():
        c_ref[...] = jnp.zeros_like(c_ref)
        m_ref[...] = jnp.zeros_like(m_ref)

    halo = jnp.where(c_idx > 0, halo_ref[...].astype(F32), 0.0)
    ext = jnp.concatenate([halo, qk_ref[...].astype(F32)], axis=0)
    acc = jnp.zeros((L, 2 * WIDTH_M), F32) + cb_ref[...]
    for w in range(CONV_WIDTH):
        off = SUBLANES - (CONV_WIDTH - 1) + w
        acc = acc + ext[off:off + L, :] * cw_ref[w:w + 1, :]
    qk = acc * jax.nn.sigmoid(acc)

    sm = sm_ref[...]
    sm_t = sm.T
    r_i = lax.broadcasted_iota(jnp.int32, (L, L), 0)
    c_i = lax.broadcasted_iota(jnp.int32, (L, L), 1)
    tril = (c_i <= r_i).astype(F32)
    triu = (r_i <= c_i).astype(F32)
    bcum_c = jnp.dot(tril, jax.nn.log_sigmoid(sm), preferred_element_type=F32,
                     precision=lax.Precision.HIGHEST)
    bcum_r = jnp.dot(jax.nn.log_sigmoid(sm_t), triu, preferred_element_type=F32,
                     precision=lax.Precision.HIGHEST)
    causal = c_i <= r_i
    ones_col = (lax.broadcasted_iota(jnp.int32, (L, dm), 1) == 0).astype(BF16)

    outs = []
    for h in range(N_HEADS_M):
        q = qk[:, h * dm:(h + 1) * dm].astype(BF16)
        k = (qk[:, WIDTH_M + h * dm:WIDTH_M + (h + 1) * dm] * (dm ** -0.5))
        v_aug = jnp.concatenate([v_ref[:, h * dm:(h + 1) * dm], ones_col], axis=-1)
        b_col = bcum_c[:, SM_F + h:SM_F + h + 1]
        g_col = sm[:, SM_I + h:SM_I + h + 1] - b_col
        g_row = sm_t[SM_I + h:SM_I + h + 1, :] - bcum_r[SM_F + h:SM_F + h + 1, :]
        b_last = b_col[L - 1:L, :]
        m_prev = m_ref[h]
        c_prev = c_ref[h]

        log_d = jnp.where(causal, b_col + g_row, NEG_INF)
        m_j = jnp.maximum(b_col + m_prev, jnp.max(log_d, axis=-1, keepdims=True))
        w_inter = jnp.exp(b_col + m_prev - m_j)
        qkt = lax.dot_general(q, k.astype(BF16), (((1,), (1,)), ((), ())), preferred_element_type=F32)
        s = qkt * jnp.exp(log_d - m_j)
        o_aug = jnp.dot(s.astype(BF16), v_aug, preferred_element_type=F32) + \
            w_inter * jnp.dot(q, c_prev.astype(BF16), preferred_element_type=F32)
        num = o_aug[:, :dm]
        den = o_aug[:, dm:dm + 1]
        hh = num / jnp.maximum(jnp.abs(den), jnp.exp(-m_j))

        lwe = b_last + g_col
        m_loc = jnp.max(lwe, axis=0, keepdims=True)
        kw = (k * jnp.exp(lwe - m_loc)).astype(BF16)
        c_loc = lax.dot_general(kw, v_aug, (((0,), (0,)), ((), ())), preferred_element_type=F32)
        m_new = jnp.maximum(b_last + m_prev, m_loc)
        c_ref[h] = jnp.exp(b_last + m_prev - m_new) * c_prev + jnp.exp(m_loc - m_new) * c_loc
        m_ref[h] = m_new

        hn = hh * lax.rsqrt(jnp.mean(hh * hh, axis=-1, keepdims=True) + LN_EPS) * g_ref[:, h * dm:(h + 1) * dm]
        og = o_ref[:, h * dm:(h + 1) * dm].astype(F32)
        outs.append(jax.nn.sigmoid(og) * hn)
    out_ref[...] = jnp.concatenate(outs, axis=-1).astype(BF16)


def _mlstm(qk, v, sm, o, conv_w, conv_b, mh_g, batch, seq):
    L = M_CHUNK
    nc = seq // L
    T = batch * seq
    hb = L // SUBLANES
    blk = lambda n: pl.BlockSpec((L, n), lambda b, c: (b * nc + c, 0))
    in_specs = [
        blk(2 * WIDTH_M),
        pl.BlockSpec((SUBLANES, 2 * WIDTH_M), lambda b, c: (jnp.maximum((b * nc + c) * hb - 1, 0), 0)),
        blk(WIDTH_M), blk(LANES), blk(WIDTH_M),
        _full(conv_w.shape), _full(conv_b.shape), _full(mh_g.shape),
    ]
    scratch = [pltpu.VMEM((N_HEADS_M, HEAD_DIM_M, 2 * HEAD_DIM_M), F32),
               pltpu.VMEM((N_HEADS_M, 1, 1), F32)]
    return pl.pallas_call(
        _mlstm_kernel, grid=(batch, nc), in_specs=in_specs, out_specs=blk(WIDTH_M),
        out_shape=jax.ShapeDtypeStruct((T, WIDTH_M), BF16), scratch_shapes=scratch,
        compiler_params=_cparams(2), name="mlstm")(qk, qk, v, sm, o, conv_w, conv_b, mh_g)


def _layer_norm(y, g, b):
    mu = jnp.mean(y, axis=-1, keepdims=True)
    var = jnp.mean(jnp.square(y - mu), axis=-1, keepdims=True)
    return (y - mu) * lax.rsqrt(var + LN_EPS) * g + b


def _merge_kernel(x_ref, oa_ref, hm_ref, ga_ref, gm_ref, wua, wum, wout, l1g, l1b, wr, br,
                  x1_ref, x1b_ref, comb_ref):
    pa = jnp.dot(oa_ref[...], wua[...], preferred_element_type=F32)
    pm = jnp.dot(hm_ref[...], wum[...], preferred_element_type=F32)
    y = jax.nn.sigmoid(ga_ref[...].astype(F32)) * pa + jax.nn.sigmoid(gm_ref[...].astype(F32)) * pm
    mix = jnp.dot(y.astype(BF16), wout[...], preferred_element_type=F32)
    x1 = _layer_norm(ALPHA * x_ref[...] + mix, l1g[...], l1b[...])
    x1_ref[...] = x1
    x1b_ref[...] = x1.astype(BF16)

    logits = jnp.dot(x1, wr[...], preferred_element_type=F32, precision=lax.Precision.HIGHEST) + br[...]
    lane = lax.broadcasted_iota(jnp.int32, logits.shape, 1)
    big = jnp.int32(LANES)
    is_grp = jnp.logical_and(lane >= N_EXPERTS, lane < N_EXPERTS + N_GROUPS)
    gl = jnp.where(is_grp, logits, NEG_INF)
    ge = jnp.exp(gl - jnp.max(gl, axis=-1, keepdims=True))
    gp = ge / jnp.sum(ge, axis=-1, keepdims=True)
    g_w = jnp.max(gp, axis=-1, keepdims=True)
    g_lane = jnp.min(jnp.where(jnp.logical_and(is_grp, gp == g_w), lane, big), axis=-1, keepdims=True)
    g_idx = g_lane - N_EXPERTS
    in_grp = jnp.logical_and(lane < N_EXPERTS, (lane // EXPERTS_PER_GROUP) == g_idx)
    el = jnp.where(in_grp, logits, NEG_INF)
    m1 = jnp.max(el, axis=-1, keepdims=True)
    i1 = jnp.min(jnp.where(el == m1, lane, big), axis=-1, keepdims=True)
    el2 = jnp.where(lane == i1, NEG_INF, el)
    m2 = jnp.max(el2, axis=-1, keepdims=True)
    i2 = jnp.min(jnp.where(el2 == m2, lane, big), axis=-1, keepdims=True)
    e2 = jnp.exp(m2 - m1)
    w1 = g_w / (1.0 + e2)
    w2 = g_w * e2 / (1.0 + e2)
    comb_ref[...] = jnp.where(lane == i1, w1, jnp.where(lane == i2, w2, 0.0))


def _merge(x2, oa, hm, ga, gm, wua, wum, wout, l1g, l1b, wr, br, tm):
    T = x2.shape[0]
    blk = lambda n: pl.BlockSpec((tm, n), lambda i: (i, 0))
    in_specs = [blk(D_MODEL), blk(WIDTH_A), blk(WIDTH_M), blk(D_MODEL), blk(D_MODEL),
                _full(wua.shape), _full(wum.shape), _full(wout.shape), _full(l1g.shape), _full(l1b.shape),
                _full(wr.shape), _full(br.shape)]
    out_specs = [blk(D_MODEL), blk(D_MODEL), blk(LANES)]
    out_shape = [jax.ShapeDtypeStruct((T, D_MODEL), F32), jax.ShapeDtypeStruct((T, D_MODEL), BF16),
                 jax.ShapeDtypeStruct((T, LANES), F32)]
    return pl.pallas_call(
        _merge_kernel, grid=(T // tm,), in_specs=in_specs, out_specs=out_specs, out_shape=out_shape,
        compiler_params=_cparams(1), name="merge")(x2, oa, hm, ga, gm, wua, wum, wout, l1g, l1b, wr, br)


def _moe_kernel(x1_ref, x1b_ref, comb_ref, wg_ref, wu_ref, wd_ref, l2g, l2b, out_ref, acc_ref):
    e = pl.program_id(1)

    @pl.when(e == 0)
    def _init():
        acc_ref[...] = jnp.zeros_like(acc_ref)

    xb = x1b_ref[...]
    g = jnp.dot(xb, wg_ref[0], preferred_element_type=F32)
    u = jnp.dot(xb, wu_ref[0], preferred_element_type=F32)
    hdn = (g * jax.nn.sigmoid(g) * u).astype(BF16)
    y = jnp.dot(hdn, wd_ref[0], preferred_element_type=F32)
    comb = comb_ref[...]
    lane = lax.broadcasted_iota(jnp.int32, comb.shape, 1)
    cw = jnp.sum(jnp.where(lane == e, comb, 0.0), axis=-1, keepdims=True)
    acc_ref[...] += cw * y

    @pl.when(e == N_EXPERTS - 1)
    def _fin():
        out_ref[...] = _layer_norm(ALPHA * x1_ref[...] + acc_ref[...], l2g[...], l2b[...])


def _moe(x1, x1b, comb, wg, wu, wd, l2g, l2b, tm):
    T = x1.shape[0]
    blk = lambda n: pl.BlockSpec((tm, n), lambda i, e: (i, 0))
    in_specs = [blk(D_MODEL), blk(D_MODEL), blk(LANES),
                pl.BlockSpec((1, D_MODEL, D_EXPERT), lambda i, e: (e, 0, 0)),
                pl.BlockSpec((1, D_MODEL, D_EXPERT), lambda i, e: (e, 0, 0)),
                pl.BlockSpec((1, D_EXPERT, D_MODEL), lambda i, e: (e, 0, 0)),
                _full(l2g.shape), _full(l2b.shape)]
    return pl.pallas_call(
        _moe_kernel, grid=(T // tm, N_EXPERTS), in_specs=in_specs, out_specs=blk(D_MODEL),
        out_shape=jax.ShapeDtypeStruct((T, D_MODEL), F32),
        scratch_shapes=[pltpu.VMEM((tm, D_MODEL), F32)],
        compiler_params=_cparams(2), name="moe")(x1, x1b, comb, wg, wu, wd, l2g, l2b)


def _pick_tile(T, pref):
    t = pref
    while T % t:
        t //= 2
    return t


def kernel(x, w_in, conv_w, conv_b, kv_norm_g, w_uk, w_uv, rel_bias, b_i, b_f, mh_norm_g, w_up_a, w_up_m,
           w_out, ln1_g, ln1_b, w_grp, b_grp, w_rt, b_rt, w_gate, w_up, w_down, ln2_g, ln2_b):
    B, S, _ = x.shape
    T = B * S
    assert S % Q_TILE == 0 and S % M_CHUNK == 0 and w_in.shape[0] == DEPTH
    far, tz = _bias_tables(rel_bias)
    x2 = x.reshape(T, D_MODEL)
    for l in range(DEPTH):
        w = w_in[l]
        o = np.cumsum((WIDTH_A, KV_RANK, WIDTH_IDX, HEAD_DIM_IDX, N_HEADS_IDX, 2 * WIDTH_M, WIDTH_M,
                       N_HEADS_M, N_HEADS_M, WIDTH_M, D_MODEL, D_MODEL)).tolist()
        o = [0] + o
        seg = lambda j: w[:, o[j]:o[j + 1]]
        pad = LANES - (HEAD_DIM_IDX + N_HEADS_IDX + 2 * N_HEADS_M)
        w_small = jnp.concatenate([seg(3), seg(4), seg(7), seg(8), jnp.zeros((D_MODEL, pad), w.dtype)], axis=1)
        ws = [seg(0), seg(1), seg(2), w_small, seg(5), seg(6), seg(9), seg(10), seg(11)]
        ws = [a.astype(BF16) for a in ws]
        smb = jnp.zeros((1, LANES), F32).at[0, SM_I:SM_I + N_HEADS_M].set(b_i[l]) \
            .at[0, SM_F:SM_F + N_HEADS_M].set(b_f[l])
        qa, ckv, qi, sm, qk, v, og, ga, gm = _proj(x2, ws, kv_norm_g[l][None, :], smb, _pick_tile(T, 512))

        wuk_t = jnp.transpose(w_uk[l], (1, 2, 0)).astype(BF16)
        wuv_t = jnp.transpose(w_uv[l], (1, 0, 2)).astype(BF16)
        oa = _dsa(far, qi, sm, qa, ckv, wuk_t, wuv_t, tz, B, S)

        hm = _mlstm(qk, v, sm, og, conv_w[l], conv_b[l][None, :], mh_norm_g[l].reshape(1, WIDTH_M), B, S)

        w_router = jnp.concatenate(
            [w_rt[l], w_grp[l], jnp.zeros((D_MODEL, LANES - N_EXPERTS - N_GROUPS), F32)], axis=1)
        b_router = jnp.concatenate(
            [b_rt[l], b_grp[l], jnp.zeros((LANES - N_EXPERTS - N_GROUPS,), F32)])[None, :]
        x1, x1b, comb = _merge(x2, oa, hm, ga, gm, w_up_a[l].astype(BF16), w_up_m[l].astype(BF16),
                               w_out[l].astype(BF16), ln1_g[l][None, :], ln1_b[l][None, :],
                               w_router, b_router, _pick_tile(T, 512))

        x2 = _moe(x1, x1b, comb, w_gate[l].astype(BF16), w_up[l].astype(BF16), w_down[l].astype(BF16),
                  ln2_g[l][None, :], ln2_b[l][None, :], _pick_tile(T, 1024))
    return x2.reshape(B, S, D_MODEL)
```

```python
import functools
import math

import jax
import jax.numpy as jnp
import numpy as np
from jax import lax
from jax.experimental import pallas as pl
from jax.experimental.pallas import tpu as pltpu

F32 = jnp.float32
BF16 = jnp.bfloat16

D_MODEL = 1024
N_HEADS_A = 8
HEAD_DIM_A = 64
WIDTH_A = N_HEADS_A * HEAD_DIM_A
KV_RANK = 256
N_HEADS_IDX = 8
HEAD_DIM_IDX = 64
WIDTH_IDX = N_HEADS_IDX * HEAD_DIM_IDX
TOPK_MAX = 256
N_BUCKETS = 32
MAX_DISTANCE = 128
N_HEADS_M = 4
HEAD_DIM_M = 128
WIDTH_M = N_HEADS_M * HEAD_DIM_M
CONV_WIDTH = 4
N_GROUPS = 4
EXPERTS_PER_GROUP = 4
N_EXPERTS = N_GROUPS * EXPERTS_PER_GROUP
D_EXPERT = 512
LN_EPS = 1e-5
DEPTH = 1
ALPHA = (2.0 * DEPTH) ** 0.25

LANES = 128
SUBLANES = 8
VMEM_LIMIT = 56 * 1024 * 1024

SM_KIDX = 0
SM_WIDX = HEAD_DIM_IDX
SM_I = SM_WIDX + N_HEADS_IDX
SM_F = SM_I + N_HEADS_M

Q_TILE = 256
K_CHUNK = Q_TILE
M_CHUNK = 128
BISECT_STEPS_PER_CHECK = 3
BISECT_MAX_ITERS = 16
NEG_INF = float("-inf")
LOG2E = math.log2(math.e)


def _cparams(n_grid):
    return pltpu.CompilerParams(dimension_semantics=("arbitrary",) * n_grid,
                                vmem_limit_bytes=VMEM_LIMIT)


def _full(shape):
    nd = len(shape)
    return pl.BlockSpec(shape, lambda *_: (0,) * nd)


def _proj_kernel(x_ref, wqa, wckv, wqi, wsm, wqk, wv, wo, wga, wgm, kvg, smb,
                 qa_o, ckv_o, qi_o, sm_o, qk_o, v_o, o_o, ga_o, gm_o):
    xb = x_ref[...].astype(BF16)

    def mm(w):
        return jnp.dot(xb, w[...], preferred_element_type=F32)

    qa_o[...] = mm(wqa).astype(BF16)
    c = mm(wckv)
    c = c * lax.rsqrt(jnp.mean(c * c, axis=-1, keepdims=True) + LN_EPS) * kvg[...]
    ckv_o[...] = c.astype(BF16)
    qi_o[...] = mm(wqi).astype(BF16)
    sm_o[...] = mm(wsm) + smb[...]
    qk_o[...] = mm(wqk).astype(BF16)
    v_o[...] = mm(wv).astype(BF16)
    o_o[...] = mm(wo).astype(BF16)
    ga_o[...] = mm(wga).astype(BF16)
    gm_o[...] = mm(wgm).astype(BF16)


def _proj(x2, ws, kvg, smb, tm):
    T = x2.shape[0]
    widths = [w.shape[1] for w in ws]
    dts = [BF16, BF16, BF16, F32, BF16, BF16, BF16, BF16, BF16]
    in_specs = [pl.BlockSpec((tm, D_MODEL), lambda i: (i, 0))]
    in_specs += [_full(w.shape) for w in ws]
    in_specs += [_full(kvg.shape), _full(smb.shape)]
    out_specs = [pl.BlockSpec((tm, n), lambda i: (i, 0)) for n in widths]
    out_shape = [jax.ShapeDtypeStruct((T, n), dt) for n, dt in zip(widths, dts)]
    return pl.pallas_call(
        _proj_kernel, grid=(T // tm,), in_specs=in_specs, out_specs=out_specs, out_shape=out_shape,
        compiler_params=_cparams(1), name="proj")(x2, *ws, kvg, smb)


def _dsa_kernel(qi_ref, smq_ref, smk_ref, qa_ref, ckv_ref, ckvt_ref, wuk_ref, wuvt_ref, tz_ref,
                oa_ref, qs_ref, sc_ref, am_ref, ql_ref, x_ref, acc_ref, m_ref, *, topk):
    tq, kc = Q_TILE, K_CHUNK
    nh = N_HEADS_A
    qb = pl.program_id(1)
    n_ch = qb + 1
    t0 = qb * tq

    def rows(c):
        return pl.ds(pl.multiple_of(c * kc, kc), kc)

    def lanes(h):
        return slice(h * tq, (h + 1) * tq)

    for h in range(N_HEADS_IDX):
        qs_ref[h * tq:(h + 1) * tq, :] = qi_ref[:, h * HEAD_DIM_IDX:(h + 1) * HEAD_DIM_IDX]
    w_t = smq_ref[...].T
    q_pos = lax.broadcasted_iota(jnp.int32, (1, tq), 1) + t0
    key_iota = lax.broadcasted_iota(jnp.int32, (kc, tq), 0)

    def score_chunk(c, carry):
        mx, mn = carry
        kk = smk_ref[rows(c), SM_KIDX:SM_KIDX + HEAD_DIM_IDX].astype(BF16)
        dots = lax.dot_general(kk, qs_ref[...], (((1,), (1,)), ((), ())), preferred_element_type=F32)
        sc = jnp.zeros((kc, tq), F32)
        for h in range(N_HEADS_IDX):
            sc = sc + w_t[SM_WIDX + h:SM_WIDX + h + 1, :] * jnp.maximum(dots[:, lanes(h)], 0.0)
        vis = (key_iota + c * kc) <= q_pos
        sc_ref[rows(c), :] = jnp.where(vis, sc, NEG_INF)
        mx = jnp.maximum(mx, jnp.max(jnp.where(vis, sc, NEG_INF), axis=0, keepdims=True))
        mn = jnp.minimum(mn, jnp.min(jnp.where(vis, sc, jnp.inf), axis=0, keepdims=True))
        return mx, mn

    mx, mn = lax.fori_loop(0, n_ch, score_chunk,
                           (jnp.full((1, tq), NEG_INF, F32), jnp.full((1, tq), jnp.inf, F32)))

    n_vis = (q_pos + 1).astype(F32)
    k_row = jnp.minimum(n_vis, float(topk))

    def count(pred):
        def body(c, a):
            hit = pred(sc_ref[rows(c), :]).astype(F32)
            return a + jnp.sum(hit.reshape(kc // SUBLANES, SUBLANES, tq), axis=0)
        a = lax.fori_loop(0, n_ch, body, jnp.zeros((SUBLANES, tq), F32))
        return jnp.sum(a, axis=0, keepdims=True)

    def unresolved(cnt_lo):
        return jnp.max(jnp.where(cnt_lo != k_row, 1.0, 0.0)) > 0.0

    def cond(carry):
        it, lo, hi, cnt_lo = carry
        return jnp.logical_and(it < BISECT_MAX_ITERS, unresolved(cnt_lo))

    def body(carry):
        it, lo, hi, cnt_lo = carry
        for _ in range(BISECT_STEPS_PER_CHECK):
            mid = lo * 0.5 + hi * 0.5
            cnt = count(lambda s: s >= mid)
            ge = cnt >= k_row
            lo, hi, cnt_lo = jnp.where(ge, mid, lo), jnp.where(ge, hi, mid), jnp.where(ge, cnt, cnt_lo)
        return it + 1, lo, hi, cnt_lo

    _, lo, hi, cnt_lo = lax.while_loop(cond, body, (jnp.int32(0), mn, mx, n_vis))

    def mask_chunk(c, _):
        am_ref[rows(c), :] = jnp.where(sc_ref[rows(c), :] >= lo, 0.0, NEG_INF)
        return 0

    lax.fori_loop(0, n_ch, mask_chunk, 0)

    @pl.when(unresolved(cnt_lo))
    def _ties():
        need = k_row - count(lambda s: s > hi)
        lower = (lax.broadcasted_iota(jnp.int32, (kc, kc), 1)
                 < lax.broadcasted_iota(jnp.int32, (kc, kc), 0)).astype(BF16)

        def tie_chunk(c, before):
            s = sc_ref[rows(c), :]
            above = s > hi
            tie = jnp.logical_and(s >= lo, jnp.logical_not(above))
            tie_f = tie.astype(F32)
            rank = jnp.dot(lower, tie_f.astype(BF16), preferred_element_type=F32) + before
            sel = jnp.logical_or(above, jnp.logical_and(tie, rank < need))
            am_ref[rows(c), :] = jnp.where(sel, 0.0, NEG_INF)
            return before + jnp.sum(tie_f, axis=0, keepdims=True)

        lax.fori_loop(0, n_ch, tie_chunk, jnp.zeros((1, tq), F32))

    for h in range(nh):
        qh = qa_ref[:, h * HEAD_DIM_A:(h + 1) * HEAD_DIM_A]
        qlt = lax.dot_general(wuk_ref[h], qh, (((1,), (1,)), ((), ())), preferred_element_type=F32)
        ql_ref[:, lanes(h)] = (qlt * (HEAD_DIM_A ** -0.5 * LOG2E)).astype(BF16)

    m_ref[...] = jnp.full(m_ref.shape, NEG_INF, F32)

    def pass_a(c, table):
        raw = jnp.dot(ckv_ref[rows(c), :], ql_ref[...], preferred_element_type=F32)
        am = am_ref[rows(c), :]
        for h in range(nh):
            x = raw[:, lanes(h)] + am
            if table is not None:
                x = x + tz_ref[table, h]
            x_ref[rows(c), lanes(h)] = x
            m_ref[:, lanes(h)] = jnp.maximum(m_ref[:, lanes(h)], jnp.max(x, axis=0, keepdims=True))

    def far_chunk(c, _):
        pass_a(c, None)
        return 0

    lax.fori_loop(0, qb - 1, far_chunk, 0)

    @pl.when(qb >= 1)
    def _prev():
        pass_a(qb - 1, 1)

    pass_a(qb, 0)

    acc_ref[...] = jnp.zeros(acc_ref.shape, F32)

    def pass_b(c, l):
        p = jnp.exp2(x_ref[rows(c), :] - m_ref[...])
        acc_ref[...] += jnp.dot(ckvt_ref[:, rows(c)], p.astype(BF16), preferred_element_type=F32)
        return l + jnp.sum(p, axis=0, keepdims=True)

    l = lax.fori_loop(0, n_ch, pass_b, jnp.zeros((1, nh * tq), F32))
    o_lat = (acc_ref[...] * (1.0 / l)).astype(BF16)
    outs = [jnp.dot(wuvt_ref[h], o_lat[:, lanes(h)], preferred_element_type=F32) for h in range(nh)]
    oa_ref[...] = jnp.concatenate(outs, axis=0).T.astype(BF16)


def _dsa(qi, sm, qa, ckv, ckv_t, wuk_t, wuv_t, tz, batch, seq):
    tq = Q_TILE
    nq = seq // tq
    topk = min(TOPK_MAX, seq // 4)
    T = batch * seq
    kern = functools.partial(_dsa_kernel, topk=topk)
    blk_q = lambda n: pl.BlockSpec((tq, n), lambda b, q: (b * nq + q, 0))
    blk_s = lambda n: pl.BlockSpec((seq, n), lambda b, q: (b, 0))
    in_specs = [
        blk_q(WIDTH_IDX), blk_q(LANES), blk_s(LANES), blk_q(WIDTH_A), blk_s(KV_RANK),
        pl.BlockSpec((None, KV_RANK, seq), lambda b, q: (b, 0, 0)),
        _full(wuk_t.shape), _full(wuv_t.shape), _full(tz.shape),
    ]
    scratch = [
        pltpu.VMEM((N_HEADS_IDX * tq, HEAD_DIM_IDX), BF16),
        pltpu.VMEM((seq, tq), F32),
        pltpu.VMEM((seq, tq), F32),
        pltpu.VMEM((KV_RANK, N_HEADS_A * tq), BF16),
        pltpu.VMEM((seq, N_HEADS_A * tq), F32),
        pltpu.VMEM((KV_RANK, N_HEADS_A * tq), F32),
        pltpu.VMEM((1, N_HEADS_A * tq), F32),
    ]
    return pl.pallas_call(
        kern, grid=(batch, nq), in_specs=in_specs,
        out_specs=pl.BlockSpec((tq, WIDTH_A), lambda b, q: (b * nq + q, 0)),
        out_shape=jax.ShapeDtypeStruct((T, WIDTH_A), BF16),
        scratch_shapes=scratch, compiler_params=_cparams(2), name="dsa")(
            qi, sm, sm, qa, ckv, ckv_t, wuk_t, wuv_t, tz)


def _t5_bucket(dist):
    max_exact = N_BUCKETS // 2
    d = jnp.maximum(dist, 0)
    ratio = jnp.log(jnp.maximum(d, 1).astype(F32) / max_exact) / math.log(MAX_DISTANCE / max_exact)
    large = jnp.minimum(max_exact + (ratio * (N_BUCKETS - max_exact)).astype(jnp.int32), N_BUCKETS - 1)
    return jnp.where(d < max_exact, d, large)


def _bias_tables(rel_bias):
    tq = Q_TILE
    span = 2 * tq
    assert int(_np_bucket(tq)) == N_BUCKETS - 1
    far = rel_bias[N_BUCKETS - 1]
    by_dist = (rel_bias[_t5_bucket(jnp.arange(span))] - far).astype(F32) * LOG2E
    diag = jnp.concatenate([by_dist[:tq], jnp.zeros_like(by_dist[:tq])])
    prev = jnp.concatenate([by_dist[tq:], by_dist[:tq]])

    def toeplitz(f):
        m = jnp.tile(f, (tq, 1))[:tq * (span - 1)].reshape(tq, span - 1, N_HEADS_A)
        return m[:, :tq]

    tz = jnp.stack([toeplitz(diag), toeplitz(prev)])
    return jnp.moveaxis(tz, -1, 1)


def _np_bucket(d):
    max_exact = N_BUCKETS // 2
    ratio = np.log(np.float32(max(d, 1)) / np.float32(max_exact)) / math.log(MAX_DISTANCE / max_exact)
    return min(max_exact + int(ratio * (N_BUCKETS - max_exact)), N_BUCKETS - 1) if d >= max_exact else d


def _mlstm_kernel(qk_ref, halo_ref, v_ref, sm_ref, o_ref, cw_ref, cb_ref, g_ref,
                  out_ref, c_ref, m_ref):
    L = M_CHUNK
    dm = HEAD_DIM_M
    c_idx = pl.program_id(1)

    @pl.when(c_idx == 0)
    def _init():
        c_ref[...] = jnp.zeros_like(c_ref)
        m_ref[...] = jnp.zeros_like(m_ref)

    halo = jnp.where(c_idx > 0, halo_ref[...].astype(F32), 0.0)
    ext = jnp.concatenate([halo, qk_ref[...].astype(F32)], axis=0)
    acc = jnp.zeros((L, 2 * WIDTH_M), F32) + cb_ref[...]
    for w in range(CONV_WIDTH):
        off = SUBLANES - (CONV_WIDTH - 1) + w
        acc = acc + ext[off:off + L, :] * cw_ref[w:w + 1, :]
    qk = acc * jax.nn.sigmoid(acc)

    sm = sm_ref[...]
    sm_t = sm.T
    r_i = lax.broadcasted_iota(jnp.int32, (L, L), 0)
    c_i = lax.broadcasted_iota(jnp.int32, (L, L), 1)
    tril = (c_i <= r_i).astype(F32)
    triu = (r_i <= c_i).astype(F32)
    bcum_c = jnp.dot(tril, jax.nn.log_sigmoid(sm), preferred_element_type=F32,
                     precision=lax.Precision.HIGHEST)
    bcum_r = jnp.dot(jax.nn.log_sigmoid(sm_t), triu, preferred_element_type=F32,
                     precision=lax.Precision.HIGHEST)
    causal = c_i <= r_i
    ones_col = (lax.broadcasted_iota(jnp.int32, (L, dm), 1) == 0).astype(BF16)

    outs = []
    for h in range(N_HEADS_M):
        q = qk[:, h * dm:(h + 1) * dm].astype(BF16)
        k = (qk[:, WIDTH_M + h * dm:WIDTH_M + (h + 1) * dm] * (dm ** -0.5))
        v_aug = jnp.concatenate([v_ref[:, h * dm:(h + 1) * dm], ones_col], axis=-1)
        b_col = bcum_c[:, SM_F + h:SM_F + h + 1]
        g_col = sm[:, SM_I + h:SM_I + h + 1] - b_col
        g_row = sm_t[SM_I + h:SM_I + h + 1, :] - bcum_r[SM_F + h:SM_F + h + 1, :]
        b_last = b_col[L - 1:L, :]
        m_prev = m_ref[h]
        c_prev = c_ref[h]

        log_d = jnp.where(causal, b_col + g_row, NEG_INF)
        m_j = jnp.maximum(b_col + m_prev, jnp.max(log_d, axis=-1, keepdims=True))
        w_inter = jnp.exp(b_col + m_prev - m_j)
        qkt = lax.dot_general(q, k.astype(BF16), (((1,), (1,)), ((), ())), preferred_element_type=F32)
        s = qkt * jnp.exp(log_d - m_j)
        o_aug = jnp.dot(s.astype(BF16), v_aug, preferred_element_type=F32) + \
            w_inter * jnp.dot(q, c_prev.astype(BF16), preferred_element_type=F32)
        num = o_aug[:, :dm]
        den = o_aug[:, dm:dm + 1]
        hh = num / jnp.maximum(jnp.abs(den), jnp.exp(-m_j))

        lwe = b_last + g_col
        m_loc = jnp.max(lwe, axis=0, keepdims=True)
        kw = (k * jnp.exp(lwe - m_loc)).astype(BF16)
        c_loc = lax.dot_general(kw, v_aug, (((0,), (0,)), ((), ())), preferred_element_type=F32)
        m_new = jnp.maximum(b_last + m_prev, m_loc)
        c_ref[h] = jnp.exp(b_last + m_prev - m_new) * c_prev + jnp.exp(m_loc - m_new) * c_loc
        m_ref[h] = m_new

        hn = hh * lax.rsqrt(jnp.mean(hh * hh, axis=-1, keepdims=True) + LN_EPS) * g_ref[:, h * dm:(h + 1) * dm]
        og = o_ref[:, h * dm:(h + 1) * dm].astype(F32)
        outs.append(jax.nn.sigmoid(og) * hn)
    out_ref[...] = jnp.concatenate(outs, axis=-1).astype(BF16)


def _mlstm(qk, v, sm, o, conv_w, conv_b, mh_g, batch, seq):
    L = M_CHUNK
    nc = seq // L
    T = batch * seq
    hb = L // SUBLANES
    blk = lambda n: pl.BlockSpec((L, n), lambda b, c: (b * nc + c, 0))
    in_specs = [
        blk(2 * WIDTH_M),
        pl.BlockSpec((SUBLANES, 2 * WIDTH_M), lambda b, c: (jnp.maximum((b * nc + c) * hb - 1, 0), 0)),
        blk(WIDTH_M), blk(LANES), blk(WIDTH_M),
        _full(conv_w.shape), _full(conv_b.shape), _full(mh_g.shape),
    ]
    scratch = [pltpu.VMEM((N_HEADS_M, HEAD_DIM_M, 2 * HEAD_DIM_M), F32),
               pltpu.VMEM((N_HEADS_M, 1, 1), F32)]
    return pl.pallas_call(
        _mlstm_kernel, grid=(batch, nc), in_specs=in_specs, out_specs=blk(WIDTH_M),
        out_shape=jax.ShapeDtypeStruct((T, WIDTH_M), BF16), scratch_shapes=scratch,
        compiler_params=_cparams(2), name="mlstm")(qk, qk, v, sm, o, conv_w, conv_b, mh_g)


def _layer_norm(y, g, b):
    mu = jnp.mean(y, axis=-1, keepdims=True)
    var = jnp.mean(jnp.square(y - mu), axis=-1, keepdims=True)
    return (y - mu) * lax.rsqrt(var + LN_EPS) * g + b


def _merge_kernel(x_ref, oa_ref, hm_ref, ga_ref, gm_ref, wua, wum, wout, l1g, l1b, wr, br,
                  x1_ref, x1b_ref, comb_ref):
    pa = jnp.dot(oa_ref[...], wua[...], preferred_element_type=F32)
    pm = jnp.dot(hm_ref[...], wum[...], preferred_element_type=F32)
    y = jax.nn.sigmoid(ga_ref[...].astype(F32)) * pa + jax.nn.sigmoid(gm_ref[...].astype(F32)) * pm
    mix = jnp.dot(y.astype(BF16), wout[...], preferred_element_type=F32)
    x1 = _layer_norm(ALPHA * x_ref[...] + mix, l1g[...], l1b[...])
    x1_ref[...] = x1
    x1b_ref[...] = x1.astype(BF16)

    logits = jnp.dot(x1, wr[...], preferred_element_type=F32, precision=lax.Precision.HIGHEST) + br[...]
    lane = lax.broadcasted_iota(jnp.int32, logits.shape, 1)
    big = jnp.int32(LANES)
    is_grp = jnp.logical_and(lane >= N_EXPERTS, lane < N_EXPERTS + N_GROUPS)
    gl = jnp.where(is_grp, logits, NEG_INF)
    ge = jnp.exp(gl - jnp.max(gl, axis=-1, keepdims=True))
    gp = ge / jnp.sum(ge, axis=-1, keepdims=True)
    g_w = jnp.max(gp, axis=-1, keepdims=True)
    g_lane = jnp.min(jnp.where(jnp.logical_and(is_grp, gp == g_w), lane, big), axis=-1, keepdims=True)
    g_idx = g_lane - N_EXPERTS
    in_grp = jnp.logical_and(lane < N_EXPERTS, (lane // EXPERTS_PER_GROUP) == g_idx)
    el = jnp.where(in_grp, logits, NEG_INF)
    m1 = jnp.max(el, axis=-1, keepdims=True)
    i1 = jnp.min(jnp.where(el == m1, lane, big), axis=-1, keepdims=True)
    el2 = jnp.where(lane == i1, NEG_INF, el)
    m2 = jnp.max(el2, axis=-1, keepdims=True)
    i2 = jnp.min(jnp.where(el2 == m2, lane, big), axis=-1, keepdims=True)
    e2 = jnp.exp(m2 - m1)
    w1 = g_w / (1.0 + e2)
    w2 = g_w * e2 / (1.0 + e2)
    comb_ref[...] = jnp.where(lane == i1, w1, jnp.where(lane == i2, w2, 0.0))


def _merge(x2, oa, hm, ga, gm, wua, wum, wout, l1g, l1b, wr, br, tm):
    T = x2.shape[0]
    blk = lambda n: pl.BlockSpec((tm, n), lambda i: (i, 0))
    in_specs = [blk(D_MODEL), blk(WIDTH_A), blk(WIDTH_M), blk(D_MODEL), blk(D_MODEL),
                _full(wua.shape), _full(wum.shape), _full(wout.shape), _full(l1g.shape), _full(l1b.shape),
                _full(wr.shape), _full(br.shape)]
    out_specs = [blk(D_MODEL), blk(D_MODEL), blk(LANES)]
    out_shape = [jax.ShapeDtypeStruct((T, D_MODEL), F32), jax.ShapeDtypeStruct((T, D_MODEL), BF16),
                 jax.ShapeDtypeStruct((T, LANES), F32)]
    return pl.pallas_call(
        _merge_kernel, grid=(T // tm,), in_specs=in_specs, out_specs=out_specs, out_shape=out_shape,
        compiler_params=_cparams(1), name="merge")(x2, oa, hm, ga, gm, wua, wum, wout, l1g, l1b, wr, br)


def _moe_kernel(x1_ref, x1b_ref, comb_ref, wg_ref, wu_ref, wd_ref, l2g, l2b, out_ref, acc_ref):
    e = pl.program_id(1)

    @pl.when(e == 0)
    def _init():
        acc_ref[...] = jnp.zeros_like(acc_ref)

    xb = x1b_ref[...]
    g = jnp.dot(xb, wg_ref[0], preferred_element_type=F32)
    u = jnp.dot(xb, wu_ref[0], preferred_element_type=F32)
    hdn = (g * jax.nn.sigmoid(g) * u).astype(BF16)
    y = jnp.dot(hdn, wd_ref[0], preferred_element_type=F32)
    comb = comb_ref[...]
    lane = lax.broadcasted_iota(jnp.int32, comb.shape, 1)
    cw = jnp.sum(jnp.where(lane == e, comb, 0.0), axis=-1, keepdims=True)
    acc_ref[...] += cw * y

    @pl.when(e == N_EXPERTS - 1)
    def _fin():
        out_ref[...] = _layer_norm(ALPHA * x1_ref[...] + acc_ref[...], l2g[...], l2b[...])


def _moe(x1, x1b, comb, wg, wu, wd, l2g, l2b, tm):
    T = x1.shape[0]
    blk = lambda n: pl.BlockSpec((tm, n), lambda i, e: (i, 0))
    in_specs = [blk(D_MODEL), blk(D_MODEL), blk(LANES),
                pl.BlockSpec((1, D_MODEL, D_EXPERT), lambda i, e: (e, 0, 0)),
                pl.BlockSpec((1, D_MODEL, D_EXPERT), lambda i, e: (e, 0, 0)),
                pl.BlockSpec((1, D_EXPERT, D_MODEL), lambda i, e: (e, 0, 0)),
                _full(l2g.shape), _full(l2b.shape)]
    return pl.pallas_call(
        _moe_kernel, grid=(T // tm, N_EXPERTS), in_specs=in_specs, out_specs=blk(D_MODEL),
        out_shape=jax.ShapeDtypeStruct((T, D_MODEL), F32),
        scratch_shapes=[pltpu.VMEM((tm, D_MODEL), F32)],
        compiler_params=_cparams(2), name="moe")(x1, x1b, comb, wg, wu, wd, l2g, l2b)


def _pick_tile(T, pref):
    t = pref
    while T % t:
        t //= 2
    return t


def kernel(x, w_in, conv_w, conv_b, kv_norm_g, w_uk, w_uv, rel_bias, b_i, b_f, mh_norm_g, w_up_a, w_up_m,
           w_out, ln1_g, ln1_b, w_grp, b_grp, w_rt, b_rt, w_gate, w_up, w_down, ln2_g, ln2_b):
    B, S, _ = x.shape
    T = B * S
    assert S % Q_TILE == 0 and S % M_CHUNK == 0 and w_in.shape[0] == DEPTH
    tz = _bias_tables(rel_bias)
    x2 = x.reshape(T, D_MODEL)
    for l in range(DEPTH):
        w = w_in[l]
        o = np.cumsum((WIDTH_A, KV_RANK, WIDTH_IDX, HEAD_DIM_IDX, N_HEADS_IDX, 2 * WIDTH_M, WIDTH_M,
                       N_HEADS_M, N_HEADS_M, WIDTH_M, D_MODEL, D_MODEL)).tolist()
        o = [0] + o
        seg = lambda j: w[:, o[j]:o[j + 1]]
        pad = LANES - (HEAD_DIM_IDX + N_HEADS_IDX + 2 * N_HEADS_M)
        w_small = jnp.concatenate([seg(3), seg(4), seg(7), seg(8), jnp.zeros((D_MODEL, pad), w.dtype)], axis=1)
        ws = [seg(0), seg(1), seg(2), w_small, seg(5), seg(6), seg(9), seg(10), seg(11)]
        ws = [a.astype(BF16) for a in ws]
        smb = jnp.zeros((1, LANES), F32).at[0, SM_I:SM_I + N_HEADS_M].set(b_i[l]) \
            .at[0, SM_F:SM_F + N_HEADS_M].set(b_f[l])
        qa, ckv, qi, sm, qk, v, og, ga, gm = _proj(x2, ws, kv_norm_g[l][None, :], smb, _pick_tile(T, 512))

        wuk_t = jnp.transpose(w_uk[l], (1, 0, 2)).astype(BF16)
        wuv_t = jnp.transpose(w_uv[l], (1, 2, 0)).astype(BF16)
        ckv_t = jnp.swapaxes(ckv.reshape(B, S, KV_RANK), 1, 2)
        oa = _dsa(qi, sm, qa, ckv, ckv_t, wuk_t, wuv_t, tz, B, S)

        hm = _mlstm(qk, v, sm, og, conv_w[l], conv_b[l][None, :], mh_norm_g[l].reshape(1, WIDTH_M), B, S)

        w_router = jnp.concatenate(
            [w_rt[l], w_grp[l], jnp.zeros((D_MODEL, LANES - N_EXPERTS - N_GROUPS), F32)], axis=1)
        b_router = jnp.concatenate(
            [b_rt[l], b_grp[l], jnp.zeros((LANES - N_EXPERTS - N_GROUPS,), F32)])[None, :]
        x1, x1b, comb = _merge(x2, oa, hm, ga, gm, w_up_a[l].astype(BF16), w_up_m[l].astype(BF16),
                               w_out[l].astype(BF16), ln1_g[l][None, :], ln1_b[l][None, :],
                               w_router, b_router, _pick_tile(T, 512))

        x2 = _moe(x1, x1b, comb, w_gate[l].astype(BF16), w_up[l].astype(BF16), w_down[l].astype(BF16),
                  ln2_g[l][None, :], ln2_b[l][None, :], _pick_tile(T, 1024))
    return x2.reshape(B, S, D_MODEL)
```

```python
import functools
import math

import jax
import jax.numpy as jnp
import numpy as np
from jax import lax
from jax.experimental import pallas as pl
from jax.experimental.pallas import tpu as pltpu

F32 = jnp.float32
BF16 = jnp.bfloat16

D_MODEL = 1024
N_HEADS_A = 8
HEAD_DIM_A = 64
WIDTH_A = N_HEADS_A * HEAD_DIM_A
KV_RANK = 256
N_HEADS_IDX = 8
HEAD_DIM_IDX = 64
WIDTH_IDX = N_HEADS_IDX * HEAD_DIM_IDX
TOPK_MAX = 256
N_BUCKETS = 32
MAX_DISTANCE = 128
N_HEADS_M = 4
HEAD_DIM_M = 128
WIDTH_M = N_HEADS_M * HEAD_DIM_M
CONV_WIDTH = 4
N_GROUPS = 4
EXPERTS_PER_GROUP = 4
N_EXPERTS = N_GROUPS * EXPERTS_PER_GROUP
D_EXPERT = 512
LN_EPS = 1e-5
DEPTH = 1
ALPHA = (2.0 * DEPTH) ** 0.25

LANES = 128
SUBLANES = 8
VMEM_LIMIT = 56 * 1024 * 1024

SM_KIDX = 0
SM_WIDX = HEAD_DIM_IDX
SM_I = SM_WIDX + N_HEADS_IDX
SM_F = SM_I + N_HEADS_M

Q_TILE = 256
K_CHUNK = Q_TILE
M_CHUNK = 128
MOE_TILE = 256
EPG_SHIFT = EXPERTS_PER_GROUP.bit_length() - 1
assert 1 << EPG_SHIFT == EXPERTS_PER_GROUP
PAIR_A, PAIR_B = zip(*[(a, b) for a in range(EXPERTS_PER_GROUP) for b in range(a + 1, EXPERTS_PER_GROUP)])
PAIRS_PER_GROUP = len(PAIR_A)
N_CLASSES = N_GROUPS * PAIRS_PER_GROUP
assert N_CLASSES <= LANES
BISECT_STEPS_PER_CHECK = 3
BISECT_MAX_CHECKS = 6
PEEL_BRACKET = 2.0
NEG_INF = float("-inf")
LOG2E = math.log2(math.e)


def _cparams(n_grid):
    return pltpu.CompilerParams(dimension_semantics=("arbitrary",) * n_grid,
                                vmem_limit_bytes=VMEM_LIMIT)


def _full(shape):
    nd = len(shape)
    return pl.BlockSpec(shape, lambda *_: (0,) * nd)


def _proj_kernel(x_ref, wqa, wckv, wqi, wsm, wqk, wv, wo, wga, wgm, kvg, smb,
                 qa_o, ckv_o, qi_o, sm_o, qk_o, v_o, o_o, ga_o, gm_o):
    xb = x_ref[...].astype(BF16)

    def mm(w):
        return jnp.dot(xb, w[...], preferred_element_type=F32)

    qa_o[...] = mm(wqa).astype(BF16)
    c = mm(wckv)
    c = c * lax.rsqrt(jnp.mean(c * c, axis=-1, keepdims=True) + LN_EPS) * kvg[...]
    ckv_o[...] = c.astype(BF16)
    qi_o[...] = mm(wqi).astype(BF16)
    sm_o[...] = mm(wsm) + smb[...]
    qk_o[...] = mm(wqk).astype(BF16)
    v_o[...] = mm(wv).astype(BF16)
    o_o[...] = mm(wo).astype(BF16)
    ga_o[...] = mm(wga).astype(BF16)
    gm_o[...] = mm(wgm).astype(BF16)


def _proj(x2, ws, kvg, smb, tm):
    T = x2.shape[0]
    widths = [w.shape[1] for w in ws]
    dts = [BF16, BF16, BF16, F32, BF16, BF16, BF16, BF16, BF16]
    in_specs = [pl.BlockSpec((tm, D_MODEL), lambda i: (i, 0))]
    in_specs += [_full(w.shape) for w in ws]
    in_specs += [_full(kvg.shape), _full(smb.shape)]
    out_specs = [pl.BlockSpec((tm, n), lambda i: (i, 0)) for n in widths]
    out_shape = [jax.ShapeDtypeStruct((T, n), dt) for n, dt in zip(widths, dts)]
    return pl.pallas_call(
        _proj_kernel, grid=(T // tm,), in_specs=in_specs, out_specs=out_specs, out_shape=out_shape,
        compiler_params=_cparams(1), name="proj")(x2, *ws, kvg, smb)


def _dsa_kernel(qi_ref, smq_ref, smk_ref, qa_ref, ckv_ref, ckvt_ref, wuk_ref, wuvt_ref, tz_ref,
                oa_ref, qs_ref, sc_ref, am_ref, ql_ref, x_ref, acc_ref, m_ref, *, topk):
    tq, kc = Q_TILE, K_CHUNK
    nh = N_HEADS_A
    qb = pl.program_id(1)
    n_ch = qb + 1
    seq_keys = n_ch * kc
    t0 = qb * tq

    def rows(c):
        return pl.ds(pl.multiple_of(c * kc, kc), kc)

    def lanes(h):
        return slice(h * tq, (h + 1) * tq)

    for h in range(N_HEADS_IDX):
        qs_ref[h * tq:(h + 1) * tq, :] = qi_ref[:, h * HEAD_DIM_IDX:(h + 1) * HEAD_DIM_IDX]
    w_t = smq_ref[...].T
    q_pos = lax.broadcasted_iota(jnp.int32, (1, tq), 1) + t0
    key_iota = lax.broadcasted_iota(jnp.int32, (kc, tq), 0)

    def score_chunk(c, carry):
        mx, mn = carry
        kk = smk_ref[rows(c), SM_KIDX:SM_KIDX + HEAD_DIM_IDX].astype(BF16)
        dots = lax.dot_general(kk, qs_ref[...], (((1,), (1,)), ((), ())), preferred_element_type=F32)
        sc = jnp.zeros((kc, tq), F32)
        for h in range(N_HEADS_IDX):
            sc = sc + w_t[SM_WIDX + h:SM_WIDX + h + 1, :] * jnp.maximum(dots[:, lanes(h)], 0.0)
        vis = (key_iota + c * kc) <= q_pos
        sc_ref[rows(c), :] = jnp.where(vis, sc, NEG_INF)
        mx = jnp.maximum(mx, jnp.max(jnp.where(vis, sc, NEG_INF), axis=0, keepdims=True))
        mn = jnp.minimum(mn, jnp.min(jnp.where(vis, sc, jnp.inf), axis=0, keepdims=True))
        return mx, mn

    mx, mn = lax.fori_loop(0, n_ch, score_chunk,
                           (jnp.full((1, tq), NEG_INF, F32), jnp.full((1, tq), jnp.inf, F32)))

    n_vis = (q_pos + 1).astype(F32)
    k_row = jnp.minimum(n_vis, float(topk))

    def count(pred):
        def body(c, a):
            hit = pred(sc_ref[rows(c), :]).astype(F32)
            return a + jnp.sum(hit.reshape(kc // SUBLANES, SUBLANES, tq), axis=0)
        a = lax.fori_loop(0, n_ch, body, jnp.zeros((SUBLANES, tq), F32))
        return jnp.sum(a, axis=0, keepdims=True)

    def any_lane(flag):
        return jnp.max(jnp.where(flag, 1.0, 0.0)) > 0.0

    def crowded(cnt_lo, c_hi):
        return any_lane(jnp.logical_and(cnt_lo != k_row, cnt_lo - c_hi > PEEL_BRACKET))

    def bisect_cond(carry):
        it, lo, hi, cnt_lo, c_hi = carry
        return jnp.logical_and(it < BISECT_MAX_CHECKS, crowded(cnt_lo, c_hi))

    def bisect_body(carry):
        it, lo, hi, cnt_lo, c_hi = carry
        for _ in range(BISECT_STEPS_PER_CHECK):
            mid = lo * 0.5 + hi * 0.5
            cnt = count(lambda s: s >= mid)
            ge = cnt >= k_row
            lo, cnt_lo = jnp.where(ge, mid, lo), jnp.where(ge, cnt, cnt_lo)
            hi, c_hi = jnp.where(ge, hi, mid), jnp.where(ge, c_hi, cnt)
        return it + 1, lo, hi, cnt_lo, c_hi

    hi0 = mx + jnp.maximum(jnp.abs(mx), 1e-30) * 1e-6
    _, lo, hi, cnt_lo, c_hi = lax.while_loop(
        bisect_cond, bisect_body, (jnp.int32(0), mn, hi0, n_vis, jnp.zeros((1, tq), F32)))

    def peel_cond(carry):
        it, lo, hi, cnt_lo, c_hi, done = carry
        return jnp.logical_and(it < seq_keys, any_lane(done == 0.0))

    def peel_body(carry):
        it, lo, hi, cnt_lo, c_hi, done = carry

        def top_body(c, v):
            s = sc_ref[rows(c), :]
            inside = jnp.logical_and(s >= lo, s < hi)
            return jnp.maximum(v, jnp.max(jnp.where(inside, s, NEG_INF), axis=0, keepdims=True))

        v = lax.fori_loop(0, n_ch, top_body, jnp.full((1, tq), NEG_INF, F32))
        c_v = count(lambda s: s >= v)
        reached = c_v >= k_row
        live = done == 0.0
        fin = jnp.logical_and(live, reached)
        cut = jnp.logical_and(live, jnp.logical_not(reached))
        return (it + 1, jnp.where(fin, v, lo), jnp.where(cut, v, hi), jnp.where(fin, c_v, cnt_lo),
                jnp.where(cut, c_v, c_hi), jnp.where(fin, 1.0, done))

    _, lo, hi, cnt_lo, c_hi, _ = lax.while_loop(
        peel_cond, peel_body,
        (jnp.int32(0), lo, hi, cnt_lo, c_hi, jnp.where(cnt_lo == k_row, 1.0, 0.0)))

    def mask_chunk(c, _):
        am_ref[rows(c), :] = jnp.where(sc_ref[rows(c), :] >= lo, 0.0, NEG_INF)
        return 0

    lax.fori_loop(0, n_ch, mask_chunk, 0)

    @pl.when(any_lane(cnt_lo != k_row))
    def _ties():
        need = k_row - c_hi
        lower = (lax.broadcasted_iota(jnp.int32, (kc, kc), 1)
                 < lax.broadcasted_iota(jnp.int32, (kc, kc), 0)).astype(BF16)

        def tie_chunk(c, before):
            s = sc_ref[rows(c), :]
            above = s >= hi
            tie = jnp.logical_and(s >= lo, jnp.logical_not(above))
            tie_f = tie.astype(F32)
            rank = jnp.dot(lower, tie_f.astype(BF16), preferred_element_type=F32) + before
            sel = jnp.logical_or(above, jnp.logical_and(tie, rank < need))
            am_ref[rows(c), :] = jnp.where(sel, 0.0, NEG_INF)
            return before + jnp.sum(tie_f, axis=0, keepdims=True)

        lax.fori_loop(0, n_ch, tie_chunk, jnp.zeros((1, tq), F32))

    for h in range(nh):
        qh = qa_ref[:, h * HEAD_DIM_A:(h + 1) * HEAD_DIM_A]
        qlt = lax.dot_general(wuk_ref[h], qh, (((1,), (1,)), ((), ())), preferred_element_type=F32)
        ql_ref[:, lanes(h)] = (qlt * (HEAD_DIM_A ** -0.5 * LOG2E)).astype(BF16)

    m_ref[...] = jnp.full(m_ref.shape, NEG_INF, F32)

    def pass_a(c, table):
        raw = jnp.dot(ckv_ref[rows(c), :], ql_ref[...], preferred_element_type=F32)
        am = am_ref[rows(c), :]
        for h in range(nh):
            x = raw[:, lanes(h)] + am
            if table is not None:
                x = x + tz_ref[table, h]
            x_ref[rows(c), lanes(h)] = x
            m_ref[:, lanes(h)] = jnp.maximum(m_ref[:, lanes(h)], jnp.max(x, axis=0, keepdims=True))

    def far_chunk(c, _):
        pass_a(c, None)
        return 0

    lax.fori_loop(0, qb - 1, far_chunk, 0)

    @pl.when(qb >= 1)
    def _prev():
        pass_a(qb - 1, 1)

    pass_a(qb, 0)

    acc_ref[...] = jnp.zeros(acc_ref.shape, F32)

    def pass_b(c, l):
        p = jnp.exp2(x_ref[rows(c), :] - m_ref[...])
        acc_ref[...] += jnp.dot(ckvt_ref[:, rows(c)], p.astype(BF16), preferred_element_type=F32)
        return l + jnp.sum(p, axis=0, keepdims=True)

    l = lax.fori_loop(0, n_ch, pass_b, jnp.zeros((1, nh * tq), F32))
    o_lat = (acc_ref[...] * (1.0 / l)).astype(BF16)
    outs = [jnp.dot(wuvt_ref[h], o_lat[:, lanes(h)], preferred_element_type=F32) for h in range(nh)]
    oa_ref[...] = jnp.concatenate(outs, axis=0).T.astype(BF16)


def _dsa(qi, sm, qa, ckv, ckv_t, wuk_t, wuv_t, tz, batch, seq):
    tq = Q_TILE
    nq = seq // tq
    topk = min(TOPK_MAX, seq // 4)
    T = batch * seq
    kern = functools.partial(_dsa_kernel, topk=topk)
    blk_q = lambda n: pl.BlockSpec((tq, n), lambda b, q: (b * nq + q, 0))
    blk_s = lambda n: pl.BlockSpec((seq, n), lambda b, q: (b, 0))
    in_specs = [
        blk_q(WIDTH_IDX), blk_q(LANES), blk_s(LANES), blk_q(WIDTH_A), blk_s(KV_RANK),
        pl.BlockSpec((None, KV_RANK, seq), lambda b, q: (b, 0, 0)),
        _full(wuk_t.shape), _full(wuv_t.shape), _full(tz.shape),
    ]
    scratch = [
        pltpu.VMEM((N_HEADS_IDX * tq, HEAD_DIM_IDX), BF16),
        pltpu.VMEM((seq, tq), F32),
        pltpu.VMEM((seq, tq), F32),
        pltpu.VMEM((KV_RANK, N_HEADS_A * tq), BF16),
        pltpu.VMEM((seq, N_HEADS_A * tq), F32),
        pltpu.VMEM((KV_RANK, N_HEADS_A * tq), F32),
        pltpu.VMEM((1, N_HEADS_A * tq), F32),
    ]
    return pl.pallas_call(
        kern, grid=(batch, nq), in_specs=in_specs,
        out_specs=pl.BlockSpec((tq, WIDTH_A), lambda b, q: (b * nq + q, 0)),
        out_shape=jax.ShapeDtypeStruct((T, WIDTH_A), BF16),
        scratch_shapes=scratch, compiler_params=_cparams(2), name="dsa")(
            qi, sm, sm, qa, ckv, ckv_t, wuk_t, wuv_t, tz)


def _t5_bucket(dist):
    max_exact = N_BUCKETS // 2
    d = jnp.maximum(dist, 0)
    ratio = jnp.log(jnp.maximum(d, 1).astype(F32) / max_exact) / math.log(MAX_DISTANCE / max_exact)
    large = jnp.minimum(max_exact + (ratio * (N_BUCKETS - max_exact)).astype(jnp.int32), N_BUCKETS - 1)
    return jnp.where(d < max_exact, d, large)


def _bias_tables(rel_bias):
    tq = Q_TILE
    span = 2 * tq
    assert int(_np_bucket(tq)) == N_BUCKETS - 1
    far = rel_bias[N_BUCKETS - 1]
    by_dist = (rel_bias[_t5_bucket(jnp.arange(span))] - far).astype(F32) * LOG2E
    diag = jnp.concatenate([by_dist[:tq], jnp.zeros_like(by_dist[:tq])])
    prev = jnp.concatenate([by_dist[tq:], by_dist[:tq]])

    def toeplitz(f):
        m = jnp.tile(f, (tq, 1))[:tq * (span - 1)].reshape(tq, span - 1, N_HEADS_A)
        return m[:, :tq]

    tz = jnp.stack([toeplitz(diag), toeplitz(prev)])
    return jnp.moveaxis(tz, -1, 1)


def _np_bucket(d):
    max_exact = N_BUCKETS // 2
    ratio = np.log(np.float32(max(d, 1)) / np.float32(max_exact)) / math.log(MAX_DISTANCE / max_exact)
    return min(max_exact + int(ratio * (N_BUCKETS - max_exact)), N_BUCKETS - 1) if d >= max_exact else d


def _mlstm_kernel(qk_ref, halo_ref, v_ref, sm_ref, o_ref, cw_ref, cb_ref, g_ref,
                  out_ref, c_ref, m_ref):
    L = M_CHUNK
    dm = HEAD_DIM_M
    c_idx = pl.program_id(1)

    @pl.when(c_idx == 0)
    def _init():
        c_ref[...] = jnp.zeros_like(c_ref)
        m_ref[...] = jnp.zeros_like(m_ref)

    halo = jnp.where(c_idx > 0, halo_ref[...].astype(F32), 0.0)
    ext = jnp.concatenate([halo, qk_ref[...].astype(F32)], axis=0)
    acc = jnp.zeros((L, 2 * WIDTH_M), F32) + cb_ref[...]
    for w in range(CONV_WIDTH):
        off = SUBLANES - (CONV_WIDTH - 1) + w
        acc = acc + ext[off:off + L, :] * cw_ref[w:w + 1, :]
    qk = acc * jax.nn.sigmoid(acc)

    sm = sm_ref[...]
    sm_t = sm.T
    r_i = lax.broadcasted_iota(jnp.int32, (L, L), 0)
    c_i = lax.broadcasted_iota(jnp.int32, (L, L), 1)
    tril = (c_i <= r_i).astype(F32)
    triu = (r_i <= c_i).astype(F32)
    bcum_c = jnp.dot(tril, jax.nn.log_sigmoid(sm), preferred_element_type=F32,
                     precision=lax.Precision.HIGHEST)
    bcum_r = jnp.dot(jax.nn.log_sigmoid(sm_t), triu, preferred_element_type=F32,
                     precision=lax.Precision.HIGHEST)
    causal = c_i <= r_i
    ones_col = (lax.broadcasted_iota(jnp.int32, (L, dm), 1) == 0).astype(BF16)

    outs = []
    for h in range(N_HEADS_M):
        q = qk[:, h * dm:(h + 1) * dm].astype(BF16)
        k = (qk[:, WIDTH_M + h * dm:WIDTH_M + (h + 1) * dm] * (dm ** -0.5))
        v_aug = jnp.concatenate([v_ref[:, h * dm:(h + 1) * dm], ones_col], axis=-1)
        b_col = bcum_c[:, SM_F + h:SM_F + h + 1]
        g_col = sm[:, SM_I + h:SM_I + h + 1] - b_col
        g_row = sm_t[SM_I + h:SM_I + h + 1, :] - bcum_r[SM_F + h:SM_F + h + 1, :]
        b_last = b_col[L - 1:L, :]
        m_prev = m_ref[h]
        c_prev = c_ref[h]

        log_d = jnp.where(causal, b_col + g_row, NEG_INF)
        m_j = jnp.maximum(b_col + m_prev, jnp.max(log_d, axis=-1, keepdims=True))
        w_inter = jnp.exp(b_col + m_prev - m_j)
        qkt = lax.dot_general(q, k.astype(BF16), (((1,), (1,)), ((), ())), preferred_element_type=F32)
        s = qkt * jnp.exp(log_d - m_j)
        o_aug = jnp.dot(s.astype(BF16), v_aug, preferred_element_type=F32) + \
            w_inter * jnp.dot(q, c_prev.astype(BF16), preferred_element_type=F32)
        num = o_aug[:, :dm]
        den = o_aug[:, dm:dm + 1]
        hh = num / jnp.maximum(jnp.abs(den), jnp.exp(-m_j))

        lwe = b_last + g_col
        m_loc = jnp.max(lwe, axis=0, keepdims=True)
        kw = (k * jnp.exp(lwe - m_loc)).astype(BF16)
        c_loc = lax.dot_general(kw, v_aug, (((0,), (0,)), ((), ())), preferred_element_type=F32)
        m_new = jnp.maximum(b_last + m_prev, m_loc)
        c_ref[h] = jnp.exp(b_last + m_prev - m_new) * c_prev + jnp.exp(m_loc - m_new) * c_loc
        m_ref[h] = m_new

        hn = hh * lax.rsqrt(jnp.mean(hh * hh, axis=-1, keepdims=True) + LN_EPS) * g_ref[:, h * dm:(h + 1) * dm]
        og = o_ref[:, h * dm:(h + 1) * dm].astype(F32)
        outs.append(jax.nn.sigmoid(og) * hn)
    out_ref[...] = jnp.concatenate(outs, axis=-1).astype(BF16)


def _mlstm(qk, v, sm, o, conv_w, conv_b, mh_g, batch, seq):
    L = M_CHUNK
    nc = seq // L
    T = batch * seq
    hb = L // SUBLANES
    blk = lambda n: pl.BlockSpec((L, n), lambda b, c: (b * nc + c, 0))
    in_specs = [
        blk(2 * WIDTH_M),
        pl.BlockSpec((SUBLANES, 2 * WIDTH_M), lambda b, c: (jnp.maximum((b * nc + c) * hb - 1, 0), 0)),
        blk(WIDTH_M), blk(LANES), blk(WIDTH_M),
        _full(conv_w.shape), _full(conv_b.shape), _full(mh_g.shape),
    ]
    scratch = [pltpu.VMEM((N_HEADS_M, HEAD_DIM_M, 2 * HEAD_DIM_M), F32),
               pltpu.VMEM((N_HEADS_M, 1, 1), F32)]
    return pl.pallas_call(
        _mlstm_kernel, grid=(batch, nc), in_specs=in_specs, out_specs=blk(WIDTH_M),
        out_shape=jax.ShapeDtypeStruct((T, WIDTH_M), BF16), scratch_shapes=scratch,
        compiler_params=_cparams(2), name="mlstm")(qk, qk, v, sm, o, conv_w, conv_b, mh_g)


def _layer_norm(y, g, b):
    mu = jnp.mean(y, axis=-1, keepdims=True)
    var = jnp.mean(jnp.square(y - mu), axis=-1, keepdims=True)
    return (y - mu) * lax.rsqrt(var + LN_EPS) * g + b


def _router_logits(x1, wr, br):
    x1_hi = x1.astype(BF16)
    x1_lo = (x1 - x1_hi.astype(F32)).astype(BF16)
    return (jnp.dot(x1_hi, wr[0], preferred_element_type=F32)
            + jnp.dot(x1_lo, wr[0], preferred_element_type=F32)
            + jnp.dot(x1_hi, wr[1], preferred_element_type=F32)) + br[...]


def _merge_kernel(x_ref, oa_ref, hm_ref, ga_ref, gm_ref, wua, wum, wout, l1g, l1b, wr, br,
                  x1_ref, route_ref, cnt_ref):
    pa = jnp.dot(oa_ref[...], wua[...], preferred_element_type=F32)
    pm = jnp.dot(hm_ref[...], wum[...], preferred_element_type=F32)
    y = jax.nn.sigmoid(ga_ref[...].astype(F32)) * pa + jax.nn.sigmoid(gm_ref[...].astype(F32)) * pm
    mix = jnp.dot(y.astype(BF16), wout[...], preferred_element_type=F32)
    x1 = _layer_norm(ALPHA * x_ref[...] + mix, l1g[...], l1b[...])
    x1_ref[...] = x1

    logits = _router_logits(x1, wr, br)
    tm = logits.shape[0]
    lane = lax.broadcasted_iota(jnp.int32, logits.shape, 1)
    big = jnp.int32(LANES)
    is_grp = jnp.logical_and(lane >= N_EXPERTS, lane < N_EXPERTS + N_GROUPS)
    gl = jnp.where(is_grp, logits, NEG_INF)
    ge = jnp.exp(gl - jnp.max(gl, axis=-1, keepdims=True))
    gp = ge / jnp.sum(ge, axis=-1, keepdims=True)
    g_w = jnp.max(gp, axis=-1, keepdims=True)
    g_lane = jnp.min(jnp.where(jnp.logical_and(is_grp, gp == g_w), lane, big), axis=-1, keepdims=True)
    g_idx = g_lane - N_EXPERTS
    in_grp = jnp.logical_and(lane < N_EXPERTS, jnp.right_shift(lane, EPG_SHIFT) == g_idx)
    el = jnp.where(in_grp, logits, NEG_INF)
    m1 = jnp.max(el, axis=-1, keepdims=True)
    i1 = jnp.min(jnp.where(el == m1, lane, big), axis=-1, keepdims=True)
    el2 = jnp.where(lane == i1, NEG_INF, el)
    m2 = jnp.max(el2, axis=-1, keepdims=True)
    i2 = jnp.min(jnp.where(el2 == m2, lane, big), axis=-1, keepdims=True)

    a = jnp.bitwise_and(jnp.minimum(i1, i2), EXPERTS_PER_GROUP - 1)
    b = jnp.bitwise_and(jnp.maximum(i1, i2), EXPERTS_PER_GROUP - 1)
    pair = jnp.right_shift(a * (2 * EXPERTS_PER_GROUP - 1 - a), 1) + (b - a - 1)
    cls = g_idx * PAIRS_PER_GROUP + pair
    onehot = (lane == cls).astype(F32)

    @pl.when(pl.program_id(0) == 0)
    def _init():
        cnt_ref[...] = jnp.zeros_like(cnt_ref)

    before = (lax.broadcasted_iota(jnp.int32, (tm, tm), 1)
              < lax.broadcasted_iota(jnp.int32, (tm, tm), 0)).astype(BF16)
    prior = jnp.dot(before, onehot.astype(BF16), preferred_element_type=F32) + cnt_ref[...]
    rank = jnp.sum(prior * onehot, axis=-1, keepdims=True)
    cnt_ref[...] += jnp.sum(onehot, axis=0, keepdims=True)
    route_ref[...] = jnp.where(lane == 0, cls.astype(F32), jnp.where(lane == 1, rank, 0.0))


def _merge(x2, oa, hm, ga, gm, wua, wum, wout, l1g, l1b, wr, br, tm):
    T = x2.shape[0]
    blk = lambda n: pl.BlockSpec((tm, n), lambda i: (i, 0))
    in_specs = [blk(D_MODEL), blk(WIDTH_A), blk(WIDTH_M), blk(D_MODEL), blk(D_MODEL),
                _full(wua.shape), _full(wum.shape), _full(wout.shape), _full(l1g.shape), _full(l1b.shape),
                _full(wr.shape), _full(br.shape)]
    out_specs = [blk(D_MODEL), blk(LANES), _full((1, LANES))]
    out_shape = [jax.ShapeDtypeStruct((T, D_MODEL), F32), jax.ShapeDtypeStruct((T, LANES), F32),
                 jax.ShapeDtypeStruct((1, LANES), F32)]
    return pl.pallas_call(
        _merge_kernel, grid=(T // tm,), in_specs=in_specs, out_specs=out_specs, out_shape=out_shape,
        compiler_params=_cparams(1), name="merge")(x2, oa, hm, ga, gm, wua, wum, wout, l1g, l1b, wr, br)


def _moe_kernel(ta_ref, tb_ref, nv_ref, tok_ref, tokn_ref, x1_hbm, wga, wua, wda, wgb, wub, wdb, wr, br, l2g, l2b,
                out_hbm, xbuf, obuf, gsem, ssem):
    i = pl.program_id(0)
    n_tiles = pl.num_programs(0)
    slot = lax.rem(i, 2)
    tm = MOE_TILE

    def gather(tok, n, s):
        def issue(j, _):
            pltpu.make_async_copy(x1_hbm.at[pl.ds(tok[0, j], 1)], xbuf.at[s, pl.ds(j, 1)], gsem.at[s]).start()
            return 0
        lax.fori_loop(0, n, issue, 0)

    def wait_n(n, copy):
        bulk = pl.multiple_of(jnp.bitwise_and(n, -SUBLANES), SUBLANES)

        @pl.when(bulk > 0)
        def _():
            copy(pl.ds(0, bulk)).wait()

        for r in range(SUBLANES - 1):
            @pl.when(bulk + r < n)
            def _():
                copy(pl.ds(0, 1)).wait()

    def wait_gather(n, s):
        wait_n(n, lambda rows: pltpu.make_async_copy(x1_hbm.at[rows], xbuf.at[s, rows], gsem.at[s]))

    def wait_scatter(n, s):
        wait_n(n, lambda rows: pltpu.make_async_copy(obuf.at[s, rows], out_hbm.at[rows], ssem.at[s]))

    @pl.when(i == 0)
    def _first():
        xbuf[...] = jnp.zeros_like(xbuf)
        gather(tok_ref, nv_ref[0], 0)

    @pl.when(i + 1 < n_tiles)
    def _ahead():
        gather(tokn_ref, nv_ref[jnp.minimum(i + 1, n_tiles - 1)], 1 - slot)

    n = nv_ref[i]
    wait_gather(n, slot)

    @pl.when(i >= 2)
    def _drain():
        wait_scatter(nv_ref[jnp.maximum(i - 2, 0)], slot)

    @pl.when(n > 0)
    def _compute():
        x = xbuf[slot]
        xb = x.astype(BF16)
        ea, eb = ta_ref[i], tb_ref[i]
        logits = _router_logits(x, wr, br)
        lane = lax.broadcasted_iota(jnp.int32, logits.shape, 1)

        def pick(idx):
            return jnp.sum(jnp.where(lane == idx, logits, 0.0), axis=-1, keepdims=True)

        la, lb, lg = pick(ea), pick(eb), pick(N_EXPERTS + jnp.right_shift(ea, EPG_SHIFT))
        is_grp = jnp.logical_and(lane >= N_EXPERTS, lane < N_EXPERTS + N_GROUPS)
        g_w = 1.0 / jnp.sum(jnp.where(is_grp, jnp.exp(logits - lg), 0.0), axis=-1, keepdims=True)
        w_a = g_w / (1.0 + jnp.exp(lb - la))
        w_b = g_w / (1.0 + jnp.exp(la - lb))

        def expert(wg, wu, wd):
            g = jnp.dot(xb, wg[...], preferred_element_type=F32)
            u = jnp.dot(xb, wu[...], preferred_element_type=F32)
            hdn = (g * jax.nn.sigmoid(g) * u).astype(BF16)
            return jnp.dot(hdn, wd[...], preferred_element_type=F32)

        ffn = w_a * expert(wga, wua, wda) + w_b * expert(wgb, wub, wdb)
        obuf[slot] = _layer_norm(ALPHA * x + ffn, l2g[...], l2b[...])

        def scatter(j, _):
            pltpu.make_async_copy(obuf.at[slot, pl.ds(j, 1)], out_hbm.at[pl.ds(tok_ref[0, j], 1)],
                                  ssem.at[slot]).start()
            return 0

        lax.fori_loop(0, n, scatter, 0)

    @pl.when(i == n_tiles - 1)
    def _last():
        wait_scatter(n, slot)
        wait_scatter(nv_ref[jnp.maximum(i - 1, 0)], 1 - slot)


def _moe(x1, tile_a, tile_b, n_valid, src_tok, wg, wu, wd, wr, br, l2g, l2b):
    T = x1.shape[0]
    tm = MOE_TILE
    n_tiles = src_tok.shape[0]
    tok_blk = lambda f: pl.BlockSpec((None, 1, tm), f, memory_space=pltpu.SMEM)
    w_in = lambda which: pl.BlockSpec((None, D_MODEL, D_EXPERT), lambda i, ta, tb, nv: ((ta, tb)[which][i], 0, 0))
    w_out = lambda which: pl.BlockSpec((None, D_EXPERT, D_MODEL), lambda i, ta, tb, nv: ((ta, tb)[which][i], 0, 0))
    const = lambda shape: pl.BlockSpec(shape, lambda i, ta, tb, nv: (0,) * len(shape))
    grid_spec = pltpu.PrefetchScalarGridSpec(
        num_scalar_prefetch=3, grid=(n_tiles,),
        in_specs=[tok_blk(lambda i, ta, tb, nv: (i, 0, 0)),
                  tok_blk(lambda i, ta, tb, nv: (jnp.minimum(i + 1, n_tiles - 1), 0, 0)),
                  pl.BlockSpec(memory_space=pl.ANY),
                  w_in(0), w_in(0), w_out(0), w_in(1), w_in(1), w_out(1),
                  const(wr.shape), const(br.shape), const(l2g.shape), const(l2b.shape)],
        out_specs=pl.BlockSpec(memory_space=pl.ANY),
        scratch_shapes=[pltpu.VMEM((2, tm, D_MODEL), F32), pltpu.VMEM((2, tm, D_MODEL), F32),
                        pltpu.SemaphoreType.DMA((2,)), pltpu.SemaphoreType.DMA((2,))])
    return pl.pallas_call(
        _moe_kernel, grid_spec=grid_spec, out_shape=jax.ShapeDtypeStruct((T, D_MODEL), F32),
        compiler_params=_cparams(1), name="moe")(
            tile_a, tile_b, n_valid, src_tok, src_tok, x1, wg, wu, wd, wg, wu, wd, wr, br, l2g, l2b)


def _route_tables(route, counts, n_tokens):
    tm = MOE_TILE
    n_tiles = n_tokens // tm + N_CLASSES
    cnt = counts[0, :N_CLASSES].astype(jnp.int32)
    tiles = (cnt + tm - 1) // tm
    tile_end = jnp.cumsum(tiles)
    tile_start = tile_end - tiles
    t_idx = jnp.arange(n_tiles, dtype=jnp.int32)
    cls_of_tile = jnp.minimum(jnp.sum(t_idx[:, None] >= tile_end[None, :], axis=1), N_CLASSES - 1).astype(jnp.int32)
    n_valid = jnp.clip(cnt[cls_of_tile] - (t_idx - tile_start[cls_of_tile]) * tm, 0, tm).astype(jnp.int32)
    grp, pair = cls_of_tile // PAIRS_PER_GROUP, cls_of_tile % PAIRS_PER_GROUP
    pa = jnp.asarray(PAIR_A, jnp.int32)[pair]
    pb = jnp.asarray(PAIR_B, jnp.int32)[pair]
    tile_a = grp * EXPERTS_PER_GROUP + pa
    tile_b = grp * EXPERTS_PER_GROUP + pb
    cls = route[:, 0].astype(jnp.int32)
    rank = route[:, 1].astype(jnp.int32)
    row0 = jnp.sum(jnp.where(cls[:, None] == jnp.arange(N_CLASSES)[None, :], (tile_start * tm)[None, :], 0), axis=1)
    pos = row0 + rank
    src = jnp.zeros((n_tiles * tm,), jnp.int32).at[pos].set(
        jnp.arange(n_tokens, dtype=jnp.int32), unique_indices=True)
    return tile_a, tile_b, n_valid, src.reshape(n_tiles, 1, tm)


def _pick_tile(T, pref):
    t = pref
    while T % t:
        t //= 2
    return t


def kernel(x, w_in, conv_w, conv_b, kv_norm_g, w_uk, w_uv, rel_bias, b_i, b_f, mh_norm_g, w_up_a, w_up_m,
           w_out, ln1_g, ln1_b, w_grp, b_grp, w_rt, b_rt, w_gate, w_up, w_down, ln2_g, ln2_b):
    B, S, _ = x.shape
    T = B * S
    assert S % Q_TILE == 0 and S % M_CHUNK == 0 and w_in.shape[0] == DEPTH
    tz = _bias_tables(rel_bias)
    x2 = x.reshape(T, D_MODEL)
    for l in range(DEPTH):
        w = w_in[l]
        o = np.cumsum((WIDTH_A, KV_RANK, WIDTH_IDX, HEAD_DIM_IDX, N_HEADS_IDX, 2 * WIDTH_M, WIDTH_M,
                       N_HEADS_M, N_HEADS_M, WIDTH_M, D_MODEL, D_MODEL)).tolist()
        o = [0] + o
        seg = lambda j: w[:, o[j]:o[j + 1]]
        pad = LANES - (HEAD_DIM_IDX + N_HEADS_IDX + 2 * N_HEADS_M)
        w_small = jnp.concatenate([seg(3), seg(4), seg(7), seg(8), jnp.zeros((D_MODEL, pad), w.dtype)], axis=1)
        ws = [seg(0), seg(1), seg(2), w_small, seg(5), seg(6), seg(9), seg(10), seg(11)]
        ws = [a.astype(BF16) for a in ws]
        smb = jnp.zeros((1, LANES), F32).at[0, SM_I:SM_I + N_HEADS_M].set(b_i[l]) \
            .at[0, SM_F:SM_F + N_HEADS_M].set(b_f[l])
        qa, ckv, qi, sm, qk, v, og, ga, gm = _proj(x2, ws, kv_norm_g[l][None, :], smb, _pick_tile(T, 512))

        wuk_t = jnp.transpose(w_uk[l], (1, 0, 2)).astype(BF16)
        wuv_t = jnp.transpose(w_uv[l], (1, 2, 0)).astype(BF16)
        ckv_t = jnp.swapaxes(ckv.reshape(B, S, KV_RANK), 1, 2)
        oa = _dsa(qi, sm, qa, ckv, ckv_t, wuk_t, wuv_t, tz, B, S)

        hm = _mlstm(qk, v, sm, og, conv_w[l], conv_b[l][None, :], mh_norm_g[l].reshape(1, WIDTH_M), B, S)

        w_router = jnp.concatenate(
            [w_rt[l], w_grp[l], jnp.zeros((D_MODEL, LANES - N_EXPERTS - N_GROUPS), F32)], axis=1)
        b_router = jnp.concatenate(
            [b_rt[l], b_grp[l], jnp.zeros((LANES - N_EXPERTS - N_GROUPS,), F32)])[None, :]
        wr_hi = w_router.astype(BF16)
        wr_split = jnp.stack([wr_hi, (w_router - wr_hi.astype(F32)).astype(BF16)])
        x1, route, counts = _merge(x2, oa, hm, ga, gm, w_up_a[l].astype(BF16), w_up_m[l].astype(BF16),
                                   w_out[l].astype(BF16), ln1_g[l][None, :], ln1_b[l][None, :],
                                   wr_split, b_router, _pick_tile(T, 256))

        tile_a, tile_b, n_valid, src_tok = _route_tables(route, counts, T)
        x2 = _moe(x1, tile_a, tile_b, n_valid, src_tok, w_gate[l].astype(BF16), w_up[l].astype(BF16),
                  w_down[l].astype(BF16), wr_split, b_router, ln2_g[l][None, :], ln2_b[l][None, :])
    return x2.reshape(B, S, D_MODEL)
```

```python
import functools
import math

import jax
import jax.numpy as jnp
import numpy as np
from jax import lax
from jax.experimental import pallas as pl
from jax.experimental.pallas import tpu as pltpu

F32 = jnp.float32
BF16 = jnp.bfloat16

D_MODEL = 1024
N_HEADS_A = 8
HEAD_DIM_A = 64
WIDTH_A = N_HEADS_A * HEAD_DIM_A
KV_RANK = 256
N_HEADS_IDX = 8
HEAD_DIM_IDX = 64
WIDTH_IDX = N_HEADS_IDX * HEAD_DIM_IDX
TOPK_MAX = 256
N_BUCKETS = 32
MAX_DISTANCE = 128
N_HEADS_M = 4
HEAD_DIM_M = 128
WIDTH_M = N_HEADS_M * HEAD_DIM_M
CONV_WIDTH = 4
N_GROUPS = 4
EXPERTS_PER_GROUP = 4
N_EXPERTS = N_GROUPS * EXPERTS_PER_GROUP
D_EXPERT = 512
LN_EPS = 1e-5
DEPTH = 1
ALPHA = (2.0 * DEPTH) ** 0.25

LANES = 128
SUBLANES = 8
VMEM_LIMIT = 56 * 1024 * 1024

SM_KIDX = 0
SM_WIDX = HEAD_DIM_IDX
SM_I = SM_WIDX + N_HEADS_IDX
SM_F = SM_I + N_HEADS_M

Q_TILE = 256
K_CHUNK = Q_TILE
ONES_ROWS = 16
M_CHUNK = 128
CONV_HALO = 16
assert CONV_HALO >= CONV_WIDTH - 1
MOE_TILE = 256
ROW_UNROLL = 8
assert ROW_UNROLL & (ROW_UNROLL - 1) == 0
EPG_SHIFT = EXPERTS_PER_GROUP.bit_length() - 1
assert 1 << EPG_SHIFT == EXPERTS_PER_GROUP
PAIR_A, PAIR_B = zip(*[(a, b) for a in range(EXPERTS_PER_GROUP) for b in range(a + 1, EXPERTS_PER_GROUP)])
PAIRS_PER_GROUP = len(PAIR_A)
N_CLASSES = N_GROUPS * PAIRS_PER_GROUP
assert N_CLASSES <= LANES
BISECT_STEPS_PER_CHECK = 3
BISECT_MAX_CHECKS = 6
PEEL_BRACKET = 2.0
NEG_INF = float("-inf")
LOG2E = math.log2(math.e)


def _cparams(n_grid):
    return pltpu.CompilerParams(dimension_semantics=("arbitrary",) * n_grid,
                                vmem_limit_bytes=VMEM_LIMIT)


def _full(shape):
    nd = len(shape)
    return pl.BlockSpec(shape, lambda *_: (0,) * nd)


def _sigmoid(x):
    return 0.5 * jnp.tanh(0.5 * x) + 0.5


def _split3(x):
    hi = x.astype(BF16)
    r = x - hi.astype(F32)
    mid = r.astype(BF16)
    return hi, mid, (r - mid.astype(F32)).astype(BF16)


def _proj_kernel(x_ref, wqa, wckv, wqi, wsm, wqk, wv, wo, wga, wgm, kvg, smb,
                 qa_o, ckv_o, qi_o, sm_o, qk_o, v_o, o_o, ga_o, gm_o):
    xb = x_ref[...].astype(BF16)

    def mm(w):
        return jnp.dot(xb, w[...], preferred_element_type=F32)

    qa_o[...] = mm(wqa).astype(BF16)
    c = mm(wckv)
    c = c * lax.rsqrt(jnp.mean(c * c, axis=-1, keepdims=True) + LN_EPS) * kvg[...]
    ckv_o[...] = c.astype(BF16)
    qi_o[...] = mm(wqi).astype(BF16)
    sm_o[...] = mm(wsm) + smb[...]
    qk_o[...] = mm(wqk).astype(BF16)
    v_o[...] = mm(wv).astype(BF16)
    o_o[...] = mm(wo).astype(BF16)
    ga_o[...] = mm(wga).astype(BF16)
    gm_o[...] = mm(wgm).astype(BF16)


def _proj(x2, ws, kvg, smb, tm):
    T = x2.shape[0]
    widths = [w.shape[1] for w in ws]
    dts = [BF16, BF16, BF16, F32, BF16, BF16, BF16, BF16, BF16]
    in_specs = [pl.BlockSpec((tm, D_MODEL), lambda i: (i, 0))]
    in_specs += [_full(w.shape) for w in ws]
    in_specs += [_full(kvg.shape), _full(smb.shape)]
    out_specs = [pl.BlockSpec((tm, n), lambda i: (i, 0)) for n in widths]
    out_shape = [jax.ShapeDtypeStruct((T, n), dt) for n, dt in zip(widths, dts)]
    return pl.pallas_call(
        _proj_kernel, grid=(T // tm,), in_specs=in_specs, out_specs=out_specs, out_shape=out_shape,
        compiler_params=_cparams(1), name="proj")(x2, *ws, kvg, smb)


def _dsa_kernel(qi_ref, smq_ref, smk_ref, qa_ref, ckv_ref, ckvt_ref, wuk_ref, wuvt_ref, tz_ref,
                oa_ref, qs_ref, sc_ref, am_ref, ql_ref, x_ref, acc_ref, m_ref, *, topk):
    tq, kc = Q_TILE, K_CHUNK
    nh = N_HEADS_A
    qb = pl.program_id(1)
    n_ch = qb + 1
    seq_keys = n_ch * kc
    t0 = qb * tq

    def rows(c):
        return pl.ds(pl.multiple_of(c * kc, kc), kc)

    def lanes(h):
        return slice(h * tq, (h + 1) * tq)

    for h in range(N_HEADS_IDX):
        qs_ref[h * tq:(h + 1) * tq, :] = qi_ref[:, h * HEAD_DIM_IDX:(h + 1) * HEAD_DIM_IDX]
    w_t = smq_ref[...].T
    q_pos = lax.broadcasted_iota(jnp.int32, (1, tq), 1) + t0
    key_iota = lax.broadcasted_iota(jnp.int32, (kc, tq), 0)

    def score_chunk(c, carry):
        mx, mn = carry
        kk = smk_ref[rows(c), SM_KIDX:SM_KIDX + HEAD_DIM_IDX].astype(BF16)
        dots = lax.dot_general(kk, qs_ref[...], (((1,), (1,)), ((), ())), preferred_element_type=F32)
        sc = jnp.zeros((kc, tq), F32)
        for h in range(N_HEADS_IDX):
            sc = sc + w_t[SM_WIDX + h:SM_WIDX + h + 1, :] * jnp.maximum(dots[:, lanes(h)], 0.0)
        vis = (key_iota + c * kc) <= q_pos
        sc_ref[rows(c), :] = jnp.where(vis, sc, NEG_INF)
        mx = jnp.maximum(mx, jnp.max(jnp.where(vis, sc, NEG_INF), axis=0, keepdims=True))
        mn = jnp.minimum(mn, jnp.min(jnp.where(vis, sc, jnp.inf), axis=0, keepdims=True))
        return mx, mn

    mx, mn = lax.fori_loop(0, n_ch, score_chunk,
                           (jnp.full((1, tq), NEG_INF, F32), jnp.full((1, tq), jnp.inf, F32)))

    n_vis = (q_pos + 1).astype(F32)
    k_row = jnp.minimum(n_vis, float(topk))

    def count(pred):
        def body(c, a):
            hit = pred(sc_ref[rows(c), :]).astype(F32)
            return a + jnp.sum(hit.reshape(kc // SUBLANES, SUBLANES, tq), axis=0)
        a = lax.fori_loop(0, n_ch, body, jnp.zeros((SUBLANES, tq), F32))
        return jnp.sum(a, axis=0, keepdims=True)

    def any_lane(flag):
        return jnp.max(jnp.where(flag, 1.0, 0.0)) > 0.0

    def crowded(cnt_lo, c_hi):
        return any_lane(jnp.logical_and(cnt_lo != k_row, cnt_lo - c_hi > PEEL_BRACKET))

    def bisect_cond(carry):
        it, lo, hi, cnt_lo, c_hi = carry
        return jnp.logical_and(it < BISECT_MAX_CHECKS, crowded(cnt_lo, c_hi))

    def bisect_body(carry):
        it, lo, hi, cnt_lo, c_hi = carry
        for _ in range(BISECT_STEPS_PER_CHECK):
            mid = lo * 0.5 + hi * 0.5
            cnt = count(lambda s: s >= mid)
            ge = cnt >= k_row
            lo, cnt_lo = jnp.where(ge, mid, lo), jnp.where(ge, cnt, cnt_lo)
            hi, c_hi = jnp.where(ge, hi, mid), jnp.where(ge, c_hi, cnt)
        return it + 1, lo, hi, cnt_lo, c_hi

    hi0 = mx + jnp.maximum(jnp.abs(mx), 1e-30) * 1e-6
    _, lo, hi, cnt_lo, c_hi = lax.while_loop(
        bisect_cond, bisect_body, (jnp.int32(0), mn, hi0, n_vis, jnp.zeros((1, tq), F32)))

    def peel_cond(carry):
        it, lo, hi, cnt_lo, c_hi, done = carry
        return jnp.logical_and(it < seq_keys, any_lane(done == 0.0))

    def peel_body(carry):
        it, lo, hi, cnt_lo, c_hi, done = carry

        def top_body(c, v):
            s = sc_ref[rows(c), :]
            inside = jnp.logical_and(s >= lo, s < hi)
            return jnp.maximum(v, jnp.max(jnp.where(inside, s, NEG_INF), axis=0, keepdims=True))

        v = lax.fori_loop(0, n_ch, top_body, jnp.full((1, tq), NEG_INF, F32))
        c_v = count(lambda s: s >= v)
        reached = c_v >= k_row
        live = done == 0.0
        fin = jnp.logical_and(live, reached)
        cut = jnp.logical_and(live, jnp.logical_not(reached))
        return (it + 1, jnp.where(fin, v, lo), jnp.where(cut, v, hi), jnp.where(fin, c_v, cnt_lo),
                jnp.where(cut, c_v, c_hi), jnp.where(fin, 1.0, done))

    _, lo, hi, cnt_lo, c_hi, _ = lax.while_loop(
        peel_cond, peel_body,
        (jnp.int32(0), lo, hi, cnt_lo, c_hi, jnp.where(cnt_lo == k_row, 1.0, 0.0)))

    def mask_chunk(c, _):
        am_ref[rows(c), :] = jnp.where(sc_ref[rows(c), :] >= lo, 0.0, NEG_INF)
        return 0

    lax.fori_loop(0, n_ch, mask_chunk, 0)

    @pl.when(any_lane(cnt_lo != k_row))
    def _ties():
        need = k_row - c_hi
        lower = (lax.broadcasted_iota(jnp.int32, (kc, kc), 1)
                 < lax.broadcasted_iota(jnp.int32, (kc, kc), 0)).astype(BF16)

        def tie_chunk(c, before):
            s = sc_ref[rows(c), :]
            above = s >= hi
            tie = jnp.logical_and(s >= lo, jnp.logical_not(above))
            tie_f = tie.astype(F32)
            rank = jnp.dot(lower, tie_f.astype(BF16), preferred_element_type=F32) + before
            sel = jnp.logical_or(above, jnp.logical_and(tie, rank < need))
            am_ref[rows(c), :] = jnp.where(sel, 0.0, NEG_INF)
            return before + jnp.sum(tie_f, axis=0, keepdims=True)

        lax.fori_loop(0, n_ch, tie_chunk, jnp.zeros((1, tq), F32))

    for h in range(nh):
        qh = qa_ref[:, h * HEAD_DIM_A:(h + 1) * HEAD_DIM_A]
        qlt = lax.dot_general(wuk_ref[h], qh, (((1,), (1,)), ((), ())), preferred_element_type=F32)
        ql_ref[:, lanes(h)] = (qlt * (HEAD_DIM_A ** -0.5 * LOG2E)).astype(BF16)

    m_ref[...] = jnp.full(m_ref.shape, NEG_INF, F32)

    def pass_a(c, table):
        raw = jnp.dot(ckv_ref[rows(c), :], ql_ref[...], preferred_element_type=F32)
        am = am_ref[rows(c), :]
        for h in range(nh):
            x = raw[:, lanes(h)] + am
            if table is not None:
                x = x + tz_ref[table, h]
            x_ref[rows(c), lanes(h)] = x
            m_ref[:, lanes(h)] = jnp.maximum(m_ref[:, lanes(h)], jnp.max(x, axis=0, keepdims=True))

    def far_chunk(c, _):
        pass_a(c, None)
        return 0

    lax.fori_loop(0, qb - 1, far_chunk, 0)

    @pl.when(qb >= 1)
    def _prev():
        pass_a(qb - 1, 1)

    pass_a(qb, 0)

    acc_ref[...] = jnp.zeros(acc_ref.shape, F32)

    def pass_b(c, _):
        p = jnp.exp2((x_ref[rows(c), :] - m_ref[...]).astype(BF16))
        acc_ref[...] += jnp.dot(ckvt_ref[:, rows(c)], p, preferred_element_type=F32)
        return 0

    lax.fori_loop(0, n_ch, pass_b, 0)
    inv_l = 1.0 / acc_ref[KV_RANK:KV_RANK + 1, :]
    o_lat = (acc_ref[0:KV_RANK, :] * inv_l).astype(BF16)
    outs = [jnp.dot(wuvt_ref[h], o_lat[:, lanes(h)], preferred_element_type=F32) for h in range(nh)]
    oa_ref[...] = jnp.concatenate(outs, axis=0).T.astype(BF16)


def _dsa(qi, sm, qa, ckv, ckv_t, wuk_t, wuv_t, tz, batch, seq):
    tq = Q_TILE
    nq = seq // tq
    topk = min(TOPK_MAX, seq // 4)
    T = batch * seq
    kern = functools.partial(_dsa_kernel, topk=topk)
    blk_q = lambda n: pl.BlockSpec((tq, n), lambda b, q: (b * nq + q, 0))
    blk_s = lambda n: pl.BlockSpec((seq, n), lambda b, q: (b, 0))
    in_specs = [
        blk_q(WIDTH_IDX), blk_q(LANES), blk_s(LANES), blk_q(WIDTH_A), blk_s(KV_RANK),
        pl.BlockSpec((None, KV_RANK + ONES_ROWS, seq), lambda b, q: (b, 0, 0)),
        _full(wuk_t.shape), _full(wuv_t.shape), _full(tz.shape),
    ]
    scratch = [
        pltpu.VMEM((N_HEADS_IDX * tq, HEAD_DIM_IDX), BF16),
        pltpu.VMEM((seq, tq), F32),
        pltpu.VMEM((seq, tq), F32),
        pltpu.VMEM((KV_RANK, N_HEADS_A * tq), BF16),
        pltpu.VMEM((seq, N_HEADS_A * tq), F32),
        pltpu.VMEM((KV_RANK + ONES_ROWS, N_HEADS_A * tq), F32),
        pltpu.VMEM((1, N_HEADS_A * tq), F32),
    ]
    return pl.pallas_call(
        kern, grid=(batch, nq), in_specs=in_specs,
        out_specs=pl.BlockSpec((tq, WIDTH_A), lambda b, q: (b * nq + q, 0)),
        out_shape=jax.ShapeDtypeStruct((T, WIDTH_A), BF16),
        scratch_shapes=scratch, compiler_params=_cparams(2), name="dsa")(
            qi, sm, sm, qa, ckv, ckv_t, wuk_t, wuv_t, tz)


def _t5_bucket(dist):
    max_exact = N_BUCKETS // 2
    d = jnp.maximum(dist, 0)
    ratio = jnp.log(jnp.maximum(d, 1).astype(F32) / max_exact) / math.log(MAX_DISTANCE / max_exact)
    large = jnp.minimum(max_exact + (ratio * (N_BUCKETS - max_exact)).astype(jnp.int32), N_BUCKETS - 1)
    return jnp.where(d < max_exact, d, large)


def _bias_tables(rel_bias):
    tq = Q_TILE
    span = 2 * tq
    assert int(_np_bucket(tq)) == N_BUCKETS - 1
    far = rel_bias[N_BUCKETS - 1]
    by_dist = (rel_bias[_t5_bucket(jnp.arange(span))] - far).astype(F32) * LOG2E
    diag = jnp.concatenate([by_dist[:tq], jnp.zeros_like(by_dist[:tq])])
    prev = jnp.concatenate([by_dist[tq:], by_dist[:tq]])

    def toeplitz(f):
        m = jnp.tile(f, (tq, 1))[:tq * (span - 1)].reshape(tq, span - 1, N_HEADS_A)
        return m[:, :tq]

    tz = jnp.stack([toeplitz(diag), toeplitz(prev)])
    return jnp.moveaxis(tz, -1, 1)


def _np_bucket(d):
    max_exact = N_BUCKETS // 2
    ratio = np.log(np.float32(max(d, 1)) / np.float32(max_exact)) / math.log(MAX_DISTANCE / max_exact)
    return min(max_exact + int(ratio * (N_BUCKETS - max_exact)), N_BUCKETS - 1) if d >= max_exact else d


def _mlstm_kernel(qk_ref, halo_ref, v_ref, sm_ref, o_ref, cw_ref, cb_ref, g_ref,
                  out_ref, c_ref, m_ref):
    L = M_CHUNK
    dm = HEAD_DIM_M
    c_idx = pl.program_id(1)

    @pl.when(c_idx == 0)
    def _init():
        c_ref[...] = jnp.zeros_like(c_ref)
        m_ref[...] = jnp.zeros_like(m_ref)

    hw = halo_ref.shape[0]
    halo = jnp.where(c_idx > 0, halo_ref[...].astype(F32), 0.0)
    ext = jnp.concatenate([halo, qk_ref[...].astype(F32)], axis=0)
    acc = jnp.zeros((L, 2 * WIDTH_M), F32) + cb_ref[...]
    for w in range(CONV_WIDTH):
        off = hw - (CONV_WIDTH - 1) + w
        acc = acc + ext[off:off + L, :] * cw_ref[w:w + 1, :]
    qk = acc * _sigmoid(acc)

    sm = sm_ref[...]
    sm_t = sm.T
    r_i = lax.broadcasted_iota(jnp.int32, (L, L), 0)
    c_i = lax.broadcasted_iota(jnp.int32, (L, L), 1)
    tril = (c_i <= r_i).astype(BF16)
    bcum_c = sum(jnp.dot(tril, part, preferred_element_type=F32)
                 for part in _split3(jax.nn.log_sigmoid(sm)))
    bcum_r = bcum_c.T
    causal = c_i <= r_i
    ones_col = (lax.broadcasted_iota(jnp.int32, (L, dm), 1) == 0).astype(BF16)

    outs = []
    for h in range(N_HEADS_M):
        q = qk[:, h * dm:(h + 1) * dm].astype(BF16)
        k = (qk[:, WIDTH_M + h * dm:WIDTH_M + (h + 1) * dm] * (dm ** -0.5))
        v_aug = jnp.concatenate([v_ref[:, h * dm:(h + 1) * dm], ones_col], axis=-1)
        b_col = bcum_c[:, SM_F + h:SM_F + h + 1]
        g_col = sm[:, SM_I + h:SM_I + h + 1] - b_col
        g_row = sm_t[SM_I + h:SM_I + h + 1, :] - bcum_r[SM_F + h:SM_F + h + 1, :]
        b_last = b_col[L - 1:L, :]
        m_prev = m_ref[h]
        c_prev = c_ref[h]

        log_d = jnp.where(causal, b_col + g_row, NEG_INF)
        m_j = jnp.maximum(b_col + m_prev, jnp.max(log_d, axis=-1, keepdims=True))
        w_inter = jnp.exp(b_col + m_prev - m_j)
        qkt = lax.dot_general(q, k.astype(BF16), (((1,), (1,)), ((), ())), preferred_element_type=F32)
        s = qkt * jnp.exp(log_d - m_j)
        o_aug = jnp.dot(s.astype(BF16), v_aug, preferred_element_type=F32) + \
            w_inter * jnp.dot(q, c_prev.astype(BF16), preferred_element_type=F32)
        num = o_aug[:, :dm]
        den = o_aug[:, dm:dm + 1]
        hh = num / jnp.maximum(jnp.abs(den), jnp.exp(-m_j))

        lwe = b_last + g_col
        m_loc = jnp.max(lwe, axis=0, keepdims=True)
        kw = (k * jnp.exp(lwe - m_loc)).astype(BF16)
        c_loc = lax.dot_general(kw, v_aug, (((0,), (0,)), ((), ())), preferred_element_type=F32)
        m_new = jnp.maximum(b_last + m_prev, m_loc)
        c_ref[h] = jnp.exp(b_last + m_prev - m_new) * c_prev + jnp.exp(m_loc - m_new) * c_loc
        m_ref[h] = m_new

        hn = hh * lax.rsqrt(jnp.mean(hh * hh, axis=-1, keepdims=True) + LN_EPS) * g_ref[:, h * dm:(h + 1) * dm]
        og = o_ref[:, h * dm:(h + 1) * dm].astype(F32)
        outs.append(_sigmoid(og) * hn)
    out_ref[...] = jnp.concatenate(outs, axis=-1).astype(BF16)


def _mlstm(qk, v, sm, o, conv_w, conv_b, mh_g, batch, seq):
    L = M_CHUNK
    nc = seq // L
    T = batch * seq
    hb = L // CONV_HALO
    blk = lambda n: pl.BlockSpec((L, n), lambda b, c: (b * nc + c, 0))
    in_specs = [
        blk(2 * WIDTH_M),
        pl.BlockSpec((CONV_HALO, 2 * WIDTH_M), lambda b, c: (jnp.maximum((b * nc + c) * hb - 1, 0), 0)),
        blk(WIDTH_M), blk(LANES), blk(WIDTH_M),
        _full(conv_w.shape), _full(conv_b.shape), _full(mh_g.shape),
    ]
    scratch = [pltpu.VMEM((N_HEADS_M, HEAD_DIM_M, 2 * HEAD_DIM_M), F32),
               pltpu.VMEM((N_HEADS_M, 1, 1), F32)]
    return pl.pallas_call(
        _mlstm_kernel, grid=(batch, nc), in_specs=in_specs, out_specs=blk(WIDTH_M),
        out_shape=jax.ShapeDtypeStruct((T, WIDTH_M), BF16), scratch_shapes=scratch,
        compiler_params=_cparams(2), name="mlstm")(qk, qk, v, sm, o, conv_w, conv_b, mh_g)


def _layer_norm(y, g, b):
    mu = jnp.mean(y, axis=-1, keepdims=True)
    var = jnp.mean(jnp.square(y - mu), axis=-1, keepdims=True)
    return (y - mu) * lax.rsqrt(var + LN_EPS) * g + b


def _router_logits(x1, wr, br):
    x1_hi = x1.astype(BF16)
    x1_lo = (x1 - x1_hi.astype(F32)).astype(BF16)
    return (jnp.dot(x1_hi, wr[0], preferred_element_type=F32)
            + jnp.dot(x1_lo, wr[0], preferred_element_type=F32)
            + jnp.dot(x1_hi, wr[1], preferred_element_type=F32)) + br[...]


def _merge_kernel(x_ref, oa_ref, hm_ref, ga_ref, gm_ref, wua, wum, wout, l1g, l1b, wr, br,
                  x1_ref, route_ref, cnt_ref):
    pa = jnp.dot(oa_ref[...], wua[...], preferred_element_type=F32)
    pm = jnp.dot(hm_ref[...], wum[...], preferred_element_type=F32)
    y = _sigmoid(ga_ref[...].astype(F32)) * pa + _sigmoid(gm_ref[...].astype(F32)) * pm
    mix = jnp.dot(y.astype(BF16), wout[...], preferred_element_type=F32)
    x1 = _layer_norm(ALPHA * x_ref[...] + mix, l1g[...], l1b[...])
    x1_ref[...] = x1

    logits = _router_logits(x1, wr, br)
    tm = logits.shape[0]
    lane = lax.broadcasted_iota(jnp.int32, logits.shape, 1)
    big = jnp.int32(LANES)
    is_grp = jnp.logical_and(lane >= N_EXPERTS, lane < N_EXPERTS + N_GROUPS)
    gl = jnp.where(is_grp, logits, NEG_INF)
    ge = jnp.exp(gl - jnp.max(gl, axis=-1, keepdims=True))
    gp = ge / jnp.sum(ge, axis=-1, keepdims=True)
    g_w = jnp.max(gp, axis=-1, keepdims=True)
    g_lane = jnp.min(jnp.where(jnp.logical_and(is_grp, gp == g_w), lane, big), axis=-1, keepdims=True)
    g_idx = g_lane - N_EXPERTS
    in_grp = jnp.logical_and(lane < N_EXPERTS, jnp.right_shift(lane, EPG_SHIFT) == g_idx)
    el = jnp.where(in_grp, logits, NEG_INF)
    m1 = jnp.max(el, axis=-1, keepdims=True)
    i1 = jnp.min(jnp.where(el == m1, lane, big), axis=-1, keepdims=True)
    el2 = jnp.where(lane == i1, NEG_INF, el)
    m2 = jnp.max(el2, axis=-1, keepdims=True)
    i2 = jnp.min(jnp.where(el2 == m2, lane, big), axis=-1, keepdims=True)

    a = jnp.bitwise_and(jnp.minimum(i1, i2), EXPERTS_PER_GROUP - 1)
    b = jnp.bitwise_and(jnp.maximum(i1, i2), EXPERTS_PER_GROUP - 1)
    pair = jnp.right_shift(a * (2 * EXPERTS_PER_GROUP - 1 - a), 1) + (b - a - 1)
    cls = g_idx * PAIRS_PER_GROUP + pair
    onehot = (lane == cls).astype(F32)

    @pl.when(pl.program_id(0) == 0)
    def _init():
        cnt_ref[...] = jnp.zeros_like(cnt_ref)

    before = (lax.broadcasted_iota(jnp.int32, (tm, tm), 1)
              < lax.broadcasted_iota(jnp.int32, (tm, tm), 0)).astype(BF16)
    prior = jnp.dot(before, onehot.astype(BF16), preferred_element_type=F32) + cnt_ref[...]
    rank = jnp.sum(prior * onehot, axis=-1, keepdims=True)
    cnt_ref[...] += jnp.sum(onehot, axis=0, keepdims=True)
    route_ref[...] = jnp.where(lane == 0, cls.astype(F32), jnp.where(lane == 1, rank, 0.0))


def _merge(x2, oa, hm, ga, gm, wua, wum, wout, l1g, l1b, wr, br, tm):
    T = x2.shape[0]
    blk = lambda n: pl.BlockSpec((tm, n), lambda i: (i, 0))
    in_specs = [blk(D_MODEL), blk(WIDTH_A), blk(WIDTH_M), blk(D_MODEL), blk(D_MODEL),
                _full(wua.shape), _full(wum.shape), _full(wout.shape), _full(l1g.shape), _full(l1b.shape),
                _full(wr.shape), _full(br.shape)]
    out_specs = [blk(D_MODEL), blk(LANES), _full((1, LANES))]
    out_shape = [jax.ShapeDtypeStruct((T, D_MODEL), F32), jax.ShapeDtypeStruct((T, LANES), F32),
                 jax.ShapeDtypeStruct((1, LANES), F32)]
    return pl.pallas_call(
        _merge_kernel, grid=(T // tm,), in_specs=in_specs, out_specs=out_specs, out_shape=out_shape,
        compiler_params=_cparams(1), name="merge")(x2, oa, hm, ga, gm, wua, wum, wout, l1g, l1b, wr, br)


def _moe_kernel(ta_ref, tb_ref, nv_ref, tok_ref, tokn_ref, x1_hbm, wga, wua, wda, wgb, wub, wdb, wr, br, l2g, l2b,
                out_hbm, xbuf, obuf, gsem, ssem):
    i = pl.program_id(0)
    n_tiles = pl.num_programs(0)
    slot = lax.rem(i, 2)
    tm = MOE_TILE

    def for_rows(n, start_row):
        groups = jnp.right_shift(n, ROW_UNROLL.bit_length() - 1)

        def group(g, _):
            for r in range(ROW_UNROLL):
                start_row(g * ROW_UNROLL + r)
            return 0

        def single(j, _):
            start_row(j)
            return 0

        lax.fori_loop(0, groups, group, 0)
        lax.fori_loop(groups * ROW_UNROLL, n, single, 0)

    def gather(tok, n, s):
        for_rows(n, lambda j: pltpu.make_async_copy(
            x1_hbm.at[pl.ds(tok[0, j], 1)], xbuf.at[s, pl.ds(j, 1)], gsem.at[s]).start())

    def wait_n(n, copy):
        bulk = pl.multiple_of(jnp.bitwise_and(n, -SUBLANES), SUBLANES)

        @pl.when(bulk > 0)
        def _():
            copy(pl.ds(0, bulk)).wait()

        for r in range(SUBLANES - 1):
            @pl.when(bulk + r < n)
            def _():
                copy(pl.ds(0, 1)).wait()

    def wait_gather(n, s):
        wait_n(n, lambda rows: pltpu.make_async_copy(x1_hbm.at[rows], xbuf.at[s, rows], gsem.at[s]))

    def wait_scatter(n, s):
        wait_n(n, lambda rows: pltpu.make_async_copy(obuf.at[s, rows], out_hbm.at[rows], ssem.at[s]))

    @pl.when(i == 0)
    def _first():
        xbuf[...] = jnp.zeros_like(xbuf)
        gather(tok_ref, nv_ref[0], 0)

    @pl.when(i + 1 < n_tiles)
    def _ahead():
        gather(tokn_ref, nv_ref[jnp.minimum(i + 1, n_tiles - 1)], 1 - slot)

    n = nv_ref[i]
    wait_gather(n, slot)

    @pl.when(i >= 2)
    def _drain():
        wait_scatter(nv_ref[jnp.maximum(i - 2, 0)], slot)

    @pl.when(n > 0)
    def _compute():
        x = xbuf[slot]
        xb = x.astype(BF16)
        ea, eb = ta_ref[i], tb_ref[i]
        logits = _router_logits(x, wr, br)
        lane = lax.broadcasted_iota(jnp.int32, logits.shape, 1)

        def pick(idx):
            return jnp.sum(jnp.where(lane == idx, logits, 0.0), axis=-1, keepdims=True)

        la, lb, lg = pick(ea), pick(eb), pick(N_EXPERTS + jnp.right_shift(ea, EPG_SHIFT))
        is_grp = jnp.logical_and(lane >= N_EXPERTS, lane < N_EXPERTS + N_GROUPS)
        g_w = 1.0 / jnp.sum(jnp.where(is_grp, jnp.exp(logits - lg), 0.0), axis=-1, keepdims=True)
        w_a = g_w / (1.0 + jnp.exp(lb - la))
        w_b = g_w / (1.0 + jnp.exp(la - lb))

        def expert(wg, wu, wd):
            g = jnp.dot(xb, wg[...], preferred_element_type=F32)
            u = jnp.dot(xb, wu[...], preferred_element_type=F32)
            hdn = (g * _sigmoid(g) * u).astype(BF16)
            return jnp.dot(hdn, wd[...], preferred_element_type=F32)

        ffn = w_a * expert(wga, wua, wda) + w_b * expert(wgb, wub, wdb)
        obuf[slot] = _layer_norm(ALPHA * x + ffn, l2g[...], l2b[...])

        for_rows(n, lambda j: pltpu.make_async_copy(
            obuf.at[slot, pl.ds(j, 1)], out_hbm.at[pl.ds(tok_ref[0, j], 1)], ssem.at[slot]).start())

    @pl.when(i == n_tiles - 1)
    def _last():
        wait_scatter(n, slot)
        wait_scatter(nv_ref[jnp.maximum(i - 1, 0)], 1 - slot)


def _moe(x1, tile_a, tile_b, n_valid, src_tok, wg, wu, wd, wr, br, l2g, l2b):
    T = x1.shape[0]
    tm = MOE_TILE
    n_tiles = src_tok.shape[0]
    tok_blk = lambda f: pl.BlockSpec((None, 1, tm), f, memory_space=pltpu.SMEM)
    w_in = lambda which: pl.BlockSpec((None, D_MODEL, D_EXPERT), lambda i, ta, tb, nv: ((ta, tb)[which][i], 0, 0))
    w_out = lambda which: pl.BlockSpec((None, D_EXPERT, D_MODEL), lambda i, ta, tb, nv: ((ta, tb)[which][i], 0, 0))
    const = lambda shape: pl.BlockSpec(shape, lambda i, ta, tb, nv: (0,) * len(shape))
    grid_spec = pltpu.PrefetchScalarGridSpec(
        num_scalar_prefetch=3, grid=(n_tiles,),
        in_specs=[tok_blk(lambda i, ta, tb, nv: (i, 0, 0)),
                  tok_blk(lambda i, ta, tb, nv: (jnp.minimum(i + 1, n_tiles - 1), 0, 0)),
                  pl.BlockSpec(memory_space=pl.ANY),
                  w_in(0), w_in(0), w_out(0), w_in(1), w_in(1), w_out(1),
                  const(wr.shape), const(br.shape), const(l2g.shape), const(l2b.shape)],
        out_specs=pl.BlockSpec(memory_space=pl.ANY),
        scratch_shapes=[pltpu.VMEM((2, tm, D_MODEL), F32), pltpu.VMEM((2, tm, D_MODEL), F32),
                        pltpu.SemaphoreType.DMA((2,)), pltpu.SemaphoreType.DMA((2,))])
    return pl.pallas_call(
        _moe_kernel, grid_spec=grid_spec, out_shape=jax.ShapeDtypeStruct((T, D_MODEL), F32),
        compiler_params=_cparams(1), name="moe")(
            tile_a, tile_b, n_valid, src_tok, src_tok, x1, wg, wu, wd, wg, wu, wd, wr, br, l2g, l2b)


def _route_tables(route, counts, n_tokens):
    tm = MOE_TILE
    n_tiles = n_tokens // tm + N_CLASSES
    cnt = counts[0, :N_CLASSES].astype(jnp.int32)
    tiles = (cnt + tm - 1) // tm
    tile_end = jnp.cumsum(tiles)
    tile_start = tile_end - tiles
    t_idx = jnp.arange(n_tiles, dtype=jnp.int32)
    cls_of_tile = jnp.minimum(jnp.sum(t_idx[:, None] >= tile_end[None, :], axis=1), N_CLASSES - 1).astype(jnp.int32)
    n_valid = jnp.clip(cnt[cls_of_tile] - (t_idx - tile_start[cls_of_tile]) * tm, 0, tm).astype(jnp.int32)
    grp, pair = cls_of_tile // PAIRS_PER_GROUP, cls_of_tile % PAIRS_PER_GROUP
    pa = jnp.asarray(PAIR_A, jnp.int32)[pair]
    pb = jnp.asarray(PAIR_B, jnp.int32)[pair]
    tile_a = grp * EXPERTS_PER_GROUP + pa
    tile_b = grp * EXPERTS_PER_GROUP + pb
    cls = route[:, 0].astype(jnp.int32)
    rank = route[:, 1].astype(jnp.int32)
    row0 = jnp.sum(jnp.where(cls[:, None] == jnp.arange(N_CLASSES)[None, :], (tile_start * tm)[None, :], 0), axis=1)
    pos = row0 + rank
    src = jnp.zeros((n_tiles * tm,), jnp.int32).at[pos].set(
        jnp.arange(n_tokens, dtype=jnp.int32), unique_indices=True)
    return tile_a, tile_b, n_valid, src.reshape(n_tiles, 1, tm)


def _pick_tile(T, pref):
    t = pref
    while T % t:
        t //= 2
    return t


def kernel(x, w_in, conv_w, conv_b, kv_norm_g, w_uk, w_uv, rel_bias, b_i, b_f, mh_norm_g, w_up_a, w_up_m,
           w_out, ln1_g, ln1_b, w_grp, b_grp, w_rt, b_rt, w_gate, w_up, w_down, ln2_g, ln2_b):
    B, S, _ = x.shape
    T = B * S
    assert S % Q_TILE == 0 and S % M_CHUNK == 0 and w_in.shape[0] == DEPTH
    tz = _bias_tables(rel_bias)
    x2 = x.reshape(T, D_MODEL)
    for l in range(DEPTH):
        w = w_in[l]
        o = np.cumsum((WIDTH_A, KV_RANK, WIDTH_IDX, HEAD_DIM_IDX, N_HEADS_IDX, 2 * WIDTH_M, WIDTH_M,
                       N_HEADS_M, N_HEADS_M, WIDTH_M, D_MODEL, D_MODEL)).tolist()
        o = [0] + o
        seg = lambda j: w[:, o[j]:o[j + 1]]
        pad = LANES - (HEAD_DIM_IDX + N_HEADS_IDX + 2 * N_HEADS_M)
        w_small = jnp.concatenate([seg(3), seg(4), seg(7), seg(8), jnp.zeros((D_MODEL, pad), w.dtype)], axis=1)
        ws = [seg(0), seg(1), seg(2), w_small, seg(5), seg(6), seg(9), seg(10), seg(11)]
        ws = [a.astype(BF16) for a in ws]
        smb = jnp.zeros((1, LANES), F32).at[0, SM_I:SM_I + N_HEADS_M].set(b_i[l]) \
            .at[0, SM_F:SM_F + N_HEADS_M].set(b_f[l])
        qa, ckv, qi, sm, qk, v, og, ga, gm = _proj(x2, ws, kv_norm_g[l][None, :], smb, _pick_tile(T, 512))

        wuk_t = jnp.transpose(w_uk[l], (1, 0, 2)).astype(BF16)
        wuv_t = jnp.transpose(w_uv[l], (1, 2, 0)).astype(BF16)
        ckv_t = jnp.concatenate([jnp.swapaxes(ckv.reshape(B, S, KV_RANK), 1, 2),
                                 jnp.ones((B, ONES_ROWS, S), BF16)], axis=1)
        oa = _dsa(qi, sm, qa, ckv, ckv_t, wuk_t, wuv_t, tz, B, S)

        hm = _mlstm(qk, v, sm, og, conv_w[l], conv_b[l][None, :], mh_norm_g[l].reshape(1, WIDTH_M), B, S)

        w_router = jnp.concatenate(
            [w_rt[l], w_grp[l], jnp.zeros((D_MODEL, LANES - N_EXPERTS - N_GROUPS), F32)], axis=1)
        b_router = jnp.concatenate(
            [b_rt[l], b_grp[l], jnp.zeros((LANES - N_EXPERTS - N_GROUPS,), F32)])[None, :]
        wr_hi = w_router.astype(BF16)
        wr_split = jnp.stack([wr_hi, (w_router - wr_hi.astype(F32)).astype(BF16)])
        x1, route, counts = _merge(x2, oa, hm, ga, gm, w_up_a[l].astype(BF16), w_up_m[l].astype(BF16),
                                   w_out[l].astype(BF16), ln1_g[l][None, :], ln1_b[l][None, :],
                                   wr_split, b_router, _pick_tile(T, 256))

        tile_a, tile_b, n_valid, src_tok = _route_tables(route, counts, T)
        x2 = _moe(x1, tile_a, tile_b, n_valid, src_tok, w_gate[l].astype(BF16), w_up[l].astype(BF16),
                  w_down[l].astype(BF16), wr_split, b_router, ln2_g[l][None, :], ln2_b[l][None, :])
    return x2.reshape(B, S, D_MODEL)
```

```python
import functools
import math

import jax
import jax.numpy as jnp
import numpy as np
from jax import lax
from jax.experimental import pallas as pl
from jax.experimental.pallas import tpu as pltpu

F32 = jnp.float32
BF16 = jnp.bfloat16

D_MODEL = 1024
N_HEADS_A = 8
HEAD_DIM_A = 64
WIDTH_A = N_HEADS_A * HEAD_DIM_A
KV_RANK = 256
N_HEADS_IDX = 8
HEAD_DIM_IDX = 64
WIDTH_IDX = N_HEADS_IDX * HEAD_DIM_IDX
TOPK_MAX = 256
N_BUCKETS = 32
MAX_DISTANCE = 128
N_HEADS_M = 4
HEAD_DIM_M = 128
WIDTH_M = N_HEADS_M * HEAD_DIM_M
CONV_WIDTH = 4
N_GROUPS = 4
EXPERTS_PER_GROUP = 4
N_EXPERTS = N_GROUPS * EXPERTS_PER_GROUP
D_EXPERT = 512
LN_EPS = 1e-5
DEPTH = 1
ALPHA = (2.0 * DEPTH) ** 0.25

LANES = 128
SUBLANES = 8
VMEM_LIMIT = 56 * 1024 * 1024

SM_KIDX = 0
SM_WIDX = HEAD_DIM_IDX
SM_I = SM_WIDX + N_HEADS_IDX
SM_F = SM_I + N_HEADS_M

Q_TILE = 256
K_CHUNK = Q_TILE
ONES_ROWS = 16
M_CHUNK = 128
CONV_HALO = 16
assert CONV_HALO >= CONV_WIDTH - 1
MOE_TILE = 256
ROW_UNROLL = 8
assert ROW_UNROLL & (ROW_UNROLL - 1) == 0
EPG_SHIFT = EXPERTS_PER_GROUP.bit_length() - 1
assert 1 << EPG_SHIFT == EXPERTS_PER_GROUP
PAIR_A, PAIR_B = zip(*[(a, b) for a in range(EXPERTS_PER_GROUP) for b in range(a + 1, EXPERTS_PER_GROUP)])
PAIRS_PER_GROUP = len(PAIR_A)
N_CLASSES = N_GROUPS * PAIRS_PER_GROUP
assert N_CLASSES <= LANES
BISECT_STEPS_PER_CHECK = 3
BISECT_MAX_CHECKS = 5
PEEL_BRACKET = 2.0
NEG_INF = float("-inf")
LOG2E = math.log2(math.e)


def _cparams(n_grid):
    return pltpu.CompilerParams(dimension_semantics=("arbitrary",) * n_grid,
                                vmem_limit_bytes=VMEM_LIMIT)


def _full(shape):
    nd = len(shape)
    return pl.BlockSpec(shape, lambda *_: (0,) * nd)


def _sigmoid(x):
    return 0.5 * jnp.tanh(0.5 * x) + 0.5


def _split3(x):
    hi = x.astype(BF16)
    r = x - hi.astype(F32)
    mid = r.astype(BF16)
    return hi, mid, (r - mid.astype(F32)).astype(BF16)


def _proj_kernel(x_ref, wqa, wckv, wqi, wsm, wqk, wv, wo, wga, wgm, kvg, smb,
                 qa_o, ckv_o, qi_o, sm_o, qk_o, v_o, o_o, ga_o, gm_o):
    xb = x_ref[...].astype(BF16)

    def mm(w):
        return jnp.dot(xb, w[...], preferred_element_type=F32)

    qa_o[...] = mm(wqa).astype(BF16)
    c = mm(wckv)
    c = c * lax.rsqrt(jnp.mean(c * c, axis=-1, keepdims=True) + LN_EPS) * kvg[...]
    ckv_o[...] = c.astype(BF16)
    qi_o[...] = mm(wqi).astype(BF16)
    sm_o[...] = mm(wsm) + smb[...]
    qk_o[...] = mm(wqk).astype(BF16)
    v_o[...] = mm(wv).astype(BF16)
    o_o[...] = mm(wo).astype(BF16)
    ga_o[...] = mm(wga).astype(BF16)
    gm_o[...] = mm(wgm).astype(BF16)


def _proj(x2, ws, kvg, smb, tm):
    T = x2.shape[0]
    widths = [w.shape[1] for w in ws]
    dts = [BF16, BF16, BF16, F32, BF16, BF16, BF16, BF16, BF16]
    in_specs = [pl.BlockSpec((tm, D_MODEL), lambda i: (i, 0))]
    in_specs += [_full(w.shape) for w in ws]
    in_specs += [_full(kvg.shape), _full(smb.shape)]
    out_specs = [pl.BlockSpec((tm, n), lambda i: (i, 0)) for n in widths]
    out_shape = [jax.ShapeDtypeStruct((T, n), dt) for n, dt in zip(widths, dts)]
    return pl.pallas_call(
        _proj_kernel, grid=(T // tm,), in_specs=in_specs, out_specs=out_specs, out_shape=out_shape,
        compiler_params=_cparams(1), name="proj")(x2, *ws, kvg, smb)


def _dsa_kernel(qi_ref, smq_ref, smk_ref, qa_ref, ckv_ref, ckvt_ref, wuk_ref, wuvt_ref, tz_ref,
                oa_ref, qs_ref, sc_ref, am_ref, ql_ref, x_ref, acc_ref, m_ref, *, topk):
    tq, kc = Q_TILE, K_CHUNK
    nh = N_HEADS_A
    qb = pl.program_id(1)
    n_ch = qb + 1
    seq_keys = n_ch * kc
    t0 = qb * tq

    def rows(c):
        return pl.ds(pl.multiple_of(c * kc, kc), kc)

    def lanes(h):
        return slice(h * tq, (h + 1) * tq)

    def chunk_loop(n, body, init):
        def pair(p, carry):
            return body(2 * p + 1, body(2 * p, carry))
        carry = lax.fori_loop(0, jnp.right_shift(n, 1), pair, init)
        return lax.cond(jnp.bitwise_and(n, 1) == 1, lambda c: body(n - 1, c), lambda c: c, carry)

    for h in range(N_HEADS_IDX):
        qs_ref[h * tq:(h + 1) * tq, :] = qi_ref[:, h * HEAD_DIM_IDX:(h + 1) * HEAD_DIM_IDX]
    w_t = smq_ref[...].T
    q_pos = lax.broadcasted_iota(jnp.int32, (1, tq), 1) + t0
    key_iota = lax.broadcasted_iota(jnp.int32, (kc, tq), 0)

    def score_chunk(c, carry):
        mx, mn = carry
        kk = smk_ref[rows(c), SM_KIDX:SM_KIDX + HEAD_DIM_IDX].astype(BF16)
        dots = lax.dot_general(kk, qs_ref[...], (((1,), (1,)), ((), ())), preferred_element_type=F32)
        sc = jnp.zeros((kc, tq), F32)
        for h in range(N_HEADS_IDX):
            sc = sc + w_t[SM_WIDX + h:SM_WIDX + h + 1, :] * jnp.maximum(dots[:, lanes(h)], 0.0)
        vis = (key_iota + c * kc) <= q_pos
        sc_ref[rows(c), :] = jnp.where(vis, sc, NEG_INF)
        mx = jnp.maximum(mx, jnp.max(jnp.where(vis, sc, NEG_INF), axis=0, keepdims=True))
        mn = jnp.minimum(mn, jnp.min(jnp.where(vis, sc, jnp.inf), axis=0, keepdims=True))
        return mx, mn

    mx, mn = chunk_loop(n_ch, score_chunk,
                        (jnp.full((1, tq), NEG_INF, F32), jnp.full((1, tq), jnp.inf, F32)))

    n_vis = (q_pos + 1).astype(F32)
    k_row = jnp.minimum(n_vis, float(topk))

    def count(pred):
        def body(c, a):
            hit = pred(sc_ref[rows(c), :]).astype(F32)
            return a + jnp.sum(hit.reshape(kc // SUBLANES, SUBLANES, tq), axis=0)
        a = chunk_loop(n_ch, body, jnp.zeros((SUBLANES, tq), F32))
        return jnp.sum(a, axis=0, keepdims=True)

    def any_lane(flag):
        return jnp.max(jnp.where(flag, 1.0, 0.0)) > 0.0

    def crowded(cnt_lo, c_hi):
        return any_lane(jnp.logical_and(cnt_lo != k_row, cnt_lo - c_hi > PEEL_BRACKET))

    def bisect_cond(carry):
        it, lo, hi, cnt_lo, c_hi = carry
        return jnp.logical_and(it < BISECT_MAX_CHECKS, crowded(cnt_lo, c_hi))

    def bisect_body(carry):
        it, lo, hi, cnt_lo, c_hi = carry
        for _ in range(BISECT_STEPS_PER_CHECK):
            mid = lo * 0.5 + hi * 0.5
            cnt = count(lambda s: s >= mid)
            ge = cnt >= k_row
            lo, cnt_lo = jnp.where(ge, mid, lo), jnp.where(ge, cnt, cnt_lo)
            hi, c_hi = jnp.where(ge, hi, mid), jnp.where(ge, c_hi, cnt)
        return it + 1, lo, hi, cnt_lo, c_hi

    hi0 = mx + jnp.maximum(jnp.abs(mx), 1e-30) * 1e-6
    _, lo, hi, cnt_lo, c_hi = lax.while_loop(
        bisect_cond, bisect_body, (jnp.int32(0), mn, hi0, n_vis, jnp.zeros((1, tq), F32)))

    def peel_cond(carry):
        it, lo, hi, cnt_lo, c_hi, done = carry
        return jnp.logical_and(it < seq_keys, any_lane(done == 0.0))

    def peel_body(carry):
        it, lo, hi, cnt_lo, c_hi, done = carry

        def top_body(c, v):
            s = sc_ref[rows(c), :]
            inside = jnp.logical_and(s >= lo, s < hi)
            return jnp.maximum(v, jnp.max(jnp.where(inside, s, NEG_INF), axis=0, keepdims=True))

        v = chunk_loop(n_ch, top_body, jnp.full((1, tq), NEG_INF, F32))
        c_v = count(lambda s: s >= v)
        reached = c_v >= k_row
        live = done == 0.0
        fin = jnp.logical_and(live, reached)
        cut = jnp.logical_and(live, jnp.logical_not(reached))
        return (it + 1, jnp.where(fin, v, lo), jnp.where(cut, v, hi), jnp.where(fin, c_v, cnt_lo),
                jnp.where(cut, c_v, c_hi), jnp.where(fin, 1.0, done))

    _, lo, hi, cnt_lo, c_hi, _ = lax.while_loop(
        peel_cond, peel_body,
        (jnp.int32(0), lo, hi, cnt_lo, c_hi, jnp.where(cnt_lo == k_row, 1.0, 0.0)))

    def mask_chunk(c, _):
        am_ref[rows(c), :] = jnp.where(sc_ref[rows(c), :] >= lo, 0.0, NEG_INF)
        return 0

    chunk_loop(n_ch, mask_chunk, 0)

    @pl.when(any_lane(cnt_lo != k_row))
    def _ties():
        need = k_row - c_hi
        lower = (lax.broadcasted_iota(jnp.int32, (kc, kc), 1)
                 < lax.broadcasted_iota(jnp.int32, (kc, kc), 0)).astype(BF16)

        def tie_chunk(c, before):
            s = sc_ref[rows(c), :]
            above = s >= hi
            tie = jnp.logical_and(s >= lo, jnp.logical_not(above))
            tie_f = tie.astype(F32)
            rank = jnp.dot(lower, tie_f.astype(BF16), preferred_element_type=F32) + before
            sel = jnp.logical_or(above, jnp.logical_and(tie, rank < need))
            am_ref[rows(c), :] = jnp.where(sel, 0.0, NEG_INF)
            return before + jnp.sum(tie_f, axis=0, keepdims=True)

        chunk_loop(n_ch, tie_chunk, jnp.zeros((1, tq), F32))

    for h in range(nh):
        qh = qa_ref[:, h * HEAD_DIM_A:(h + 1) * HEAD_DIM_A]
        qlt = lax.dot_general(wuk_ref[h], qh, (((1,), (1,)), ((), ())), preferred_element_type=F32)
        ql_ref[:, lanes(h)] = (qlt * (HEAD_DIM_A ** -0.5 * LOG2E)).astype(BF16)

    m_ref[...] = jnp.full(m_ref.shape, NEG_INF, F32)

    def pass_a(c, table):
        raw = jnp.dot(ckv_ref[rows(c), :], ql_ref[...], preferred_element_type=F32)
        am = am_ref[rows(c), :]
        for h in range(nh):
            x = raw[:, lanes(h)] + am
            if table is not None:
                x = x + tz_ref[table, h]
            x_ref[rows(c), lanes(h)] = x
            m_ref[:, lanes(h)] = jnp.maximum(m_ref[:, lanes(h)], jnp.max(x, axis=0, keepdims=True))

    def far_chunk(c, _):
        pass_a(c, None)
        return 0

    chunk_loop(jnp.maximum(qb - 1, 0), far_chunk, 0)

    @pl.when(qb >= 1)
    def _prev():
        pass_a(qb - 1, 1)

    pass_a(qb, 0)

    acc_ref[...] = jnp.zeros(acc_ref.shape, F32)

    def pass_b(c, _):
        p = jnp.exp2((x_ref[rows(c), :] - m_ref[...]).astype(BF16))
        acc_ref[...] += jnp.dot(ckvt_ref[:, rows(c)], p, preferred_element_type=F32)
        return 0

    chunk_loop(n_ch, pass_b, 0)
    inv_l = 1.0 / acc_ref[KV_RANK:KV_RANK + 1, :]
    o_lat = (acc_ref[0:KV_RANK, :] * inv_l).astype(BF16)
    outs = [jnp.dot(wuvt_ref[h], o_lat[:, lanes(h)], preferred_element_type=F32) for h in range(nh)]
    oa_ref[...] = jnp.concatenate(outs, axis=0).T.astype(BF16)


def _dsa(qi, sm, qa, ckv, ckv_t, wuk_t, wuv_t, tz, batch, seq):
    tq = Q_TILE
    nq = seq // tq
    topk = min(TOPK_MAX, seq // 4)
    T = batch * seq
    kern = functools.partial(_dsa_kernel, topk=topk)
    blk_q = lambda n: pl.BlockSpec((tq, n), lambda b, q: (b * nq + q, 0))
    blk_s = lambda n: pl.BlockSpec((seq, n), lambda b, q: (b, 0))
    in_specs = [
        blk_q(WIDTH_IDX), blk_q(LANES), blk_s(LANES), blk_q(WIDTH_A), blk_s(KV_RANK),
        pl.BlockSpec((None, KV_RANK + ONES_ROWS, seq), lambda b, q: (b, 0, 0)),
        _full(wuk_t.shape), _full(wuv_t.shape), _full(tz.shape),
    ]
    scratch = [
        pltpu.VMEM((N_HEADS_IDX * tq, HEAD_DIM_IDX), BF16),
        pltpu.VMEM((seq, tq), F32),
        pltpu.VMEM((seq, tq), F32),
        pltpu.VMEM((KV_RANK, N_HEADS_A * tq), BF16),
        pltpu.VMEM((seq, N_HEADS_A * tq), F32),
        pltpu.VMEM((KV_RANK + ONES_ROWS, N_HEADS_A * tq), F32),
        pltpu.VMEM((1, N_HEADS_A * tq), F32),
    ]
    return pl.pallas_call(
        kern, grid=(batch, nq), in_specs=in_specs,
        out_specs=pl.BlockSpec((tq, WIDTH_A), lambda b, q: (b * nq + q, 0)),
        out_shape=jax.ShapeDtypeStruct((T, WIDTH_A), BF16),
        scratch_shapes=scratch, compiler_params=_cparams(2), name="dsa")(
            qi, sm, sm, qa, ckv, ckv_t, wuk_t, wuv_t, tz)


def _t5_bucket(dist):
    max_exact = N_BUCKETS // 2
    d = jnp.maximum(dist, 0)
    ratio = jnp.log(jnp.maximum(d, 1).astype(F32) / max_exact) / math.log(MAX_DISTANCE / max_exact)
    large = jnp.minimum(max_exact + (ratio * (N_BUCKETS - max_exact)).astype(jnp.int32), N_BUCKETS - 1)
    return jnp.where(d < max_exact, d, large)


def _bias_tables(rel_bias):
    tq = Q_TILE
    span = 2 * tq
    assert int(_np_bucket(tq)) == N_BUCKETS - 1
    far = rel_bias[N_BUCKETS - 1]
    by_dist = (rel_bias[_t5_bucket(jnp.arange(span))] - far).astype(F32) * LOG2E
    diag = jnp.concatenate([by_dist[:tq], jnp.zeros_like(by_dist[:tq])])
    prev = jnp.concatenate([by_dist[tq:], by_dist[:tq]])

    def toeplitz(f):
        m = jnp.tile(f, (tq, 1))[:tq * (span - 1)].reshape(tq, span - 1, N_HEADS_A)
        return m[:, :tq]

    tz = jnp.stack([toeplitz(diag), toeplitz(prev)])
    return jnp.moveaxis(tz, -1, 1)


def _np_bucket(d):
    max_exact = N_BUCKETS // 2
    ratio = np.log(np.float32(max(d, 1)) / np.float32(max_exact)) / math.log(MAX_DISTANCE / max_exact)
    return min(max_exact + int(ratio * (N_BUCKETS - max_exact)), N_BUCKETS - 1) if d >= max_exact else d


def _mlstm_kernel(qk_ref, halo_ref, v_ref, sm_ref, o_ref, cw_ref, cb_ref, g_ref,
                  out_ref, c_ref, m_ref):
    L = M_CHUNK
    dm = HEAD_DIM_M
    c_idx = pl.program_id(1)

    @pl.when(c_idx == 0)
    def _init():
        c_ref[...] = jnp.zeros_like(c_ref)
        m_ref[...] = jnp.zeros_like(m_ref)

    hw = halo_ref.shape[0]
    halo = jnp.where(c_idx > 0, halo_ref[...].astype(F32), 0.0)
    ext = jnp.concatenate([halo, qk_ref[...].astype(F32)], axis=0)
    acc = jnp.zeros((L, 2 * WIDTH_M), F32) + cb_ref[...]
    for w in range(CONV_WIDTH):
        off = hw - (CONV_WIDTH - 1) + w
        acc = acc + ext[off:off + L, :] * cw_ref[w:w + 1, :]
    qk = acc * _sigmoid(acc)

    sm = sm_ref[...]
    sm_t = sm.T
    r_i = lax.broadcasted_iota(jnp.int32, (L, L), 0)
    c_i = lax.broadcasted_iota(jnp.int32, (L, L), 1)
    tril = (c_i <= r_i).astype(BF16)
    bcum_c = sum(jnp.dot(tril, part, preferred_element_type=F32)
                 for part in _split3(jax.nn.log_sigmoid(sm)))
    bcum_r = bcum_c.T
    causal = c_i <= r_i
    ones_col = (lax.broadcasted_iota(jnp.int32, (L, dm), 1) == 0).astype(BF16)

    outs = []
    for h in range(N_HEADS_M):
        q = qk[:, h * dm:(h + 1) * dm].astype(BF16)
        k = (qk[:, WIDTH_M + h * dm:WIDTH_M + (h + 1) * dm] * (dm ** -0.5))
        v_aug = jnp.concatenate([v_ref[:, h * dm:(h + 1) * dm], ones_col], axis=-1)
        b_col = bcum_c[:, SM_F + h:SM_F + h + 1]
        g_col = sm[:, SM_I + h:SM_I + h + 1] - b_col
        g_row = sm_t[SM_I + h:SM_I + h + 1, :] - bcum_r[SM_F + h:SM_F + h + 1, :]
        b_last = b_col[L - 1:L, :]
        m_prev = m_ref[h]
        c_prev = c_ref[h]

        log_d = jnp.where(causal, b_col + g_row, NEG_INF)
        m_j = jnp.maximum(b_col + m_prev, jnp.max(log_d, axis=-1, keepdims=True))
        w_inter = jnp.exp(b_col + m_prev - m_j)
        qkt = lax.dot_general(q, k.astype(BF16), (((1,), (1,)), ((), ())), preferred_element_type=F32)
        s = qkt * jnp.exp(log_d - m_j)
        o_aug = jnp.dot(s.astype(BF16), v_aug, preferred_element_type=F32) + \
            w_inter * jnp.dot(q, c_prev.astype(BF16), preferred_element_type=F32)
        num = o_aug[:, :dm]
        den = o_aug[:, dm:dm + 1]
        hh = num / jnp.maximum(jnp.abs(den), jnp.exp(-m_j))

        lwe = b_last + g_col
        m_loc = jnp.max(lwe, axis=0, keepdims=True)
        kw = (k * jnp.exp(lwe - m_loc)).astype(BF16)
        c_loc = lax.dot_general(kw, v_aug, (((0,), (0,)), ((), ())), preferred_element_type=F32)
        m_new = jnp.maximum(b_last + m_prev, m_loc)
        c_ref[h] = jnp.exp(b_last + m_prev - m_new) * c_prev + jnp.exp(m_loc - m_new) * c_loc
        m_ref[h] = m_new

        hn = hh * lax.rsqrt(jnp.mean(hh * hh, axis=-1, keepdims=True) + LN_EPS) * g_ref[:, h * dm:(h + 1) * dm]
        og = o_ref[:, h * dm:(h + 1) * dm].astype(F32)
        outs.append(_sigmoid(og) * hn)
    out_ref[...] = jnp.concatenate(outs, axis=-1).astype(BF16)


def _mlstm(qk, v, sm, o, conv_w, conv_b, mh_g, batch, seq):
    L = M_CHUNK
    nc = seq // L
    T = batch * seq
    hb = L // CONV_HALO
    blk = lambda n: pl.BlockSpec((L, n), lambda b, c: (b * nc + c, 0))
    in_specs = [
        blk(2 * WIDTH_M),
        pl.BlockSpec((CONV_HALO, 2 * WIDTH_M), lambda b, c: (jnp.maximum((b * nc + c) * hb - 1, 0), 0)),
        blk(WIDTH_M), blk(LANES), blk(WIDTH_M),
        _full(conv_w.shape), _full(conv_b.shape), _full(mh_g.shape),
    ]
    scratch = [pltpu.VMEM((N_HEADS_M, HEAD_DIM_M, 2 * HEAD_DIM_M), F32),
               pltpu.VMEM((N_HEADS_M, 1, 1), F32)]
    return pl.pallas_call(
        _mlstm_kernel, grid=(batch, nc), in_specs=in_specs, out_specs=blk(WIDTH_M),
        out_shape=jax.ShapeDtypeStruct((T, WIDTH_M), BF16), scratch_shapes=scratch,
        compiler_params=_cparams(2), name="mlstm")(qk, qk, v, sm, o, conv_w, conv_b, mh_g)


def _layer_norm(y, g, b):
    mu = jnp.mean(y, axis=-1, keepdims=True)
    var = jnp.mean(jnp.square(y - mu), axis=-1, keepdims=True)
    return (y - mu) * lax.rsqrt(var + LN_EPS) * g + b


def _router_logits(x1, wr, br):
    x1_hi = x1.astype(BF16)
    x1_lo = (x1 - x1_hi.astype(F32)).astype(BF16)
    return (jnp.dot(x1_hi, wr[0], preferred_element_type=F32)
            + jnp.dot(x1_lo, wr[0], preferred_element_type=F32)
            + jnp.dot(x1_hi, wr[1], preferred_element_type=F32)) + br[...]


def _merge_kernel(x_ref, oa_ref, hm_ref, ga_ref, gm_ref, wua, wum, wout, l1g, l1b, wr, br,
                  x1_ref, route_ref, cnt_ref):
    pa = jnp.dot(oa_ref[...], wua[...], preferred_element_type=F32)
    pm = jnp.dot(hm_ref[...], wum[...], preferred_element_type=F32)
    y = _sigmoid(ga_ref[...]) * pa.astype(BF16) + _sigmoid(gm_ref[...]) * pm.astype(BF16)
    mix = jnp.dot(y.astype(BF16), wout[...], preferred_element_type=F32)
    x1 = _layer_norm(ALPHA * x_ref[...] + mix, l1g[...], l1b[...])
    x1_ref[...] = x1

    logits = _router_logits(x1, wr, br)
    tm = logits.shape[0]
    lane = lax.broadcasted_iota(jnp.int32, logits.shape, 1)
    big = jnp.int32(LANES)
    is_grp = jnp.logical_and(lane >= N_EXPERTS, lane < N_EXPERTS + N_GROUPS)
    gl = jnp.where(is_grp, logits, NEG_INF)
    ge = jnp.exp(gl - jnp.max(gl, axis=-1, keepdims=True))
    gp = ge / jnp.sum(ge, axis=-1, keepdims=True)
    g_w = jnp.max(gp, axis=-1, keepdims=True)
    g_lane = jnp.min(jnp.where(jnp.logical_and(is_grp, gp == g_w), lane, big), axis=-1, keepdims=True)
    g_idx = g_lane - N_EXPERTS
    in_grp = jnp.logical_and(lane < N_EXPERTS, jnp.right_shift(lane, EPG_SHIFT) == g_idx)
    el = jnp.where(in_grp, logits, NEG_INF)
    m1 = jnp.max(el, axis=-1, keepdims=True)
    i1 = jnp.min(jnp.where(el == m1, lane, big), axis=-1, keepdims=True)
    el2 = jnp.where(lane == i1, NEG_INF, el)
    m2 = jnp.max(el2, axis=-1, keepdims=True)
    i2 = jnp.min(jnp.where(el2 == m2, lane, big), axis=-1, keepdims=True)

    a = jnp.bitwise_and(jnp.minimum(i1, i2), EXPERTS_PER_GROUP - 1)
    b = jnp.bitwise_and(jnp.maximum(i1, i2), EXPERTS_PER_GROUP - 1)
    pair = jnp.right_shift(a * (2 * EXPERTS_PER_GROUP - 1 - a), 1) + (b - a - 1)
    cls = g_idx * PAIRS_PER_GROUP + pair
    onehot = (lane == cls).astype(F32)

    @pl.when(pl.program_id(0) == 0)
    def _init():
        cnt_ref[...] = jnp.zeros_like(cnt_ref)

    before = (lax.broadcasted_iota(jnp.int32, (tm, tm), 1)
              < lax.broadcasted_iota(jnp.int32, (tm, tm), 0)).astype(BF16)
    prior = jnp.dot(before, onehot.astype(BF16), preferred_element_type=F32) + cnt_ref[...]
    rank = jnp.sum(prior * onehot, axis=-1, keepdims=True)
    cnt_ref[...] += jnp.sum(onehot, axis=0, keepdims=True)
    route_ref[...] = jnp.where(lane == 0, cls.astype(F32), jnp.where(lane == 1, rank, 0.0))


def _merge(x2, oa, hm, ga, gm, wua, wum, wout, l1g, l1b, wr, br, tm):
    T = x2.shape[0]
    blk = lambda n: pl.BlockSpec((tm, n), lambda i: (i, 0))
    in_specs = [blk(D_MODEL), blk(WIDTH_A), blk(WIDTH_M), blk(D_MODEL), blk(D_MODEL),
                _full(wua.shape), _full(wum.shape), _full(wout.shape), _full(l1g.shape), _full(l1b.shape),
                _full(wr.shape), _full(br.shape)]
    out_specs = [blk(D_MODEL), blk(LANES), _full((1, LANES))]
    out_shape = [jax.ShapeDtypeStruct((T, D_MODEL), F32), jax.ShapeDtypeStruct((T, LANES), F32),
                 jax.ShapeDtypeStruct((1, LANES), F32)]
    return pl.pallas_call(
        _merge_kernel, grid=(T // tm,), in_specs=in_specs, out_specs=out_specs, out_shape=out_shape,
        compiler_params=_cparams(1), name="merge")(x2, oa, hm, ga, gm, wua, wum, wout, l1g, l1b, wr, br)


def _moe_kernel(ta_ref, tb_ref, nv_ref, tok_ref, tokn_ref, x1_hbm, wga, wua, wda, wgb, wub, wdb, wr, br, l2g, l2b,
                out_hbm, xbuf, obuf, gsem, ssem):
    i = pl.program_id(0)
    n_tiles = pl.num_programs(0)
    slot = lax.rem(i, 2)
    tm = MOE_TILE

    def for_rows(n, start_row):
        groups = jnp.right_shift(n, ROW_UNROLL.bit_length() - 1)

        def group(g, _):
            for r in range(ROW_UNROLL):
                start_row(g * ROW_UNROLL + r)
            return 0

        def single(j, _):
            start_row(j)
            return 0

        lax.fori_loop(0, groups, group, 0)
        lax.fori_loop(groups * ROW_UNROLL, n, single, 0)

    def gather(tok, n, s):
        for_rows(n, lambda j: pltpu.make_async_copy(
            x1_hbm.at[pl.ds(tok[0, j], 1)], xbuf.at[s, pl.ds(j, 1)], gsem.at[s]).start())

    def wait_n(n, copy):
        bulk = pl.multiple_of(jnp.bitwise_and(n, -SUBLANES), SUBLANES)

        @pl.when(bulk > 0)
        def _():
            copy(pl.ds(0, bulk)).wait()

        for r in range(SUBLANES - 1):
            @pl.when(bulk + r < n)
            def _():
                copy(pl.ds(0, 1)).wait()

    def wait_gather(n, s):
        wait_n(n, lambda rows: pltpu.make_async_copy(x1_hbm.at[rows], xbuf.at[s, rows], gsem.at[s]))

    def wait_scatter(n, s):
        wait_n(n, lambda rows: pltpu.make_async_copy(obuf.at[s, rows], out_hbm.at[rows], ssem.at[s]))

    @pl.when(i == 0)
    def _first():
        xbuf[...] = jnp.zeros_like(xbuf)
        gather(tok_ref, nv_ref[0], 0)

    @pl.when(i + 1 < n_tiles)
    def _ahead():
        gather(tokn_ref, nv_ref[jnp.minimum(i + 1, n_tiles - 1)], 1 - slot)

    n = nv_ref[i]
    wait_gather(n, slot)

    @pl.when(i >= 2)
    def _drain():
        wait_scatter(nv_ref[jnp.maximum(i - 2, 0)], slot)

    @pl.when(n > 0)
    def _compute():
        x = xbuf[slot]
        xb = x.astype(BF16)
        ea, eb = ta_ref[i], tb_ref[i]
        logits = _router_logits(x, wr, br)
        lane = lax.broadcasted_iota(jnp.int32, logits.shape, 1)

        def pick(idx):
            return jnp.sum(jnp.where(lane == idx, logits, 0.0), axis=-1, keepdims=True)

        la, lb, lg = pick(ea), pick(eb), pick(N_EXPERTS + jnp.right_shift(ea, EPG_SHIFT))
        is_grp = jnp.logical_and(lane >= N_EXPERTS, lane < N_EXPERTS + N_GROUPS)
        g_w = 1.0 / jnp.sum(jnp.where(is_grp, jnp.exp(logits - lg), 0.0), axis=-1, keepdims=True)
        w_a = g_w / (1.0 + jnp.exp(lb - la))
        w_b = g_w / (1.0 + jnp.exp(la - lb))

        def expert(wg, wu, wd):
            g = jnp.dot(xb, wg[...], preferred_element_type=F32)
            u = jnp.dot(xb, wu[...], preferred_element_type=F32)
            hdn = (g * _sigmoid(g) * u).astype(BF16)
            return jnp.dot(hdn, wd[...], preferred_element_type=F32)

        ffn = w_a * expert(wga, wua, wda) + w_b * expert(wgb, wub, wdb)
        obuf[slot] = _layer_norm(ALPHA * x + ffn, l2g[...], l2b[...])

        for_rows(n, lambda j: pltpu.make_async_copy(
            obuf.at[slot, pl.ds(j, 1)], out_hbm.at[pl.ds(tok_ref[0, j], 1)], ssem.at[slot]).start())

    @pl.when(i == n_tiles - 1)
    def _last():
        wait_scatter(n, slot)
        wait_scatter(nv_ref[jnp.maximum(i - 1, 0)], 1 - slot)


def _moe(x1, tile_a, tile_b, n_valid, src_tok, wg, wu, wd, wr, br, l2g, l2b):
    T = x1.shape[0]
    tm = MOE_TILE
    n_tiles = src_tok.shape[0]
    tok_blk = lambda f: pl.BlockSpec((None, 1, tm), f, memory_space=pltpu.SMEM)
    w_in = lambda which: pl.BlockSpec((None, D_MODEL, D_EXPERT), lambda i, ta, tb, nv: ((ta, tb)[which][i], 0, 0))
    w_out = lambda which: pl.BlockSpec((None, D_EXPERT, D_MODEL), lambda i, ta, tb, nv: ((ta, tb)[which][i], 0, 0))
    const = lambda shape: pl.BlockSpec(shape, lambda i, ta, tb, nv: (0,) * len(shape))
    grid_spec = pltpu.PrefetchScalarGridSpec(
        num_scalar_prefetch=3, grid=(n_tiles,),
        in_specs=[tok_blk(lambda i, ta, tb, nv: (i, 0, 0)),
                  tok_blk(lambda i, ta, tb, nv: (jnp.minimum(i + 1, n_tiles - 1), 0, 0)),
                  pl.BlockSpec(memory_space=pl.ANY),
                  w_in(0), w_in(0), w_out(0), w_in(1), w_in(1), w_out(1),
                  const(wr.shape), const(br.shape), const(l2g.shape), const(l2b.shape)],
        out_specs=pl.BlockSpec(memory_space=pl.ANY),
        scratch_shapes=[pltpu.VMEM((2, tm, D_MODEL), F32), pltpu.VMEM((2, tm, D_MODEL), F32),
                        pltpu.SemaphoreType.DMA((2,)), pltpu.SemaphoreType.DMA((2,))])
    return pl.pallas_call(
        _moe_kernel, grid_spec=grid_spec, out_shape=jax.ShapeDtypeStruct((T, D_MODEL), F32),
        compiler_params=_cparams(1), name="moe")(
            tile_a, tile_b, n_valid, src_tok, src_tok, x1, wg, wu, wd, wg, wu, wd, wr, br, l2g, l2b)


def _route_tables(route, counts, n_tokens):
    tm = MOE_TILE
    n_tiles = n_tokens // tm + N_CLASSES
    cnt = counts[0, :N_CLASSES].astype(jnp.int32)
    tiles = (cnt + tm - 1) // tm
    tile_end = jnp.cumsum(tiles)
    tile_start = tile_end - tiles
    t_idx = jnp.arange(n_tiles, dtype=jnp.int32)
    cls_of_tile = jnp.minimum(jnp.sum(t_idx[:, None] >= tile_end[None, :], axis=1), N_CLASSES - 1).astype(jnp.int32)
    n_valid = jnp.clip(cnt[cls_of_tile] - (t_idx - tile_start[cls_of_tile]) * tm, 0, tm).astype(jnp.int32)
    grp, pair = cls_of_tile // PAIRS_PER_GROUP, cls_of_tile % PAIRS_PER_GROUP
    pa = jnp.asarray(PAIR_A, jnp.int32)[pair]
    pb = jnp.asarray(PAIR_B, jnp.int32)[pair]
    tile_a = grp * EXPERTS_PER_GROUP + pa
    tile_b = grp * EXPERTS_PER_GROUP + pb
    cls = route[:, 0].astype(jnp.int32)
    rank = route[:, 1].astype(jnp.int32)
    row0 = jnp.sum(jnp.where(cls[:, None] == jnp.arange(N_CLASSES)[None, :], (tile_start * tm)[None, :], 0), axis=1)
    pos = row0 + rank
    src = jnp.zeros((n_tiles * tm,), jnp.int32).at[pos].set(
        jnp.arange(n_tokens, dtype=jnp.int32), unique_indices=True)
    return tile_a, tile_b, n_valid, src.reshape(n_tiles, 1, tm)


def _pick_tile(T, pref):
    t = pref
    while T % t:
        t //= 2
    return t


def kernel(x, w_in, conv_w, conv_b, kv_norm_g, w_uk, w_uv, rel_bias, b_i, b_f, mh_norm_g, w_up_a, w_up_m,
           w_out, ln1_g, ln1_b, w_grp, b_grp, w_rt, b_rt, w_gate, w_up, w_down, ln2_g, ln2_b):
    B, S, _ = x.shape
    T = B * S
    assert S % Q_TILE == 0 and S % M_CHUNK == 0 and w_in.shape[0] == DEPTH
    tz = _bias_tables(rel_bias)
    x2 = x.reshape(T, D_MODEL)
    for l in range(DEPTH):
        w = w_in[l]
        o = np.cumsum((WIDTH_A, KV_RANK, WIDTH_IDX, HEAD_DIM_IDX, N_HEADS_IDX, 2 * WIDTH_M, WIDTH_M,
                       N_HEADS_M, N_HEADS_M, WIDTH_M, D_MODEL, D_MODEL)).tolist()
        o = [0] + o
        seg = lambda j: w[:, o[j]:o[j + 1]]
        pad = LANES - (HEAD_DIM_IDX + N_HEADS_IDX + 2 * N_HEADS_M)
        w_small = jnp.concatenate([seg(3), seg(4), seg(7), seg(8), jnp.zeros((D_MODEL, pad), w.dtype)], axis=1)
        ws = [seg(0), seg(1), seg(2), w_small, seg(5), seg(6), seg(9), seg(10), seg(11)]
        ws = [a.astype(BF16) for a in ws]
        smb = jnp.zeros((1, LANES), F32).at[0, SM_I:SM_I + N_HEADS_M].set(b_i[l]) \
            .at[0, SM_F:SM_F + N_HEADS_M].set(b_f[l])
        qa, ckv, qi, sm, qk, v, og, ga, gm = _proj(x2, ws, kv_norm_g[l][None, :], smb, _pick_tile(T, 512))

        wuk_t = jnp.transpose(w_uk[l], (1, 0, 2)).astype(BF16)
        wuv_t = jnp.transpose(w_uv[l], (1, 2, 0)).astype(BF16)
        ckv_t = jnp.concatenate([jnp.swapaxes(ckv.reshape(B, S, KV_RANK), 1, 2),
                                 jnp.ones((B, ONES_ROWS, S), BF16)], axis=1)
        oa = _dsa(qi, sm, qa, ckv, ckv_t, wuk_t, wuv_t, tz, B, S)

        hm = _mlstm(qk, v, sm, og, conv_w[l], conv_b[l][None, :], mh_norm_g[l].reshape(1, WIDTH_M), B, S)

        w_router = jnp.concatenate(
            [w_rt[l], w_grp[l], jnp.zeros((D_MODEL, LANES - N_EXPERTS - N_GROUPS), F32)], axis=1)
        b_router = jnp.concatenate(
            [b_rt[l], b_grp[l], jnp.zeros((LANES - N_EXPERTS - N_GROUPS,), F32)])[None, :]
        wr_hi = w_router.astype(BF16)
        wr_split = jnp.stack([wr_hi, (w_router - wr_hi.astype(F32)).astype(BF16)])
        x1, route, counts = _merge(x2, oa, hm, ga, gm, w_up_a[l].astype(BF16), w_up_m[l].astype(BF16),
                                   w_out[l].astype(BF16), ln1_g[l][None, :], ln1_b[l][None, :],
                                   wr_split, b_router, _pick_tile(T, 256))

        tile_a, tile_b, n_valid, src_tok = _route_tables(route, counts, T)
        x2 = _moe(x1, tile_a, tile_b, n_valid, src_tok, w_gate[l].astype(BF16), w_up[l].astype(BF16),
                  w_down[l].astype(BF16), wr_split, b_router, ln2_g[l][None, :], ln2_b[l][None, :])
    return x2.reshape(B, S, D_MODEL)
```

```python
import functools
import math

import jax
import jax.numpy as jnp
import numpy as np
from jax import lax
from jax.experimental import pallas as pl
from jax.experimental.pallas import tpu as pltpu

F32 = jnp.float32
BF16 = jnp.bfloat16

D_MODEL = 1024
N_HEADS_A = 8
HEAD_DIM_A = 64
WIDTH_A = N_HEADS_A * HEAD_DIM_A
KV_RANK = 256
N_HEADS_IDX = 8
HEAD_DIM_IDX = 64
WIDTH_IDX = N_HEADS_IDX * HEAD_DIM_IDX
TOPK_MAX = 256
N_BUCKETS = 32
MAX_DISTANCE = 128
N_HEADS_M = 4
HEAD_DIM_M = 128
WIDTH_M = N_HEADS_M * HEAD_DIM_M
CONV_WIDTH = 4
N_GROUPS = 4
EXPERTS_PER_GROUP = 4
N_EXPERTS = N_GROUPS * EXPERTS_PER_GROUP
D_EXPERT = 512
LN_EPS = 1e-5
DEPTH = 1
ALPHA = (2.0 * DEPTH) ** 0.25

LANES = 128
SUBLANES = 8
VMEM_LIMIT = 56 * 1024 * 1024

SM_KIDX = 0
SM_WIDX = HEAD_DIM_IDX
SM_I = SM_WIDX + N_HEADS_IDX
SM_F = SM_I + N_HEADS_M

Q_TILE = 256
K_CHUNK = Q_TILE
ONES_ROWS = 16
M_CHUNK = 128
CONV_HALO = 16
assert CONV_HALO >= CONV_WIDTH - 1
MOE_TILE = 256
ROW_UNROLL = 8
assert ROW_UNROLL & (ROW_UNROLL - 1) == 0
EPG_SHIFT = EXPERTS_PER_GROUP.bit_length() - 1
assert 1 << EPG_SHIFT == EXPERTS_PER_GROUP
PAIR_A, PAIR_B = zip(*[(a, b) for a in range(EXPERTS_PER_GROUP) for b in range(a + 1, EXPERTS_PER_GROUP)])
PAIRS_PER_GROUP = len(PAIR_A)
N_CLASSES = N_GROUPS * PAIRS_PER_GROUP
CLASS_ROWS = -(-N_CLASSES // SUBLANES) * SUBLANES
BISECT_STEPS_PER_CHECK = 3
BISECT_MAX_CHECKS = 5
PEEL_BRACKET = 2.0
NEG_INF = float("-inf")
LOG2E = math.log2(math.e)


def _cparams(n_grid):
    return pltpu.CompilerParams(dimension_semantics=("arbitrary",) * n_grid,
                                vmem_limit_bytes=VMEM_LIMIT)


def _full(shape):
    nd = len(shape)
    return pl.BlockSpec(shape, lambda *_: (0,) * nd)


def _sigmoid(x):
    return 0.5 * jnp.tanh(0.5 * x) + 0.5


def _split3(x):
    hi = x.astype(BF16)
    r = x - hi.astype(F32)
    mid = r.astype(BF16)
    return hi, mid, (r - mid.astype(F32)).astype(BF16)


def _proj_kernel(x_ref, wqa, wckv, wqi, wsm, wqk, wv, wo, wga, wgm, kvg, smb,
                 qa_o, ckv_o, qi_o, sm_o, qk_o, v_o, o_o, ga_o, gm_o):
    xb = x_ref[...].astype(BF16)

    def mm(w):
        return jnp.dot(xb, w[...], preferred_element_type=F32)

    qa_o[...] = mm(wqa).astype(BF16)
    c = mm(wckv)
    c = c * lax.rsqrt(jnp.mean(c * c, axis=-1, keepdims=True) + LN_EPS) * kvg[...]
    ckv_o[...] = c.astype(BF16)
    qi_o[...] = mm(wqi).astype(BF16)
    sm_o[...] = mm(wsm) + smb[...]
    qk_o[...] = mm(wqk).astype(BF16)
    v_o[...] = mm(wv).astype(BF16)
    o_o[...] = mm(wo).astype(BF16)
    ga_o[...] = mm(wga).astype(BF16)
    gm_o[...] = mm(wgm).astype(BF16)


def _proj(x2, ws, kvg, smb, tm):
    T = x2.shape[0]
    widths = [w.shape[1] for w in ws]
    dts = [BF16, BF16, BF16, F32, BF16, BF16, BF16, BF16, BF16]
    in_specs = [pl.BlockSpec((tm, D_MODEL), lambda i: (i, 0))]
    in_specs += [_full(w.shape) for w in ws]
    in_specs += [_full(kvg.shape), _full(smb.shape)]
    out_specs = [pl.BlockSpec((tm, n), lambda i: (i, 0)) for n in widths]
    out_shape = [jax.ShapeDtypeStruct((T, n), dt) for n, dt in zip(widths, dts)]
    return pl.pallas_call(
        _proj_kernel, grid=(T // tm,), in_specs=in_specs, out_specs=out_specs, out_shape=out_shape,
        compiler_params=_cparams(1), name="proj")(x2, *ws, kvg, smb)


def _dsa_kernel(qi_ref, smq_ref, smk_ref, qa_ref, ckv_ref, ckvt_ref, wuk_ref, wuvt_ref, tz_ref,
                oa_ref, qs_ref, sc_ref, am_ref, ql_ref, x_ref, acc_ref, m_ref, *, topk):
    tq, kc = Q_TILE, K_CHUNK
    nh = N_HEADS_A
    qb = pl.program_id(1)
    n_ch = qb + 1
    seq_keys = n_ch * kc
    t0 = qb * tq

    def rows(c):
        return pl.ds(pl.multiple_of(c * kc, kc), kc)

    def lanes(h):
        return slice(h * tq, (h + 1) * tq)

    def chunk_loop(n, body, init):
        def pair(p, carry):
            return body(2 * p + 1, body(2 * p, carry))
        carry = lax.fori_loop(0, jnp.right_shift(n, 1), pair, init)
        return lax.cond(jnp.bitwise_and(n, 1) == 1, lambda c: body(n - 1, c), lambda c: c, carry)

    for h in range(N_HEADS_IDX):
        qs_ref[h * tq:(h + 1) * tq, :] = qi_ref[:, h * HEAD_DIM_IDX:(h + 1) * HEAD_DIM_IDX]
    w_t = smq_ref[...].T
    q_pos = lax.broadcasted_iota(jnp.int32, (1, tq), 1) + t0
    key_iota = lax.broadcasted_iota(jnp.int32, (kc, tq), 0)

    def score_chunk(c, carry):
        mx, mn = carry
        kk = smk_ref[rows(c), SM_KIDX:SM_KIDX + HEAD_DIM_IDX].astype(BF16)
        dots = lax.dot_general(kk, qs_ref[...], (((1,), (1,)), ((), ())), preferred_element_type=F32)
        sc = jnp.zeros((kc, tq), F32)
        for h in range(N_HEADS_IDX):
            sc = sc + w_t[SM_WIDX + h:SM_WIDX + h + 1, :] * jnp.maximum(dots[:, lanes(h)], 0.0)
        vis = (key_iota + c * kc) <= q_pos
        sc_ref[rows(c), :] = jnp.where(vis, sc, NEG_INF)
        mx = jnp.maximum(mx, jnp.max(jnp.where(vis, sc, NEG_INF), axis=0, keepdims=True))
        mn = jnp.minimum(mn, jnp.min(jnp.where(vis, sc, jnp.inf), axis=0, keepdims=True))
        return mx, mn

    mx, mn = chunk_loop(n_ch, score_chunk,
                        (jnp.full((1, tq), NEG_INF, F32), jnp.full((1, tq), jnp.inf, F32)))

    n_vis = (q_pos + 1).astype(F32)
    k_row = jnp.minimum(n_vis, float(topk))

    def count(pred):
        def body(c, a):
            hit = pred(sc_ref[rows(c), :]).astype(F32)
            return a + jnp.sum(hit.reshape(kc // SUBLANES, SUBLANES, tq), axis=0)
        a = chunk_loop(n_ch, body, jnp.zeros((SUBLANES, tq), F32))
        return jnp.sum(a, axis=0, keepdims=True)

    def any_lane(flag):
        return jnp.max(jnp.where(flag, 1.0, 0.0)) > 0.0

    def crowded(cnt_lo, c_hi):
        return any_lane(jnp.logical_and(cnt_lo != k_row, cnt_lo - c_hi > PEEL_BRACKET))

    def bisect_cond(carry):
        it, lo, hi, cnt_lo, c_hi = carry
        return jnp.logical_and(it < BISECT_MAX_CHECKS, crowded(cnt_lo, c_hi))

    def bisect_body(carry):
        it, lo, hi, cnt_lo, c_hi = carry
        for _ in range(BISECT_STEPS_PER_CHECK):
            mid = lo * 0.5 + hi * 0.5
            cnt = count(lambda s: s >= mid)
            ge = cnt >= k_row
            lo, cnt_lo = jnp.where(ge, mid, lo), jnp.where(ge, cnt, cnt_lo)
            hi, c_hi = jnp.where(ge, hi, mid), jnp.where(ge, c_hi, cnt)
        return it + 1, lo, hi, cnt_lo, c_hi

    hi0 = mx + jnp.maximum(jnp.abs(mx), 1e-30) * 1e-6
    _, lo, hi, cnt_lo, c_hi = lax.while_loop(
        bisect_cond, bisect_body, (jnp.int32(0), mn, hi0, n_vis, jnp.zeros((1, tq), F32)))

    def peel_cond(carry):
        it, lo, hi, cnt_lo, c_hi, done = carry
        return jnp.logical_and(it < seq_keys, any_lane(done == 0.0))

    def peel_body(carry):
        it, lo, hi, cnt_lo, c_hi, done = carry

        def top_body(c, v):
            s = sc_ref[rows(c), :]
            inside = jnp.logical_and(s >= lo, s < hi)
            return jnp.maximum(v, jnp.max(jnp.where(inside, s, NEG_INF), axis=0, keepdims=True))

        v = chunk_loop(n_ch, top_body, jnp.full((1, tq), NEG_INF, F32))
        c_v = count(lambda s: s >= v)
        reached = c_v >= k_row
        live = done == 0.0
        fin = jnp.logical_and(live, reached)
        cut = jnp.logical_and(live, jnp.logical_not(reached))
        return (it + 1, jnp.where(fin, v, lo), jnp.where(cut, v, hi), jnp.where(fin, c_v, cnt_lo),
                jnp.where(cut, c_v, c_hi), jnp.where(fin, 1.0, done))

    _, lo, hi, cnt_lo, c_hi, _ = lax.while_loop(
        peel_cond, peel_body,
        (jnp.int32(0), lo, hi, cnt_lo, c_hi, jnp.where(cnt_lo == k_row, 1.0, 0.0)))

    def mask_chunk(c, _):
        am_ref[rows(c), :] = jnp.where(sc_ref[rows(c), :] >= lo, 0.0, NEG_INF)
        return 0

    chunk_loop(n_ch, mask_chunk, 0)

    tied = cnt_lo != k_row
    for part in range(tq // LANES):
        ls = slice(part * LANES, (part + 1) * LANES)

        @pl.when(any_lane(tied[:, ls]))
        def _ties():
            need, lo_p, hi_p = (k_row - c_hi)[:, ls], lo[:, ls], hi[:, ls]
            lower = (lax.broadcasted_iota(jnp.int32, (kc, kc), 1)
                     < lax.broadcasted_iota(jnp.int32, (kc, kc), 0)).astype(BF16)

            def tie_chunk(c, before):
                s = sc_ref[rows(c), ls]
                above = s >= hi_p
                tie = jnp.logical_and(s >= lo_p, jnp.logical_not(above))
                tie_f = tie.astype(F32)
                rank = jnp.dot(lower, tie_f.astype(BF16), preferred_element_type=F32) + before
                sel = jnp.logical_or(above, jnp.logical_and(tie, rank < need))
                am_ref[rows(c), ls] = jnp.where(sel, 0.0, NEG_INF)
                return before + jnp.sum(tie_f, axis=0, keepdims=True)

            chunk_loop(n_ch, tie_chunk, jnp.zeros((1, LANES), F32))

    for h in range(nh):
        qh = qa_ref[:, h * HEAD_DIM_A:(h + 1) * HEAD_DIM_A]
        qlt = lax.dot_general(wuk_ref[h], qh, (((1,), (1,)), ((), ())), preferred_element_type=F32)
        ql_ref[:, lanes(h)] = (qlt * (HEAD_DIM_A ** -0.5 * LOG2E)).astype(BF16)

    m_ref[...] = jnp.full(m_ref.shape, NEG_INF, F32)

    def pass_a(c, table):
        raw = jnp.dot(ckv_ref[rows(c), :], ql_ref[...], preferred_element_type=F32)
        am = am_ref[rows(c), :]
        for h in range(nh):
            x = raw[:, lanes(h)] + am
            if table is not None:
                x = x + tz_ref[table, h]
            x_ref[rows(c), lanes(h)] = x
            m_ref[:, lanes(h)] = jnp.maximum(m_ref[:, lanes(h)], jnp.max(x, axis=0, keepdims=True))

    def far_chunk(c, _):
        pass_a(c, None)
        return 0

    chunk_loop(jnp.maximum(qb - 1, 0), far_chunk, 0)

    @pl.when(qb >= 1)
    def _prev():
        pass_a(qb - 1, 1)

    pass_a(qb, 0)

    def pv(c):
        p = jnp.exp2((x_ref[rows(c), :] - m_ref[...]).astype(BF16))
        return jnp.dot(ckvt_ref[:, rows(c)], p, preferred_element_type=F32)

    def pass_b(c, _):
        acc_ref[...] += pv(c)
        return 0

    acc_ref[...] = pv(qb)
    chunk_loop(qb, pass_b, 0)
    inv_l = 1.0 / acc_ref[KV_RANK:KV_RANK + 1, :]
    o_lat = (acc_ref[0:KV_RANK, :] * inv_l).astype(BF16)
    outs = [jnp.dot(wuvt_ref[h], o_lat[:, lanes(h)], preferred_element_type=F32) for h in range(nh)]
    oa_ref[...] = jnp.concatenate(outs, axis=0).T.astype(BF16)


def _dsa(qi, sm, qa, ckv, ckv_t, wuk_t, wuv_t, tz, batch, seq):
    tq = Q_TILE
    nq = seq // tq
    topk = min(TOPK_MAX, seq // 4)
    T = batch * seq
    kern = functools.partial(_dsa_kernel, topk=topk)
    blk_q = lambda n: pl.BlockSpec((tq, n), lambda b, q: (b * nq + q, 0))
    blk_s = lambda n: pl.BlockSpec((seq, n), lambda b, q: (b, 0))
    in_specs = [
        blk_q(WIDTH_IDX), blk_q(LANES), blk_s(LANES), blk_q(WIDTH_A), blk_s(KV_RANK),
        pl.BlockSpec((None, KV_RANK + ONES_ROWS, seq), lambda b, q: (b, 0, 0)),
        _full(wuk_t.shape), _full(wuv_t.shape), _full(tz.shape),
    ]
    scratch = [
        pltpu.VMEM((N_HEADS_IDX * tq, HEAD_DIM_IDX), BF16),
        pltpu.VMEM((seq, tq), F32),
        pltpu.VMEM((seq, tq), F32),
        pltpu.VMEM((KV_RANK, N_HEADS_A * tq), BF16),
        pltpu.VMEM((seq, N_HEADS_A * tq), F32),
        pltpu.VMEM((KV_RANK + ONES_ROWS, N_HEADS_A * tq), F32),
        pltpu.VMEM((1, N_HEADS_A * tq), F32),
    ]
    return pl.pallas_call(
        kern, grid=(batch, nq), in_specs=in_specs,
        out_specs=pl.BlockSpec((tq, WIDTH_A), lambda b, q: (b * nq + q, 0)),
        out_shape=jax.ShapeDtypeStruct((T, WIDTH_A), BF16),
        scratch_shapes=scratch, compiler_params=_cparams(2), name="dsa")(
            qi, sm, sm, qa, ckv, ckv_t, wuk_t, wuv_t, tz)


def _t5_bucket(dist):
    max_exact = N_BUCKETS // 2
    d = jnp.maximum(dist, 0)
    ratio = jnp.log(jnp.maximum(d, 1).astype(F32) / max_exact) / math.log(MAX_DISTANCE / max_exact)
    large = jnp.minimum(max_exact + (ratio * (N_BUCKETS - max_exact)).astype(jnp.int32), N_BUCKETS - 1)
    return jnp.where(d < max_exact, d, large)


def _bias_tables(rel_bias):
    tq = Q_TILE
    span = 2 * tq
    assert int(_np_bucket(tq)) == N_BUCKETS - 1
    far = rel_bias[N_BUCKETS - 1]
    by_dist = (rel_bias[_t5_bucket(jnp.arange(span))] - far).astype(F32) * LOG2E
    diag = jnp.concatenate([by_dist[:tq], jnp.zeros_like(by_dist[:tq])])
    prev = jnp.concatenate([by_dist[tq:], by_dist[:tq]])

    def toeplitz(f):
        m = jnp.tile(f, (tq, 1))[:tq * (span - 1)].reshape(tq, span - 1, N_HEADS_A)
        return m[:, :tq]

    tz = jnp.stack([toeplitz(diag), toeplitz(prev)])
    return jnp.moveaxis(tz, -1, 1)


def _np_bucket(d):
    max_exact = N_BUCKETS // 2
    ratio = np.log(np.float32(max(d, 1)) / np.float32(max_exact)) / math.log(MAX_DISTANCE / max_exact)
    return min(max_exact + int(ratio * (N_BUCKETS - max_exact)), N_BUCKETS - 1) if d >= max_exact else d


def _mlstm_kernel(qk_ref, halo_ref, v_ref, sm_ref, o_ref, cw_ref, cb_ref, g_ref,
                  out_ref, c_ref, m_ref):
    L = M_CHUNK
    dm = HEAD_DIM_M
    c_idx = pl.program_id(1)

    @pl.when(c_idx == 0)
    def _init():
        c_ref[...] = jnp.zeros_like(c_ref)
        m_ref[...] = jnp.zeros_like(m_ref)

    hw = halo_ref.shape[0]
    halo = jnp.where(c_idx > 0, halo_ref[...].astype(F32), 0.0)
    ext = jnp.concatenate([halo, qk_ref[...].astype(F32)], axis=0)
    acc = jnp.zeros((L, 2 * WIDTH_M), F32) + cb_ref[...]
    for w in range(CONV_WIDTH):
        off = hw - (CONV_WIDTH - 1) + w
        acc = acc + ext[off:off + L, :] * cw_ref[w:w + 1, :]
    qk = acc * _sigmoid(acc)

    sm = sm_ref[...]
    sm_t = sm.T
    r_i = lax.broadcasted_iota(jnp.int32, (L, L), 0)
    c_i = lax.broadcasted_iota(jnp.int32, (L, L), 1)
    tril = (c_i <= r_i).astype(BF16)
    bcum_c = sum(jnp.dot(tril, part, preferred_element_type=F32)
                 for part in _split3(jax.nn.log_sigmoid(sm)))
    bcum_r = bcum_c.T
    causal = c_i <= r_i
    ones_col = (lax.broadcasted_iota(jnp.int32, (L, dm), 1) == 0).astype(BF16)

    outs = []
    for h in range(N_HEADS_M):
        q = qk[:, h * dm:(h + 1) * dm].astype(BF16)
        k = (qk[:, WIDTH_M + h * dm:WIDTH_M + (h + 1) * dm] * (dm ** -0.5))
        v_aug = jnp.concatenate([v_ref[:, h * dm:(h + 1) * dm], ones_col], axis=-1)
        b_col = bcum_c[:, SM_F + h:SM_F + h + 1]
        g_col = sm[:, SM_I + h:SM_I + h + 1] - b_col
        g_row = sm_t[SM_I + h:SM_I + h + 1, :] - bcum_r[SM_F + h:SM_F + h + 1, :]
        b_last = b_col[L - 1:L, :]
        m_prev = m_ref[h]
        c_prev = c_ref[h]

        log_d = jnp.where(causal, b_col + g_row, NEG_INF)
        m_j = jnp.maximum(b_col + m_prev, jnp.max(log_d, axis=-1, keepdims=True))
        w_inter = jnp.exp(b_col + m_prev - m_j)
        qkt = lax.dot_general(q, k.astype(BF16), (((1,), (1,)), ((), ())), preferred_element_type=F32)
        s = qkt * jnp.exp(log_d - m_j)
        o_aug = jnp.dot(s.astype(BF16), v_aug, preferred_element_type=F32) + \
            w_inter * jnp.dot(q, c_prev.astype(BF16), preferred_element_type=F32)
        num = o_aug[:, :dm]
        den = o_aug[:, dm:dm + 1]
        hh = num / jnp.maximum(jnp.abs(den), jnp.exp(-m_j))

        lwe = b_last + g_col
        m_loc = jnp.max(lwe, axis=0, keepdims=True)
        kw = (k * jnp.exp(lwe - m_loc)).astype(BF16)
        c_loc = lax.dot_general(kw, v_aug, (((0,), (0,)), ((), ())), preferred_element_type=F32)
        m_new = jnp.maximum(b_last + m_prev, m_loc)
        c_ref[h] = jnp.exp(b_last + m_prev - m_new) * c_prev + jnp.exp(m_loc - m_new) * c_loc
        m_ref[h] = m_new

        hn = hh * lax.rsqrt(jnp.mean(hh * hh, axis=-1, keepdims=True) + LN_EPS) * g_ref[:, h * dm:(h + 1) * dm]
        og = o_ref[:, h * dm:(h + 1) * dm].astype(F32)
        outs.append(_sigmoid(og) * hn)
    out_ref[...] = jnp.concatenate(outs, axis=-1).astype(BF16)


def _mlstm(qk, v, sm, o, conv_w, conv_b, mh_g, batch, seq):
    L = M_CHUNK
    nc = seq // L
    T = batch * seq
    hb = L // CONV_HALO
    blk = lambda n: pl.BlockSpec((L, n), lambda b, c: (b * nc + c, 0))
    in_specs = [
        blk(2 * WIDTH_M),
        pl.BlockSpec((CONV_HALO, 2 * WIDTH_M), lambda b, c: (jnp.maximum((b * nc + c) * hb - 1, 0), 0)),
        blk(WIDTH_M), blk(LANES), blk(WIDTH_M),
        _full(conv_w.shape), _full(conv_b.shape), _full(mh_g.shape),
    ]
    scratch = [pltpu.VMEM((N_HEADS_M, HEAD_DIM_M, 2 * HEAD_DIM_M), F32),
               pltpu.VMEM((N_HEADS_M, 1, 1), F32)]
    return pl.pallas_call(
        _mlstm_kernel, grid=(batch, nc), in_specs=in_specs, out_specs=blk(WIDTH_M),
        out_shape=jax.ShapeDtypeStruct((T, WIDTH_M), BF16), scratch_shapes=scratch,
        compiler_params=_cparams(2), name="mlstm")(qk, qk, v, sm, o, conv_w, conv_b, mh_g)


def _layer_norm(y, g, b):
    mu = jnp.mean(y, axis=-1, keepdims=True)
    var = jnp.mean(jnp.square(y - mu), axis=-1, keepdims=True)
    return (y - mu) * lax.rsqrt(var + LN_EPS) * g + b


def _router_logits(x1, wr, br):
    x1_hi = x1.astype(BF16)
    x1_lo = (x1 - x1_hi.astype(F32)).astype(BF16)
    return (jnp.dot(x1_hi, wr[0], preferred_element_type=F32)
            + jnp.dot(x1_lo, wr[0], preferred_element_type=F32)
            + jnp.dot(x1_hi, wr[1], preferred_element_type=F32)) + br[...]


def _merge_kernel(x_ref, oa_ref, hm_ref, ga_ref, gm_ref, wua, wum, wout, l1g, l1b, wr, br,
                  x1_ref, route_ref, cnt_ref):
    pa = jnp.dot(oa_ref[...], wua[...], preferred_element_type=F32)
    pm = jnp.dot(hm_ref[...], wum[...], preferred_element_type=F32)
    y = _sigmoid(ga_ref[...]) * pa.astype(BF16) + _sigmoid(gm_ref[...]) * pm.astype(BF16)
    mix = jnp.dot(y.astype(BF16), wout[...], preferred_element_type=F32)
    x1 = _layer_norm(ALPHA * x_ref[...] + mix, l1g[...], l1b[...])
    x1_ref[...] = x1

    lt = _router_logits(x1, wr, br).T
    tm = lt.shape[1]
    big = jnp.int32(LANES)
    le = lt[0:N_EXPERTS, :]
    lg = lt[N_EXPERTS:N_EXPERTS + SUBLANES, :]
    row_g = lax.broadcasted_iota(jnp.int32, lg.shape, 0)
    row_e = lax.broadcasted_iota(jnp.int32, le.shape, 0)
    is_grp = row_g < N_GROUPS
    gl = jnp.where(is_grp, lg, NEG_INF)
    ge = jnp.exp(gl - jnp.max(gl, axis=0, keepdims=True))
    gp = ge / jnp.sum(ge, axis=0, keepdims=True)
    g_w = jnp.max(gp, axis=0, keepdims=True)
    g_idx = jnp.min(jnp.where(jnp.logical_and(is_grp, gp == g_w), row_g, big), axis=0, keepdims=True)
    el = jnp.where(jnp.right_shift(row_e, EPG_SHIFT) == g_idx, le, NEG_INF)
    m1 = jnp.max(el, axis=0, keepdims=True)
    i1 = jnp.min(jnp.where(el == m1, row_e, big), axis=0, keepdims=True)
    el2 = jnp.where(row_e == i1, NEG_INF, el)
    m2 = jnp.max(el2, axis=0, keepdims=True)
    i2 = jnp.min(jnp.where(el2 == m2, row_e, big), axis=0, keepdims=True)

    a = jnp.bitwise_and(jnp.minimum(i1, i2), EXPERTS_PER_GROUP - 1)
    b = jnp.bitwise_and(jnp.maximum(i1, i2), EXPERTS_PER_GROUP - 1)
    pair = jnp.right_shift(a * (2 * EXPERTS_PER_GROUP - 1 - a), 1) + (b - a - 1)
    cls = g_idx * PAIRS_PER_GROUP + pair
    row_c = lax.broadcasted_iota(jnp.int32, (CLASS_ROWS, tm), 0)
    onehot = (row_c == cls).astype(F32)

    @pl.when(pl.program_id(0) == 0)
    def _init():
        cnt_ref[...] = jnp.zeros_like(cnt_ref)

    earlier = (lax.broadcasted_iota(jnp.int32, (tm, tm), 0)
               < lax.broadcasted_iota(jnp.int32, (tm, tm), 1)).astype(BF16)
    prior = jnp.dot(onehot.astype(BF16), earlier, preferred_element_type=F32) + cnt_ref[:, 0:1]
    rank = jnp.sum(prior * onehot, axis=0, keepdims=True)
    cnt_ref[...] += jnp.sum(onehot, axis=1, keepdims=True)
    row_o = lax.broadcasted_iota(jnp.int32, route_ref.shape, 0)
    route_ref[...] = jnp.where(row_o == 0, cls.astype(F32), jnp.where(row_o == 1, rank, 0.0))


def _merge(x2, oa, hm, ga, gm, wua, wum, wout, l1g, l1b, wr, br, tm):
    T = x2.shape[0]
    blk = lambda n: pl.BlockSpec((tm, n), lambda i: (i, 0))
    in_specs = [blk(D_MODEL), blk(WIDTH_A), blk(WIDTH_M), blk(D_MODEL), blk(D_MODEL),
                _full(wua.shape), _full(wum.shape), _full(wout.shape), _full(l1g.shape), _full(l1b.shape),
                _full(wr.shape), _full(br.shape)]
    out_specs = [blk(D_MODEL), pl.BlockSpec((SUBLANES, tm), lambda i: (0, i)), _full((CLASS_ROWS, LANES))]
    out_shape = [jax.ShapeDtypeStruct((T, D_MODEL), F32), jax.ShapeDtypeStruct((SUBLANES, T), F32),
                 jax.ShapeDtypeStruct((CLASS_ROWS, LANES), F32)]
    return pl.pallas_call(
        _merge_kernel, grid=(T // tm,), in_specs=in_specs, out_specs=out_specs, out_shape=out_shape,
        compiler_params=_cparams(1), name="merge")(x2, oa, hm, ga, gm, wua, wum, wout, l1g, l1b, wr, br)


def _moe_kernel(ta_ref, tb_ref, nv_ref, tok_ref, tokn_ref, x1_hbm, wga, wua, wda, wgb, wub, wdb, wr, br, l2g, l2b,
                out_hbm, xbuf, obuf, gsem, ssem):
    i = pl.program_id(0)
    n_tiles = pl.num_programs(0)
    slot = lax.rem(i, 2)
    tm = MOE_TILE

    def for_rows(n, start_row):
        groups = jnp.right_shift(n, ROW_UNROLL.bit_length() - 1)

        def group(g, _):
            for r in range(ROW_UNROLL):
                start_row(g * ROW_UNROLL + r)
            return 0

        def single(j, _):
            start_row(j)
            return 0

        lax.fori_loop(0, groups, group, 0)
        lax.fori_loop(groups * ROW_UNROLL, n, single, 0)

    def gather(tok, n, s):
        for_rows(n, lambda j: pltpu.make_async_copy(
            x1_hbm.at[pl.ds(tok[0, j], 1)], xbuf.at[s, pl.ds(j, 1)], gsem.at[s]).start())

    def wait_n(n, copy):
        bulk = pl.multiple_of(jnp.bitwise_and(n, -SUBLANES), SUBLANES)

        @pl.when(bulk > 0)
        def _():
            copy(pl.ds(0, bulk)).wait()

        for r in range(SUBLANES - 1):
            @pl.when(bulk + r < n)
            def _():
                copy(pl.ds(0, 1)).wait()

    def wait_gather(n, s):
        wait_n(n, lambda rows: pltpu.make_async_copy(x1_hbm.at[rows], xbuf.at[s, rows], gsem.at[s]))

    def wait_scatter(n, s):
        wait_n(n, lambda rows: pltpu.make_async_copy(obuf.at[s, rows], out_hbm.at[rows], ssem.at[s]))

    @pl.when(i == 0)
    def _first():
        xbuf[...] = jnp.zeros_like(xbuf)
        gather(tok_ref, nv_ref[0], 0)

    @pl.when(i + 1 < n_tiles)
    def _ahead():
        gather(tokn_ref, nv_ref[jnp.minimum(i + 1, n_tiles - 1)], 1 - slot)

    n = nv_ref[i]
    wait_gather(n, slot)

    @pl.when(i >= 2)
    def _drain():
        wait_scatter(nv_ref[jnp.maximum(i - 2, 0)], slot)

    @pl.when(n > 0)
    def _compute():
        x = xbuf[slot]
        xb = x.astype(BF16)
        ea, eb = ta_ref[i], tb_ref[i]
        logits = _router_logits(x, wr, br)
        lane = lax.broadcasted_iota(jnp.int32, logits.shape, 1)

        def pick(idx):
            return jnp.sum(jnp.where(lane == idx, logits, 0.0), axis=-1, keepdims=True)

        la, lb, lg = pick(ea), pick(eb), pick(N_EXPERTS + jnp.right_shift(ea, EPG_SHIFT))
        is_grp = jnp.logical_and(lane >= N_EXPERTS, lane < N_EXPERTS + N_GROUPS)
        g_w = 1.0 / jnp.sum(jnp.where(is_grp, jnp.exp(logits - lg), 0.0), axis=-1, keepdims=True)
        w_a = g_w / (1.0 + jnp.exp(lb - la))
        w_b = g_w / (1.0 + jnp.exp(la - lb))

        def expert(wg, wu, wd):
            g = jnp.dot(xb, wg[...], preferred_element_type=F32)
            u = jnp.dot(xb, wu[...], preferred_element_type=F32)
            hdn = (g * _sigmoid(g) * u).astype(BF16)
            return jnp.dot(hdn, wd[...], preferred_element_type=F32)

        ffn = w_a * expert(wga, wua, wda) + w_b * expert(wgb, wub, wdb)
        obuf[slot] = _layer_norm(ALPHA * x + ffn, l2g[...], l2b[...])

        for_rows(n, lambda j: pltpu.make_async_copy(
            obuf.at[slot, pl.ds(j, 1)], out_hbm.at[pl.ds(tok_ref[0, j], 1)], ssem.at[slot]).start())

    @pl.when(i == n_tiles - 1)
    def _last():
        wait_scatter(n, slot)
        wait_scatter(nv_ref[jnp.maximum(i - 1, 0)], 1 - slot)


def _moe(x1, tile_a, tile_b, n_valid, src_tok, wg, wu, wd, wr, br, l2g, l2b):
    T = x1.shape[0]
    tm = MOE_TILE
    n_tiles = src_tok.shape[0]
    tok_blk = lambda f: pl.BlockSpec((None, 1, tm), f, memory_space=pltpu.SMEM)
    w_in = lambda which: pl.BlockSpec((None, D_MODEL, D_EXPERT), lambda i, ta, tb, nv: ((ta, tb)[which][i], 0, 0))
    w_out = lambda which: pl.BlockSpec((None, D_EXPERT, D_MODEL), lambda i, ta, tb, nv: ((ta, tb)[which][i], 0, 0))
    const = lambda shape: pl.BlockSpec(shape, lambda i, ta, tb, nv: (0,) * len(shape))
    grid_spec = pltpu.PrefetchScalarGridSpec(
        num_scalar_prefetch=3, grid=(n_tiles,),
        in_specs=[tok_blk(lambda i, ta, tb, nv: (i, 0, 0)),
                  tok_blk(lambda i, ta, tb, nv: (jnp.minimum(i + 1, n_tiles - 1), 0, 0)),
                  pl.BlockSpec(memory_space=pl.ANY),
                  w_in(0), w_in(0), w_out(0), w_in(1), w_in(1), w_out(1),
                  const(wr.shape), const(br.shape), const(l2g.shape), const(l2b.shape)],
        out_specs=pl.BlockSpec(memory_space=pl.ANY),
        scratch_shapes=[pltpu.VMEM((2, tm, D_MODEL), F32), pltpu.VMEM((2, tm, D_MODEL), F32),
                        pltpu.SemaphoreType.DMA((2,)), pltpu.SemaphoreType.DMA((2,))])
    return pl.pallas_call(
        _moe_kernel, grid_spec=grid_spec, out_shape=jax.ShapeDtypeStruct((T, D_MODEL), F32),
        compiler_params=_cparams(1), name="moe")(
            tile_a, tile_b, n_valid, src_tok, src_tok, x1, wg, wu, wd, wg, wu, wd, wr, br, l2g, l2b)


def _route_tables(route, counts, n_tokens):
    tm = MOE_TILE
    n_tiles = n_tokens // tm + N_CLASSES
    cnt = counts[:N_CLASSES, 0].astype(jnp.int32)
    tiles = (cnt + tm - 1) // tm
    tile_end = jnp.cumsum(tiles)
    tile_start = tile_end - tiles
    t_idx = jnp.arange(n_tiles, dtype=jnp.int32)
    cls_of_tile = jnp.minimum(jnp.sum(t_idx[:, None] >= tile_end[None, :], axis=1), N_CLASSES - 1).astype(jnp.int32)
    n_valid = jnp.clip(cnt[cls_of_tile] - (t_idx - tile_start[cls_of_tile]) * tm, 0, tm).astype(jnp.int32)
    grp, pair = cls_of_tile // PAIRS_PER_GROUP, cls_of_tile % PAIRS_PER_GROUP
    pa = jnp.asarray(PAIR_A, jnp.int32)[pair]
    pb = jnp.asarray(PAIR_B, jnp.int32)[pair]
    tile_a = grp * EXPERTS_PER_GROUP + pa
    tile_b = grp * EXPERTS_PER_GROUP + pb
    cls = route[0].astype(jnp.int32)
    rank = route[1].astype(jnp.int32)
    row0 = jnp.sum(jnp.where(cls[:, None] == jnp.arange(N_CLASSES)[None, :], (tile_start * tm)[None, :], 0), axis=1)
    pos = row0 + rank
    src = jnp.zeros((n_tiles * tm,), jnp.int32).at[pos].set(
        jnp.arange(n_tokens, dtype=jnp.int32), unique_indices=True)
    return tile_a, tile_b, n_valid, src.reshape(n_tiles, 1, tm)


def _pick_tile(T, pref):
    t = pref
    while T % t:
        t //= 2
    return t


def kernel(x, w_in, conv_w, conv_b, kv_norm_g, w_uk, w_uv, rel_bias, b_i, b_f, mh_norm_g, w_up_a, w_up_m,
           w_out, ln1_g, ln1_b, w_grp, b_grp, w_rt, b_rt, w_gate, w_up, w_down, ln2_g, ln2_b):
    B, S, _ = x.shape
    T = B * S
    assert S % Q_TILE == 0 and S % M_CHUNK == 0 and T % MOE_TILE == 0 and w_in.shape[0] == DEPTH
    tz = _bias_tables(rel_bias)
    x2 = x.reshape(T, D_MODEL)
    for l in range(DEPTH):
        w = w_in[l]
        o = np.cumsum((WIDTH_A, KV_RANK, WIDTH_IDX, HEAD_DIM_IDX, N_HEADS_IDX, 2 * WIDTH_M, WIDTH_M,
                       N_HEADS_M, N_HEADS_M, WIDTH_M, D_MODEL, D_MODEL)).tolist()
        o = [0] + o
        seg = lambda j: w[:, o[j]:o[j + 1]]
        pad = LANES - (HEAD_DIM_IDX + N_HEADS_IDX + 2 * N_HEADS_M)
        w_small = jnp.concatenate([seg(3), seg(4), seg(7), seg(8), jnp.zeros((D_MODEL, pad), w.dtype)], axis=1)
        ws = [seg(0), seg(1), seg(2), w_small, seg(5), seg(6), seg(9), seg(10), seg(11)]
        ws = [a.astype(BF16) for a in ws]
        smb = jnp.zeros((1, LANES), F32).at[0, SM_I:SM_I + N_HEADS_M].set(b_i[l]) \
            .at[0, SM_F:SM_F + N_HEADS_M].set(b_f[l])
        qa, ckv, qi, sm, qk, v, og, ga, gm = _proj(x2, ws, kv_norm_g[l][None, :], smb, _pick_tile(T, 512))

        wuk_t = jnp.transpose(w_uk[l], (1, 0, 2)).astype(BF16)
        wuv_t = jnp.transpose(w_uv[l], (1, 2, 0)).astype(BF16)
        ckv_t = jnp.concatenate([jnp.swapaxes(ckv.reshape(B, S, KV_RANK), 1, 2),
                                 jnp.ones((B, ONES_ROWS, S), BF16)], axis=1)
        oa = _dsa(qi, sm, qa, ckv, ckv_t, wuk_t, wuv_t, tz, B, S)

        hm = _mlstm(qk, v, sm, og, conv_w[l], conv_b[l][None, :], mh_norm_g[l].reshape(1, WIDTH_M), B, S)

        w_router = jnp.concatenate(
            [w_rt[l], w_grp[l], jnp.zeros((D_MODEL, LANES - N_EXPERTS - N_GROUPS), F32)], axis=1)
        b_router = jnp.concatenate(
            [b_rt[l], b_grp[l], jnp.zeros((LANES - N_EXPERTS - N_GROUPS,), F32)])[None, :]
        wr_hi = w_router.astype(BF16)
        wr_split = jnp.stack([wr_hi, (w_router - wr_hi.astype(F32)).astype(BF16)])
        x1, route, counts = _merge(x2, oa, hm, ga, gm, w_up_a[l].astype(BF16), w_up_m[l].astype(BF16),
                                   w_out[l].astype(BF16), ln1_g[l][None, :], ln1_b[l][None, :],
                                   wr_split, b_router, _pick_tile(T, 256))

        tile_a, tile_b, n_valid, src_tok = _route_tables(route, counts, T)
        x2 = _moe(x1, tile_a, tile_b, n_valid, src_tok, w_gate[l].astype(BF16), w_up[l].astype(BF16),
                  w_down[l].astype(BF16), wr_split, b_router, ln2_g[l][None, :], ln2_b[l][None, :])
    return x2.reshape(B, S, D_MODEL)
```

```python
import functools
import math

import jax
import jax.numpy as jnp
import numpy as np
from jax import lax
from jax.experimental import pallas as pl
from jax.experimental.pallas import tpu as pltpu

F32 = jnp.float32
BF16 = jnp.bfloat16

D_MODEL = 1024
N_HEADS_A = 8
HEAD_DIM_A = 64
WIDTH_A = N_HEADS_A * HEAD_DIM_A
KV_RANK = 256
N_HEADS_IDX = 8
HEAD_DIM_IDX = 64
WIDTH_IDX = N_HEADS_IDX * HEAD_DIM_IDX
TOPK_MAX = 256
N_BUCKETS = 32
MAX_DISTANCE = 128
N_HEADS_M = 4
HEAD_DIM_M = 128
WIDTH_M = N_HEADS_M * HEAD_DIM_M
CONV_WIDTH = 4
N_GROUPS = 4
EXPERTS_PER_GROUP = 4
N_EXPERTS = N_GROUPS * EXPERTS_PER_GROUP
D_EXPERT = 512
LN_EPS = 1e-5
DEPTH = 1
ALPHA = (2.0 * DEPTH) ** 0.25

LANES = 128
SUBLANES = 8
VMEM_LIMIT = 56 * 1024 * 1024

SM_KIDX = 0
SM_WIDX = HEAD_DIM_IDX
SM_I = SM_WIDX + N_HEADS_IDX
SM_F = SM_I + N_HEADS_M

Q_TILE = 256
K_CHUNK = Q_TILE
ONES_ROWS = 16
M_CHUNK = 128
M_GROUPS = 2
CONV_HALO = 16
assert CONV_HALO >= CONV_WIDTH - 1
MOE_TILE = 256
ROW_UNROLL = 8
assert ROW_UNROLL & (ROW_UNROLL - 1) == 0
EPG_SHIFT = EXPERTS_PER_GROUP.bit_length() - 1
assert 1 << EPG_SHIFT == EXPERTS_PER_GROUP
PAIR_A, PAIR_B = zip(*[(a, b) for a in range(EXPERTS_PER_GROUP) for b in range(a + 1, EXPERTS_PER_GROUP)])
PAIRS_PER_GROUP = len(PAIR_A)
N_CLASSES = N_GROUPS * PAIRS_PER_GROUP
CLASS_ROWS = -(-N_CLASSES // SUBLANES) * SUBLANES
BISECT_STEPS_PER_CHECK = 3
BISECT_MAX_CHECKS = 5
PEEL_BRACKET = 2.0
NEG_INF = float("-inf")
LOG2E = math.log2(math.e)


def _cparams(n_grid):
    return pltpu.CompilerParams(dimension_semantics=("arbitrary",) * n_grid,
                                vmem_limit_bytes=VMEM_LIMIT)


def _full(shape):
    nd = len(shape)
    return pl.BlockSpec(shape, lambda *_: (0,) * nd)


def _sigmoid(x):
    return 0.5 * jnp.tanh(0.5 * x) + 0.5


def _split3(x):
    hi = x.astype(BF16)
    r = x - hi.astype(F32)
    mid = r.astype(BF16)
    return hi, mid, (r - mid.astype(F32)).astype(BF16)


def _proj_kernel(x_ref, wqa, wckv, wqi, wsm, wqk, wv, wo, wga, wgm, kvg, smb,
                 qa_o, ckv_o, qi_o, sm_o, qk_o, v_o, o_o, ga_o, gm_o, ckvt_o):
    xb = x_ref[...].astype(BF16)

    def mm(w):
        return jnp.dot(xb, w[...], preferred_element_type=F32)

    qa_o[...] = mm(wqa).astype(BF16)
    c = mm(wckv)
    c = c * lax.rsqrt(jnp.mean(c * c, axis=-1, keepdims=True) + LN_EPS) * kvg[...]
    ckv_o[...] = c.astype(BF16)
    ckvt_o[0:KV_RANK, :] = c.T.astype(BF16)
    ckvt_o[KV_RANK:KV_RANK + ONES_ROWS, :] = jnp.ones((ONES_ROWS, c.shape[0]), BF16)
    qi_o[...] = mm(wqi).astype(BF16)
    sm_o[...] = mm(wsm) + smb[...]
    qk_o[...] = mm(wqk).astype(BF16)
    v_o[...] = mm(wv).astype(BF16)
    o_o[...] = mm(wo).astype(BF16)
    ga_o[...] = mm(wga).astype(BF16)
    gm_o[...] = mm(wgm).astype(BF16)


def _proj(x2, ws, kvg, smb, tm, seq):
    T = x2.shape[0]
    per_seq = seq // tm
    widths = [w.shape[1] for w in ws]
    dts = [BF16, BF16, BF16, F32, BF16, BF16, BF16, BF16, BF16]
    in_specs = [pl.BlockSpec((tm, D_MODEL), lambda i: (i, 0))]
    in_specs += [_full(w.shape) for w in ws]
    in_specs += [_full(kvg.shape), _full(smb.shape)]
    out_specs = [pl.BlockSpec((tm, n), lambda i: (i, 0)) for n in widths]
    out_specs.append(pl.BlockSpec((None, KV_RANK + ONES_ROWS, tm), lambda i: (i // per_seq, 0, i % per_seq)))
    out_shape = [jax.ShapeDtypeStruct((T, n), dt) for n, dt in zip(widths, dts)]
    out_shape.append(jax.ShapeDtypeStruct((T // seq, KV_RANK + ONES_ROWS, seq), BF16))
    return pl.pallas_call(
        _proj_kernel, grid=(T // tm,), in_specs=in_specs, out_specs=out_specs, out_shape=out_shape,
        compiler_params=_cparams(1), name="proj")(x2, *ws, kvg, smb)


def _dsa_kernel(qi_ref, smq_ref, smk_ref, qa_ref, ckv_ref, ckvt_ref, wuk_ref, wuvt_ref, tz_ref,
                oa_ref, qs_ref, sc_ref, am_ref, ql_ref, x_ref, acc_ref, m_ref, *, topk):
    tq, kc = Q_TILE, K_CHUNK
    nh = N_HEADS_A
    qb = pl.program_id(1)
    n_ch = qb + 1
    seq_keys = n_ch * kc
    t0 = qb * tq

    def rows(c):
        return pl.ds(pl.multiple_of(c * kc, kc), kc)

    def lanes(h):
        return slice(h * tq, (h + 1) * tq)

    def chunk_loop(n, body, init):
        def pair(p, carry):
            return body(2 * p + 1, body(2 * p, carry))
        carry = lax.fori_loop(0, jnp.right_shift(n, 1), pair, init)
        return lax.cond(jnp.bitwise_and(n, 1) == 1, lambda c: body(n - 1, c), lambda c: c, carry)

    for h in range(N_HEADS_IDX):
        qs_ref[h * tq:(h + 1) * tq, :] = qi_ref[:, h * HEAD_DIM_IDX:(h + 1) * HEAD_DIM_IDX]
    w_t = smq_ref[...].T
    q_pos = lax.broadcasted_iota(jnp.int32, (1, tq), 1) + t0
    key_iota = lax.broadcasted_iota(jnp.int32, (kc, tq), 0)

    def score_chunk(c, carry):
        mx, mn = carry
        kk = smk_ref[rows(c), SM_KIDX:SM_KIDX + HEAD_DIM_IDX].astype(BF16)
        dots = lax.dot_general(kk, qs_ref[...], (((1,), (1,)), ((), ())), preferred_element_type=F32)
        sc = jnp.zeros((kc, tq), F32)
        for h in range(N_HEADS_IDX):
            sc = sc + w_t[SM_WIDX + h:SM_WIDX + h + 1, :] * jnp.maximum(dots[:, lanes(h)], 0.0)
        vis = (key_iota + c * kc) <= q_pos
        sc_ref[rows(c), :] = jnp.where(vis, sc, NEG_INF)
        mx = jnp.maximum(mx, jnp.max(jnp.where(vis, sc, NEG_INF), axis=0, keepdims=True))
        mn = jnp.minimum(mn, jnp.min(jnp.where(vis, sc, jnp.inf), axis=0, keepdims=True))
        return mx, mn

    mx, mn = chunk_loop(n_ch, score_chunk,
                        (jnp.full((1, tq), NEG_INF, F32), jnp.full((1, tq), jnp.inf, F32)))

    n_vis = (q_pos + 1).astype(F32)
    k_row = jnp.minimum(n_vis, float(topk))

    def count(pred):
        def body(c, a):
            hit = pred(sc_ref[rows(c), :]).astype(F32)
            return a + jnp.sum(hit.reshape(kc // SUBLANES, SUBLANES, tq), axis=0)
        a = chunk_loop(n_ch, body, jnp.zeros((SUBLANES, tq), F32))
        return jnp.sum(a, axis=0, keepdims=True)

    def any_lane(flag):
        return jnp.max(jnp.where(flag, 1.0, 0.0)) > 0.0

    def crowded(cnt_lo, c_hi):
        return any_lane(jnp.logical_and(cnt_lo != k_row, cnt_lo - c_hi > PEEL_BRACKET))

    def bisect_cond(carry):
        it, lo, hi, cnt_lo, c_hi = carry
        return jnp.logical_and(it < BISECT_MAX_CHECKS, crowded(cnt_lo, c_hi))

    def bisect_body(carry):
        it, lo, hi, cnt_lo, c_hi = carry
        for _ in range(BISECT_STEPS_PER_CHECK):
            mid = lo * 0.5 + hi * 0.5
            cnt = count(lambda s: s >= mid)
            ge = cnt >= k_row
            lo, cnt_lo = jnp.where(ge, mid, lo), jnp.where(ge, cnt, cnt_lo)
            hi, c_hi = jnp.where(ge, hi, mid), jnp.where(ge, c_hi, cnt)
        return it + 1, lo, hi, cnt_lo, c_hi

    hi0 = mx + jnp.maximum(jnp.abs(mx), 1e-30) * 1e-6
    _, lo, hi, cnt_lo, c_hi = lax.while_loop(
        bisect_cond, bisect_body, (jnp.int32(0), mn, hi0, n_vis, jnp.zeros((1, tq), F32)))

    def peel_cond(carry):
        it, lo, hi, cnt_lo, c_hi, done = carry
        return jnp.logical_and(it < seq_keys, any_lane(done == 0.0))

    def peel_body(carry):
        it, lo, hi, cnt_lo, c_hi, done = carry

        def top_body(c, v):
            s = sc_ref[rows(c), :]
            inside = jnp.logical_and(s >= lo, s < hi)
            return jnp.maximum(v, jnp.max(jnp.where(inside, s, NEG_INF), axis=0, keepdims=True))

        v = chunk_loop(n_ch, top_body, jnp.full((1, tq), NEG_INF, F32))
        c_v = count(lambda s: s >= v)
        reached = c_v >= k_row
        live = done == 0.0
        fin = jnp.logical_and(live, reached)
        cut = jnp.logical_and(live, jnp.logical_not(reached))
        return (it + 1, jnp.where(fin, v, lo), jnp.where(cut, v, hi), jnp.where(fin, c_v, cnt_lo),
                jnp.where(cut, c_v, c_hi), jnp.where(fin, 1.0, done))

    _, lo, hi, cnt_lo, c_hi, _ = lax.while_loop(
        peel_cond, peel_body,
        (jnp.int32(0), lo, hi, cnt_lo, c_hi, jnp.where(cnt_lo == k_row, 1.0, 0.0)))

    def mask_chunk(c, _):
        am_ref[rows(c), :] = jnp.where(sc_ref[rows(c), :] >= lo, 0.0, NEG_INF)
        return 0

    chunk_loop(n_ch, mask_chunk, 0)

    tied = cnt_lo != k_row
    for part in range(tq // LANES):
        ls = slice(part * LANES, (part + 1) * LANES)

        @pl.when(any_lane(tied[:, ls]))
        def _ties():
            need, lo_p, hi_p = (k_row - c_hi)[:, ls], lo[:, ls], hi[:, ls]
            lower = (lax.broadcasted_iota(jnp.int32, (kc, kc), 1)
                     < lax.broadcasted_iota(jnp.int32, (kc, kc), 0)).astype(BF16)

            def tie_chunk(c, before):
                s = sc_ref[rows(c), ls]
                above = s >= hi_p
                tie = jnp.logical_and(s >= lo_p, jnp.logical_not(above))
                tie_f = tie.astype(F32)
                rank = jnp.dot(lower, tie_f.astype(BF16), preferred_element_type=F32) + before
                sel = jnp.logical_or(above, jnp.logical_and(tie, rank < need))
                am_ref[rows(c), ls] = jnp.where(sel, 0.0, NEG_INF)
                return before + jnp.sum(tie_f, axis=0, keepdims=True)

            chunk_loop(n_ch, tie_chunk, jnp.zeros((1, LANES), F32))

    for h in range(nh):
        qh = qa_ref[:, h * HEAD_DIM_A:(h + 1) * HEAD_DIM_A]
        qlt = lax.dot_general(wuk_ref[h], qh, (((1,), (1,)), ((), ())), preferred_element_type=F32)
        ql_ref[:, lanes(h)] = (qlt * (HEAD_DIM_A ** -0.5 * LOG2E)).astype(BF16)

    m_ref[...] = jnp.full(m_ref.shape, NEG_INF, F32)

    def pass_a(c, table):
        raw = jnp.dot(ckv_ref[rows(c), :], ql_ref[...], preferred_element_type=F32)
        am = am_ref[rows(c), :]
        for h in range(nh):
            x = raw[:, lanes(h)] + am
            if table is not None:
                x = x + tz_ref[table, h]
            x_ref[rows(c), lanes(h)] = x
            m_ref[:, lanes(h)] = jnp.maximum(m_ref[:, lanes(h)], jnp.max(x, axis=0, keepdims=True))

    def far_chunk(c, _):
        pass_a(c, None)
        return 0

    chunk_loop(jnp.maximum(qb - 1, 0), far_chunk, 0)

    @pl.when(qb >= 1)
    def _prev():
        pass_a(qb - 1, 1)

    pass_a(qb, 0)

    def pv(c):
        p = jnp.exp2((x_ref[rows(c), :] - m_ref[...]).astype(BF16))
        return jnp.dot(ckvt_ref[:, rows(c)], p, preferred_element_type=F32)

    def pass_b(c, _):
        acc_ref[...] += pv(c)
        return 0

    acc_ref[...] = pv(qb)
    chunk_loop(qb, pass_b, 0)
    inv_l = 1.0 / acc_ref[KV_RANK:KV_RANK + 1, :]
    o_lat = (acc_ref[0:KV_RANK, :] * inv_l).astype(BF16)
    outs = [jnp.dot(wuvt_ref[h], o_lat[:, lanes(h)], preferred_element_type=F32) for h in range(nh)]
    oa_ref[...] = jnp.concatenate(outs, axis=0).T.astype(BF16)


def _dsa(qi, sm, qa, ckv, ckv_t, wuk_t, wuv_t, tz, batch, seq):
    tq = Q_TILE
    nq = seq // tq
    topk = min(TOPK_MAX, seq // 4)
    T = batch * seq
    kern = functools.partial(_dsa_kernel, topk=topk)
    blk_q = lambda n: pl.BlockSpec((tq, n), lambda b, q: (b * nq + q, 0))
    blk_s = lambda n: pl.BlockSpec((seq, n), lambda b, q: (b, 0))
    in_specs = [
        blk_q(WIDTH_IDX), blk_q(LANES), blk_s(LANES), blk_q(WIDTH_A), blk_s(KV_RANK),
        pl.BlockSpec((None, KV_RANK + ONES_ROWS, seq), lambda b, q: (b, 0, 0)),
        _full(wuk_t.shape), _full(wuv_t.shape), _full(tz.shape),
    ]
    scratch = [
        pltpu.VMEM((N_HEADS_IDX * tq, HEAD_DIM_IDX), BF16),
        pltpu.VMEM((seq, tq), F32),
        pltpu.VMEM((seq, tq), F32),
        pltpu.VMEM((KV_RANK, N_HEADS_A * tq), BF16),
        pltpu.VMEM((seq, N_HEADS_A * tq), F32),
        pltpu.VMEM((KV_RANK + ONES_ROWS, N_HEADS_A * tq), F32),
        pltpu.VMEM((1, N_HEADS_A * tq), F32),
    ]
    return pl.pallas_call(
        kern, grid=(batch, nq), in_specs=in_specs,
        out_specs=pl.BlockSpec((tq, WIDTH_A), lambda b, q: (b * nq + q, 0)),
        out_shape=jax.ShapeDtypeStruct((T, WIDTH_A), BF16),
        scratch_shapes=scratch, compiler_params=_cparams(2), name="dsa")(
            qi, sm, sm, qa, ckv, ckv_t, wuk_t, wuv_t, tz)


def _t5_bucket(dist):
    max_exact = N_BUCKETS // 2
    d = jnp.maximum(dist, 0)
    ratio = jnp.log(jnp.maximum(d, 1).astype(F32) / max_exact) / math.log(MAX_DISTANCE / max_exact)
    large = jnp.minimum(max_exact + (ratio * (N_BUCKETS - max_exact)).astype(jnp.int32), N_BUCKETS - 1)
    return jnp.where(d < max_exact, d, large)


def _bias_tables(rel_bias):
    tq = Q_TILE
    span = 2 * tq
    assert int(_np_bucket(tq)) == N_BUCKETS - 1
    far = rel_bias[N_BUCKETS - 1]
    by_dist = (rel_bias[_t5_bucket(jnp.arange(span))] - far).astype(F32) * LOG2E
    diag = jnp.concatenate([by_dist[:tq], jnp.zeros_like(by_dist[:tq])])
    prev = jnp.concatenate([by_dist[tq:], by_dist[:tq]])

    def toeplitz(f):
        m = jnp.tile(f, (tq, 1))[:tq * (span - 1)].reshape(tq, span - 1, N_HEADS_A)
        return m[:, :tq]

    tz = jnp.stack([toeplitz(diag), toeplitz(prev)])
    return jnp.moveaxis(tz, -1, 1)


def _np_bucket(d):
    max_exact = N_BUCKETS // 2
    ratio = np.log(np.float32(max(d, 1)) / np.float32(max_exact)) / math.log(MAX_DISTANCE / max_exact)
    return min(max_exact + int(ratio * (N_BUCKETS - max_exact)), N_BUCKETS - 1) if d >= max_exact else d


def _mlstm_kernel(qk_ref, halo_ref, v_ref, sm_ref, o_ref, cw_ref, cb_ref, g_ref,
                  out_ref, c_ref, m_ref):
    c_idx = pl.program_id(1)

    @pl.when(c_idx == 0)
    def _init():
        c_ref[...] = jnp.zeros_like(c_ref)
        m_ref[...] = jnp.zeros_like(m_ref)

    for g in range(M_GROUPS):
        gates = _mlstm_gates(g, c_idx, qk_ref, halo_ref, sm_ref, cw_ref, cb_ref)
        outs = [_mlstm_head(g, h, gates, v_ref, o_ref, g_ref, c_ref, m_ref) for h in range(N_HEADS_M)]
        out_ref[g] = jnp.concatenate(outs, axis=-1).astype(BF16)


def _mlstm_gates(g, c_idx, qk_ref, halo_ref, sm_ref, cw_ref, cb_ref):
    L = M_CHUNK

    hw = halo_ref.shape[1]
    halo = jnp.where(c_idx > 0, halo_ref[g].astype(F32), 0.0)
    ext = jnp.concatenate([halo, qk_ref[g].astype(F32)], axis=0)
    acc = jnp.zeros((L, 2 * WIDTH_M), F32) + cb_ref[...]
    for w in range(CONV_WIDTH):
        off = hw - (CONV_WIDTH - 1) + w
        acc = acc + ext[off:off + L, :] * cw_ref[w:w + 1, :]
    qk = acc * _sigmoid(acc)

    sm = sm_ref[g]
    sm_t = sm.T
    r_i = lax.broadcasted_iota(jnp.int32, (L, L), 0)
    c_i = lax.broadcasted_iota(jnp.int32, (L, L), 1)
    tril = (c_i <= r_i).astype(BF16)
    bcum_c = sum(jnp.dot(tril, part, preferred_element_type=F32)
                 for part in _split3(jax.nn.log_sigmoid(sm)))
    return qk, sm, sm_t, bcum_c, bcum_c.T, c_i <= r_i


def _mlstm_head(g, h, gates, v_ref, o_ref, g_ref, c_ref, m_ref):
    L = M_CHUNK
    dm = HEAD_DIM_M
    qk, sm, sm_t, bcum_c, bcum_r, causal = gates
    ones_col = (lax.broadcasted_iota(jnp.int32, (L, dm), 1) == 0).astype(BF16)
    st = g * N_HEADS_M + h
    q = qk[:, h * dm:(h + 1) * dm].astype(BF16)
    k = (qk[:, WIDTH_M + h * dm:WIDTH_M + (h + 1) * dm] * (dm ** -0.5))
    v_aug = jnp.concatenate([v_ref[g, :, h * dm:(h + 1) * dm], ones_col], axis=-1)
    b_col = bcum_c[:, SM_F + h:SM_F + h + 1]
    g_col = sm[:, SM_I + h:SM_I + h + 1] - b_col
    g_row = sm_t[SM_I + h:SM_I + h + 1, :] - bcum_r[SM_F + h:SM_F + h + 1, :]
    b_last = b_col[L - 1:L, :]
    m_prev = m_ref[st]
    c_prev = c_ref[st]

    log_d = jnp.where(causal, b_col + g_row, NEG_INF)
    m_j = jnp.maximum(b_col + m_prev, jnp.max(log_d, axis=-1, keepdims=True))
    w_inter = jnp.exp(b_col + m_prev - m_j)
    qkt = lax.dot_general(q, k.astype(BF16), (((1,), (1,)), ((), ())), preferred_element_type=F32)
    s = qkt * jnp.exp(log_d - m_j)
    o_aug = jnp.dot(s.astype(BF16), v_aug, preferred_element_type=F32) + \
        w_inter * jnp.dot(q, c_prev.astype(BF16), preferred_element_type=F32)
    num = o_aug[:, :dm]
    den = o_aug[:, dm:dm + 1]
    hh = num / jnp.maximum(jnp.abs(den), jnp.exp(-m_j))

    lwe = b_last + g_col
    m_loc = jnp.max(lwe, axis=0, keepdims=True)
    kw = (k * jnp.exp(lwe - m_loc)).astype(BF16)
    c_loc = lax.dot_general(kw, v_aug, (((0,), (0,)), ((), ())), preferred_element_type=F32)
    m_new = jnp.maximum(b_last + m_prev, m_loc)
    c_ref[st] = jnp.exp(b_last + m_prev - m_new) * c_prev + jnp.exp(m_loc - m_new) * c_loc
    m_ref[st] = m_new

    hn = hh * lax.rsqrt(jnp.mean(hh * hh, axis=-1, keepdims=True) + LN_EPS) * g_ref[:, h * dm:(h + 1) * dm]
    og = o_ref[g, :, h * dm:(h + 1) * dm].astype(F32)
    return _sigmoid(og) * hn


def _mlstm(qk, v, sm, o, conv_w, conv_b, mh_g, batch, seq):
    L = M_CHUNK
    G = M_GROUPS
    nc = seq // L
    T = batch * seq
    hb = L // CONV_HALO
    grouped = lambda a: a.reshape(G, T // G, a.shape[-1])
    blk = lambda n: pl.BlockSpec((G, L, n), lambda b, c: (0, b * nc + c, 0))
    in_specs = [
        blk(2 * WIDTH_M),
        pl.BlockSpec((G, CONV_HALO, 2 * WIDTH_M), lambda b, c: (0, jnp.maximum((b * nc + c) * hb - 1, 0), 0)),
        blk(WIDTH_M), blk(LANES), blk(WIDTH_M),
        _full(conv_w.shape), _full(conv_b.shape), _full(mh_g.shape),
    ]
    scratch = [pltpu.VMEM((G * N_HEADS_M, HEAD_DIM_M, 2 * HEAD_DIM_M), F32),
               pltpu.VMEM((G * N_HEADS_M, 1, 1), F32)]
    qk_g = grouped(qk)
    out = pl.pallas_call(
        _mlstm_kernel, grid=(batch // G, nc), in_specs=in_specs, out_specs=blk(WIDTH_M),
        out_shape=jax.ShapeDtypeStruct((G, T // G, WIDTH_M), BF16), scratch_shapes=scratch,
        compiler_params=_cparams(2), name="mlstm")(
            qk_g, qk_g, grouped(v), grouped(sm), grouped(o), conv_w, conv_b, mh_g)
    return out.reshape(T, WIDTH_M)


def _layer_norm(y, g, b):
    mu = jnp.mean(y, axis=-1, keepdims=True)
    var = jnp.mean(jnp.square(y - mu), axis=-1, keepdims=True)
    return (y - mu) * lax.rsqrt(var + LN_EPS) * g + b


def _router_logits(x1, wr, br):
    x1_hi = x1.astype(BF16)
    x1_lo = (x1 - x1_hi.astype(F32)).astype(BF16)
    return (jnp.dot(x1_hi, wr[0], preferred_element_type=F32)
            + jnp.dot(x1_lo, wr[0], preferred_element_type=F32)
            + jnp.dot(x1_hi, wr[1], preferred_element_type=F32)) + br[...]


def _merge_kernel(x_ref, oa_ref, hm_ref, ga_ref, gm_ref, wua, wum, wout, l1g, l1b, wr, br,
                  x1_ref, route_ref, cnt_ref):
    pa = jnp.dot(oa_ref[...], wua[...], preferred_element_type=F32)
    pm = jnp.dot(hm_ref[...], wum[...], preferred_element_type=F32)
    y = _sigmoid(ga_ref[...]) * pa.astype(BF16) + _sigmoid(gm_ref[...]) * pm.astype(BF16)
    mix = jnp.dot(y.astype(BF16), wout[...], preferred_element_type=F32)
    x1 = _layer_norm(ALPHA * x_ref[...] + mix, l1g[...], l1b[...])
    x1_ref[...] = x1

    lt = _router_logits(x1, wr, br).T
    tm = lt.shape[1]
    big = jnp.int32(LANES)
    le = lt[0:N_EXPERTS, :]
    lg = lt[N_EXPERTS:N_EXPERTS + SUBLANES, :]
    row_g = lax.broadcasted_iota(jnp.int32, lg.shape, 0)
    row_e = lax.broadcasted_iota(jnp.int32, le.shape, 0)
    is_grp = row_g < N_GROUPS
    gl = jnp.where(is_grp, lg, NEG_INF)
    ge = jnp.exp(gl - jnp.max(gl, axis=0, keepdims=True))
    gp = ge / jnp.sum(ge, axis=0, keepdims=True)
    g_w = jnp.max(gp, axis=0, keepdims=True)
    g_idx = jnp.min(jnp.where(jnp.logical_and(is_grp, gp == g_w), row_g, big), axis=0, keepdims=True)
    el = jnp.where(jnp.right_shift(row_e, EPG_SHIFT) == g_idx, le, NEG_INF)
    m1 = jnp.max(el, axis=0, keepdims=True)
    i1 = jnp.min(jnp.where(el == m1, row_e, big), axis=0, keepdims=True)
    el2 = jnp.where(row_e == i1, NEG_INF, el)
    m2 = jnp.max(el2, axis=0, keepdims=True)
    i2 = jnp.min(jnp.where(el2 == m2, row_e, big), axis=0, keepdims=True)

    a = jnp.bitwise_and(jnp.minimum(i1, i2), EXPERTS_PER_GROUP - 1)
    b = jnp.bitwise_and(jnp.maximum(i1, i2), EXPERTS_PER_GROUP - 1)
    pair = jnp.right_shift(a * (2 * EXPERTS_PER_GROUP - 1 - a), 1) + (b - a - 1)
    cls = g_idx * PAIRS_PER_GROUP + pair
    row_c = lax.broadcasted_iota(jnp.int32, (CLASS_ROWS, tm), 0)
    onehot = (row_c == cls).astype(F32)

    @pl.when(pl.program_id(0) == 0)
    def _init():
        cnt_ref[...] = jnp.zeros_like(cnt_ref)

    earlier = (lax.broadcasted_iota(jnp.int32, (tm, tm), 0)
               < lax.broadcasted_iota(jnp.int32, (tm, tm), 1)).astype(BF16)
    prior = jnp.dot(onehot.astype(BF16), earlier, preferred_element_type=F32) + cnt_ref[:, 0:1]
    rank = jnp.sum(prior * onehot, axis=0, keepdims=True)
    cnt_ref[...] += jnp.sum(onehot, axis=1, keepdims=True)
    row_o = lax.broadcasted_iota(jnp.int32, route_ref.shape, 0)
    route_ref[...] = jnp.where(row_o == 0, cls.astype(F32), jnp.where(row_o == 1, rank, 0.0))


def _merge(x2, oa, hm, ga, gm, wua, wum, wout, l1g, l1b, wr, br, tm):
    T = x2.shape[0]
    blk = lambda n: pl.BlockSpec((tm, n), lambda i: (i, 0))
    in_specs = [blk(D_MODEL), blk(WIDTH_A), blk(WIDTH_M), blk(D_MODEL), blk(D_MODEL),
                _full(wua.shape), _full(wum.shape), _full(wout.shape), _full(l1g.shape), _full(l1b.shape),
                _full(wr.shape), _full(br.shape)]
    out_specs = [blk(D_MODEL), pl.BlockSpec((SUBLANES, tm), lambda i: (0, i)), _full((CLASS_ROWS, LANES))]
    out_shape = [jax.ShapeDtypeStruct((T, D_MODEL), F32), jax.ShapeDtypeStruct((SUBLANES, T), F32),
                 jax.ShapeDtypeStruct((CLASS_ROWS, LANES), F32)]
    return pl.pallas_call(
        _merge_kernel, grid=(T // tm,), in_specs=in_specs, out_specs=out_specs, out_shape=out_shape,
        compiler_params=_cparams(1), name="merge")(x2, oa, hm, ga, gm, wua, wum, wout, l1g, l1b, wr, br)


def _moe_kernel(ta_ref, tb_ref, nv_ref, tok_ref, tokn_ref, x1_hbm, wga, wua, wda, wgb, wub, wdb, wr, br, l2g, l2b,
                out_hbm, xbuf, obuf, gsem, ssem):
    i = pl.program_id(0)
    n_tiles = pl.num_programs(0)
    slot = lax.rem(i, 2)
    tm = MOE_TILE

    def for_rows(n, start_row):
        groups = jnp.right_shift(n, ROW_UNROLL.bit_length() - 1)

        def group(g, _):
            for r in range(ROW_UNROLL):
                start_row(g * ROW_UNROLL + r)
            return 0

        def single(j, _):
            start_row(j)
            return 0

        lax.fori_loop(0, groups, group, 0)
        lax.fori_loop(groups * ROW_UNROLL, n, single, 0)

    def gather(tok, n, s):
        for_rows(n, lambda j: pltpu.make_async_copy(
            x1_hbm.at[pl.ds(tok[0, j], 1)], xbuf.at[s, pl.ds(j, 1)], gsem.at[s]).start())

    def wait_n(n, copy):
        bulk = pl.multiple_of(jnp.bitwise_and(n, -SUBLANES), SUBLANES)

        @pl.when(bulk > 0)
        def _():
            copy(pl.ds(0, bulk)).wait()

        for r in range(SUBLANES - 1):
            @pl.when(bulk + r < n)
            def _():
                copy(pl.ds(0, 1)).wait()

    def wait_gather(n, s):
        wait_n(n, lambda rows: pltpu.make_async_copy(x1_hbm.at[rows], xbuf.at[s, rows], gsem.at[s]))

    def wait_scatter(n, s):
        wait_n(n, lambda rows: pltpu.make_async_copy(obuf.at[s, rows], out_hbm.at[rows], ssem.at[s]))

    @pl.when(i == 0)
    def _first():
        xbuf[...] = jnp.zeros_like(xbuf)
        gather(tok_ref, nv_ref[0], 0)

    @pl.when(i + 1 < n_tiles)
    def _ahead():
        gather(tokn_ref, nv_ref[jnp.minimum(i + 1, n_tiles - 1)], 1 - slot)

    n = nv_ref[i]
    wait_gather(n, slot)

    @pl.when(i >= 2)
    def _drain():
        wait_scatter(nv_ref[jnp.maximum(i - 2, 0)], slot)

    @pl.when(n > 0)
    def _compute():
        x = xbuf[slot]
        xb = x.astype(BF16)
        ea, eb = ta_ref[i], tb_ref[i]
        logits = _router_logits(x, wr, br)
        lane = lax.broadcasted_iota(jnp.int32, logits.shape, 1)

        def pick(idx):
            return jnp.sum(jnp.where(lane == idx, logits, 0.0), axis=-1, keepdims=True)

        la, lb, lg = pick(ea), pick(eb), pick(N_EXPERTS + jnp.right_shift(ea, EPG_SHIFT))
        is_grp = jnp.logical_and(lane >= N_EXPERTS, lane < N_EXPERTS + N_GROUPS)
        g_w = 1.0 / jnp.sum(jnp.where(is_grp, jnp.exp(logits - lg), 0.0), axis=-1, keepdims=True)
        w_a = g_w / (1.0 + jnp.exp(lb - la))
        w_b = g_w / (1.0 + jnp.exp(la - lb))

        def expert(wg, wu, wd):
            g = jnp.dot(xb, wg[...], preferred_element_type=F32)
            u = jnp.dot(xb, wu[...], preferred_element_type=F32)
            hdn = (g * _sigmoid(g) * u).astype(BF16)
            return jnp.dot(hdn, wd[...], preferred_element_type=F32)

        ffn = w_a * expert(wga, wua, wda) + w_b * expert(wgb, wub, wdb)
        obuf[slot] = _layer_norm(ALPHA * x + ffn, l2g[...], l2b[...])

        for_rows(n, lambda j: pltpu.make_async_copy(
            obuf.at[slot, pl.ds(j, 1)], out_hbm.at[pl.ds(tok_ref[0, j], 1)], ssem.at[slot]).start())

    @pl.when(i == n_tiles - 1)
    def _last():
        wait_scatter(n, slot)
        wait_scatter(nv_ref[jnp.maximum(i - 1, 0)], 1 - slot)


def _moe(x1, tile_a, tile_b, n_valid, src_tok, wg, wu, wd, wr, br, l2g, l2b):
    T = x1.shape[0]
    tm = MOE_TILE
    n_tiles = src_tok.shape[0]
    tok_blk = lambda f: pl.BlockSpec((None, 1, tm), f, memory_space=pltpu.SMEM)
    w_in = lambda which: pl.BlockSpec((None, D_MODEL, D_EXPERT), lambda i, ta, tb, nv: ((ta, tb)[which][i], 0, 0))
    w_out = lambda which: pl.BlockSpec((None, D_EXPERT, D_MODEL), lambda i, ta, tb, nv: ((ta, tb)[which][i], 0, 0))
    const = lambda shape: pl.BlockSpec(shape, lambda i, ta, tb, nv: (0,) * len(shape))
    grid_spec = pltpu.PrefetchScalarGridSpec(
        num_scalar_prefetch=3, grid=(n_tiles,),
        in_specs=[tok_blk(lambda i, ta, tb, nv: (i, 0, 0)),
                  tok_blk(lambda i, ta, tb, nv: (jnp.minimum(i + 1, n_tiles - 1), 0, 0)),
                  pl.BlockSpec(memory_space=pl.ANY),
                  w_in(0), w_in(0), w_out(0), w_in(1), w_in(1), w_out(1),
                  const(wr.shape), const(br.shape), const(l2g.shape), const(l2b.shape)],
        out_specs=pl.BlockSpec(memory_space=pl.ANY),
        scratch_shapes=[pltpu.VMEM((2, tm, D_MODEL), F32), pltpu.VMEM((2, tm, D_MODEL), F32),
                        pltpu.SemaphoreType.DMA((2,)), pltpu.SemaphoreType.DMA((2,))])
    return pl.pallas_call(
        _moe_kernel, grid_spec=grid_spec, out_shape=jax.ShapeDtypeStruct((T, D_MODEL), F32),
        compiler_params=_cparams(1), name="moe")(
            tile_a, tile_b, n_valid, src_tok, src_tok, x1, wg, wu, wd, wg, wu, wd, wr, br, l2g, l2b)


def _route_tables(route, counts, n_tokens):
    tm = MOE_TILE
    n_tiles = n_tokens // tm + N_CLASSES
    cnt = counts[:N_CLASSES, 0].astype(jnp.int32)
    tiles = (cnt + tm - 1) // tm
    tile_end = jnp.cumsum(tiles)
    tile_start = tile_end - tiles
    t_idx = jnp.arange(n_tiles, dtype=jnp.int32)
    cls_of_tile = jnp.minimum(jnp.sum(t_idx[:, None] >= tile_end[None, :], axis=1), N_CLASSES - 1).astype(jnp.int32)
    n_valid = jnp.clip(cnt[cls_of_tile] - (t_idx - tile_start[cls_of_tile]) * tm, 0, tm).astype(jnp.int32)
    grp, pair = cls_of_tile // PAIRS_PER_GROUP, cls_of_tile % PAIRS_PER_GROUP
    pa = jnp.asarray(PAIR_A, jnp.int32)[pair]
    pb = jnp.asarray(PAIR_B, jnp.int32)[pair]
    tile_a = grp * EXPERTS_PER_GROUP + pa
    tile_b = grp * EXPERTS_PER_GROUP + pb
    cls = route[0].astype(jnp.int32)
    rank = route[1].astype(jnp.int32)
    row0 = jnp.sum(jnp.where(cls[:, None] == jnp.arange(N_CLASSES)[None, :], (tile_start * tm)[None, :], 0), axis=1)
    pos = row0 + rank
    src = jnp.zeros((n_tiles * tm,), jnp.int32).at[pos].set(
        jnp.arange(n_tokens, dtype=jnp.int32), unique_indices=True)
    return tile_a, tile_b, n_valid, src.reshape(n_tiles, 1, tm)


def _pick_tile(T, pref):
    t = pref
    while T % t:
        t //= 2
    return t


def kernel(x, w_in, conv_w, conv_b, kv_norm_g, w_uk, w_uv, rel_bias, b_i, b_f, mh_norm_g, w_up_a, w_up_m,
           w_out, ln1_g, ln1_b, w_grp, b_grp, w_rt, b_rt, w_gate, w_up, w_down, ln2_g, ln2_b):
    B, S, _ = x.shape
    T = B * S
    assert S % Q_TILE == 0 and S % M_CHUNK == 0 and T % MOE_TILE == 0 and w_in.shape[0] == DEPTH
    tz = _bias_tables(rel_bias)
    x2 = x.reshape(T, D_MODEL)
    for l in range(DEPTH):
        w = w_in[l]
        o = np.cumsum((WIDTH_A, KV_RANK, WIDTH_IDX, HEAD_DIM_IDX, N_HEADS_IDX, 2 * WIDTH_M, WIDTH_M,
                       N_HEADS_M, N_HEADS_M, WIDTH_M, D_MODEL, D_MODEL)).tolist()
        o = [0] + o
        seg = lambda j: w[:, o[j]:o[j + 1]]
        pad = LANES - (HEAD_DIM_IDX + N_HEADS_IDX + 2 * N_HEADS_M)
        w_small = jnp.concatenate([seg(3), seg(4), seg(7), seg(8), jnp.zeros((D_MODEL, pad), w.dtype)], axis=1)
        ws = [seg(0), seg(1), seg(2), w_small, seg(5), seg(6), seg(9), seg(10), seg(11)]
        ws = [a.astype(BF16) for a in ws]
        smb = jnp.zeros((1, LANES), F32).at[0, SM_I:SM_I + N_HEADS_M].set(b_i[l]) \
            .at[0, SM_F:SM_F + N_HEADS_M].set(b_f[l])
        qa, ckv, qi, sm, qk, v, og, ga, gm, ckv_t = _proj(x2, ws, kv_norm_g[l][None, :], smb,
                                                          _pick_tile(S, 512), S)

        wuk_t = jnp.transpose(w_uk[l], (1, 0, 2)).astype(BF16)
        wuv_t = jnp.transpose(w_uv[l], (1, 2, 0)).astype(BF16)
        oa = _dsa(qi, sm, qa, ckv, ckv_t, wuk_t, wuv_t, tz, B, S)

        hm = _mlstm(qk, v, sm, og, conv_w[l], conv_b[l][None, :], mh_norm_g[l].reshape(1, WIDTH_M), B, S)

        w_router = jnp.concatenate(
            [w_rt[l], w_grp[l], jnp.zeros((D_MODEL, LANES - N_EXPERTS - N_GROUPS), F32)], axis=1)
        b_router = jnp.concatenate(
            [b_rt[l], b_grp[l], jnp.zeros((LANES - N_EXPERTS - N_GROUPS,), F32)])[None, :]
        wr_hi = w_router.astype(BF16)
        wr_split = jnp.stack([wr_hi, (w_router - wr_hi.astype(F32)).astype(BF16)])
        x1, route, counts = _merge(x2, oa, hm, ga, gm, w_up_a[l].astype(BF16), w_up_m[l].astype(BF16),
                                   w_out[l].astype(BF16), ln1_g[l][None, :], ln1_b[l][None, :],
                                   wr_split, b_router, _pick_tile(T, 256))

        tile_a, tile_b, n_valid, src_tok = _route_tables(route, counts, T)
        x2 = _moe(x1, tile_a, tile_b, n_valid, src_tok, w_gate[l].astype(BF16), w_up[l].astype(BF16),
                  w_down[l].astype(BF16), wr_split, b_router, ln2_g[l][None, :], ln2_b[l][None, :])
    return x2.reshape(B, S, D_MODEL)
```

```python
import functools
import math

import jax
import jax.numpy as jnp
import numpy as np
from jax import lax
from jax.experimental import pallas as pl
from jax.experimental.pallas import tpu as pltpu
from jax.experimental.pallas import tpu_sc as plsc

F32 = jnp.float32
BF16 = jnp.bfloat16

D_MODEL = 1024
N_HEADS_A = 8
HEAD_DIM_A = 64
WIDTH_A = N_HEADS_A * HEAD_DIM_A
KV_RANK = 256
N_HEADS_IDX = 8
HEAD_DIM_IDX = 64
WIDTH_IDX = N_HEADS_IDX * HEAD_DIM_IDX
TOPK_MAX = 256
N_BUCKETS = 32
MAX_DISTANCE = 128
N_HEADS_M = 4
HEAD_DIM_M = 128
WIDTH_M = N_HEADS_M * HEAD_DIM_M
CONV_WIDTH = 4
N_GROUPS = 4
EXPERTS_PER_GROUP = 4
N_EXPERTS = N_GROUPS * EXPERTS_PER_GROUP
D_EXPERT = 512
LN_EPS = 1e-5
DEPTH = 1
ALPHA = (2.0 * DEPTH) ** 0.25

LANES = 128
SUBLANES = 8
VMEM_LIMIT = 56 * 1024 * 1024

SM_KIDX = 0
SM_WIDX = HEAD_DIM_IDX
SM_I = SM_WIDX + N_HEADS_IDX
SM_F = SM_I + N_HEADS_M

Q_TILE = 256
K_CHUNK = Q_TILE
ONES_ROWS = 16
M_CHUNK = 128
M_GROUPS = 2
CONV_HALO = 16
assert CONV_HALO >= CONV_WIDTH - 1
MOE_TILE = 256
SC_CORES = 2
SC_SUBCORES = 16
SC_ROWS = 32
EPG_SHIFT = EXPERTS_PER_GROUP.bit_length() - 1
assert 1 << EPG_SHIFT == EXPERTS_PER_GROUP
PAIR_A, PAIR_B = zip(*[(a, b) for a in range(EXPERTS_PER_GROUP) for b in range(a + 1, EXPERTS_PER_GROUP)])
PAIRS_PER_GROUP = len(PAIR_A)
N_CLASSES = N_GROUPS * PAIRS_PER_GROUP
CLASS_ROWS = -(-N_CLASSES // SUBLANES) * SUBLANES
BISECT_STEPS_PER_CHECK = 3
BISECT_MAX_CHECKS = 5
PEEL_BRACKET = 2.0
NEG_INF = float("-inf")
LOG2E = math.log2(math.e)


def _cparams(n_grid):
    return pltpu.CompilerParams(dimension_semantics=("arbitrary",) * n_grid,
                                vmem_limit_bytes=VMEM_LIMIT)


def _full(shape):
    nd = len(shape)
    return pl.BlockSpec(shape, lambda *_: (0,) * nd)


def _sigmoid(x):
    return 0.5 * jnp.tanh(0.5 * x) + 0.5


def _split3(x):
    hi = x.astype(BF16)
    r = x - hi.astype(F32)
    mid = r.astype(BF16)
    return hi, mid, (r - mid.astype(F32)).astype(BF16)


def _proj_kernel(x_ref, wqa, wckv, wqi, wsm, wqk, wv, wo, wga, wgm, kvg, smb,
                 qa_o, ckv_o, qi_o, sm_o, qk_o, v_o, o_o, ga_o, gm_o, ckvt_o):
    xb = x_ref[...].astype(BF16)

    def mm(w):
        return jnp.dot(xb, w[...], preferred_element_type=F32)

    qa_o[...] = mm(wqa).astype(BF16)
    c = mm(wckv)
    c = c * lax.rsqrt(jnp.mean(c * c, axis=-1, keepdims=True) + LN_EPS) * kvg[...]
    ckv_o[...] = c.astype(BF16)
    ckvt_o[0:KV_RANK, :] = c.T.astype(BF16)
    ckvt_o[KV_RANK:KV_RANK + ONES_ROWS, :] = jnp.ones((ONES_ROWS, c.shape[0]), BF16)
    qi_o[...] = mm(wqi).astype(BF16)
    sm_o[...] = mm(wsm) + smb[...]
    qk_o[...] = mm(wqk).astype(BF16)
    v_o[...] = mm(wv).astype(BF16)
    o_o[...] = mm(wo).astype(BF16)
    ga_o[...] = mm(wga).astype(BF16)
    gm_o[...] = mm(wgm).astype(BF16)


def _proj(x2, ws, kvg, smb, tm, seq):
    T = x2.shape[0]
    per_seq = seq // tm
    widths = [w.shape[1] for w in ws]
    dts = [BF16, BF16, BF16, F32, BF16, BF16, BF16, BF16, BF16]
    in_specs = [pl.BlockSpec((tm, D_MODEL), lambda i: (i, 0))]
    in_specs += [_full(w.shape) for w in ws]
    in_specs += [_full(kvg.shape), _full(smb.shape)]
    out_specs = [pl.BlockSpec((tm, n), lambda i: (i, 0)) for n in widths]
    out_specs.append(pl.BlockSpec((None, KV_RANK + ONES_ROWS, tm), lambda i: (i // per_seq, 0, i % per_seq)))
    out_shape = [jax.ShapeDtypeStruct((T, n), dt) for n, dt in zip(widths, dts)]
    out_shape.append(jax.ShapeDtypeStruct((T // seq, KV_RANK + ONES_ROWS, seq), BF16))
    return pl.pallas_call(
        _proj_kernel, grid=(T // tm,), in_specs=in_specs, out_specs=out_specs, out_shape=out_shape,
        compiler_params=_cparams(1), name="proj")(x2, *ws, kvg, smb)


def _dsa_kernel(qi_ref, smq_ref, smk_ref, qa_ref, ckv_ref, ckvt_ref, wuk_ref, wuvt_ref, tz_ref,
                oa_ref, qs_ref, sc_ref, am_ref, ql_ref, x_ref, acc_ref, m_ref, *, topk):
    tq, kc = Q_TILE, K_CHUNK
    nh = N_HEADS_A
    qb = pl.program_id(1)
    n_ch = qb + 1
    seq_keys = n_ch * kc
    t0 = qb * tq

    def rows(c):
        return pl.ds(pl.multiple_of(c * kc, kc), kc)

    def lanes(h):
        return slice(h * tq, (h + 1) * tq)

    def chunk_loop(n, body, init):
        def pair(p, carry):
            return body(2 * p + 1, body(2 * p, carry))
        carry = lax.fori_loop(0, jnp.right_shift(n, 1), pair, init)
        return lax.cond(jnp.bitwise_and(n, 1) == 1, lambda c: body(n - 1, c), lambda c: c, carry)

    for h in range(N_HEADS_IDX):
        qs_ref[h * tq:(h + 1) * tq, :] = qi_ref[:, h * HEAD_DIM_IDX:(h + 1) * HEAD_DIM_IDX]
    w_t = smq_ref[...].T
    q_pos = lax.broadcasted_iota(jnp.int32, (1, tq), 1) + t0
    key_iota = lax.broadcasted_iota(jnp.int32, (kc, tq), 0)

    def score_chunk(c, carry):
        mx, mn = carry
        kk = smk_ref[rows(c), SM_KIDX:SM_KIDX + HEAD_DIM_IDX].astype(BF16)
        dots = lax.dot_general(kk, qs_ref[...], (((1,), (1,)), ((), ())), preferred_element_type=F32)
        sc = jnp.zeros((kc, tq), F32)
        for h in range(N_HEADS_IDX):
            sc = sc + w_t[SM_WIDX + h:SM_WIDX + h + 1, :] * jnp.maximum(dots[:, lanes(h)], 0.0)
        vis = (key_iota + c * kc) <= q_pos
        sc_ref[rows(c), :] = jnp.where(vis, sc, NEG_INF)
        mx = jnp.maximum(mx, jnp.max(jnp.where(vis, sc, NEG_INF), axis=0, keepdims=True))
        mn = jnp.minimum(mn, jnp.min(jnp.where(vis, sc, jnp.inf), axis=0, keepdims=True))
        return mx, mn

    mx, mn = chunk_loop(n_ch, score_chunk,
                        (jnp.full((1, tq), NEG_INF, F32), jnp.full((1, tq), jnp.inf, F32)))

    n_vis = (q_pos + 1).astype(F32)
    k_row = jnp.minimum(n_vis, float(topk))

    def count(pred):
        def body(c, a):
            hit = pred(sc_ref[rows(c), :]).astype(F32)
            return a + jnp.sum(hit.reshape(kc // SUBLANES, SUBLANES, tq), axis=0)
        a = chunk_loop(n_ch, body, jnp.zeros((SUBLANES, tq), F32))
        return jnp.sum(a, axis=0, keepdims=True)

    def any_lane(flag):
        return jnp.max(jnp.where(flag, 1.0, 0.0)) > 0.0

    def crowded(cnt_lo, c_hi):
        return any_lane(jnp.logical_and(cnt_lo != k_row, cnt_lo - c_hi > PEEL_BRACKET))

    def bisect_cond(carry):
        it, lo, hi, cnt_lo, c_hi = carry
        return jnp.logical_and(it < BISECT_MAX_CHECKS, crowded(cnt_lo, c_hi))

    def bisect_body(carry):
        it, lo, hi, cnt_lo, c_hi = carry
        for _ in range(BISECT_STEPS_PER_CHECK):
            mid = lo * 0.5 + hi * 0.5
            cnt = count(lambda s: s >= mid)
            ge = cnt >= k_row
            lo, cnt_lo = jnp.where(ge, mid, lo), jnp.where(ge, cnt, cnt_lo)
            hi, c_hi = jnp.where(ge, hi, mid), jnp.where(ge, c_hi, cnt)
        return it + 1, lo, hi, cnt_lo, c_hi

    hi0 = mx + jnp.maximum(jnp.abs(mx), 1e-30) * 1e-6
    _, lo, hi, cnt_lo, c_hi = lax.while_loop(
        bisect_cond, bisect_body, (jnp.int32(0), mn, hi0, n_vis, jnp.zeros((1, tq), F32)))

    def peel_cond(carry):
        it, lo, hi, cnt_lo, c_hi, done = carry
        return jnp.logical_and(it < seq_keys, any_lane(done == 0.0))

    def peel_body(carry):
        it, lo, hi, cnt_lo, c_hi, done = carry

        def top_body(c, v):
            s = sc_ref[rows(c), :]
            inside = jnp.logical_and(s >= lo, s < hi)
            return jnp.maximum(v, jnp.max(jnp.where(inside, s, NEG_INF), axis=0, keepdims=True))

        v = chunk_loop(n_ch, top_body, jnp.full((1, tq), NEG_INF, F32))
        c_v = count(lambda s: s >= v)
        reached = c_v >= k_row
        live = done == 0.0
        fin = jnp.logical_and(live, reached)
        cut = jnp.logical_and(live, jnp.logical_not(reached))
        return (it + 1, jnp.where(fin, v, lo), jnp.where(cut, v, hi), jnp.where(fin, c_v, cnt_lo),
                jnp.where(cut, c_v, c_hi), jnp.where(fin, 1.0, done))

    _, lo, hi, cnt_lo, c_hi, _ = lax.while_loop(
        peel_cond, peel_body,
        (jnp.int32(0), lo, hi, cnt_lo, c_hi, jnp.where(cnt_lo == k_row, 1.0, 0.0)))

    def mask_chunk(c, _):
        am_ref[rows(c), :] = jnp.where(sc_ref[rows(c), :] >= lo, 0.0, NEG_INF)
        return 0

    chunk_loop(n_ch, mask_chunk, 0)

    tied = cnt_lo != k_row
    for part in range(tq // LANES):
        ls = slice(part * LANES, (part + 1) * LANES)

        @pl.when(any_lane(tied[:, ls]))
        def _ties():
            need, lo_p, hi_p = (k_row - c_hi)[:, ls], lo[:, ls], hi[:, ls]
            lower = (lax.broadcasted_iota(jnp.int32, (kc, kc), 1)
                     < lax.broadcasted_iota(jnp.int32, (kc, kc), 0)).astype(BF16)

            def tie_chunk(c, before):
                s = sc_ref[rows(c), ls]
                above = s >= hi_p
                tie = jnp.logical_and(s >= lo_p, jnp.logical_not(above))
                tie_f = tie.astype(F32)
                rank = jnp.dot(lower, tie_f.astype(BF16), preferred_element_type=F32) + before
                sel = jnp.logical_or(above, jnp.logical_and(tie, rank < need))
                am_ref[rows(c), ls] = jnp.where(sel, 0.0, NEG_INF)
                return before + jnp.sum(tie_f, axis=0, keepdims=True)

            chunk_loop(n_ch, tie_chunk, jnp.zeros((1, LANES), F32))

    for h in range(nh):
        qh = qa_ref[:, h * HEAD_DIM_A:(h + 1) * HEAD_DIM_A]
        qlt = lax.dot_general(wuk_ref[h], qh, (((1,), (1,)), ((), ())), preferred_element_type=F32)
        ql_ref[:, lanes(h)] = (qlt * (HEAD_DIM_A ** -0.5 * LOG2E)).astype(BF16)

    m_ref[...] = jnp.full(m_ref.shape, NEG_INF, F32)

    def pass_a(c, table):
        raw = jnp.dot(ckv_ref[rows(c), :], ql_ref[...], preferred_element_type=F32)
        am = am_ref[rows(c), :]
        for h in range(nh):
            x = raw[:, lanes(h)] + am
            if table is not None:
                x = x + tz_ref[table, h]
            x_ref[rows(c), lanes(h)] = x
            m_ref[:, lanes(h)] = jnp.maximum(m_ref[:, lanes(h)], jnp.max(x, axis=0, keepdims=True))

    def far_chunk(c, _):
        pass_a(c, None)
        return 0

    chunk_loop(jnp.maximum(qb - 1, 0), far_chunk, 0)

    @pl.when(qb >= 1)
    def _prev():
        pass_a(qb - 1, 1)

    pass_a(qb, 0)

    def pv(c):
        p = jnp.exp2((x_ref[rows(c), :] - m_ref[...]).astype(BF16))
        return jnp.dot(ckvt_ref[:, rows(c)], p, preferred_element_type=F32)

    def pass_b(c, _):
        acc_ref[...] += pv(c)
        return 0

    acc_ref[...] = pv(qb)
    chunk_loop(qb, pass_b, 0)
    inv_l = 1.0 / acc_ref[KV_RANK:KV_RANK + 1, :]
    o_lat = (acc_ref[0:KV_RANK, :] * inv_l).astype(BF16)
    outs = [jnp.dot(wuvt_ref[h], o_lat[:, lanes(h)], preferred_element_type=F32) for h in range(nh)]
    oa_ref[...] = jnp.concatenate(outs, axis=0).T.astype(BF16)


def _dsa(qi, sm, qa, ckv, ckv_t, wuk_t, wuv_t, tz, batch, seq):
    tq = Q_TILE
    nq = seq // tq
    topk = min(TOPK_MAX, seq // 4)
    T = batch * seq
    kern = functools.partial(_dsa_kernel, topk=topk)
    blk_q = lambda n: pl.BlockSpec((tq, n), lambda b, q: (b * nq + q, 0))
    blk_s = lambda n: pl.BlockSpec((seq, n), lambda b, q: (b, 0))
    in_specs = [
        blk_q(WIDTH_IDX), blk_q(LANES), blk_s(LANES), blk_q(WIDTH_A), blk_s(KV_RANK),
        pl.BlockSpec((None, KV_RANK + ONES_ROWS, seq), lambda b, q: (b, 0, 0)),
        _full(wuk_t.shape), _full(wuv_t.shape), _full(tz.shape),
    ]
    scratch = [
        pltpu.VMEM((N_HEADS_IDX * tq, HEAD_DIM_IDX), BF16),
        pltpu.VMEM((seq, tq), F32),
        pltpu.VMEM((seq, tq), F32),
        pltpu.VMEM((KV_RANK, N_HEADS_A * tq), BF16),
        pltpu.VMEM((seq, N_HEADS_A * tq), F32),
        pltpu.VMEM((KV_RANK + ONES_ROWS, N_HEADS_A * tq), F32),
        pltpu.VMEM((1, N_HEADS_A * tq), F32),
    ]
    return pl.pallas_call(
        kern, grid=(batch, nq), in_specs=in_specs,
        out_specs=pl.BlockSpec((tq, WIDTH_A), lambda b, q: (b * nq + q, 0)),
        out_shape=jax.ShapeDtypeStruct((T, WIDTH_A), BF16),
        scratch_shapes=scratch, compiler_params=_cparams(2), name="dsa")(
            qi, sm, sm, qa, ckv, ckv_t, wuk_t, wuv_t, tz)


def _t5_bucket(dist):
    max_exact = N_BUCKETS // 2
    d = jnp.maximum(dist, 0)
    ratio = jnp.log(jnp.maximum(d, 1).astype(F32) / max_exact) / math.log(MAX_DISTANCE / max_exact)
    large = jnp.minimum(max_exact + (ratio * (N_BUCKETS - max_exact)).astype(jnp.int32), N_BUCKETS - 1)
    return jnp.where(d < max_exact, d, large)


def _bias_tables(rel_bias):
    tq = Q_TILE
    span = 2 * tq
    assert int(_np_bucket(tq)) == N_BUCKETS - 1
    far = rel_bias[N_BUCKETS - 1]
    by_dist = ((rel_bias[_t5_bucket(jnp.arange(span))] - far).astype(F32) * LOG2E).T
    diag = jnp.concatenate([by_dist[:, :tq], jnp.zeros_like(by_dist[:, :tq])], axis=1)
    prev = jnp.concatenate([by_dist[:, tq:], by_dist[:, :tq]], axis=1)

    def toeplitz(f):
        m = jnp.tile(f, (1, tq))[:, :tq * (span - 1)].reshape(N_HEADS_A, tq, span - 1)
        return m[:, :, :tq]

    return jnp.stack([toeplitz(diag), toeplitz(prev)])


def _np_bucket(d):
    max_exact = N_BUCKETS // 2
    ratio = np.log(np.float32(max(d, 1)) / np.float32(max_exact)) / math.log(MAX_DISTANCE / max_exact)
    return min(max_exact + int(ratio * (N_BUCKETS - max_exact)), N_BUCKETS - 1) if d >= max_exact else d


def _mlstm_kernel(qk_ref, halo_ref, v_ref, sm_ref, o_ref, cw_ref, cb_ref, g_ref,
                  out_ref, c_ref, m_ref):
    c_idx = pl.program_id(1)

    @pl.when(c_idx == 0)
    def _init():
        c_ref[...] = jnp.zeros_like(c_ref)
        m_ref[...] = jnp.zeros_like(m_ref)

    for g in range(M_GROUPS):
        gates = _mlstm_gates(g, c_idx, qk_ref, halo_ref, sm_ref, cw_ref, cb_ref)
        outs = [_mlstm_head(g, h, gates, v_ref, o_ref, g_ref, c_ref, m_ref) for h in range(N_HEADS_M)]
        out_ref[g] = jnp.concatenate(outs, axis=-1).astype(BF16)


def _mlstm_gates(g, c_idx, qk_ref, halo_ref, sm_ref, cw_ref, cb_ref):
    L = M_CHUNK

    hw = halo_ref.shape[1]
    halo = jnp.where(c_idx > 0, halo_ref[g].astype(F32), 0.0)
    ext = jnp.concatenate([halo, qk_ref[g].astype(F32)], axis=0)
    acc = jnp.zeros((L, 2 * WIDTH_M), F32) + cb_ref[...]
    for w in range(CONV_WIDTH):
        off = hw - (CONV_WIDTH - 1) + w
        acc = acc + ext[off:off + L, :] * cw_ref[w:w + 1, :]
    qk = acc * _sigmoid(acc)

    sm = sm_ref[g]
    sm_t = sm.T
    r_i = lax.broadcasted_iota(jnp.int32, (L, L), 0)
    c_i = lax.broadcasted_iota(jnp.int32, (L, L), 1)
    tril = (c_i <= r_i).astype(BF16)
    bcum_c = sum(jnp.dot(tril, part, preferred_element_type=F32)
                 for part in _split3(jax.nn.log_sigmoid(sm)))
    return qk, sm, sm_t, bcum_c, bcum_c.T, c_i <= r_i


def _mlstm_head(g, h, gates, v_ref, o_ref, g_ref, c_ref, m_ref):
    L = M_CHUNK
    dm = HEAD_DIM_M
    qk, sm, sm_t, bcum_c, bcum_r, causal = gates
    ones_col = (lax.broadcasted_iota(jnp.int32, (L, dm), 1) == 0).astype(BF16)
    st = g * N_HEADS_M + h
    q = qk[:, h * dm:(h + 1) * dm].astype(BF16)
    k = (qk[:, WIDTH_M + h * dm:WIDTH_M + (h + 1) * dm] * (dm ** -0.5))
    v_aug = jnp.concatenate([v_ref[g, :, h * dm:(h + 1) * dm], ones_col], axis=-1)
    b_col = bcum_c[:, SM_F + h:SM_F + h + 1]
    g_col = sm[:, SM_I + h:SM_I + h + 1] - b_col
    g_row = sm_t[SM_I + h:SM_I + h + 1, :] - bcum_r[SM_F + h:SM_F + h + 1, :]
    b_last = b_col[L - 1:L, :]
    m_prev = m_ref[st]
    c_prev = c_ref[st]

    log_d = jnp.where(causal, b_col + g_row, NEG_INF)
    m_j = jnp.maximum(b_col + m_prev, jnp.max(log_d, axis=-1, keepdims=True))
    w_inter = jnp.exp(b_col + m_prev - m_j)
    qkt = lax.dot_general(q, k.astype(BF16), (((1,), (1,)), ((), ())), preferred_element_type=F32)
    s = qkt * jnp.exp(log_d - m_j)
    o_aug = jnp.dot(s.astype(BF16), v_aug, preferred_element_type=F32) + \
        w_inter * jnp.dot(q, c_prev.astype(BF16), preferred_element_type=F32)
    num = o_aug[:, :dm]
    den = o_aug[:, dm:dm + 1]
    hh = num / jnp.maximum(jnp.abs(den), jnp.exp(-m_j))

    lwe = b_last + g_col
    m_loc = jnp.max(lwe, axis=0, keepdims=True)
    kw = (k * jnp.exp(lwe - m_loc)).astype(BF16)
    c_loc = lax.dot_general(kw, v_aug, (((0,), (0,)), ((), ())), preferred_element_type=F32)
    m_new = jnp.maximum(b_last + m_prev, m_loc)
    c_ref[st] = jnp.exp(b_last + m_prev - m_new) * c_prev + jnp.exp(m_loc - m_new) * c_loc
    m_ref[st] = m_new

    hn = hh * lax.rsqrt(jnp.mean(hh * hh, axis=-1, keepdims=True) + LN_EPS) * g_ref[:, h * dm:(h + 1) * dm]
    og = o_ref[g, :, h * dm:(h + 1) * dm].astype(F32)
    return _sigmoid(og) * hn


def _mlstm(qk, v, sm, o, conv_w, conv_b, mh_g, batch, seq):
    L = M_CHUNK
    G = M_GROUPS
    nc = seq // L
    T = batch * seq
    hb = L // CONV_HALO
    grouped = lambda a: a.reshape(G, T // G, a.shape[-1])
    blk = lambda n: pl.BlockSpec((G, L, n), lambda b, c: (0, b * nc + c, 0))
    in_specs = [
        blk(2 * WIDTH_M),
        pl.BlockSpec((G, CONV_HALO, 2 * WIDTH_M), lambda b, c: (0, jnp.maximum((b * nc + c) * hb - 1, 0), 0)),
        blk(WIDTH_M), blk(LANES), blk(WIDTH_M),
        _full(conv_w.shape), _full(conv_b.shape), _full(mh_g.shape),
    ]
    scratch = [pltpu.VMEM((G * N_HEADS_M, HEAD_DIM_M, 2 * HEAD_DIM_M), F32),
               pltpu.VMEM((G * N_HEADS_M, 1, 1), F32)]
    qk_g = grouped(qk)
    out = pl.pallas_call(
        _mlstm_kernel, grid=(batch // G, nc), in_specs=in_specs, out_specs=blk(WIDTH_M),
        out_shape=jax.ShapeDtypeStruct((G, T // G, WIDTH_M), BF16), scratch_shapes=scratch,
        compiler_params=_cparams(2), name="mlstm")(
            qk_g, qk_g, grouped(v), grouped(sm), grouped(o), conv_w, conv_b, mh_g)
    return out.reshape(T, WIDTH_M)


def _layer_norm(y, g, b):
    mu = jnp.mean(y, axis=-1, keepdims=True)
    var = jnp.mean(jnp.square(y - mu), axis=-1, keepdims=True)
    return (y - mu) * lax.rsqrt(var + LN_EPS) * g + b


def _router_logits(x1, wr, br):
    x1_hi = x1.astype(BF16)
    x1_lo = (x1 - x1_hi.astype(F32)).astype(BF16)
    return (jnp.dot(x1_hi, wr[0], preferred_element_type=F32)
            + jnp.dot(x1_lo, wr[0], preferred_element_type=F32)
            + jnp.dot(x1_hi, wr[1], preferred_element_type=F32)) + br[...]


def _merge_kernel(x_ref, oa_ref, hm_ref, ga_ref, gm_ref, wua, wum, wout, l1g, l1b, wr, br,
                  x1_ref, route_ref, cnt_ref):
    pa = jnp.dot(oa_ref[...], wua[...], preferred_element_type=F32)
    pm = jnp.dot(hm_ref[...], wum[...], preferred_element_type=F32)
    y = _sigmoid(ga_ref[...]) * pa.astype(BF16) + _sigmoid(gm_ref[...]) * pm.astype(BF16)
    mix = jnp.dot(y.astype(BF16), wout[...], preferred_element_type=F32)
    x1 = _layer_norm(ALPHA * x_ref[...] + mix, l1g[...], l1b[...])
    x1_ref[...] = x1

    lt = _router_logits(x1, wr, br).T
    tm = lt.shape[1]
    big = jnp.int32(LANES)
    le = lt[0:N_EXPERTS, :]
    lg = lt[N_EXPERTS:N_EXPERTS + SUBLANES, :]
    row_g = lax.broadcasted_iota(jnp.int32, lg.shape, 0)
    row_e = lax.broadcasted_iota(jnp.int32, le.shape, 0)
    is_grp = row_g < N_GROUPS
    gl = jnp.where(is_grp, lg, NEG_INF)
    ge = jnp.exp(gl - jnp.max(gl, axis=0, keepdims=True))
    gp = ge / jnp.sum(ge, axis=0, keepdims=True)
    g_w = jnp.max(gp, axis=0, keepdims=True)
    g_idx = jnp.min(jnp.where(jnp.logical_and(is_grp, gp == g_w), row_g, big), axis=0, keepdims=True)
    el = jnp.where(jnp.right_shift(row_e, EPG_SHIFT) == g_idx, le, NEG_INF)
    m1 = jnp.max(el, axis=0, keepdims=True)
    i1 = jnp.min(jnp.where(el == m1, row_e, big), axis=0, keepdims=True)
    el2 = jnp.where(row_e == i1, NEG_INF, el)
    m2 = jnp.max(el2, axis=0, keepdims=True)
    i2 = jnp.min(jnp.where(el2 == m2, row_e, big), axis=0, keepdims=True)

    a = jnp.bitwise_and(jnp.minimum(i1, i2), EXPERTS_PER_GROUP - 1)
    b = jnp.bitwise_and(jnp.maximum(i1, i2), EXPERTS_PER_GROUP - 1)
    pair = jnp.right_shift(a * (2 * EXPERTS_PER_GROUP - 1 - a), 1) + (b - a - 1)
    cls = g_idx * PAIRS_PER_GROUP + pair
    row_c = lax.broadcasted_iota(jnp.int32, (CLASS_ROWS, tm), 0)
    onehot = (row_c == cls).astype(F32)

    @pl.when(pl.program_id(0) == 0)
    def _init():
        cnt_ref[...] = jnp.zeros_like(cnt_ref)

    earlier = (lax.broadcasted_iota(jnp.int32, (tm, tm), 0)
               < lax.broadcasted_iota(jnp.int32, (tm, tm), 1)).astype(BF16)
    prior = jnp.dot(onehot.astype(BF16), earlier, preferred_element_type=F32) + cnt_ref[:, 0:1]
    rank = jnp.sum(prior * onehot, axis=0, keepdims=True)
    cnt_ref[...] += jnp.sum(onehot, axis=1, keepdims=True)
    row_o = lax.broadcasted_iota(jnp.int32, route_ref.shape, 0)
    route_ref[...] = jnp.where(row_o == 0, cls.astype(F32), jnp.where(row_o == 1, rank, 0.0))


def _merge(x2, oa, hm, ga, gm, wua, wum, wout, l1g, l1b, wr, br, tm):
    T = x2.shape[0]
    blk = lambda n: pl.BlockSpec((tm, n), lambda i: (i, 0))
    in_specs = [blk(D_MODEL), blk(WIDTH_A), blk(WIDTH_M), blk(D_MODEL), blk(D_MODEL),
                _full(wua.shape), _full(wum.shape), _full(wout.shape), _full(l1g.shape), _full(l1b.shape),
                _full(wr.shape), _full(br.shape)]
    out_specs = [blk(D_MODEL), pl.BlockSpec((SUBLANES, tm), lambda i: (0, i)), _full((CLASS_ROWS, LANES))]
    out_shape = [jax.ShapeDtypeStruct((T, D_MODEL), F32), jax.ShapeDtypeStruct((SUBLANES, T), F32),
                 jax.ShapeDtypeStruct((CLASS_ROWS, LANES), F32)]
    return pl.pallas_call(
        _merge_kernel, grid=(T // tm,), in_specs=in_specs, out_specs=out_specs, out_shape=out_shape,
        compiler_params=_cparams(1), name="merge")(x2, oa, hm, ga, gm, wua, wum, wout, l1g, l1b, wr, br)


def _sc_mesh():
    return plsc.VectorSubcoreMesh(core_axis_name="c", subcore_axis_name="s",
                                  num_cores=SC_CORES, num_subcores=SC_SUBCORES)


def _sc_worker_rows(n_rows):
    workers = SC_CORES * SC_SUBCORES
    assert n_rows % (workers * SC_ROWS) == 0
    per_worker = n_rows // workers
    return per_worker, (lax.axis_index("s") * SC_CORES + lax.axis_index("c")) * per_worker


def _sc_scatter_rows(rows, idx, n_out):
    n_in, width = rows.shape

    @functools.partial(
        pl.kernel, mesh=_sc_mesh(), out_type=jax.ShapeDtypeStruct((n_out, width), rows.dtype),
        scratch_types=[pltpu.VMEM((SC_ROWS,), jnp.int32), pltpu.VMEM((SC_ROWS, width), rows.dtype),
                       pltpu.SemaphoreType.DMA], name="sc_dispatch")
    def scatter(rows_hbm, idx_hbm, out_hbm, idx_v, rows_v, sem):
        per_worker, base = _sc_worker_rows(n_in)

        @pl.loop(0, per_worker // SC_ROWS)
        def _(j):
            off = base + j * SC_ROWS
            pltpu.sync_copy(idx_hbm.at[pl.ds(off, SC_ROWS)], idx_v)
            pltpu.sync_copy(rows_hbm.at[pl.ds(off, SC_ROWS)], rows_v)
            pltpu.async_copy(rows_v, out_hbm.at[idx_v], sem).wait()

    return scatter(rows, idx)


def _sc_gather_rows(table, idx):
    n_out, width = idx.shape[0], table.shape[1]

    @functools.partial(
        pl.kernel, mesh=_sc_mesh(), out_type=jax.ShapeDtypeStruct((n_out, width), table.dtype),
        scratch_types=[pltpu.VMEM((SC_ROWS,), jnp.int32), pltpu.VMEM((SC_ROWS, width), table.dtype),
                       pltpu.SemaphoreType.DMA], name="sc_combine")
    def gather(table_hbm, idx_hbm, out_hbm, idx_v, rows_v, sem):
        per_worker, base = _sc_worker_rows(n_out)

        @pl.loop(0, per_worker // SC_ROWS)
        def _(j):
            off = base + j * SC_ROWS
            pltpu.sync_copy(idx_hbm.at[pl.ds(off, SC_ROWS)], idx_v)
            pltpu.async_copy(table_hbm.at[idx_v], rows_v, sem).wait()
            pltpu.sync_copy(rows_v, out_hbm.at[pl.ds(off, SC_ROWS)])

    return gather(table, idx)


def _moe_kernel(ta_ref, tb_ref, nv_ref, xs_ref, wga, wua, wda, wgb, wub, wdb, wr, br, l2g, l2b, ys_ref):
    i = pl.program_id(0)

    @pl.when(nv_ref[i] > 0)
    def _compute():
        x = xs_ref[...]
        xb = x.astype(BF16)
        ea, eb = ta_ref[i], tb_ref[i]
        logits = _router_logits(x, wr, br)
        lane = lax.broadcasted_iota(jnp.int32, logits.shape, 1)

        def pick(idx):
            return jnp.sum(jnp.where(lane == idx, logits, 0.0), axis=-1, keepdims=True)

        la, lb, lg = pick(ea), pick(eb), pick(N_EXPERTS + jnp.right_shift(ea, EPG_SHIFT))
        is_grp = jnp.logical_and(lane >= N_EXPERTS, lane < N_EXPERTS + N_GROUPS)
        g_w = 1.0 / jnp.sum(jnp.where(is_grp, jnp.exp(logits - lg), 0.0), axis=-1, keepdims=True)
        w_a = g_w / (1.0 + jnp.exp(lb - la))
        w_b = g_w / (1.0 + jnp.exp(la - lb))

        def expert(wg, wu, wd):
            g = jnp.dot(xb, wg[...], preferred_element_type=F32)
            u = jnp.dot(xb, wu[...], preferred_element_type=F32)
            hdn = (g * _sigmoid(g) * u).astype(BF16)
            return jnp.dot(hdn, wd[...], preferred_element_type=F32)

        ffn = w_a * expert(wga, wua, wda) + w_b * expert(wgb, wub, wdb)
        ys_ref[...] = _layer_norm(ALPHA * x + ffn, l2g[...], l2b[...])


def _moe(xs, tile_a, tile_b, n_valid, wg, wu, wd, wr, br, l2g, l2b):
    tm = MOE_TILE
    n_tiles = xs.shape[0] // tm
    rows = pl.BlockSpec((tm, D_MODEL), lambda i, ta, tb, nv: (i, 0))
    w_in = lambda which: pl.BlockSpec((None, D_MODEL, D_EXPERT), lambda i, ta, tb, nv: ((ta, tb)[which][i], 0, 0))
    w_out = lambda which: pl.BlockSpec((None, D_EXPERT, D_MODEL), lambda i, ta, tb, nv: ((ta, tb)[which][i], 0, 0))
    const = lambda shape: pl.BlockSpec(shape, lambda i, ta, tb, nv: (0,) * len(shape))
    grid_spec = pltpu.PrefetchScalarGridSpec(
        num_scalar_prefetch=3, grid=(n_tiles,),
        in_specs=[rows, w_in(0), w_in(0), w_out(0), w_in(1), w_in(1), w_out(1),
                  const(wr.shape), const(br.shape), const(l2g.shape), const(l2b.shape)],
        out_specs=rows)
    return pl.pallas_call(
        _moe_kernel, grid_spec=grid_spec, out_shape=jax.ShapeDtypeStruct(xs.shape, F32),
        compiler_params=_cparams(1), name="moe")(
            tile_a, tile_b, n_valid, xs, wg, wu, wd, wg, wu, wd, wr, br, l2g, l2b)


def _route_tables(route, counts, n_tokens):
    tm = MOE_TILE
    n_tiles = n_tokens // tm + N_CLASSES
    cnt = counts[:N_CLASSES, 0].astype(jnp.int32)
    tiles = (cnt + tm - 1) // tm
    tile_end = jnp.cumsum(tiles)
    tile_start = tile_end - tiles
    t_idx = jnp.arange(n_tiles, dtype=jnp.int32)
    cls_of_tile = jnp.minimum(jnp.sum(t_idx[:, None] >= tile_end[None, :], axis=1), N_CLASSES - 1).astype(jnp.int32)
    n_valid = jnp.clip(cnt[cls_of_tile] - (t_idx - tile_start[cls_of_tile]) * tm, 0, tm).astype(jnp.int32)
    grp, pair = cls_of_tile // PAIRS_PER_GROUP, cls_of_tile % PAIRS_PER_GROUP
    pa = jnp.asarray(PAIR_A, jnp.int32)[pair]
    pb = jnp.asarray(PAIR_B, jnp.int32)[pair]
    tile_a = grp * EXPERTS_PER_GROUP + pa
    tile_b = grp * EXPERTS_PER_GROUP + pb
    cls = route[0].astype(jnp.int32)
    rank = route[1].astype(jnp.int32)
    row0 = jnp.sum(jnp.where(cls[:, None] == jnp.arange(N_CLASSES)[None, :], (tile_start * tm)[None, :], 0), axis=1)
    return tile_a, tile_b, n_valid, row0 + rank, n_tiles * tm


def _pick_tile(T, pref):
    t = pref
    while T % t:
        t //= 2
    return t


def kernel(x, w_in, conv_w, conv_b, kv_norm_g, w_uk, w_uv, rel_bias, b_i, b_f, mh_norm_g, w_up_a, w_up_m,
           w_out, ln1_g, ln1_b, w_grp, b_grp, w_rt, b_rt, w_gate, w_up, w_down, ln2_g, ln2_b):
    B, S, _ = x.shape
    T = B * S
    assert S % Q_TILE == 0 and S % M_CHUNK == 0 and T % MOE_TILE == 0 and w_in.shape[0] == DEPTH
    tz = _bias_tables(rel_bias)
    x2 = x.reshape(T, D_MODEL)
    for l in range(DEPTH):
        w = w_in[l]
        o = np.cumsum((WIDTH_A, KV_RANK, WIDTH_IDX, HEAD_DIM_IDX, N_HEADS_IDX, 2 * WIDTH_M, WIDTH_M,
                       N_HEADS_M, N_HEADS_M, WIDTH_M, D_MODEL, D_MODEL)).tolist()
        o = [0] + o
        seg = lambda j: w[:, o[j]:o[j + 1]]
        pad = LANES - (HEAD_DIM_IDX + N_HEADS_IDX + 2 * N_HEADS_M)
        w_small = jnp.concatenate([seg(3), seg(4), seg(7), seg(8), jnp.zeros((D_MODEL, pad), w.dtype)], axis=1)
        ws = [seg(0), seg(1), seg(2), w_small, seg(5), seg(6), seg(9), seg(10), seg(11)]
        ws = [a.astype(BF16) for a in ws]
        smb = jnp.zeros((1, LANES), F32).at[0, SM_I:SM_I + N_HEADS_M].set(b_i[l]) \
            .at[0, SM_F:SM_F + N_HEADS_M].set(b_f[l])
        qa, ckv, qi, sm, qk, v, og, ga, gm, ckv_t = _proj(x2, ws, kv_norm_g[l][None, :], smb,
                                                          _pick_tile(S, 512), S)

        wuk_t = jnp.transpose(w_uk[l], (1, 0, 2)).astype(BF16)
        wuv_t = jnp.transpose(w_uv[l], (1, 2, 0)).astype(BF16)
        oa = _dsa(qi, sm, qa, ckv, ckv_t, wuk_t, wuv_t, tz, B, S)

        hm = _mlstm(qk, v, sm, og, conv_w[l], conv_b[l][None, :], mh_norm_g[l].reshape(1, WIDTH_M), B, S)

        w_router = jnp.concatenate(
            [w_rt[l], w_grp[l], jnp.zeros((D_MODEL, LANES - N_EXPERTS - N_GROUPS), F32)], axis=1)
        b_router = jnp.concatenate(
            [b_rt[l], b_grp[l], jnp.zeros((LANES - N_EXPERTS - N_GROUPS,), F32)])[None, :]
        wr_hi = w_router.astype(BF16)
        wr_split = jnp.stack([wr_hi, (w_router - wr_hi.astype(F32)).astype(BF16)])
        x1, route, counts = _merge(x2, oa, hm, ga, gm, w_up_a[l].astype(BF16), w_up_m[l].astype(BF16),
                                   w_out[l].astype(BF16), ln1_g[l][None, :], ln1_b[l][None, :],
                                   wr_split, b_router, _pick_tile(T, 256))

        tile_a, tile_b, n_valid, pos, n_sorted = _route_tables(route, counts, T)
        xs = _sc_scatter_rows(x1, pos, n_sorted)
        ys = _moe(xs, tile_a, tile_b, n_valid, w_gate[l].astype(BF16), w_up[l].astype(BF16),
                  w_down[l].astype(BF16), wr_split, b_router, ln2_g[l][None, :], ln2_b[l][None, :])
        x2 = _sc_gather_rows(ys, pos)
    return x2.reshape(B, S, D_MODEL)
```

```python
import functools
import math

import jax
import jax.numpy as jnp
import numpy as np
from jax import lax
from jax.experimental import pallas as pl
from jax.experimental.pallas import tpu as pltpu
from jax.experimental.pallas import tpu_sc as plsc

F32 = jnp.float32
BF16 = jnp.bfloat16

D_MODEL = 1024
N_HEADS_A = 8
HEAD_DIM_A = 64
WIDTH_A = N_HEADS_A * HEAD_DIM_A
KV_RANK = 256
N_HEADS_IDX = 8
HEAD_DIM_IDX = 64
WIDTH_IDX = N_HEADS_IDX * HEAD_DIM_IDX
TOPK_MAX = 256
N_BUCKETS = 32
MAX_DISTANCE = 128
N_HEADS_M = 4
HEAD_DIM_M = 128
WIDTH_M = N_HEADS_M * HEAD_DIM_M
CONV_WIDTH = 4
N_GROUPS = 4
EXPERTS_PER_GROUP = 4
N_EXPERTS = N_GROUPS * EXPERTS_PER_GROUP
D_EXPERT = 512
LN_EPS = 1e-5
DEPTH = 1
ALPHA = (2.0 * DEPTH) ** 0.25

LANES = 128
SUBLANES = 8
VMEM_LIMIT = 56 * 1024 * 1024

SM_KIDX = 0
SM_WIDX = HEAD_DIM_IDX
SM_I = SM_WIDX + N_HEADS_IDX
SM_F = SM_I + N_HEADS_M

Q_TILE = 256
K_CHUNK = Q_TILE
ONES_ROWS = 16
M_CHUNK = 128
M_GROUPS = 2
CONV_HALO = 16
assert CONV_HALO >= CONV_WIDTH - 1
MOE_TILE = 256
SC_CORES = 2
SC_SUBCORES = 16
SC_ROWS = 32
EPG_SHIFT = EXPERTS_PER_GROUP.bit_length() - 1
assert 1 << EPG_SHIFT == EXPERTS_PER_GROUP
PAIR_A, PAIR_B = zip(*[(a, b) for a in range(EXPERTS_PER_GROUP) for b in range(a + 1, EXPERTS_PER_GROUP)])
PAIRS_PER_GROUP = len(PAIR_A)
N_CLASSES = N_GROUPS * PAIRS_PER_GROUP
CLASS_ROWS = -(-N_CLASSES // SUBLANES) * SUBLANES
BISECT_STEPS_PER_CHECK = 3
BISECT_MAX_CHECKS = 5
PEEL_BRACKET = 2.0
NEG_INF = float("-inf")
LOG2E = math.log2(math.e)


def _cparams(n_grid):
    return pltpu.CompilerParams(dimension_semantics=("arbitrary",) * n_grid,
                                vmem_limit_bytes=VMEM_LIMIT)


def _full(shape):
    nd = len(shape)
    return pl.BlockSpec(shape, lambda *_: (0,) * nd)


def _sigmoid(x):
    return 0.5 * jnp.tanh(0.5 * x) + 0.5


def _split3(x):
    hi = x.astype(BF16)
    r = x - hi.astype(F32)
    mid = r.astype(BF16)
    return hi, mid, (r - mid.astype(F32)).astype(BF16)


def _proj_kernel(x_ref, wqa, wckv, wqi, wsm, wqk, wv, wo, wga, wgm, kvg, smb,
                 qa_o, ckv_o, qi_o, sm_o, qk_o, v_o, o_o, ga_o, gm_o, ckvt_o):
    xb = x_ref[...].astype(BF16)

    def mm(w):
        return jnp.dot(xb, w[...], preferred_element_type=F32)

    qa_o[...] = mm(wqa).astype(BF16)
    c = mm(wckv)
    c = c * lax.rsqrt(jnp.mean(c * c, axis=-1, keepdims=True) + LN_EPS) * kvg[...]
    ckv_o[...] = c.astype(BF16)
    ckvt_o[0:KV_RANK, :] = c.T.astype(BF16)
    ckvt_o[KV_RANK:KV_RANK + ONES_ROWS, :] = jnp.ones((ONES_ROWS, c.shape[0]), BF16)
    qi_o[...] = mm(wqi).astype(BF16)
    sm_o[...] = mm(wsm) + smb[...]
    qk_o[...] = mm(wqk).astype(BF16)
    v_o[...] = mm(wv).astype(BF16)
    o_o[...] = mm(wo).astype(BF16)
    ga_o[...] = mm(wga).astype(BF16)
    gm_o[...] = mm(wgm).astype(BF16)


def _proj(x2, ws, kvg, smb, tm, seq):
    T = x2.shape[0]
    per_seq = seq // tm
    widths = [w.shape[1] for w in ws]
    dts = [BF16, BF16, BF16, F32, BF16, BF16, BF16, BF16, BF16]
    in_specs = [pl.BlockSpec((tm, D_MODEL), lambda i: (i, 0))]
    in_specs += [_full(w.shape) for w in ws]
    in_specs += [_full(kvg.shape), _full(smb.shape)]
    out_specs = [pl.BlockSpec((tm, n), lambda i: (i, 0)) for n in widths]
    out_specs.append(pl.BlockSpec((None, KV_RANK + ONES_ROWS, tm), lambda i: (i // per_seq, 0, i % per_seq)))
    out_shape = [jax.ShapeDtypeStruct((T, n), dt) for n, dt in zip(widths, dts)]
    out_shape.append(jax.ShapeDtypeStruct((T // seq, KV_RANK + ONES_ROWS, seq), BF16))
    return pl.pallas_call(
        _proj_kernel, grid=(T // tm,), in_specs=in_specs, out_specs=out_specs, out_shape=out_shape,
        compiler_params=_cparams(1), name="proj")(x2, *ws, kvg, smb)


def _dsa_kernel(qi_ref, smq_ref, smk_ref, qa_ref, ckv_ref, ckvt_ref, wuk_ref, wuvt_ref, tz_ref,
                oa_ref, qs_ref, sc_ref, am_ref, ql_ref, x_ref, acc_ref, m_ref, *, topk):
    tq, kc = Q_TILE, K_CHUNK
    nh = N_HEADS_A
    qb = pl.program_id(1)
    n_ch = qb + 1
    seq_keys = n_ch * kc
    t0 = qb * tq

    def rows(c):
        return pl.ds(pl.multiple_of(c * kc, kc), kc)

    def lanes(h):
        return slice(h * tq, (h + 1) * tq)

    def chunk_loop(n, body, init):
        def pair(p, carry):
            return body(2 * p + 1, body(2 * p, carry))
        carry = lax.fori_loop(0, jnp.right_shift(n, 1), pair, init)
        return lax.cond(jnp.bitwise_and(n, 1) == 1, lambda c: body(n - 1, c), lambda c: c, carry)

    for h in range(N_HEADS_IDX):
        qs_ref[h * tq:(h + 1) * tq, :] = qi_ref[:, h * HEAD_DIM_IDX:(h + 1) * HEAD_DIM_IDX]
    w_t = smq_ref[...].T
    q_pos = lax.broadcasted_iota(jnp.int32, (1, tq), 1) + t0
    key_iota = lax.broadcasted_iota(jnp.int32, (kc, tq), 0)

    def score_chunk(c, carry):
        mx, mn = carry
        kk = smk_ref[rows(c), SM_KIDX:SM_KIDX + HEAD_DIM_IDX].astype(BF16)
        dots = lax.dot_general(kk, qs_ref[...], (((1,), (1,)), ((), ())), preferred_element_type=F32)
        sc = jnp.zeros((kc, tq), F32)
        for h in range(N_HEADS_IDX):
            sc = sc + w_t[SM_WIDX + h:SM_WIDX + h + 1, :] * jnp.maximum(dots[:, lanes(h)], 0.0)
        vis = (key_iota + c * kc) <= q_pos
        sc_ref[rows(c), :] = jnp.where(vis, sc, NEG_INF)
        mx = jnp.maximum(mx, jnp.max(jnp.where(vis, sc, NEG_INF), axis=0, keepdims=True))
        mn = jnp.minimum(mn, jnp.min(jnp.where(vis, sc, jnp.inf), axis=0, keepdims=True))
        return mx, mn

    mx, mn = chunk_loop(n_ch, score_chunk,
                        (jnp.full((1, tq), NEG_INF, F32), jnp.full((1, tq), jnp.inf, F32)))

    n_vis = (q_pos + 1).astype(F32)
    k_row = jnp.minimum(n_vis, float(topk))

    def count(pred):
        def body(c, a):
            hit = pred(sc_ref[rows(c), :]).astype(F32)
            return a + jnp.sum(hit.reshape(kc // SUBLANES, SUBLANES, tq), axis=0)
        a = chunk_loop(n_ch, body, jnp.zeros((SUBLANES, tq), F32))
        return jnp.sum(a, axis=0, keepdims=True)

    def any_lane(flag):
        return jnp.max(jnp.where(flag, 1.0, 0.0)) > 0.0

    def crowded(cnt_lo, c_hi):
        return any_lane(jnp.logical_and(cnt_lo != k_row, cnt_lo - c_hi > PEEL_BRACKET))

    def bisect_cond(carry):
        it, lo, hi, cnt_lo, c_hi = carry
        return jnp.logical_and(it < BISECT_MAX_CHECKS, crowded(cnt_lo, c_hi))

    def bisect_body(carry):
        it, lo, hi, cnt_lo, c_hi = carry
        for _ in range(BISECT_STEPS_PER_CHECK):
            mid = lo * 0.5 + hi * 0.5
            cnt = count(lambda s: s >= mid)
            ge = cnt >= k_row
            lo, cnt_lo = jnp.where(ge, mid, lo), jnp.where(ge, cnt, cnt_lo)
            hi, c_hi = jnp.where(ge, hi, mid), jnp.where(ge, c_hi, cnt)
        return it + 1, lo, hi, cnt_lo, c_hi

    hi0 = mx + jnp.maximum(jnp.abs(mx), 1e-30) * 1e-6
    _, lo, hi, cnt_lo, c_hi = lax.while_loop(
        bisect_cond, bisect_body, (jnp.int32(0), mn, hi0, n_vis, jnp.zeros((1, tq), F32)))

    def peel_cond(carry):
        it, lo, hi, cnt_lo, c_hi, done = carry
        return jnp.logical_and(it < seq_keys, any_lane(done == 0.0))

    def peel_body(carry):
        it, lo, hi, cnt_lo, c_hi, done = carry

        def top_body(c, v):
            s = sc_ref[rows(c), :]
            inside = jnp.logical_and(s >= lo, s < hi)
            return jnp.maximum(v, jnp.max(jnp.where(inside, s, NEG_INF), axis=0, keepdims=True))

        v = chunk_loop(n_ch, top_body, jnp.full((1, tq), NEG_INF, F32))
        c_v = count(lambda s: s >= v)
        reached = c_v >= k_row
        live = done == 0.0
        fin = jnp.logical_and(live, reached)
        cut = jnp.logical_and(live, jnp.logical_not(reached))
        return (it + 1, jnp.where(fin, v, lo), jnp.where(cut, v, hi), jnp.where(fin, c_v, cnt_lo),
                jnp.where(cut, c_v, c_hi), jnp.where(fin, 1.0, done))

    _, lo, hi, cnt_lo, c_hi, _ = lax.while_loop(
        peel_cond, peel_body,
        (jnp.int32(0), lo, hi, cnt_lo, c_hi, jnp.where(cnt_lo == k_row, 1.0, 0.0)))

    def mask_chunk(c, _):
        am_ref[rows(c), :] = jnp.where(sc_ref[rows(c), :] >= lo, 0.0, NEG_INF)
        return 0

    chunk_loop(n_ch, mask_chunk, 0)

    tied = cnt_lo != k_row
    for part in range(tq // LANES):
        ls = slice(part * LANES, (part + 1) * LANES)

        @pl.when(any_lane(tied[:, ls]))
        def _ties():
            need, lo_p, hi_p = (k_row - c_hi)[:, ls], lo[:, ls], hi[:, ls]
            lower = (lax.broadcasted_iota(jnp.int32, (kc, kc), 1)
                     < lax.broadcasted_iota(jnp.int32, (kc, kc), 0)).astype(BF16)

            def tie_chunk(c, before):
                s = sc_ref[rows(c), ls]
                above = s >= hi_p
                tie = jnp.logical_and(s >= lo_p, jnp.logical_not(above))
                tie_f = tie.astype(F32)
                rank = jnp.dot(lower, tie_f.astype(BF16), preferred_element_type=F32) + before
                sel = jnp.logical_or(above, jnp.logical_and(tie, rank < need))
                am_ref[rows(c), ls] = jnp.where(sel, 0.0, NEG_INF)
                return before + jnp.sum(tie_f, axis=0, keepdims=True)

            chunk_loop(n_ch, tie_chunk, jnp.zeros((1, LANES), F32))

    for h in range(nh):
        qh = qa_ref[:, h * HEAD_DIM_A:(h + 1) * HEAD_DIM_A]
        qlt = lax.dot_general(wuk_ref[h], qh, (((1,), (1,)), ((), ())), preferred_element_type=F32)
        ql_ref[:, lanes(h)] = (qlt * (HEAD_DIM_A ** -0.5 * LOG2E)).astype(BF16)

    m_ref[...] = jnp.full(m_ref.shape, NEG_INF, F32)

    def pass_a(c, table):
        raw = jnp.dot(ckv_ref[rows(c), :], ql_ref[...], preferred_element_type=F32)
        am = am_ref[rows(c), :]
        for h in range(nh):
            x = raw[:, lanes(h)] + am
            if table is not None:
                x = x + tz_ref[table, h]
            x_ref[rows(c), lanes(h)] = x
            m_ref[:, lanes(h)] = jnp.maximum(m_ref[:, lanes(h)], jnp.max(x, axis=0, keepdims=True))

    def far_chunk(c, _):
        pass_a(c, None)
        return 0

    chunk_loop(jnp.maximum(qb - 1, 0), far_chunk, 0)

    @pl.when(qb >= 1)
    def _prev():
        pass_a(qb - 1, 1)

    pass_a(qb, 0)

    def pv(c):
        p = jnp.exp2((x_ref[rows(c), :] - m_ref[...]).astype(BF16))
        return jnp.dot(ckvt_ref[:, rows(c)], p, preferred_element_type=F32)

    def pass_b(c, _):
        acc_ref[...] += pv(c)
        return 0

    acc_ref[...] = pv(qb)
    chunk_loop(qb, pass_b, 0)
    inv_l = 1.0 / acc_ref[KV_RANK:KV_RANK + 1, :]
    o_lat = (acc_ref[0:KV_RANK, :] * inv_l).astype(BF16)
    outs = [jnp.dot(wuvt_ref[h], o_lat[:, lanes(h)], preferred_element_type=F32) for h in range(nh)]
    oa_ref[...] = jnp.concatenate(outs, axis=0).T.astype(BF16)


def _dsa(qi, sm, qa, ckv, ckv_t, wuk_t, wuv_t, tz, batch, seq):
    tq = Q_TILE
    nq = seq // tq
    topk = min(TOPK_MAX, seq // 4)
    T = batch * seq
    kern = functools.partial(_dsa_kernel, topk=topk)
    blk_q = lambda n: pl.BlockSpec((tq, n), lambda b, q: (b * nq + q, 0))
    blk_s = lambda n: pl.BlockSpec((seq, n), lambda b, q: (b, 0))
    in_specs = [
        blk_q(WIDTH_IDX), blk_q(LANES), blk_s(LANES), blk_q(WIDTH_A), blk_s(KV_RANK),
        pl.BlockSpec((None, KV_RANK + ONES_ROWS, seq), lambda b, q: (b, 0, 0)),
        _full(wuk_t.shape), _full(wuv_t.shape), _full(tz.shape),
    ]
    scratch = [
        pltpu.VMEM((N_HEADS_IDX * tq, HEAD_DIM_IDX), BF16),
        pltpu.VMEM((seq, tq), F32),
        pltpu.VMEM((seq, tq), F32),
        pltpu.VMEM((KV_RANK, N_HEADS_A * tq), BF16),
        pltpu.VMEM((seq, N_HEADS_A * tq), F32),
        pltpu.VMEM((KV_RANK + ONES_ROWS, N_HEADS_A * tq), F32),
        pltpu.VMEM((1, N_HEADS_A * tq), F32),
    ]
    return pl.pallas_call(
        kern, grid=(batch, nq), in_specs=in_specs,
        out_specs=pl.BlockSpec((tq, WIDTH_A), lambda b, q: (b * nq + q, 0)),
        out_shape=jax.ShapeDtypeStruct((T, WIDTH_A), BF16),
        scratch_shapes=scratch, compiler_params=_cparams(2), name="dsa")(
            qi, sm, sm, qa, ckv, ckv_t, wuk_t, wuv_t, tz)


def _t5_bucket(dist):
    max_exact = N_BUCKETS // 2
    d = jnp.maximum(dist, 0)
    ratio = jnp.log(jnp.maximum(d, 1).astype(F32) / max_exact) / math.log(MAX_DISTANCE / max_exact)
    large = jnp.minimum(max_exact + (ratio * (N_BUCKETS - max_exact)).astype(jnp.int32), N_BUCKETS - 1)
    return jnp.where(d < max_exact, d, large)


def _bias_tables(rel_bias):
    tq = Q_TILE
    span = 2 * tq
    assert int(_np_bucket(tq)) == N_BUCKETS - 1
    far = rel_bias[N_BUCKETS - 1]
    by_dist = ((rel_bias[_t5_bucket(jnp.arange(span))] - far).astype(F32) * LOG2E).T
    diag = jnp.concatenate([by_dist[:, :tq], jnp.zeros_like(by_dist[:, :tq])], axis=1)
    prev = jnp.concatenate([by_dist[:, tq:], by_dist[:, :tq]], axis=1)

    def toeplitz(f):
        m = jnp.tile(f, (1, tq))[:, :tq * (span - 1)].reshape(N_HEADS_A, tq, span - 1)
        return m[:, :, :tq]

    return jnp.stack([toeplitz(diag), toeplitz(prev)])


def _np_bucket(d):
    max_exact = N_BUCKETS // 2
    ratio = np.log(np.float32(max(d, 1)) / np.float32(max_exact)) / math.log(MAX_DISTANCE / max_exact)
    return min(max_exact + int(ratio * (N_BUCKETS - max_exact)), N_BUCKETS - 1) if d >= max_exact else d


def _mlstm_kernel(qk_ref, halo_ref, v_ref, sm_ref, o_ref, cw_ref, cb_ref, g_ref,
                  out_ref, c_ref, m_ref):
    c_idx = pl.program_id(1)

    @pl.when(c_idx == 0)
    def _init():
        c_ref[...] = jnp.zeros_like(c_ref)
        m_ref[...] = jnp.zeros_like(m_ref)

    for g in range(M_GROUPS):
        gates = _mlstm_gates(g, c_idx, qk_ref, halo_ref, sm_ref, cw_ref, cb_ref)
        outs = [_mlstm_head(g, h, gates, v_ref, o_ref, g_ref, c_ref, m_ref) for h in range(N_HEADS_M)]
        out_ref[g] = jnp.concatenate(outs, axis=-1).astype(BF16)


def _mlstm_gates(g, c_idx, qk_ref, halo_ref, sm_ref, cw_ref, cb_ref):
    L = M_CHUNK

    hw = halo_ref.shape[1]
    halo = jnp.where(c_idx > 0, halo_ref[g].astype(F32), 0.0)
    ext = jnp.concatenate([halo, qk_ref[g].astype(F32)], axis=0)
    acc = jnp.zeros((L, 2 * WIDTH_M), F32) + cb_ref[...]
    for w in range(CONV_WIDTH):
        off = hw - (CONV_WIDTH - 1) + w
        acc = acc + ext[off:off + L, :] * cw_ref[w:w + 1, :]
    qk = acc * _sigmoid(acc)

    sm = sm_ref[g]
    sm_t = sm.T
    r_i = lax.broadcasted_iota(jnp.int32, (L, L), 0)
    c_i = lax.broadcasted_iota(jnp.int32, (L, L), 1)
    tril = (c_i <= r_i).astype(BF16)
    bcum_c = sum(jnp.dot(tril, part, preferred_element_type=F32)
                 for part in _split3(jax.nn.log_sigmoid(sm)))
    return qk, sm, sm_t, bcum_c, bcum_c.T, c_i <= r_i


def _mlstm_head(g, h, gates, v_ref, o_ref, g_ref, c_ref, m_ref):
    L = M_CHUNK
    dm = HEAD_DIM_M
    qk, sm, sm_t, bcum_c, bcum_r, causal = gates
    ones_col = (lax.broadcasted_iota(jnp.int32, (L, dm), 1) == 0).astype(BF16)
    st = g * N_HEADS_M + h
    q = qk[:, h * dm:(h + 1) * dm].astype(BF16)
    k = (qk[:, WIDTH_M + h * dm:WIDTH_M + (h + 1) * dm] * (dm ** -0.5))
    v_aug = jnp.concatenate([v_ref[g, :, h * dm:(h + 1) * dm], ones_col], axis=-1)
    b_col = bcum_c[:, SM_F + h:SM_F + h + 1]
    g_col = sm[:, SM_I + h:SM_I + h + 1] - b_col
    g_row = sm_t[SM_I + h:SM_I + h + 1, :] - bcum_r[SM_F + h:SM_F + h + 1, :]
    b_last = b_col[L - 1:L, :]
    m_prev = m_ref[st]
    c_prev = c_ref[st]

    log_d = jnp.where(causal, b_col + g_row, NEG_INF)
    m_j = jnp.maximum(b_col + m_prev, jnp.max(log_d, axis=-1, keepdims=True))
    w_inter = jnp.exp(b_col + m_prev - m_j)
    qkt = lax.dot_general(q, k.astype(BF16), (((1,), (1,)), ((), ())), preferred_element_type=F32)
    s = qkt * jnp.exp(log_d - m_j)
    o_aug = jnp.dot(s.astype(BF16), v_aug, preferred_element_type=F32) + \
        w_inter * jnp.dot(q, c_prev.astype(BF16), preferred_element_type=F32)
    num = o_aug[:, :dm]
    den = o_aug[:, dm:dm + 1]
    hh = num / jnp.maximum(jnp.abs(den), jnp.exp(-m_j))

    lwe = b_last + g_col
    m_loc = jnp.max(lwe, axis=0, keepdims=True)
    kw = (k * jnp.exp(lwe - m_loc)).astype(BF16)
    c_loc = lax.dot_general(kw, v_aug, (((0,), (0,)), ((), ())), preferred_element_type=F32)
    m_new = jnp.maximum(b_last + m_prev, m_loc)
    c_ref[st] = jnp.exp(b_last + m_prev - m_new) * c_prev + jnp.exp(m_loc - m_new) * c_loc
    m_ref[st] = m_new

    hn = hh * lax.rsqrt(jnp.mean(hh * hh, axis=-1, keepdims=True) + LN_EPS) * g_ref[:, h * dm:(h + 1) * dm]
    og = o_ref[g, :, h * dm:(h + 1) * dm].astype(F32)
    return _sigmoid(og) * hn


def _mlstm(qk, v, sm, o, conv_w, conv_b, mh_g, batch, seq):
    L = M_CHUNK
    G = M_GROUPS
    nc = seq // L
    T = batch * seq
    hb = L // CONV_HALO
    grouped = lambda a: a.reshape(G, T // G, a.shape[-1])
    blk = lambda n: pl.BlockSpec((G, L, n), lambda b, c: (0, b * nc + c, 0))
    in_specs = [
        blk(2 * WIDTH_M),
        pl.BlockSpec((G, CONV_HALO, 2 * WIDTH_M), lambda b, c: (0, jnp.maximum((b * nc + c) * hb - 1, 0), 0)),
        blk(WIDTH_M), blk(LANES), blk(WIDTH_M),
        _full(conv_w.shape), _full(conv_b.shape), _full(mh_g.shape),
    ]
    scratch = [pltpu.VMEM((G * N_HEADS_M, HEAD_DIM_M, 2 * HEAD_DIM_M), F32),
               pltpu.VMEM((G * N_HEADS_M, 1, 1), F32)]
    qk_g = grouped(qk)
    out = pl.pallas_call(
        _mlstm_kernel, grid=(batch // G, nc), in_specs=in_specs, out_specs=blk(WIDTH_M),
        out_shape=jax.ShapeDtypeStruct((G, T // G, WIDTH_M), BF16), scratch_shapes=scratch,
        compiler_params=_cparams(2), name="mlstm")(
            qk_g, qk_g, grouped(v), grouped(sm), grouped(o), conv_w, conv_b, mh_g)
    return out.reshape(T, WIDTH_M)


def _layer_norm(y, g, b):
    mu = jnp.mean(y, axis=-1, keepdims=True)
    var = jnp.mean(jnp.square(y - mu), axis=-1, keepdims=True)
    return (y - mu) * lax.rsqrt(var + LN_EPS) * g + b


def _router_logits(x1, wr, br):
    x1_hi = x1.astype(BF16)
    x1_lo = (x1 - x1_hi.astype(F32)).astype(BF16)
    return (jnp.dot(x1_hi, wr[0], preferred_element_type=F32)
            + jnp.dot(x1_lo, wr[0], preferred_element_type=F32)
            + jnp.dot(x1_hi, wr[1], preferred_element_type=F32)) + br[...]


def _merge_kernel(x_ref, oa_ref, hm_ref, ga_ref, gm_ref, wua, wum, wout, l1g, l1b, wr, br,
                  x1_ref, route_ref, cnt_ref):
    pa = jnp.dot(oa_ref[...], wua[...], preferred_element_type=F32)
    pm = jnp.dot(hm_ref[...], wum[...], preferred_element_type=F32)
    y = _sigmoid(ga_ref[...]) * pa.astype(BF16) + _sigmoid(gm_ref[...]) * pm.astype(BF16)
    mix = jnp.dot(y.astype(BF16), wout[...], preferred_element_type=F32)
    x1 = _layer_norm(ALPHA * x_ref[...] + mix, l1g[...], l1b[...])
    x1_ref[...] = x1

    lt = _router_logits(x1, wr, br).T
    tm = lt.shape[1]
    big = jnp.int32(LANES)
    le = lt[0:N_EXPERTS, :]
    lg = lt[N_EXPERTS:N_EXPERTS + SUBLANES, :]
    row_g = lax.broadcasted_iota(jnp.int32, lg.shape, 0)
    row_e = lax.broadcasted_iota(jnp.int32, le.shape, 0)
    is_grp = row_g < N_GROUPS
    gl = jnp.where(is_grp, lg, NEG_INF)
    ge = jnp.exp(gl - jnp.max(gl, axis=0, keepdims=True))
    gp = ge / jnp.sum(ge, axis=0, keepdims=True)
    g_w = jnp.max(gp, axis=0, keepdims=True)
    g_idx = jnp.min(jnp.where(jnp.logical_and(is_grp, gp == g_w), row_g, big), axis=0, keepdims=True)
    el = jnp.where(jnp.right_shift(row_e, EPG_SHIFT) == g_idx, le, NEG_INF)
    m1 = jnp.max(el, axis=0, keepdims=True)
    i1 = jnp.min(jnp.where(el == m1, row_e, big), axis=0, keepdims=True)
    el2 = jnp.where(row_e == i1, NEG_INF, el)
    m2 = jnp.max(el2, axis=0, keepdims=True)
    i2 = jnp.min(jnp.where(el2 == m2, row_e, big), axis=0, keepdims=True)

    a = jnp.bitwise_and(jnp.minimum(i1, i2), EXPERTS_PER_GROUP - 1)
    b = jnp.bitwise_and(jnp.maximum(i1, i2), EXPERTS_PER_GROUP - 1)
    pair = jnp.right_shift(a * (2 * EXPERTS_PER_GROUP - 1 - a), 1) + (b - a - 1)
    cls = g_idx * PAIRS_PER_GROUP + pair
    row_c = lax.broadcasted_iota(jnp.int32, (CLASS_ROWS, tm), 0)
    onehot = (row_c == cls).astype(F32)

    @pl.when(pl.program_id(0) == 0)
    def _init():
        cnt_ref[...] = jnp.zeros_like(cnt_ref)

    earlier = (lax.broadcasted_iota(jnp.int32, (tm, tm), 0)
               < lax.broadcasted_iota(jnp.int32, (tm, tm), 1)).astype(BF16)
    prior = jnp.dot(onehot.astype(BF16), earlier, preferred_element_type=F32) + cnt_ref[:, 0:1]
    rank = jnp.sum(prior * onehot, axis=0, keepdims=True)
    cnt_ref[...] += jnp.sum(onehot, axis=1, keepdims=True)
    row_o = lax.broadcasted_iota(jnp.int32, route_ref.shape, 0)
    route_ref[...] = jnp.where(row_o == 0, cls.astype(F32), jnp.where(row_o == 1, rank, 0.0))


def _merge(x2, oa, hm, ga, gm, wua, wum, wout, l1g, l1b, wr, br, tm):
    T = x2.shape[0]
    blk = lambda n: pl.BlockSpec((tm, n), lambda i: (i, 0))
    in_specs = [blk(D_MODEL), blk(WIDTH_A), blk(WIDTH_M), blk(D_MODEL), blk(D_MODEL),
                _full(wua.shape), _full(wum.shape), _full(wout.shape), _full(l1g.shape), _full(l1b.shape),
                _full(wr.shape), _full(br.shape)]
    out_specs = [blk(D_MODEL), pl.BlockSpec((SUBLANES, tm), lambda i: (0, i)), _full((CLASS_ROWS, LANES))]
    out_shape = [jax.ShapeDtypeStruct((T, D_MODEL), F32), jax.ShapeDtypeStruct((SUBLANES, T), F32),
                 jax.ShapeDtypeStruct((CLASS_ROWS, LANES), F32)]
    return pl.pallas_call(
        _merge_kernel, grid=(T // tm,), in_specs=in_specs, out_specs=out_specs, out_shape=out_shape,
        compiler_params=_cparams(1), name="merge")(x2, oa, hm, ga, gm, wua, wum, wout, l1g, l1b, wr, br)


def _sc_mesh():
    return plsc.VectorSubcoreMesh(core_axis_name="c", subcore_axis_name="s",
                                  num_cores=SC_CORES, num_subcores=SC_SUBCORES)


def _sc_chunks(n_rows):
    workers = SC_CORES * SC_SUBCORES
    assert n_rows % (workers * SC_ROWS * 2) == 0
    return n_rows // (workers * SC_ROWS)


def _sc_scratch(n_chunks, width, dtype):
    return [pltpu.VMEM((n_chunks, SC_ROWS), jnp.int32),
            pltpu.VMEM((SC_ROWS, width), dtype), pltpu.VMEM((SC_ROWS, width), dtype),
            pltpu.SemaphoreType.DMA, pltpu.SemaphoreType.DMA]


def _sc_scatter_rows(rows, idx, n_out):
    n_in, width = rows.shape
    n_chunks = _sc_chunks(n_in)

    @functools.partial(
        pl.kernel, mesh=_sc_mesh(), out_type=jax.ShapeDtypeStruct((n_out, width), rows.dtype),
        scratch_types=_sc_scratch(n_chunks, width, rows.dtype), name="sc_dispatch")
    def scatter(rows_hbm, idx_hbm, out_hbm, idx_v, rows_a, rows_b, sem_a, sem_b):
        first = (lax.axis_index("s") * SC_CORES + lax.axis_index("c")) * n_chunks
        pltpu.sync_copy(idx_hbm.at[pl.ds(first, n_chunks)], idx_v)

        def load(c, buf):
            pltpu.sync_copy(rows_hbm.at[pl.ds((first + c) * SC_ROWS, SC_ROWS)], buf)

        def put(c, buf, sem):
            return pltpu.make_async_copy(buf, out_hbm.at[idx_v.at[c]], sem)

        load(0, rows_a)
        put(0, rows_a, sem_a).start()

        @pl.loop(0, n_chunks, step=2)
        def _(j):
            load(j + 1, rows_b)
            put(j + 1, rows_b, sem_b).start()
            put(j, rows_a, sem_a).wait()

            @pl.when(j + 2 < n_chunks)
            def _():
                load(j + 2, rows_a)
                put(j + 2, rows_a, sem_a).start()

            put(j + 1, rows_b, sem_b).wait()

    return scatter(rows, idx.reshape(n_in // SC_ROWS, SC_ROWS))


def _sc_gather_rows(table, idx):
    n_out, width = idx.shape[0], table.shape[1]
    n_chunks = _sc_chunks(n_out)

    @functools.partial(
        pl.kernel, mesh=_sc_mesh(), out_type=jax.ShapeDtypeStruct((n_out, width), table.dtype),
        scratch_types=_sc_scratch(n_chunks, width, table.dtype), name="sc_combine")
    def gather(table_hbm, idx_hbm, out_hbm, idx_v, rows_a, rows_b, sem_a, sem_b):
        first = (lax.axis_index("s") * SC_CORES + lax.axis_index("c")) * n_chunks
        pltpu.sync_copy(idx_hbm.at[pl.ds(first, n_chunks)], idx_v)

        def fetch(c, buf, sem):
            return pltpu.make_async_copy(table_hbm.at[idx_v.at[c]], buf, sem)

        def store(c, buf):
            pltpu.sync_copy(buf, out_hbm.at[pl.ds((first + c) * SC_ROWS, SC_ROWS)])

        fetch(0, rows_a, sem_a).start()

        @pl.loop(0, n_chunks, step=2)
        def _(j):
            fetch(j + 1, rows_b, sem_b).start()
            fetch(j, rows_a, sem_a).wait()
            store(j, rows_a)

            @pl.when(j + 2 < n_chunks)
            def _():
                fetch(j + 2, rows_a, sem_a).start()

            fetch(j + 1, rows_b, sem_b).wait()
            store(j + 1, rows_b)

    return gather(table, idx.reshape(n_out // SC_ROWS, SC_ROWS))


def _moe_kernel(ta_ref, tb_ref, nv_ref, xs_ref, wga, wua, wda, wgb, wub, wdb, wr, br, l2g, l2b, ys_ref):
    i = pl.program_id(0)

    @pl.when(nv_ref[i] > 0)
    def _compute():
        x = xs_ref[...]
        xb = x.astype(BF16)
        ea, eb = ta_ref[i], tb_ref[i]
        logits = _router_logits(x, wr, br)
        lane = lax.broadcasted_iota(jnp.int32, logits.shape, 1)

        def pick(idx):
            return jnp.sum(jnp.where(lane == idx, logits, 0.0), axis=-1, keepdims=True)

        la, lb, lg = pick(ea), pick(eb), pick(N_EXPERTS + jnp.right_shift(ea, EPG_SHIFT))
        is_grp = jnp.logical_and(lane >= N_EXPERTS, lane < N_EXPERTS + N_GROUPS)
        g_w = 1.0 / jnp.sum(jnp.where(is_grp, jnp.exp(logits - lg), 0.0), axis=-1, keepdims=True)
        w_a = g_w / (1.0 + jnp.exp(lb - la))
        w_b = g_w / (1.0 + jnp.exp(la - lb))

        def expert(wg, wu, wd):
            g = jnp.dot(xb, wg[...], preferred_element_type=F32)
            u = jnp.dot(xb, wu[...], preferred_element_type=F32)
            hdn = (g * _sigmoid(g) * u).astype(BF16)
            return jnp.dot(hdn, wd[...], preferred_element_type=F32)

        ffn = w_a * expert(wga, wua, wda) + w_b * expert(wgb, wub, wdb)
        ys_ref[...] = _layer_norm(ALPHA * x + ffn, l2g[...], l2b[...])


def _moe(xs, tile_a, tile_b, n_valid, wg, wu, wd, wr, br, l2g, l2b):
    tm = MOE_TILE
    n_tiles = xs.shape[0] // tm
    rows = pl.BlockSpec((tm, D_MODEL), lambda i, ta, tb, nv: (i, 0))
    w_in = lambda which: pl.BlockSpec((None, D_MODEL, D_EXPERT), lambda i, ta, tb, nv: ((ta, tb)[which][i], 0, 0))
    w_out = lambda which: pl.BlockSpec((None, D_EXPERT, D_MODEL), lambda i, ta, tb, nv: ((ta, tb)[which][i], 0, 0))
    const = lambda shape: pl.BlockSpec(shape, lambda i, ta, tb, nv: (0,) * len(shape))
    grid_spec = pltpu.PrefetchScalarGridSpec(
        num_scalar_prefetch=3, grid=(n_tiles,),
        in_specs=[rows, w_in(0), w_in(0), w_out(0), w_in(1), w_in(1), w_out(1),
                  const(wr.shape), const(br.shape), const(l2g.shape), const(l2b.shape)],
        out_specs=rows)
    return pl.pallas_call(
        _moe_kernel, grid_spec=grid_spec, out_shape=jax.ShapeDtypeStruct(xs.shape, F32),
        compiler_params=_cparams(1), name="moe")(
            tile_a, tile_b, n_valid, xs, wg, wu, wd, wg, wu, wd, wr, br, l2g, l2b)


def _route_tables(route, counts, n_tokens):
    tm = MOE_TILE
    n_tiles = n_tokens // tm + N_CLASSES
    cnt = counts[:N_CLASSES, 0].astype(jnp.int32)
    tiles = (cnt + tm - 1) // tm
    tile_end = jnp.cumsum(tiles)
    tile_start = tile_end - tiles
    t_idx = jnp.arange(n_tiles, dtype=jnp.int32)
    cls_of_tile = jnp.minimum(jnp.sum(t_idx[:, None] >= tile_end[None, :], axis=1), N_CLASSES - 1).astype(jnp.int32)
    n_valid = jnp.clip(cnt[cls_of_tile] - (t_idx - tile_start[cls_of_tile]) * tm, 0, tm).astype(jnp.int32)
    grp, pair = cls_of_tile // PAIRS_PER_GROUP, cls_of_tile % PAIRS_PER_GROUP
    pa = jnp.asarray(PAIR_A, jnp.int32)[pair]
    pb = jnp.asarray(PAIR_B, jnp.int32)[pair]
    tile_a = grp * EXPERTS_PER_GROUP + pa
    tile_b = grp * EXPERTS_PER_GROUP + pb
    cls = route[0].astype(jnp.int32)
    rank = route[1].astype(jnp.int32)
    row0 = jnp.sum(jnp.where(cls[:, None] == jnp.arange(N_CLASSES)[None, :], (tile_start * tm)[None, :], 0), axis=1)
    return tile_a, tile_b, n_valid, row0 + rank, n_tiles * tm


def _pick_tile(T, pref):
    t = pref
    while T % t:
        t //= 2
    return t


def kernel(x, w_in, conv_w, conv_b, kv_norm_g, w_uk, w_uv, rel_bias, b_i, b_f, mh_norm_g, w_up_a, w_up_m,
           w_out, ln1_g, ln1_b, w_grp, b_grp, w_rt, b_rt, w_gate, w_up, w_down, ln2_g, ln2_b):
    B, S, _ = x.shape
    T = B * S
    assert S % Q_TILE == 0 and S % M_CHUNK == 0 and T % MOE_TILE == 0 and w_in.shape[0] == DEPTH
    tz = _bias_tables(rel_bias)
    x2 = x.reshape(T, D_MODEL)
    for l in range(DEPTH):
        w = w_in[l]
        o = np.cumsum((WIDTH_A, KV_RANK, WIDTH_IDX, HEAD_DIM_IDX, N_HEADS_IDX, 2 * WIDTH_M, WIDTH_M,
                       N_HEADS_M, N_HEADS_M, WIDTH_M, D_MODEL, D_MODEL)).tolist()
        o = [0] + o
        seg = lambda j: w[:, o[j]:o[j + 1]]
        pad = LANES - (HEAD_DIM_IDX + N_HEADS_IDX + 2 * N_HEADS_M)
        w_small = jnp.concatenate([seg(3), seg(4), seg(7), seg(8), jnp.zeros((D_MODEL, pad), w.dtype)], axis=1)
        ws = [seg(0), seg(1), seg(2), w_small, seg(5), seg(6), seg(9), seg(10), seg(11)]
        ws = [a.astype(BF16) for a in ws]
        smb = jnp.zeros((1, LANES), F32).at[0, SM_I:SM_I + N_HEADS_M].set(b_i[l]) \
            .at[0, SM_F:SM_F + N_HEADS_M].set(b_f[l])
        qa, ckv, qi, sm, qk, v, og, ga, gm, ckv_t = _proj(x2, ws, kv_norm_g[l][None, :], smb,
                                                          _pick_tile(S, 512), S)

        wuk_t = jnp.transpose(w_uk[l], (1, 0, 2)).astype(BF16)
        wuv_t = jnp.transpose(w_uv[l], (1, 2, 0)).astype(BF16)
        oa = _dsa(qi, sm, qa, ckv, ckv_t, wuk_t, wuv_t, tz, B, S)

        hm = _mlstm(qk, v, sm, og, conv_w[l], conv_b[l][None, :], mh_norm_g[l].reshape(1, WIDTH_M), B, S)

        w_router = jnp.concatenate(
            [w_rt[l], w_grp[l], jnp.zeros((D_MODEL, LANES - N_EXPERTS - N_GROUPS), F32)], axis=1)
        b_router = jnp.concatenate(
            [b_rt[l], b_grp[l], jnp.zeros((LANES - N_EXPERTS - N_GROUPS,), F32)])[None, :]
        wr_hi = w_router.astype(BF16)
        wr_split = jnp.stack([wr_hi, (w_router - wr_hi.astype(F32)).astype(BF16)])
        x1, route, counts = _merge(x2, oa, hm, ga, gm, w_up_a[l].astype(BF16), w_up_m[l].astype(BF16),
                                   w_out[l].astype(BF16), ln1_g[l][None, :], ln1_b[l][None, :],
                                   wr_split, b_router, _pick_tile(T, 256))

        tile_a, tile_b, n_valid, pos, n_sorted = _route_tables(route, counts, T)
        xs = _sc_scatter_rows(x1, pos, n_sorted)
        ys = _moe(xs, tile_a, tile_b, n_valid, w_gate[l].astype(BF16), w_up[l].astype(BF16),
                  w_down[l].astype(BF16), wr_split, b_router, ln2_g[l][None, :], ln2_b[l][None, :])
        x2 = _sc_gather_rows(ys, pos)
    return x2.reshape(B, S, D_MODEL)
```

```python
import functools
import math

import jax
import jax.numpy as jnp
import numpy as np
from jax import lax
from jax.experimental import pallas as pl
from jax.experimental.pallas import tpu as pltpu
from jax.experimental.pallas import tpu_sc as plsc

F32 = jnp.float32
BF16 = jnp.bfloat16

D_MODEL = 1024
N_HEADS_A = 8
HEAD_DIM_A = 64
WIDTH_A = N_HEADS_A * HEAD_DIM_A
KV_RANK = 256
N_HEADS_IDX = 8
HEAD_DIM_IDX = 64
WIDTH_IDX = N_HEADS_IDX * HEAD_DIM_IDX
TOPK_MAX = 256
N_BUCKETS = 32
MAX_DISTANCE = 128
N_HEADS_M = 4
HEAD_DIM_M = 128
WIDTH_M = N_HEADS_M * HEAD_DIM_M
CONV_WIDTH = 4
N_GROUPS = 4
EXPERTS_PER_GROUP = 4
N_EXPERTS = N_GROUPS * EXPERTS_PER_GROUP
D_EXPERT = 512
LN_EPS = 1e-5
DEPTH = 1
ALPHA = (2.0 * DEPTH) ** 0.25

LANES = 128
SUBLANES = 8
VMEM_LIMIT = 56 * 1024 * 1024

SM_KIDX = 0
SM_WIDX = HEAD_DIM_IDX
SM_I = SM_WIDX + N_HEADS_IDX
SM_F = SM_I + N_HEADS_M

Q_TILE = 256
K_CHUNK = Q_TILE
ONES_ROWS = 16
M_CHUNK = 128
M_GROUPS = 2
CONV_HALO = 16
assert CONV_HALO >= CONV_WIDTH - 1
MOE_TILE = 256
SC_CORES = 2
SC_SUBCORES = 16
SC_ROWS = 32
EPG_SHIFT = EXPERTS_PER_GROUP.bit_length() - 1
assert 1 << EPG_SHIFT == EXPERTS_PER_GROUP
PAIR_A, PAIR_B = zip(*[(a, b) for a in range(EXPERTS_PER_GROUP) for b in range(a + 1, EXPERTS_PER_GROUP)])
PAIRS_PER_GROUP = len(PAIR_A)
N_CLASSES = N_GROUPS * PAIRS_PER_GROUP
CLASS_ROWS = -(-N_CLASSES // SUBLANES) * SUBLANES
BISECT_STEPS_PER_CHECK = 3
BISECT_MAX_CHECKS = 5
PEEL_BRACKET = 2.0
NEG_INF = float("-inf")
LOG2E = math.log2(math.e)


def _cparams(n_grid):
    return pltpu.CompilerParams(dimension_semantics=("arbitrary",) * n_grid,
                                vmem_limit_bytes=VMEM_LIMIT)


def _full(shape):
    nd = len(shape)
    return pl.BlockSpec(shape, lambda *_: (0,) * nd)


def _sigmoid(x):
    return 0.5 * jnp.tanh(0.5 * x) + 0.5


def _split3(x):
    hi = x.astype(BF16)
    r = x - hi.astype(F32)
    mid = r.astype(BF16)
    return hi, mid, (r - mid.astype(F32)).astype(BF16)


def _proj_kernel(x_ref, wqa, wckv, wqi, wsm, wqk, wv, wo, wga, wgm, kvg, smb,
                 qa_o, ckv_o, qi_o, sm_o, qk_o, v_o, o_o, ga_o, gm_o, ckvt_o):
    xb = x_ref[...].astype(BF16)

    def mm(w):
        return jnp.dot(xb, w[...], preferred_element_type=F32)

    qa_o[...] = mm(wqa).astype(BF16)
    c = mm(wckv)
    c = c * lax.rsqrt(jnp.mean(c * c, axis=-1, keepdims=True) + LN_EPS) * kvg[...]
    ckv_o[...] = c.astype(BF16)
    ckvt_o[0:KV_RANK, :] = c.T.astype(BF16)
    ckvt_o[KV_RANK:KV_RANK + ONES_ROWS, :] = jnp.ones((ONES_ROWS, c.shape[0]), BF16)
    qi_o[...] = mm(wqi).astype(BF16)
    sm_o[...] = mm(wsm) + smb[...]
    qk_o[...] = mm(wqk).astype(BF16)
    v_o[...] = mm(wv).astype(BF16)
    o_o[...] = mm(wo).astype(BF16)
    ga_o[...] = mm(wga).astype(BF16)
    gm_o[...] = mm(wgm).astype(BF16)


def _proj(x2, ws, kvg, smb, tm, seq):
    T = x2.shape[0]
    per_seq = seq // tm
    widths = [w.shape[1] for w in ws]
    dts = [BF16, BF16, BF16, F32, BF16, BF16, BF16, BF16, BF16]
    in_specs = [pl.BlockSpec((tm, D_MODEL), lambda i: (i, 0))]
    in_specs += [_full(w.shape) for w in ws]
    in_specs += [_full(kvg.shape), _full(smb.shape)]
    out_specs = [pl.BlockSpec((tm, n), lambda i: (i, 0)) for n in widths]
    out_specs.append(pl.BlockSpec((None, KV_RANK + ONES_ROWS, tm), lambda i: (i // per_seq, 0, i % per_seq)))
    out_shape = [jax.ShapeDtypeStruct((T, n), dt) for n, dt in zip(widths, dts)]
    out_shape.append(jax.ShapeDtypeStruct((T // seq, KV_RANK + ONES_ROWS, seq), BF16))
    return pl.pallas_call(
        _proj_kernel, grid=(T // tm,), in_specs=in_specs, out_specs=out_specs, out_shape=out_shape,
        compiler_params=_cparams(1), name="proj")(x2, *ws, kvg, smb)


def _dsa_kernel(qi_ref, smq_ref, smk_ref, qa_ref, ckv_ref, ckvt_ref, wuk_ref, wuvt_ref, tz_ref,
                oa_ref, qs_ref, sc_ref, am_ref, ql_ref, x_ref, acc_ref, m_ref, *, topk):
    tq, kc = Q_TILE, K_CHUNK
    nh = N_HEADS_A
    qb = pl.program_id(1)
    n_ch = qb + 1
    seq_keys = n_ch * kc
    t0 = qb * tq

    def rows(c):
        return pl.ds(pl.multiple_of(c * kc, kc), kc)

    def lanes(h):
        return slice(h * tq, (h + 1) * tq)

    def chunk_loop(n, body, init):
        def pair(p, carry):
            return body(2 * p + 1, body(2 * p, carry))
        carry = lax.fori_loop(0, jnp.right_shift(n, 1), pair, init)
        return lax.cond(jnp.bitwise_and(n, 1) == 1, lambda c: body(n - 1, c), lambda c: c, carry)

    for h in range(N_HEADS_IDX):
        qs_ref[h * tq:(h + 1) * tq, :] = qi_ref[:, h * HEAD_DIM_IDX:(h + 1) * HEAD_DIM_IDX]
    w_t = smq_ref[...].T
    q_pos = lax.broadcasted_iota(jnp.int32, (1, tq), 1) + t0
    key_iota = lax.broadcasted_iota(jnp.int32, (kc, tq), 0)

    def score_chunk(c, carry):
        mx, mn = carry
        kk = smk_ref[rows(c), SM_KIDX:SM_KIDX + HEAD_DIM_IDX].astype(BF16)
        dots = lax.dot_general(kk, qs_ref[...], (((1,), (1,)), ((), ())), preferred_element_type=F32)
        sc = jnp.zeros((kc, tq), F32)
        for h in range(N_HEADS_IDX):
            sc = sc + w_t[SM_WIDX + h:SM_WIDX + h + 1, :] * jnp.maximum(dots[:, lanes(h)], 0.0)
        vis = (key_iota + c * kc) <= q_pos
        sc_ref[rows(c), :] = jnp.where(vis, sc, NEG_INF)
        mx = jnp.maximum(mx, jnp.max(jnp.where(vis, sc, NEG_INF), axis=0, keepdims=True))
        mn = jnp.minimum(mn, jnp.min(jnp.where(vis, sc, jnp.inf), axis=0, keepdims=True))
        return mx, mn

    mx, mn = chunk_loop(n_ch, score_chunk,
                        (jnp.full((1, tq), NEG_INF, F32), jnp.full((1, tq), jnp.inf, F32)))

    n_vis = (q_pos + 1).astype(F32)
    k_row = jnp.minimum(n_vis, float(topk))

    def count(pred):
        def body(c, a):
            hit = pred(sc_ref[rows(c), :]).astype(F32)
            return a + jnp.sum(hit.reshape(kc // SUBLANES, SUBLANES, tq), axis=0)
        a = chunk_loop(n_ch, body, jnp.zeros((SUBLANES, tq), F32))
        return jnp.sum(a, axis=0, keepdims=True)

    def any_lane(flag):
        return jnp.max(jnp.where(flag, 1.0, 0.0)) > 0.0

    def crowded(cnt_lo, c_hi):
        return any_lane(jnp.logical_and(cnt_lo != k_row, cnt_lo - c_hi > PEEL_BRACKET))

    def bisect_cond(carry):
        it, lo, hi, cnt_lo, c_hi = carry
        return jnp.logical_and(it < BISECT_MAX_CHECKS, crowded(cnt_lo, c_hi))

    def bisect_body(carry):
        it, lo, hi, cnt_lo, c_hi = carry
        for _ in range(BISECT_STEPS_PER_CHECK):
            mid = lo * 0.5 + hi * 0.5
            cnt = count(lambda s: s >= mid)
            ge = cnt >= k_row
            lo, cnt_lo = jnp.where(ge, mid, lo), jnp.where(ge, cnt, cnt_lo)
            hi, c_hi = jnp.where(ge, hi, mid), jnp.where(ge, c_hi, cnt)
        return it + 1, lo, hi, cnt_lo, c_hi

    hi0 = mx + jnp.maximum(jnp.abs(mx), 1e-30) * 1e-6
    _, lo, hi, cnt_lo, c_hi = lax.while_loop(
        bisect_cond, bisect_body, (jnp.int32(0), mn, hi0, n_vis, jnp.zeros((1, tq), F32)))

    def peel_cond(carry):
        it, lo, hi, cnt_lo, c_hi, done = carry
        return jnp.logical_and(it < seq_keys, any_lane(done == 0.0))

    def peel_body(carry):
        it, lo, hi, cnt_lo, c_hi, done = carry

        def top_body(c, v):
            s = sc_ref[rows(c), :]
            inside = jnp.logical_and(s >= lo, s < hi)
            return jnp.maximum(v, jnp.max(jnp.where(inside, s, NEG_INF), axis=0, keepdims=True))

        v = chunk_loop(n_ch, top_body, jnp.full((1, tq), NEG_INF, F32))
        c_v = count(lambda s: s >= v)
        reached = c_v >= k_row
        live = done == 0.0
        fin = jnp.logical_and(live, reached)
        cut = jnp.logical_and(live, jnp.logical_not(reached))
        return (it + 1, jnp.where(fin, v, lo), jnp.where(cut, v, hi), jnp.where(fin, c_v, cnt_lo),
                jnp.where(cut, c_v, c_hi), jnp.where(fin, 1.0, done))

    _, lo, hi, cnt_lo, c_hi, _ = lax.while_loop(
        peel_cond, peel_body,
        (jnp.int32(0), lo, hi, cnt_lo, c_hi, jnp.where(cnt_lo == k_row, 1.0, 0.0)))

    def mask_chunk(c, _):
        am_ref[rows(c), :] = jnp.where(sc_ref[rows(c), :] >= lo, 0.0, NEG_INF)
        return 0

    chunk_loop(n_ch, mask_chunk, 0)

    tied = cnt_lo != k_row
    for part in range(tq // LANES):
        ls = slice(part * LANES, (part + 1) * LANES)

        @pl.when(any_lane(tied[:, ls]))
        def _ties():
            need, lo_p, hi_p = (k_row - c_hi)[:, ls], lo[:, ls], hi[:, ls]
            lower = (lax.broadcasted_iota(jnp.int32, (kc, kc), 1)
                     < lax.broadcasted_iota(jnp.int32, (kc, kc), 0)).astype(BF16)

            def tie_chunk(c, before):
                s = sc_ref[rows(c), ls]
                above = s >= hi_p
                tie = jnp.logical_and(s >= lo_p, jnp.logical_not(above))
                tie_f = tie.astype(F32)
                rank = jnp.dot(lower, tie_f.astype(BF16), preferred_element_type=F32) + before
                sel = jnp.logical_or(above, jnp.logical_and(tie, rank < need))
                am_ref[rows(c), ls] = jnp.where(sel, 0.0, NEG_INF)
                return before + jnp.sum(tie_f, axis=0, keepdims=True)

            chunk_loop(n_ch, tie_chunk, jnp.zeros((1, LANES), F32))

    for h in range(nh):
        qh = qa_ref[:, h * HEAD_DIM_A:(h + 1) * HEAD_DIM_A]
        qlt = lax.dot_general(wuk_ref[h], qh, (((1,), (1,)), ((), ())), preferred_element_type=F32)
        ql_ref[:, lanes(h)] = (qlt * (HEAD_DIM_A ** -0.5 * LOG2E)).astype(BF16)

    m_ref[...] = jnp.full(m_ref.shape, NEG_INF, F32)

    def pass_a(c, table):
        raw = jnp.dot(ckv_ref[rows(c), :], ql_ref[...], preferred_element_type=F32)
        am = am_ref[rows(c), :]
        for h in range(nh):
            x = raw[:, lanes(h)] + am
            if table is not None:
                x = x + tz_ref[table, h]
            x_ref[rows(c), lanes(h)] = x
            m_ref[:, lanes(h)] = jnp.maximum(m_ref[:, lanes(h)], jnp.max(x, axis=0, keepdims=True))

    def far_chunk(c, _):
        pass_a(c, None)
        return 0

    chunk_loop(jnp.maximum(qb - 1, 0), far_chunk, 0)

    @pl.when(qb >= 1)
    def _prev():
        pass_a(qb - 1, 1)

    pass_a(qb, 0)

    def pv(c):
        p = jnp.exp2((x_ref[rows(c), :] - m_ref[...]).astype(BF16))
        return jnp.dot(ckvt_ref[:, rows(c)], p, preferred_element_type=F32)

    def pass_b(c, _):
        acc_ref[...] += pv(c)
        return 0

    acc_ref[...] = pv(qb)
    chunk_loop(qb, pass_b, 0)
    inv_l = 1.0 / acc_ref[KV_RANK:KV_RANK + 1, :]
    o_lat = (acc_ref[0:KV_RANK, :] * inv_l).astype(BF16)
    outs = [jnp.dot(wuvt_ref[h], o_lat[:, lanes(h)], preferred_element_type=F32) for h in range(nh)]
    oa_ref[...] = jnp.concatenate(outs, axis=0).T.astype(BF16)


def _dsa(qi, sm, qa, ckv, ckv_t, wuk_t, wuv_t, tz, batch, seq):
    tq = Q_TILE
    nq = seq // tq
    topk = min(TOPK_MAX, seq // 4)
    T = batch * seq
    kern = functools.partial(_dsa_kernel, topk=topk)
    blk_q = lambda n: pl.BlockSpec((tq, n), lambda b, q: (b * nq + q, 0))
    blk_s = lambda n: pl.BlockSpec((seq, n), lambda b, q: (b, 0))
    in_specs = [
        blk_q(WIDTH_IDX), blk_q(LANES), blk_s(LANES), blk_q(WIDTH_A), blk_s(KV_RANK),
        pl.BlockSpec((None, KV_RANK + ONES_ROWS, seq), lambda b, q: (b, 0, 0)),
        _full(wuk_t.shape), _full(wuv_t.shape), _full(tz.shape),
    ]
    scratch = [
        pltpu.VMEM((N_HEADS_IDX * tq, HEAD_DIM_IDX), BF16),
        pltpu.VMEM((seq, tq), F32),
        pltpu.VMEM((seq, tq), F32),
        pltpu.VMEM((KV_RANK, N_HEADS_A * tq), BF16),
        pltpu.VMEM((seq, N_HEADS_A * tq), F32),
        pltpu.VMEM((KV_RANK + ONES_ROWS, N_HEADS_A * tq), F32),
        pltpu.VMEM((1, N_HEADS_A * tq), F32),
    ]
    return pl.pallas_call(
        kern, grid=(batch, nq), in_specs=in_specs,
        out_specs=pl.BlockSpec((tq, WIDTH_A), lambda b, q: (b * nq + q, 0)),
        out_shape=jax.ShapeDtypeStruct((T, WIDTH_A), BF16),
        scratch_shapes=scratch, compiler_params=_cparams(2), name="dsa")(
            qi, sm, sm, qa, ckv, ckv_t, wuk_t, wuv_t, tz)


def _t5_bucket(dist):
    max_exact = N_BUCKETS // 2
    d = jnp.maximum(dist, 0)
    ratio = jnp.log(jnp.maximum(d, 1).astype(F32) / max_exact) / math.log(MAX_DISTANCE / max_exact)
    large = jnp.minimum(max_exact + (ratio * (N_BUCKETS - max_exact)).astype(jnp.int32), N_BUCKETS - 1)
    return jnp.where(d < max_exact, d, large)


def _bias_tables(rel_bias):
    tq = Q_TILE
    span = 2 * tq
    assert int(_np_bucket(tq)) == N_BUCKETS - 1
    far = rel_bias[N_BUCKETS - 1]
    by_dist = ((rel_bias[_t5_bucket(jnp.arange(span))] - far).astype(F32) * LOG2E).T
    diag = jnp.concatenate([by_dist[:, :tq], jnp.zeros_like(by_dist[:, :tq])], axis=1)
    prev = jnp.concatenate([by_dist[:, tq:], by_dist[:, :tq]], axis=1)

    def toeplitz(f):
        m = jnp.tile(f, (1, tq))[:, :tq * (span - 1)].reshape(N_HEADS_A, tq, span - 1)
        return m[:, :, :tq]

    return jnp.stack([toeplitz(diag), toeplitz(prev)])


def _np_bucket(d):
    max_exact = N_BUCKETS // 2
    ratio = np.log(np.float32(max(d, 1)) / np.float32(max_exact)) / math.log(MAX_DISTANCE / max_exact)
    return min(max_exact + int(ratio * (N_BUCKETS - max_exact)), N_BUCKETS - 1) if d >= max_exact else d


def _mlstm_kernel(qk_ref, halo_ref, v_ref, sm_ref, o_ref, cw_ref, cb_ref, g_ref,
                  out_ref, c_ref, m_ref):
    c_idx = pl.program_id(1)

    @pl.when(c_idx == 0)
    def _init():
        c_ref[...] = jnp.zeros_like(c_ref)
        m_ref[...] = jnp.zeros_like(m_ref)

    for g in range(M_GROUPS):
        gates = _mlstm_gates(g, c_idx, qk_ref, halo_ref, sm_ref, cw_ref, cb_ref)
        outs = [_mlstm_head(g, h, gates, v_ref, o_ref, g_ref, c_ref, m_ref) for h in range(N_HEADS_M)]
        out_ref[g] = jnp.concatenate(outs, axis=-1).astype(BF16)


def _mlstm_gates(g, c_idx, qk_ref, halo_ref, sm_ref, cw_ref, cb_ref):
    L = M_CHUNK

    hw = halo_ref.shape[1]
    halo = jnp.where(c_idx > 0, halo_ref[g].astype(F32), 0.0)
    ext = jnp.concatenate([halo, qk_ref[g].astype(F32)], axis=0)
    acc = jnp.zeros((L, 2 * WIDTH_M), F32) + cb_ref[...]
    for w in range(CONV_WIDTH):
        off = hw - (CONV_WIDTH - 1) + w
        acc = acc + ext[off:off + L, :] * cw_ref[w:w + 1, :]
    qk = acc * _sigmoid(acc)

    sm = sm_ref[g]
    sm_t = sm.T
    r_i = lax.broadcasted_iota(jnp.int32, (L, L), 0)
    c_i = lax.broadcasted_iota(jnp.int32, (L, L), 1)
    tril = (c_i <= r_i).astype(BF16)
    bcum_c = sum(jnp.dot(tril, part, preferred_element_type=F32)
                 for part in _split3(jax.nn.log_sigmoid(sm)))
    return qk, sm, sm_t, bcum_c, bcum_c.T, c_i <= r_i


def _mlstm_head(g, h, gates, v_ref, o_ref, g_ref, c_ref, m_ref):
    L = M_CHUNK
    dm = HEAD_DIM_M
    qk, sm, sm_t, bcum_c, bcum_r, causal = gates
    ones_col = (lax.broadcasted_iota(jnp.int32, (L, dm), 1) == 0).astype(BF16)
    st = g * N_HEADS_M + h
    q = qk[:, h * dm:(h + 1) * dm].astype(BF16)
    k = (qk[:, WIDTH_M + h * dm:WIDTH_M + (h + 1) * dm] * (dm ** -0.5))
    v_aug = jnp.concatenate([v_ref[g, :, h * dm:(h + 1) * dm], ones_col], axis=-1)
    b_col = bcum_c[:, SM_F + h:SM_F + h + 1]
    g_col = sm[:, SM_I + h:SM_I + h + 1] - b_col
    g_row = sm_t[SM_I + h:SM_I + h + 1, :] - bcum_r[SM_F + h:SM_F + h + 1, :]
    b_last = b_col[L - 1:L, :]
    m_prev = m_ref[st]
    c_prev = c_ref[st]

    log_d = jnp.where(causal, b_col + g_row, NEG_INF)
    m_j = jnp.maximum(b_col + m_prev, jnp.max(log_d, axis=-1, keepdims=True))
    w_inter = jnp.exp(b_col + m_prev - m_j)
    qkt = lax.dot_general(q, k.astype(BF16), (((1,), (1,)), ((), ())), preferred_element_type=F32)
    s = qkt * jnp.exp(log_d - m_j)
    o_aug = jnp.dot(s.astype(BF16), v_aug, preferred_element_type=F32) + \
        w_inter * jnp.dot(q, c_prev.astype(BF16), preferred_element_type=F32)
    num = o_aug[:, :dm]
    den = o_aug[:, dm:dm + 1]
    hh = num / jnp.maximum(jnp.abs(den), jnp.exp(-m_j))

    lwe = b_last + g_col
    m_loc = jnp.max(lwe, axis=0, keepdims=True)
    kw = (k * jnp.exp(lwe - m_loc)).astype(BF16)
    c_loc = lax.dot_general(kw, v_aug, (((0,), (0,)), ((), ())), preferred_element_type=F32)
    m_new = jnp.maximum(b_last + m_prev, m_loc)
    c_ref[st] = jnp.exp(b_last + m_prev - m_new) * c_prev + jnp.exp(m_loc - m_new) * c_loc
    m_ref[st] = m_new

    hn = hh * lax.rsqrt(jnp.mean(hh * hh, axis=-1, keepdims=True) + LN_EPS) * g_ref[:, h * dm:(h + 1) * dm]
    og = o_ref[g, :, h * dm:(h + 1) * dm].astype(F32)
    return _sigmoid(og) * hn


def _mlstm(qk, v, sm, o, conv_w, conv_b, mh_g, batch, seq):
    L = M_CHUNK
    G = M_GROUPS
    nc = seq // L
    T = batch * seq
    hb = L // CONV_HALO
    grouped = lambda a: a.reshape(G, T // G, a.shape[-1])
    blk = lambda n: pl.BlockSpec((G, L, n), lambda b, c: (0, b * nc + c, 0))
    in_specs = [
        blk(2 * WIDTH_M),
        pl.BlockSpec((G, CONV_HALO, 2 * WIDTH_M), lambda b, c: (0, jnp.maximum((b * nc + c) * hb - 1, 0), 0)),
        blk(WIDTH_M), blk(LANES), blk(WIDTH_M),
        _full(conv_w.shape), _full(conv_b.shape), _full(mh_g.shape),
    ]
    scratch = [pltpu.VMEM((G * N_HEADS_M, HEAD_DIM_M, 2 * HEAD_DIM_M), F32),
               pltpu.VMEM((G * N_HEADS_M, 1, 1), F32)]
    qk_g = grouped(qk)
    out = pl.pallas_call(
        _mlstm_kernel, grid=(batch // G, nc), in_specs=in_specs, out_specs=blk(WIDTH_M),
        out_shape=jax.ShapeDtypeStruct((G, T // G, WIDTH_M), BF16), scratch_shapes=scratch,
        compiler_params=_cparams(2), name="mlstm")(
            qk_g, qk_g, grouped(v), grouped(sm), grouped(o), conv_w, conv_b, mh_g)
    return out.reshape(T, WIDTH_M)


def _layer_norm(y, g, b):
    mu = jnp.mean(y, axis=-1, keepdims=True)
    var = jnp.mean(jnp.square(y - mu), axis=-1, keepdims=True)
    return (y - mu) * lax.rsqrt(var + LN_EPS) * g + b


def _router_logits(x1, wr, br):
    x1_hi = x1.astype(BF16)
    x1_lo = (x1 - x1_hi.astype(F32)).astype(BF16)
    return (jnp.dot(x1_hi, wr[0], preferred_element_type=F32)
            + jnp.dot(x1_lo, wr[0], preferred_element_type=F32)
            + jnp.dot(x1_hi, wr[1], preferred_element_type=F32)) + br[...]


def _merge_kernel(x_ref, oa_ref, hm_ref, ga_ref, gm_ref, wua, wum, wout, l1g, l1b, wr, br,
                  x1_ref, route_ref, cnt_ref):
    pa = jnp.dot(oa_ref[...], wua[...], preferred_element_type=F32)
    pm = jnp.dot(hm_ref[...], wum[...], preferred_element_type=F32)
    y = _sigmoid(ga_ref[...]) * pa.astype(BF16) + _sigmoid(gm_ref[...]) * pm.astype(BF16)
    mix = jnp.dot(y.astype(BF16), wout[...], preferred_element_type=F32)
    x1 = _layer_norm(ALPHA * x_ref[...] + mix, l1g[...], l1b[...])
    x1_ref[:, 0:D_MODEL] = x1

    lt = _router_logits(x1, wr, br).T
    tm = lt.shape[1]
    big = jnp.int32(LANES)
    le = lt[0:N_EXPERTS, :]
    lg = lt[N_EXPERTS:N_EXPERTS + SUBLANES, :]
    row_g = lax.broadcasted_iota(jnp.int32, lg.shape, 0)
    row_e = lax.broadcasted_iota(jnp.int32, le.shape, 0)
    is_grp = row_g < N_GROUPS
    gl = jnp.where(is_grp, lg, NEG_INF)
    ge = jnp.exp(gl - jnp.max(gl, axis=0, keepdims=True))
    gp = ge / jnp.sum(ge, axis=0, keepdims=True)
    g_w = jnp.max(gp, axis=0, keepdims=True)
    g_idx = jnp.min(jnp.where(jnp.logical_and(is_grp, gp == g_w), row_g, big), axis=0, keepdims=True)
    el = jnp.where(jnp.right_shift(row_e, EPG_SHIFT) == g_idx, le, NEG_INF)
    m1 = jnp.max(el, axis=0, keepdims=True)
    i1 = jnp.min(jnp.where(el == m1, row_e, big), axis=0, keepdims=True)
    el2 = jnp.where(row_e == i1, NEG_INF, el)
    m2 = jnp.max(el2, axis=0, keepdims=True)
    i2 = jnp.min(jnp.where(el2 == m2, row_e, big), axis=0, keepdims=True)

    e2 = jnp.exp(m2 - m1)
    w1 = g_w / (1.0 + e2)
    w2 = g_w * e2 / (1.0 + e2)
    first_is_a = i1 < i2
    row_w = lax.broadcasted_iota(jnp.int32, (LANES, tm), 0)
    w_rows = jnp.where(row_w == 0, jnp.where(first_is_a, w1, w2),
                       jnp.where(row_w == 1, jnp.where(first_is_a, w2, w1), 0.0))
    x1_ref[:, D_MODEL:D_MODEL + LANES] = w_rows.T

    a = jnp.bitwise_and(jnp.minimum(i1, i2), EXPERTS_PER_GROUP - 1)
    b = jnp.bitwise_and(jnp.maximum(i1, i2), EXPERTS_PER_GROUP - 1)
    pair = jnp.right_shift(a * (2 * EXPERTS_PER_GROUP - 1 - a), 1) + (b - a - 1)
    cls = g_idx * PAIRS_PER_GROUP + pair
    row_c = lax.broadcasted_iota(jnp.int32, (CLASS_ROWS, tm), 0)
    onehot = (row_c == cls).astype(F32)

    @pl.when(pl.program_id(0) == 0)
    def _init():
        cnt_ref[...] = jnp.zeros_like(cnt_ref)

    earlier = (lax.broadcasted_iota(jnp.int32, (tm, tm), 0)
               < lax.broadcasted_iota(jnp.int32, (tm, tm), 1)).astype(BF16)
    prior = jnp.dot(onehot.astype(BF16), earlier, preferred_element_type=F32) + cnt_ref[:, 0:1]
    rank = jnp.sum(prior * onehot, axis=0, keepdims=True)
    cnt_ref[...] += jnp.sum(onehot, axis=1, keepdims=True)
    row_o = lax.broadcasted_iota(jnp.int32, route_ref.shape, 0)
    route_ref[...] = jnp.where(row_o == 0, cls.astype(F32), jnp.where(row_o == 1, rank, 0.0))


def _merge(x2, oa, hm, ga, gm, wua, wum, wout, l1g, l1b, wr, br, tm):
    T = x2.shape[0]
    blk = lambda n: pl.BlockSpec((tm, n), lambda i: (i, 0))
    in_specs = [blk(D_MODEL), blk(WIDTH_A), blk(WIDTH_M), blk(D_MODEL), blk(D_MODEL),
                _full(wua.shape), _full(wum.shape), _full(wout.shape), _full(l1g.shape), _full(l1b.shape),
                _full(wr.shape), _full(br.shape)]
    out_specs = [blk(D_MODEL + LANES), pl.BlockSpec((SUBLANES, tm), lambda i: (0, i)), _full((CLASS_ROWS, LANES))]
    out_shape = [jax.ShapeDtypeStruct((T, D_MODEL + LANES), F32), jax.ShapeDtypeStruct((SUBLANES, T), F32),
                 jax.ShapeDtypeStruct((CLASS_ROWS, LANES), F32)]
    return pl.pallas_call(
        _merge_kernel, grid=(T // tm,), in_specs=in_specs, out_specs=out_specs, out_shape=out_shape,
        compiler_params=_cparams(1), name="merge")(x2, oa, hm, ga, gm, wua, wum, wout, l1g, l1b, wr, br)


def _sc_mesh():
    return plsc.VectorSubcoreMesh(core_axis_name="c", subcore_axis_name="s",
                                  num_cores=SC_CORES, num_subcores=SC_SUBCORES)


def _sc_chunks(n_rows):
    workers = SC_CORES * SC_SUBCORES
    assert n_rows % (workers * SC_ROWS * 2) == 0
    return n_rows // (workers * SC_ROWS)


def _sc_scratch(n_chunks, width, dtype):
    return [pltpu.VMEM((n_chunks, SC_ROWS), jnp.int32),
            pltpu.VMEM((SC_ROWS, width), dtype), pltpu.VMEM((SC_ROWS, width), dtype),
            pltpu.SemaphoreType.DMA, pltpu.SemaphoreType.DMA]


def _sc_scatter_rows(rows, idx, n_out):
    n_in, width = rows.shape
    n_chunks = _sc_chunks(n_in)

    @functools.partial(
        pl.kernel, mesh=_sc_mesh(), out_type=jax.ShapeDtypeStruct((n_out, width), rows.dtype),
        scratch_types=_sc_scratch(n_chunks, width, rows.dtype), name="sc_dispatch")
    def scatter(rows_hbm, idx_hbm, out_hbm, idx_v, rows_a, rows_b, sem_a, sem_b):
        first = (lax.axis_index("s") * SC_CORES + lax.axis_index("c")) * n_chunks
        pltpu.sync_copy(idx_hbm.at[pl.ds(first, n_chunks)], idx_v)

        def load(c, buf):
            pltpu.sync_copy(rows_hbm.at[pl.ds((first + c) * SC_ROWS, SC_ROWS)], buf)

        def put(c, buf, sem):
            return pltpu.make_async_copy(buf, out_hbm.at[idx_v.at[c]], sem)

        load(0, rows_a)
        put(0, rows_a, sem_a).start()

        @pl.loop(0, n_chunks, step=2)
        def _(j):
            load(j + 1, rows_b)
            put(j + 1, rows_b, sem_b).start()
            put(j, rows_a, sem_a).wait()

            @pl.when(j + 2 < n_chunks)
            def _():
                load(j + 2, rows_a)
                put(j + 2, rows_a, sem_a).start()

            put(j + 1, rows_b, sem_b).wait()

    return scatter(rows, idx.reshape(n_in // SC_ROWS, SC_ROWS))


def _sc_gather_rows(table, idx):
    n_out, width = idx.shape[0], table.shape[1]
    n_chunks = _sc_chunks(n_out)

    @functools.partial(
        pl.kernel, mesh=_sc_mesh(), out_type=jax.ShapeDtypeStruct((n_out, width), table.dtype),
        scratch_types=_sc_scratch(n_chunks, width, table.dtype), name="sc_combine")
    def gather(table_hbm, idx_hbm, out_hbm, idx_v, rows_a, rows_b, sem_a, sem_b):
        first = (lax.axis_index("s") * SC_CORES + lax.axis_index("c")) * n_chunks
        pltpu.sync_copy(idx_hbm.at[pl.ds(first, n_chunks)], idx_v)

        def fetch(c, buf, sem):
            return pltpu.make_async_copy(table_hbm.at[idx_v.at[c]], buf, sem)

        def store(c, buf):
            pltpu.sync_copy(buf, out_hbm.at[pl.ds((first + c) * SC_ROWS, SC_ROWS)])

        fetch(0, rows_a, sem_a).start()

        @pl.loop(0, n_chunks, step=2)
        def _(j):
            fetch(j + 1, rows_b, sem_b).start()
            fetch(j, rows_a, sem_a).wait()
            store(j, rows_a)

            @pl.when(j + 2 < n_chunks)
            def _():
                fetch(j + 2, rows_a, sem_a).start()

            fetch(j + 1, rows_b, sem_b).wait()
            store(j + 1, rows_b)

    return gather(table, idx.reshape(n_out // SC_ROWS, SC_ROWS))


def _moe_kernel(ta_ref, tb_ref, nv_ref, xs_ref, wga, wua, wda, wgb, wub, wdb, l2g, l2b, ys_ref):
    i = pl.program_id(0)

    @pl.when(nv_ref[i] > 0)
    def _compute():
        x = xs_ref[:, 0:D_MODEL]
        xb = x.astype(BF16)
        w_a = xs_ref[:, D_MODEL:D_MODEL + 1]
        w_b = xs_ref[:, D_MODEL + 1:D_MODEL + 2]

        def expert(wg, wu, wd):
            g = jnp.dot(xb, wg[...], preferred_element_type=F32)
            u = jnp.dot(xb, wu[...], preferred_element_type=F32)
            hdn = (g * _sigmoid(g) * u).astype(BF16)
            return jnp.dot(hdn, wd[...], preferred_element_type=F32)

        ffn = w_a * expert(wga, wua, wda) + w_b * expert(wgb, wub, wdb)
        ys_ref[...] = _layer_norm(ALPHA * x + ffn, l2g[...], l2b[...])


def _moe(xs, tile_a, tile_b, n_valid, wg, wu, wd, l2g, l2b):
    tm = MOE_TILE
    n_tiles = xs.shape[0] // tm
    rows = lambda n: pl.BlockSpec((tm, n), lambda i, ta, tb, nv: (i, 0))
    w_in = lambda which: pl.BlockSpec((None, D_MODEL, D_EXPERT), lambda i, ta, tb, nv: ((ta, tb)[which][i], 0, 0))
    w_out = lambda which: pl.BlockSpec((None, D_EXPERT, D_MODEL), lambda i, ta, tb, nv: ((ta, tb)[which][i], 0, 0))
    const = lambda shape: pl.BlockSpec(shape, lambda i, ta, tb, nv: (0,) * len(shape))
    grid_spec = pltpu.PrefetchScalarGridSpec(
        num_scalar_prefetch=3, grid=(n_tiles,),
        in_specs=[rows(xs.shape[1]), w_in(0), w_in(0), w_out(0), w_in(1), w_in(1), w_out(1),
                  const(l2g.shape), const(l2b.shape)],
        out_specs=rows(D_MODEL))
    return pl.pallas_call(
        _moe_kernel, grid_spec=grid_spec, out_shape=jax.ShapeDtypeStruct((xs.shape[0], D_MODEL), F32),
        compiler_params=_cparams(1), name="moe")(
            tile_a, tile_b, n_valid, xs, wg, wu, wd, wg, wu, wd, l2g, l2b)


def _route_tables(route, counts, n_tokens):
    tm = MOE_TILE
    n_tiles = n_tokens // tm + N_CLASSES
    cnt = counts[:N_CLASSES, 0].astype(jnp.int32)
    tiles = (cnt + tm - 1) // tm
    tile_end = jnp.cumsum(tiles)
    tile_start = tile_end - tiles
    t_idx = jnp.arange(n_tiles, dtype=jnp.int32)
    cls_of_tile = jnp.minimum(jnp.sum(t_idx[:, None] >= tile_end[None, :], axis=1), N_CLASSES - 1).astype(jnp.int32)
    classes = np.arange(N_CLASSES)
    tile_is = cls_of_tile[:, None] == classes[None, :]
    per_tile = lambda table: jnp.sum(jnp.where(tile_is, jnp.asarray(table, jnp.int32)[None, :], 0), axis=1)
    n_valid = jnp.clip(per_tile(cnt) - (t_idx - per_tile(tile_start)) * tm, 0, tm).astype(jnp.int32)
    first_expert = classes // PAIRS_PER_GROUP * EXPERTS_PER_GROUP
    tile_a = per_tile(first_expert + np.asarray(PAIR_A)[classes % PAIRS_PER_GROUP])
    tile_b = per_tile(first_expert + np.asarray(PAIR_B)[classes % PAIRS_PER_GROUP])
    cls = route[0].astype(jnp.int32)
    rank = route[1].astype(jnp.int32)
    row0 = jnp.sum(jnp.where(cls[:, None] == jnp.arange(N_CLASSES)[None, :], (tile_start * tm)[None, :], 0), axis=1)
    return tile_a, tile_b, n_valid, row0 + rank, n_tiles * tm


def _pick_tile(T, pref):
    t = pref
    while T % t:
        t //= 2
    return t


def kernel(x, w_in, conv_w, conv_b, kv_norm_g, w_uk, w_uv, rel_bias, b_i, b_f, mh_norm_g, w_up_a, w_up_m,
           w_out, ln1_g, ln1_b, w_grp, b_grp, w_rt, b_rt, w_gate, w_up, w_down, ln2_g, ln2_b):
    B, S, _ = x.shape
    T = B * S
    assert S % Q_TILE == 0 and S % M_CHUNK == 0 and T % MOE_TILE == 0 and w_in.shape[0] == DEPTH
    tz = _bias_tables(rel_bias)
    x2 = x.reshape(T, D_MODEL)
    for l in range(DEPTH):
        w = w_in[l]
        o = np.cumsum((WIDTH_A, KV_RANK, WIDTH_IDX, HEAD_DIM_IDX, N_HEADS_IDX, 2 * WIDTH_M, WIDTH_M,
                       N_HEADS_M, N_HEADS_M, WIDTH_M, D_MODEL, D_MODEL)).tolist()
        o = [0] + o
        seg = lambda j: w[:, o[j]:o[j + 1]]
        pad = LANES - (HEAD_DIM_IDX + N_HEADS_IDX + 2 * N_HEADS_M)
        w_small = jnp.concatenate([seg(3), seg(4), seg(7), seg(8), jnp.zeros((D_MODEL, pad), w.dtype)], axis=1)
        ws = [seg(0), seg(1), seg(2), w_small, seg(5), seg(6), seg(9), seg(10), seg(11)]
        ws = [a.astype(BF16) for a in ws]
        smb = jnp.zeros((1, LANES), F32).at[0, SM_I:SM_I + N_HEADS_M].set(b_i[l]) \
            .at[0, SM_F:SM_F + N_HEADS_M].set(b_f[l])
        qa, ckv, qi, sm, qk, v, og, ga, gm, ckv_t = _proj(x2, ws, kv_norm_g[l][None, :], smb,
                                                          _pick_tile(S, 512), S)

        wuk_t = jnp.transpose(w_uk[l], (1, 0, 2)).astype(BF16)
        wuv_t = jnp.transpose(w_uv[l], (1, 2, 0)).astype(BF16)
        oa = _dsa(qi, sm, qa, ckv, ckv_t, wuk_t, wuv_t, tz, B, S)

        hm = _mlstm(qk, v, sm, og, conv_w[l], conv_b[l][None, :], mh_norm_g[l].reshape(1, WIDTH_M), B, S)

        w_router = jnp.concatenate(
            [w_rt[l], w_grp[l], jnp.zeros((D_MODEL, LANES - N_EXPERTS - N_GROUPS), F32)], axis=1)
        b_router = jnp.concatenate(
            [b_rt[l], b_grp[l], jnp.zeros((LANES - N_EXPERTS - N_GROUPS,), F32)])[None, :]
        wr_hi = w_router.astype(BF16)
        wr_split = jnp.stack([wr_hi, (w_router - wr_hi.astype(F32)).astype(BF16)])
        x1, route, counts = _merge(x2, oa, hm, ga, gm, w_up_a[l].astype(BF16), w_up_m[l].astype(BF16),
                                   w_out[l].astype(BF16), ln1_g[l][None, :], ln1_b[l][None, :],
                                   wr_split, b_router, _pick_tile(T, 256))

        tile_a, tile_b, n_valid, pos, n_sorted = _route_tables(route, counts, T)
        xs = _sc_scatter_rows(x1, pos, n_sorted)
        ys = _moe(xs, tile_a, tile_b, n_valid, w_gate[l].astype(BF16), w_up[l].astype(BF16),
                  w_down[l].astype(BF16), ln2_g[l][None, :], ln2_b[l][None, :])
        x2 = _sc_gather_rows(ys, pos)
    return x2.reshape(B, S, D_MODEL)
```

```python
import functools
import math

import jax
import jax.numpy as jnp
import numpy as np
from jax import lax
from jax.experimental import pallas as pl
from jax.experimental.pallas import tpu as pltpu
from jax.experimental.pallas import tpu_sc as plsc

F32 = jnp.float32
BF16 = jnp.bfloat16

D_MODEL = 1024
N_HEADS_A = 8
HEAD_DIM_A = 64
WIDTH_A = N_HEADS_A * HEAD_DIM_A
KV_RANK = 256
N_HEADS_IDX = 8
HEAD_DIM_IDX = 64
WIDTH_IDX = N_HEADS_IDX * HEAD_DIM_IDX
TOPK_MAX = 256
N_BUCKETS = 32
MAX_DISTANCE = 128
N_HEADS_M = 4
HEAD_DIM_M = 128
WIDTH_M = N_HEADS_M * HEAD_DIM_M
CONV_WIDTH = 4
N_GROUPS = 4
EXPERTS_PER_GROUP = 4
N_EXPERTS = N_GROUPS * EXPERTS_PER_GROUP
D_EXPERT = 512
LN_EPS = 1e-5
DEPTH = 1
ALPHA = (2.0 * DEPTH) ** 0.25

LANES = 128
SUBLANES = 8
VMEM_LIMIT = 56 * 1024 * 1024

SM_KIDX = 0
SM_WIDX = HEAD_DIM_IDX
SM_I = SM_WIDX + N_HEADS_IDX
SM_F = SM_I + N_HEADS_M

Q_TILE = 256
K_CHUNK = Q_TILE
ONES_ROWS = 16
M_CHUNK = 128
M_GROUPS = 2
CONV_HALO = 16
assert CONV_HALO >= CONV_WIDTH - 1
MOE_TILE = 256
TOKEN_PARTS = 2
SC_CORES = 2
SC_SUBCORES = 16
SC_ROWS = 32
EPG_SHIFT = EXPERTS_PER_GROUP.bit_length() - 1
assert 1 << EPG_SHIFT == EXPERTS_PER_GROUP
PAIR_A, PAIR_B = zip(*[(a, b) for a in range(EXPERTS_PER_GROUP) for b in range(a + 1, EXPERTS_PER_GROUP)])
PAIRS_PER_GROUP = len(PAIR_A)
N_CLASSES = N_GROUPS * PAIRS_PER_GROUP
CLASS_ROWS = -(-N_CLASSES // SUBLANES) * SUBLANES
BISECT_STEPS_PER_CHECK = 3
BISECT_MAX_CHECKS = 5
PEEL_BRACKET = 2.0
NEG_INF = float("-inf")
LOG2E = math.log2(math.e)


def _cparams(n_grid):
    return pltpu.CompilerParams(dimension_semantics=("arbitrary",) * n_grid,
                                vmem_limit_bytes=VMEM_LIMIT)


def _full(shape):
    nd = len(shape)
    return pl.BlockSpec(shape, lambda *_: (0,) * nd)


def _sigmoid(x):
    return 0.5 * jnp.tanh(0.5 * x) + 0.5


def _split3(x):
    hi = x.astype(BF16)
    r = x - hi.astype(F32)
    mid = r.astype(BF16)
    return hi, mid, (r - mid.astype(F32)).astype(BF16)


def _proj_kernel(x_ref, wqa, wckv, wqi, wsm, wqk, wv, wo, wga, wgm, kvg, smb,
                 qa_o, ckv_o, qi_o, sm_o, qk_o, v_o, o_o, ga_o, gm_o, ckvt_o):
    xb = x_ref[...].astype(BF16)

    def mm(w):
        return jnp.dot(xb, w[...], preferred_element_type=F32)

    qa_o[...] = mm(wqa).astype(BF16)
    c = mm(wckv)
    c = c * lax.rsqrt(jnp.mean(c * c, axis=-1, keepdims=True) + LN_EPS) * kvg[...]
    ckv_o[...] = c.astype(BF16)
    ckvt_o[0:KV_RANK, :] = c.T.astype(BF16)
    ckvt_o[KV_RANK:KV_RANK + ONES_ROWS, :] = jnp.ones((ONES_ROWS, c.shape[0]), BF16)
    qi_o[...] = mm(wqi).astype(BF16)
    sm_o[...] = mm(wsm) + smb[...]
    qk_o[...] = mm(wqk).astype(BF16)
    v_o[...] = mm(wv).astype(BF16)
    o_o[...] = mm(wo).astype(BF16)
    ga_o[...] = mm(wga).astype(BF16)
    gm_o[...] = mm(wgm).astype(BF16)


def _proj(x2, ws, kvg, smb, tm, seq):
    T = x2.shape[0]
    per_seq = seq // tm
    widths = [w.shape[1] for w in ws]
    dts = [BF16, BF16, BF16, F32, BF16, BF16, BF16, BF16, BF16]
    in_specs = [pl.BlockSpec((tm, D_MODEL), lambda i: (i, 0))]
    in_specs += [_full(w.shape) for w in ws]
    in_specs += [_full(kvg.shape), _full(smb.shape)]
    out_specs = [pl.BlockSpec((tm, n), lambda i: (i, 0)) for n in widths]
    out_specs.append(pl.BlockSpec((None, KV_RANK + ONES_ROWS, tm), lambda i: (i // per_seq, 0, i % per_seq)))
    out_shape = [jax.ShapeDtypeStruct((T, n), dt) for n, dt in zip(widths, dts)]
    out_shape.append(jax.ShapeDtypeStruct((T // seq, KV_RANK + ONES_ROWS, seq), BF16))
    return pl.pallas_call(
        _proj_kernel, grid=(T // tm,), in_specs=in_specs, out_specs=out_specs, out_shape=out_shape,
        compiler_params=_cparams(1), name="proj")(x2, *ws, kvg, smb)


def _dsa_kernel(qi_ref, smq_ref, smk_ref, qa_ref, ckv_ref, ckvt_ref, wuk_ref, wuvt_ref, tz_ref,
                oa_ref, qs_ref, sc_ref, am_ref, ql_ref, x_ref, acc_ref, m_ref, *, topk):
    tq, kc = Q_TILE, K_CHUNK
    nh = N_HEADS_A
    qb = pl.program_id(1)
    n_ch = qb + 1
    seq_keys = n_ch * kc
    t0 = qb * tq

    def rows(c):
        return pl.ds(pl.multiple_of(c * kc, kc), kc)

    def lanes(h):
        return slice(h * tq, (h + 1) * tq)

    def chunk_loop(n, body, init):
        def pair(p, carry):
            return body(2 * p + 1, body(2 * p, carry))
        carry = lax.fori_loop(0, jnp.right_shift(n, 1), pair, init)
        return lax.cond(jnp.bitwise_and(n, 1) == 1, lambda c: body(n - 1, c), lambda c: c, carry)

    for h in range(N_HEADS_IDX):
        qs_ref[h * tq:(h + 1) * tq, :] = qi_ref[:, h * HEAD_DIM_IDX:(h + 1) * HEAD_DIM_IDX]
    w_t = smq_ref[...].T
    q_pos = lax.broadcasted_iota(jnp.int32, (1, tq), 1) + t0
    key_iota = lax.broadcasted_iota(jnp.int32, (kc, tq), 0)

    def score_chunk(c, carry):
        mx, mn = carry
        kk = smk_ref[rows(c), SM_KIDX:SM_KIDX + HEAD_DIM_IDX].astype(BF16)
        dots = lax.dot_general(kk, qs_ref[...], (((1,), (1,)), ((), ())), preferred_element_type=F32)
        sc = jnp.zeros((kc, tq), F32)
        for h in range(N_HEADS_IDX):
            sc = sc + w_t[SM_WIDX + h:SM_WIDX + h + 1, :] * jnp.maximum(dots[:, lanes(h)], 0.0)
        vis = (key_iota + c * kc) <= q_pos
        sc_ref[rows(c), :] = jnp.where(vis, sc, NEG_INF)
        mx = jnp.maximum(mx, jnp.max(jnp.where(vis, sc, NEG_INF), axis=0, keepdims=True))
        mn = jnp.minimum(mn, jnp.min(jnp.where(vis, sc, jnp.inf), axis=0, keepdims=True))
        return mx, mn

    mx, mn = chunk_loop(n_ch, score_chunk,
                        (jnp.full((1, tq), NEG_INF, F32), jnp.full((1, tq), jnp.inf, F32)))

    n_vis = (q_pos + 1).astype(F32)
    k_row = jnp.minimum(n_vis, float(topk))

    def count(pred):
        def body(c, a):
            hit = pred(sc_ref[rows(c), :]).astype(F32)
            return a + jnp.sum(hit.reshape(kc // SUBLANES, SUBLANES, tq), axis=0)
        a = chunk_loop(n_ch, body, jnp.zeros((SUBLANES, tq), F32))
        return jnp.sum(a, axis=0, keepdims=True)

    def any_lane(flag):
        return jnp.max(jnp.where(flag, 1.0, 0.0)) > 0.0

    def crowded(cnt_lo, c_hi):
        return any_lane(jnp.logical_and(cnt_lo != k_row, cnt_lo - c_hi > PEEL_BRACKET))

    def bisect_cond(carry):
        it, lo, hi, cnt_lo, c_hi = carry
        return jnp.logical_and(it < BISECT_MAX_CHECKS, crowded(cnt_lo, c_hi))

    def bisect_body(carry):
        it, lo, hi, cnt_lo, c_hi = carry
        for _ in range(BISECT_STEPS_PER_CHECK):
            mid = lo * 0.5 + hi * 0.5
            cnt = count(lambda s: s >= mid)
            ge = cnt >= k_row
            lo, cnt_lo = jnp.where(ge, mid, lo), jnp.where(ge, cnt, cnt_lo)
            hi, c_hi = jnp.where(ge, hi, mid), jnp.where(ge, c_hi, cnt)
        return it + 1, lo, hi, cnt_lo, c_hi

    hi0 = mx + jnp.maximum(jnp.abs(mx), 1e-30) * 1e-6
    _, lo, hi, cnt_lo, c_hi = lax.while_loop(
        bisect_cond, bisect_body, (jnp.int32(0), mn, hi0, n_vis, jnp.zeros((1, tq), F32)))

    def peel_cond(carry):
        it, lo, hi, cnt_lo, c_hi, done = carry
        return jnp.logical_and(it < seq_keys, any_lane(done == 0.0))

    def peel_body(carry):
        it, lo, hi, cnt_lo, c_hi, done = carry

        def top_body(c, v):
            s = sc_ref[rows(c), :]
            inside = jnp.logical_and(s >= lo, s < hi)
            return jnp.maximum(v, jnp.max(jnp.where(inside, s, NEG_INF), axis=0, keepdims=True))

        v = chunk_loop(n_ch, top_body, jnp.full((1, tq), NEG_INF, F32))
        c_v = count(lambda s: s >= v)
        reached = c_v >= k_row
        live = done == 0.0
        fin = jnp.logical_and(live, reached)
        cut = jnp.logical_and(live, jnp.logical_not(reached))
        return (it + 1, jnp.where(fin, v, lo), jnp.where(cut, v, hi), jnp.where(fin, c_v, cnt_lo),
                jnp.where(cut, c_v, c_hi), jnp.where(fin, 1.0, done))

    _, lo, hi, cnt_lo, c_hi, _ = lax.while_loop(
        peel_cond, peel_body,
        (jnp.int32(0), lo, hi, cnt_lo, c_hi, jnp.where(cnt_lo == k_row, 1.0, 0.0)))

    def mask_chunk(c, _):
        am_ref[rows(c), :] = jnp.where(sc_ref[rows(c), :] >= lo, 0.0, NEG_INF)
        return 0

    chunk_loop(n_ch, mask_chunk, 0)

    tied = cnt_lo != k_row
    for part in range(tq // LANES):
        ls = slice(part * LANES, (part + 1) * LANES)

        @pl.when(any_lane(tied[:, ls]))
        def _ties():
            need, lo_p, hi_p = (k_row - c_hi)[:, ls], lo[:, ls], hi[:, ls]
            lower = (lax.broadcasted_iota(jnp.int32, (kc, kc), 1)
                     < lax.broadcasted_iota(jnp.int32, (kc, kc), 0)).astype(BF16)

            def tie_chunk(c, before):
                s = sc_ref[rows(c), ls]
                above = s >= hi_p
                tie = jnp.logical_and(s >= lo_p, jnp.logical_not(above))
                tie_f = tie.astype(F32)
                rank = jnp.dot(lower, tie_f.astype(BF16), preferred_element_type=F32) + before
                sel = jnp.logical_or(above, jnp.logical_and(tie, rank < need))
                am_ref[rows(c), ls] = jnp.where(sel, 0.0, NEG_INF)
                return before + jnp.sum(tie_f, axis=0, keepdims=True)

            chunk_loop(n_ch, tie_chunk, jnp.zeros((1, LANES), F32))

    for h in range(nh):
        qh = qa_ref[:, h * HEAD_DIM_A:(h + 1) * HEAD_DIM_A]
        qlt = lax.dot_general(wuk_ref[h], qh, (((1,), (1,)), ((), ())), preferred_element_type=F32)
        ql_ref[:, lanes(h)] = (qlt * (HEAD_DIM_A ** -0.5 * LOG2E)).astype(BF16)

    m_ref[...] = jnp.full(m_ref.shape, NEG_INF, F32)

    def pass_a(c, table):
        raw = jnp.dot(ckv_ref[rows(c), :], ql_ref[...], preferred_element_type=F32)
        am = am_ref[rows(c), :]
        for h in range(nh):
            x = raw[:, lanes(h)] + am
            if table is not None:
                x = x + tz_ref[table, h]
            x_ref[rows(c), lanes(h)] = x
            m_ref[:, lanes(h)] = jnp.maximum(m_ref[:, lanes(h)], jnp.max(x, axis=0, keepdims=True))

    def far_chunk(c, _):
        pass_a(c, None)
        return 0

    chunk_loop(jnp.maximum(qb - 1, 0), far_chunk, 0)

    @pl.when(qb >= 1)
    def _prev():
        pass_a(qb - 1, 1)

    pass_a(qb, 0)

    def pv(c):
        p = jnp.exp2((x_ref[rows(c), :] - m_ref[...]).astype(BF16))
        return jnp.dot(ckvt_ref[:, rows(c)], p, preferred_element_type=F32)

    def pass_b(c, _):
        acc_ref[...] += pv(c)
        return 0

    acc_ref[...] = pv(qb)
    chunk_loop(qb, pass_b, 0)
    inv_l = 1.0 / acc_ref[KV_RANK:KV_RANK + 1, :]
    o_lat = (acc_ref[0:KV_RANK, :] * inv_l).astype(BF16)
    outs = [jnp.dot(wuvt_ref[h], o_lat[:, lanes(h)], preferred_element_type=F32) for h in range(nh)]
    oa_ref[...] = jnp.concatenate(outs, axis=0).T.astype(BF16)


def _dsa(qi, sm, qa, ckv, ckv_t, wuk_t, wuv_t, tz, batch, seq):
    tq = Q_TILE
    nq = seq // tq
    topk = min(TOPK_MAX, seq // 4)
    T = batch * seq
    kern = functools.partial(_dsa_kernel, topk=topk)
    blk_q = lambda n: pl.BlockSpec((tq, n), lambda b, q: (b * nq + q, 0))
    blk_s = lambda n: pl.BlockSpec((seq, n), lambda b, q: (b, 0))
    in_specs = [
        blk_q(WIDTH_IDX), blk_q(LANES), blk_s(LANES), blk_q(WIDTH_A), blk_s(KV_RANK),
        pl.BlockSpec((None, KV_RANK + ONES_ROWS, seq), lambda b, q: (b, 0, 0)),
        _full(wuk_t.shape), _full(wuv_t.shape), _full(tz.shape),
    ]
    scratch = [
        pltpu.VMEM((N_HEADS_IDX * tq, HEAD_DIM_IDX), BF16),
        pltpu.VMEM((seq, tq), F32),
        pltpu.VMEM((seq, tq), F32),
        pltpu.VMEM((KV_RANK, N_HEADS_A * tq), BF16),
        pltpu.VMEM((seq, N_HEADS_A * tq), F32),
        pltpu.VMEM((KV_RANK + ONES_ROWS, N_HEADS_A * tq), F32),
        pltpu.VMEM((1, N_HEADS_A * tq), F32),
    ]
    return pl.pallas_call(
        kern, grid=(batch, nq), in_specs=in_specs,
        out_specs=pl.BlockSpec((tq, WIDTH_A), lambda b, q: (b * nq + q, 0)),
        out_shape=jax.ShapeDtypeStruct((T, WIDTH_A), BF16),
        scratch_shapes=scratch, compiler_params=_cparams(2), name="dsa")(
            qi, sm, sm, qa, ckv, ckv_t, wuk_t, wuv_t, tz)


def _t5_bucket(dist):
    max_exact = N_BUCKETS // 2
    d = jnp.maximum(dist, 0)
    ratio = jnp.log(jnp.maximum(d, 1).astype(F32) / max_exact) / math.log(MAX_DISTANCE / max_exact)
    large = jnp.minimum(max_exact + (ratio * (N_BUCKETS - max_exact)).astype(jnp.int32), N_BUCKETS - 1)
    return jnp.where(d < max_exact, d, large)


def _bias_tables(rel_bias):
    tq = Q_TILE
    span = 2 * tq
    assert int(_np_bucket(tq)) == N_BUCKETS - 1
    far = rel_bias[N_BUCKETS - 1]
    by_dist = ((rel_bias[_t5_bucket(jnp.arange(span))] - far).astype(F32) * LOG2E).T
    diag = jnp.concatenate([by_dist[:, :tq], jnp.zeros_like(by_dist[:, :tq])], axis=1)
    prev = jnp.concatenate([by_dist[:, tq:], by_dist[:, :tq]], axis=1)

    def toeplitz(f):
        m = jnp.tile(f, (1, tq))[:, :tq * (span - 1)].reshape(N_HEADS_A, tq, span - 1)
        return m[:, :, :tq]

    return jnp.stack([toeplitz(diag), toeplitz(prev)])


def _np_bucket(d):
    max_exact = N_BUCKETS // 2
    ratio = np.log(np.float32(max(d, 1)) / np.float32(max_exact)) / math.log(MAX_DISTANCE / max_exact)
    return min(max_exact + int(ratio * (N_BUCKETS - max_exact)), N_BUCKETS - 1) if d >= max_exact else d


def _mlstm_kernel(qk_ref, halo_ref, v_ref, sm_ref, o_ref, cw_ref, cb_ref, g_ref,
                  out_ref, c_ref, m_ref):
    c_idx = pl.program_id(1)

    @pl.when(c_idx == 0)
    def _init():
        c_ref[...] = jnp.zeros_like(c_ref)
        m_ref[...] = jnp.zeros_like(m_ref)

    for g in range(M_GROUPS):
        gates = _mlstm_gates(g, c_idx, qk_ref, halo_ref, sm_ref, cw_ref, cb_ref)
        outs = [_mlstm_head(g, h, gates, v_ref, o_ref, g_ref, c_ref, m_ref) for h in range(N_HEADS_M)]
        out_ref[g] = jnp.concatenate(outs, axis=-1).astype(BF16)


def _mlstm_gates(g, c_idx, qk_ref, halo_ref, sm_ref, cw_ref, cb_ref):
    L = M_CHUNK

    hw = halo_ref.shape[1]
    halo = jnp.where(c_idx > 0, halo_ref[g].astype(F32), 0.0)
    ext = jnp.concatenate([halo, qk_ref[g].astype(F32)], axis=0)
    acc = jnp.zeros((L, 2 * WIDTH_M), F32) + cb_ref[...]
    for w in range(CONV_WIDTH):
        off = hw - (CONV_WIDTH - 1) + w
        acc = acc + ext[off:off + L, :] * cw_ref[w:w + 1, :]
    qk = acc * _sigmoid(acc)

    sm = sm_ref[g]
    sm_t = sm.T
    r_i = lax.broadcasted_iota(jnp.int32, (L, L), 0)
    c_i = lax.broadcasted_iota(jnp.int32, (L, L), 1)
    tril = (c_i <= r_i).astype(BF16)
    bcum_c = sum(jnp.dot(tril, part, preferred_element_type=F32)
                 for part in _split3(jax.nn.log_sigmoid(sm)))
    return qk, sm, sm_t, bcum_c, bcum_c.T, c_i <= r_i


def _mlstm_head(g, h, gates, v_ref, o_ref, g_ref, c_ref, m_ref):
    L = M_CHUNK
    dm = HEAD_DIM_M
    qk, sm, sm_t, bcum_c, bcum_r, causal = gates
    ones_col = (lax.broadcasted_iota(jnp.int32, (L, dm), 1) == 0).astype(BF16)
    st = g * N_HEADS_M + h
    q = qk[:, h * dm:(h + 1) * dm].astype(BF16)
    k = (qk[:, WIDTH_M + h * dm:WIDTH_M + (h + 1) * dm] * (dm ** -0.5))
    v_aug = jnp.concatenate([v_ref[g, :, h * dm:(h + 1) * dm], ones_col], axis=-1)
    b_col = bcum_c[:, SM_F + h:SM_F + h + 1]
    g_col = sm[:, SM_I + h:SM_I + h + 1] - b_col
    g_row = sm_t[SM_I + h:SM_I + h + 1, :] - bcum_r[SM_F + h:SM_F + h + 1, :]
    b_last = b_col[L - 1:L, :]
    m_prev = m_ref[st]
    c_prev = c_ref[st]

    log_d = jnp.where(causal, b_col + g_row, NEG_INF)
    m_j = jnp.maximum(b_col + m_prev, jnp.max(log_d, axis=-1, keepdims=True))
    w_inter = jnp.exp(b_col + m_prev - m_j)
    qkt = lax.dot_general(q, k.astype(BF16), (((1,), (1,)), ((), ())), preferred_element_type=F32)
    s = qkt * jnp.exp(log_d - m_j)
    o_aug = jnp.dot(s.astype(BF16), v_aug, preferred_element_type=F32) + \
        w_inter * jnp.dot(q, c_prev.astype(BF16), preferred_element_type=F32)
    num = o_aug[:, :dm]
    den = o_aug[:, dm:dm + 1]
    hh = num / jnp.maximum(jnp.abs(den), jnp.exp(-m_j))

    lwe = b_last + g_col
    m_loc = jnp.max(lwe, axis=0, keepdims=True)
    kw = (k * jnp.exp(lwe - m_loc)).astype(BF16)
    c_loc = lax.dot_general(kw, v_aug, (((0,), (0,)), ((), ())), preferred_element_type=F32)
    m_new = jnp.maximum(b_last + m_prev, m_loc)
    c_ref[st] = jnp.exp(b_last + m_prev - m_new) * c_prev + jnp.exp(m_loc - m_new) * c_loc
    m_ref[st] = m_new

    hn = hh * lax.rsqrt(jnp.mean(hh * hh, axis=-1, keepdims=True) + LN_EPS) * g_ref[:, h * dm:(h + 1) * dm]
    og = o_ref[g, :, h * dm:(h + 1) * dm].astype(F32)
    return _sigmoid(og) * hn


def _mlstm(qk, v, sm, o, conv_w, conv_b, mh_g, batch, seq):
    L = M_CHUNK
    G = M_GROUPS
    nc = seq // L
    T = batch * seq
    hb = L // CONV_HALO
    grouped = lambda a: a.reshape(G, T // G, a.shape[-1])
    blk = lambda n: pl.BlockSpec((G, L, n), lambda b, c: (0, b * nc + c, 0))
    in_specs = [
        blk(2 * WIDTH_M),
        pl.BlockSpec((G, CONV_HALO, 2 * WIDTH_M), lambda b, c: (0, jnp.maximum((b * nc + c) * hb - 1, 0), 0)),
        blk(WIDTH_M), blk(LANES), blk(WIDTH_M),
        _full(conv_w.shape), _full(conv_b.shape), _full(mh_g.shape),
    ]
    scratch = [pltpu.VMEM((G * N_HEADS_M, HEAD_DIM_M, 2 * HEAD_DIM_M), F32),
               pltpu.VMEM((G * N_HEADS_M, 1, 1), F32)]
    qk_g = grouped(qk)
    out = pl.pallas_call(
        _mlstm_kernel, grid=(batch // G, nc), in_specs=in_specs, out_specs=blk(WIDTH_M),
        out_shape=jax.ShapeDtypeStruct((G, T // G, WIDTH_M), BF16), scratch_shapes=scratch,
        compiler_params=_cparams(2), name="mlstm")(
            qk_g, qk_g, grouped(v), grouped(sm), grouped(o), conv_w, conv_b, mh_g)
    return out.reshape(T, WIDTH_M)


def _layer_norm(y, g, b):
    mu = jnp.mean(y, axis=-1, keepdims=True)
    var = jnp.mean(jnp.square(y - mu), axis=-1, keepdims=True)
    return (y - mu) * lax.rsqrt(var + LN_EPS) * g + b


def _router_logits(x1, wr, br):
    x1_hi = x1.astype(BF16)
    x1_lo = (x1 - x1_hi.astype(F32)).astype(BF16)
    return (jnp.dot(x1_hi, wr[0], preferred_element_type=F32)
            + jnp.dot(x1_lo, wr[0], preferred_element_type=F32)
            + jnp.dot(x1_hi, wr[1], preferred_element_type=F32)) + br[...]


def _merge_kernel(x_ref, oa_ref, hm_ref, ga_ref, gm_ref, wua, wum, wout, l1g, l1b, wr, br,
                  x1_ref, route_ref, cnt_ref):
    pa = jnp.dot(oa_ref[...], wua[...], preferred_element_type=F32)
    pm = jnp.dot(hm_ref[...], wum[...], preferred_element_type=F32)
    y = _sigmoid(ga_ref[...]) * pa.astype(BF16) + _sigmoid(gm_ref[...]) * pm.astype(BF16)
    mix = jnp.dot(y.astype(BF16), wout[...], preferred_element_type=F32)
    x1 = _layer_norm(ALPHA * x_ref[...] + mix, l1g[...], l1b[...])
    x1_ref[:, 0:D_MODEL] = x1

    lt = _router_logits(x1, wr, br).T
    tm = lt.shape[1]
    big = jnp.int32(LANES)
    le = lt[0:N_EXPERTS, :]
    lg = lt[N_EXPERTS:N_EXPERTS + SUBLANES, :]
    row_g = lax.broadcasted_iota(jnp.int32, lg.shape, 0)
    row_e = lax.broadcasted_iota(jnp.int32, le.shape, 0)
    is_grp = row_g < N_GROUPS
    gl = jnp.where(is_grp, lg, NEG_INF)
    ge = jnp.exp(gl - jnp.max(gl, axis=0, keepdims=True))
    gp = ge / jnp.sum(ge, axis=0, keepdims=True)
    g_w = jnp.max(gp, axis=0, keepdims=True)
    g_idx = jnp.min(jnp.where(jnp.logical_and(is_grp, gp == g_w), row_g, big), axis=0, keepdims=True)
    el = jnp.where(jnp.right_shift(row_e, EPG_SHIFT) == g_idx, le, NEG_INF)
    m1 = jnp.max(el, axis=0, keepdims=True)
    i1 = jnp.min(jnp.where(el == m1, row_e, big), axis=0, keepdims=True)
    el2 = jnp.where(row_e == i1, NEG_INF, el)
    m2 = jnp.max(el2, axis=0, keepdims=True)
    i2 = jnp.min(jnp.where(el2 == m2, row_e, big), axis=0, keepdims=True)

    e2 = jnp.exp(m2 - m1)
    w1 = g_w / (1.0 + e2)
    w2 = g_w * e2 / (1.0 + e2)
    first_is_a = i1 < i2
    row_w = lax.broadcasted_iota(jnp.int32, (LANES, tm), 0)
    w_rows = jnp.where(row_w == 0, jnp.where(first_is_a, w1, w2),
                       jnp.where(row_w == 1, jnp.where(first_is_a, w2, w1), 0.0))
    x1_ref[:, D_MODEL:D_MODEL + LANES] = w_rows.T

    a = jnp.bitwise_and(jnp.minimum(i1, i2), EXPERTS_PER_GROUP - 1)
    b = jnp.bitwise_and(jnp.maximum(i1, i2), EXPERTS_PER_GROUP - 1)
    pair = jnp.right_shift(a * (2 * EXPERTS_PER_GROUP - 1 - a), 1) + (b - a - 1)
    cls = g_idx * PAIRS_PER_GROUP + pair
    row_c = lax.broadcasted_iota(jnp.int32, (CLASS_ROWS, tm), 0)
    onehot = (row_c == cls).astype(F32)

    @pl.when(pl.program_id(0) == 0)
    def _init():
        cnt_ref[...] = jnp.zeros_like(cnt_ref)

    earlier = (lax.broadcasted_iota(jnp.int32, (tm, tm), 0)
               < lax.broadcasted_iota(jnp.int32, (tm, tm), 1)).astype(BF16)
    prior = jnp.dot(onehot.astype(BF16), earlier, preferred_element_type=F32) + cnt_ref[:, 0:1]
    rank = jnp.sum(prior * onehot, axis=0, keepdims=True)
    cnt_ref[...] += jnp.sum(onehot, axis=1, keepdims=True)
    row_o = lax.broadcasted_iota(jnp.int32, route_ref.shape, 0)
    route_ref[...] = jnp.where(row_o == 0, cls.astype(F32), jnp.where(row_o == 1, rank, 0.0))


def _merge(x2, oa, hm, ga, gm, wua, wum, wout, l1g, l1b, wr, br, tm, part):
    T = x2.shape[0] // TOKEN_PARTS
    first = part * (T // tm)
    blk_in = lambda n: pl.BlockSpec((tm, n), lambda i: (first + i, 0))
    blk = lambda n: pl.BlockSpec((tm, n), lambda i: (i, 0))
    in_specs = [blk_in(D_MODEL), blk_in(WIDTH_A), blk_in(WIDTH_M), blk_in(D_MODEL), blk_in(D_MODEL),
                _full(wua.shape), _full(wum.shape), _full(wout.shape), _full(l1g.shape), _full(l1b.shape),
                _full(wr.shape), _full(br.shape)]
    out_specs = [blk(D_MODEL + LANES), pl.BlockSpec((SUBLANES, tm), lambda i: (0, i)), _full((CLASS_ROWS, LANES))]
    out_shape = [jax.ShapeDtypeStruct((T, D_MODEL + LANES), F32), jax.ShapeDtypeStruct((SUBLANES, T), F32),
                 jax.ShapeDtypeStruct((CLASS_ROWS, LANES), F32)]
    return pl.pallas_call(
        _merge_kernel, grid=(T // tm,), in_specs=in_specs, out_specs=out_specs, out_shape=out_shape,
        compiler_params=_cparams(1), name="merge")(x2, oa, hm, ga, gm, wua, wum, wout, l1g, l1b, wr, br)


def _sc_mesh():
    return plsc.VectorSubcoreMesh(core_axis_name="c", subcore_axis_name="s",
                                  num_cores=SC_CORES, num_subcores=SC_SUBCORES)


def _sc_chunks(n_rows):
    workers = SC_CORES * SC_SUBCORES
    assert n_rows % (workers * SC_ROWS * 2) == 0
    return n_rows // (workers * SC_ROWS)


def _sc_scratch(n_chunks, width, dtype):
    return [pltpu.VMEM((n_chunks, SC_ROWS), jnp.int32),
            pltpu.VMEM((SC_ROWS, width), dtype), pltpu.VMEM((SC_ROWS, width), dtype),
            pltpu.SemaphoreType.DMA, pltpu.SemaphoreType.DMA]


def _sc_scatter_rows(rows, idx, n_out):
    n_in, width = rows.shape
    n_chunks = _sc_chunks(n_in)

    @functools.partial(
        pl.kernel, mesh=_sc_mesh(), out_type=jax.ShapeDtypeStruct((n_out, width), rows.dtype),
        scratch_types=_sc_scratch(n_chunks, width, rows.dtype), name="sc_dispatch")
    def scatter(rows_hbm, idx_hbm, out_hbm, idx_v, rows_a, rows_b, sem_a, sem_b):
        first = (lax.axis_index("s") * SC_CORES + lax.axis_index("c")) * n_chunks
        pltpu.sync_copy(idx_hbm.at[pl.ds(first, n_chunks)], idx_v)

        def load(c, buf):
            pltpu.sync_copy(rows_hbm.at[pl.ds((first + c) * SC_ROWS, SC_ROWS)], buf)

        def put(c, buf, sem):
            return pltpu.make_async_copy(buf, out_hbm.at[idx_v.at[c]], sem)

        load(0, rows_a)
        put(0, rows_a, sem_a).start()

        @pl.loop(0, n_chunks, step=2)
        def _(j):
            load(j + 1, rows_b)
            put(j + 1, rows_b, sem_b).start()
            put(j, rows_a, sem_a).wait()

            @pl.when(j + 2 < n_chunks)
            def _():
                load(j + 2, rows_a)
                put(j + 2, rows_a, sem_a).start()

            put(j + 1, rows_b, sem_b).wait()

    return scatter(rows, idx.reshape(n_in // SC_ROWS, SC_ROWS))


def _sc_gather_rows(table, idx):
    n_out, width = idx.shape[0], table.shape[1]
    n_chunks = _sc_chunks(n_out)

    @functools.partial(
        pl.kernel, mesh=_sc_mesh(), out_type=jax.ShapeDtypeStruct((n_out, width), table.dtype),
        scratch_types=_sc_scratch(n_chunks, width, table.dtype), name="sc_combine")
    def gather(table_hbm, idx_hbm, out_hbm, idx_v, rows_a, rows_b, sem_a, sem_b):
        first = (lax.axis_index("s") * SC_CORES + lax.axis_index("c")) * n_chunks
        pltpu.sync_copy(idx_hbm.at[pl.ds(first, n_chunks)], idx_v)

        def fetch(c, buf, sem):
            return pltpu.make_async_copy(table_hbm.at[idx_v.at[c]], buf, sem)

        def store(c, buf):
            pltpu.sync_copy(buf, out_hbm.at[pl.ds((first + c) * SC_ROWS, SC_ROWS)])

        fetch(0, rows_a, sem_a).start()

        @pl.loop(0, n_chunks, step=2)
        def _(j):
            fetch(j + 1, rows_b, sem_b).start()
            fetch(j, rows_a, sem_a).wait()
            store(j, rows_a)

            @pl.when(j + 2 < n_chunks)
            def _():
                fetch(j + 2, rows_a, sem_a).start()

            fetch(j + 1, rows_b, sem_b).wait()
            store(j + 1, rows_b)

    return gather(table, idx.reshape(n_out // SC_ROWS, SC_ROWS))


def _moe_kernel(ta_ref, tb_ref, nv_ref, xs_ref, wga, wua, wda, wgb, wub, wdb, l2g, l2b, *ys_refs):
    ys_ref = ys_refs[-1]
    i = pl.program_id(0)

    @pl.when(nv_ref[i] > 0)
    def _compute():
        x = xs_ref[:, 0:D_MODEL]
        xb = x.astype(BF16)
        w_a = xs_ref[:, D_MODEL:D_MODEL + 1]
        w_b = xs_ref[:, D_MODEL + 1:D_MODEL + 2]

        def expert(wg, wu, wd):
            g = jnp.dot(xb, wg[...], preferred_element_type=F32)
            u = jnp.dot(xb, wu[...], preferred_element_type=F32)
            hdn = (g * _sigmoid(g) * u).astype(BF16)
            return jnp.dot(hdn, wd[...], preferred_element_type=F32)

        ffn = w_a * expert(wga, wua, wda) + w_b * expert(wgb, wub, wdb)
        ys_ref[...] = _layer_norm(ALPHA * x + ffn, l2g[...], l2b[...])


def _moe(xs, tile_a, tile_b, n_valid, wg, wu, wd, l2g, l2b, part, ys_so_far):
    tm = MOE_TILE
    n_tiles = xs.shape[0] // tm
    first = part * n_tiles
    w_in = lambda which: pl.BlockSpec((None, D_MODEL, D_EXPERT), lambda i, ta, tb, nv: ((ta, tb)[which][i], 0, 0))
    w_out = lambda which: pl.BlockSpec((None, D_EXPERT, D_MODEL), lambda i, ta, tb, nv: ((ta, tb)[which][i], 0, 0))
    const = lambda shape: pl.BlockSpec(shape, lambda i, ta, tb, nv: (0,) * len(shape))
    in_specs = [pl.BlockSpec((tm, xs.shape[1]), lambda i, ta, tb, nv: (i, 0)),
                w_in(0), w_in(0), w_out(0), w_in(1), w_in(1), w_out(1), const(l2g.shape), const(l2b.shape)]
    args = [tile_a, tile_b, n_valid, xs, wg, wu, wd, wg, wu, wd, l2g, l2b]
    aliases = {}
    if ys_so_far is not None:
        in_specs.append(pl.BlockSpec(memory_space=pl.ANY))
        aliases = {len(args): 0}
        args.append(ys_so_far)
    grid_spec = pltpu.PrefetchScalarGridSpec(
        num_scalar_prefetch=3, grid=(n_tiles,), in_specs=in_specs,
        out_specs=pl.BlockSpec((tm, D_MODEL), lambda i, ta, tb, nv: (first + i, 0)))
    return pl.pallas_call(
        _moe_kernel, grid_spec=grid_spec, input_output_aliases=aliases,
        out_shape=jax.ShapeDtypeStruct((TOKEN_PARTS * xs.shape[0], D_MODEL), F32),
        compiler_params=_cparams(1), name="moe")(*args)


def _route_tables(route, counts, n_tokens):
    tm = MOE_TILE
    n_tiles = n_tokens // tm + N_CLASSES
    cnt = counts[:N_CLASSES, 0].astype(jnp.int32)
    tiles = (cnt + tm - 1) // tm
    tile_end = jnp.cumsum(tiles)
    tile_start = tile_end - tiles
    t_idx = jnp.arange(n_tiles, dtype=jnp.int32)
    cls_of_tile = jnp.minimum(jnp.sum(t_idx[:, None] >= tile_end[None, :], axis=1), N_CLASSES - 1).astype(jnp.int32)
    classes = np.arange(N_CLASSES)
    tile_is = cls_of_tile[:, None] == classes[None, :]
    per_tile = lambda table: jnp.sum(jnp.where(tile_is, jnp.asarray(table, jnp.int32)[None, :], 0), axis=1)
    n_valid = jnp.clip(per_tile(cnt) - (t_idx - per_tile(tile_start)) * tm, 0, tm).astype(jnp.int32)
    first_expert = classes // PAIRS_PER_GROUP * EXPERTS_PER_GROUP
    tile_a = per_tile(first_expert + np.asarray(PAIR_A)[classes % PAIRS_PER_GROUP])
    tile_b = per_tile(first_expert + np.asarray(PAIR_B)[classes % PAIRS_PER_GROUP])
    cls = route[0].astype(jnp.int32)
    rank = route[1].astype(jnp.int32)
    row0 = jnp.sum(jnp.where(cls[:, None] == jnp.arange(N_CLASSES)[None, :], (tile_start * tm)[None, :], 0), axis=1)
    return tile_a, tile_b, n_valid, row0 + rank, n_tiles * tm


def _pick_tile(T, pref):
    t = pref
    while T % t:
        t //= 2
    return t


def kernel(x, w_in, conv_w, conv_b, kv_norm_g, w_uk, w_uv, rel_bias, b_i, b_f, mh_norm_g, w_up_a, w_up_m,
           w_out, ln1_g, ln1_b, w_grp, b_grp, w_rt, b_rt, w_gate, w_up, w_down, ln2_g, ln2_b):
    B, S, _ = x.shape
    T = B * S
    assert S % Q_TILE == 0 and S % M_CHUNK == 0 and T % MOE_TILE == 0 and w_in.shape[0] == DEPTH
    tz = _bias_tables(rel_bias)
    x2 = x.reshape(T, D_MODEL)
    for l in range(DEPTH):
        w = w_in[l]
        o = np.cumsum((WIDTH_A, KV_RANK, WIDTH_IDX, HEAD_DIM_IDX, N_HEADS_IDX, 2 * WIDTH_M, WIDTH_M,
                       N_HEADS_M, N_HEADS_M, WIDTH_M, D_MODEL, D_MODEL)).tolist()
        o = [0] + o
        seg = lambda j: w[:, o[j]:o[j + 1]]
        pad = LANES - (HEAD_DIM_IDX + N_HEADS_IDX + 2 * N_HEADS_M)
        w_small = jnp.concatenate([seg(3), seg(4), seg(7), seg(8), jnp.zeros((D_MODEL, pad), w.dtype)], axis=1)
        ws = [seg(0), seg(1), seg(2), w_small, seg(5), seg(6), seg(9), seg(10), seg(11)]
        ws = [a.astype(BF16) for a in ws]
        smb = jnp.zeros((1, LANES), F32).at[0, SM_I:SM_I + N_HEADS_M].set(b_i[l]) \
            .at[0, SM_F:SM_F + N_HEADS_M].set(b_f[l])
        qa, ckv, qi, sm, qk, v, og, ga, gm, ckv_t = _proj(x2, ws, kv_norm_g[l][None, :], smb,
                                                          _pick_tile(S, 512), S)

        wuk_t = jnp.transpose(w_uk[l], (1, 0, 2)).astype(BF16)
        wuv_t = jnp.transpose(w_uv[l], (1, 2, 0)).astype(BF16)
        oa = _dsa(qi, sm, qa, ckv, ckv_t, wuk_t, wuv_t, tz, B, S)

        hm = _mlstm(qk, v, sm, og, conv_w[l], conv_b[l][None, :], mh_norm_g[l].reshape(1, WIDTH_M), B, S)

        w_router = jnp.concatenate(
            [w_rt[l], w_grp[l], jnp.zeros((D_MODEL, LANES - N_EXPERTS - N_GROUPS), F32)], axis=1)
        b_router = jnp.concatenate(
            [b_rt[l], b_grp[l], jnp.zeros((LANES - N_EXPERTS - N_GROUPS,), F32)])[None, :]
        wr_hi = w_router.astype(BF16)
        wr_split = jnp.stack([wr_hi, (w_router - wr_hi.astype(F32)).astype(BF16)])
        merge_w = (w_up_a[l].astype(BF16), w_up_m[l].astype(BF16), w_out[l].astype(BF16),
                   ln1_g[l][None, :], ln1_b[l][None, :], wr_split, b_router)
        moe_w = (w_gate[l].astype(BF16), w_up[l].astype(BF16), w_down[l].astype(BF16),
                 ln2_g[l][None, :], ln2_b[l][None, :])
        part_tokens = T // TOKEN_PARTS
        ys, pos_parts = None, []
        for part in range(TOKEN_PARTS):
            x1, route, counts = _merge(x2, oa, hm, ga, gm, *merge_w, _pick_tile(part_tokens, 256), part)
            tile_a, tile_b, n_valid, pos, n_sorted = _route_tables(route, counts, part_tokens)
            xs = _sc_scatter_rows(x1, pos, n_sorted)
            ys = _moe(xs, tile_a, tile_b, n_valid, *moe_w, part, ys)
            pos_parts.append(pos + part * n_sorted)
        x2 = _sc_gather_rows(ys, jnp.concatenate(pos_parts))
    return x2.reshape(B, S, D_MODEL)
```

```python
import functools
import math

import jax
import jax.numpy as jnp
import numpy as np
from jax import lax
from jax.experimental import pallas as pl
from jax.experimental.pallas import tpu as pltpu
from jax.experimental.pallas import tpu_sc as plsc

F32 = jnp.float32
BF16 = jnp.bfloat16

D_MODEL = 1024
N_HEADS_A = 8
HEAD_DIM_A = 64
WIDTH_A = N_HEADS_A * HEAD_DIM_A
KV_RANK = 256
N_HEADS_IDX = 8
HEAD_DIM_IDX = 64
WIDTH_IDX = N_HEADS_IDX * HEAD_DIM_IDX
TOPK_MAX = 256
N_BUCKETS = 32
MAX_DISTANCE = 128
N_HEADS_M = 4
HEAD_DIM_M = 128
WIDTH_M = N_HEADS_M * HEAD_DIM_M
CONV_WIDTH = 4
N_GROUPS = 4
EXPERTS_PER_GROUP = 4
N_EXPERTS = N_GROUPS * EXPERTS_PER_GROUP
D_EXPERT = 512
LN_EPS = 1e-5
DEPTH = 1
ALPHA = (2.0 * DEPTH) ** 0.25

LANES = 128
SUBLANES = 8
VMEM_LIMIT = 56 * 1024 * 1024

SM_KIDX = 0
SM_WIDX = HEAD_DIM_IDX
SM_I = SM_WIDX + N_HEADS_IDX
SM_F = SM_I + N_HEADS_M

Q_TILE = 256
K_CHUNK = Q_TILE
MXU_CHUNKS_PER_TRIP = 4
ONES_ROWS = 16
M_CHUNK = 128
M_GROUPS = 2
CONV_HALO = 16
assert CONV_HALO >= CONV_WIDTH - 1
PROJ_TILE = 512
MERGE_TILE = 256
MOE_TILE = 256
SC_CORES = 2
SC_SUBCORES = 16
SC_ROWS = 32
EPG_SHIFT = EXPERTS_PER_GROUP.bit_length() - 1
assert 1 << EPG_SHIFT == EXPERTS_PER_GROUP
PAIR_A, PAIR_B = zip(*[(a, b) for a in range(EXPERTS_PER_GROUP) for b in range(a + 1, EXPERTS_PER_GROUP)])
PAIRS_PER_GROUP = len(PAIR_A)
N_CLASSES = N_GROUPS * PAIRS_PER_GROUP
CLASS_ROWS = -(-N_CLASSES // SUBLANES) * SUBLANES
BISECT_STEPS_PER_CHECK = 3
BISECT_MAX_CHECKS = 5
PEEL_BRACKET = 2.0
NEG_INF = float("-inf")
LOG2E = math.log2(math.e)


def _cparams(n_grid):
    return pltpu.CompilerParams(dimension_semantics=("arbitrary",) * n_grid,
                                vmem_limit_bytes=VMEM_LIMIT)


def _full(shape):
    nd = len(shape)
    return pl.BlockSpec(shape, lambda *_: (0,) * nd)


def _sigmoid(x):
    return 0.5 * jnp.tanh(0.5 * x) + 0.5


def _split3(x):
    hi = x.astype(BF16)
    r = x - hi.astype(F32)
    mid = r.astype(BF16)
    return hi, mid, (r - mid.astype(F32)).astype(BF16)


def _proj_kernel(x_ref, wqa, wckv, wqi, wsm, wqk, wv, wo, wga, wgm, kvg, smb,
                 qa_o, ckv_o, qi_o, sm_o, qk_o, v_o, o_o, ga_o, gm_o, ckvt_o):
    xb = x_ref[...].astype(BF16)

    def mm(w):
        return jnp.dot(xb, w[...], preferred_element_type=F32)

    qa_o[...] = mm(wqa).astype(BF16)
    c = mm(wckv)
    c = c * lax.rsqrt(jnp.mean(c * c, axis=-1, keepdims=True) + LN_EPS) * kvg[...]
    ckv_o[...] = c.astype(BF16)
    ckvt_o[0:KV_RANK, :] = c.T.astype(BF16)
    ckvt_o[KV_RANK:KV_RANK + ONES_ROWS, :] = jnp.ones((ONES_ROWS, c.shape[0]), BF16)
    qi_o[...] = mm(wqi).astype(BF16)
    sm_o[...] = mm(wsm) + smb[...]
    qk_o[...] = mm(wqk).astype(BF16)
    v_o[...] = mm(wv).astype(BF16)
    o_o[...] = mm(wo).astype(BF16)
    ga_o[...] = mm(wga).astype(BF16)
    gm_o[...] = mm(wgm).astype(BF16)


def _proj(x2, ws, kvg, smb, tm, seq):
    T = x2.shape[0]
    per_seq = seq // tm
    widths = [w.shape[1] for w in ws]
    dts = [BF16, BF16, BF16, F32, BF16, BF16, BF16, BF16, BF16]
    in_specs = [pl.BlockSpec((tm, D_MODEL), lambda i: (i, 0))]
    in_specs += [_full(w.shape) for w in ws]
    in_specs += [_full(kvg.shape), _full(smb.shape)]
    out_specs = [pl.BlockSpec((tm, n), lambda i: (i, 0)) for n in widths]
    out_specs.append(pl.BlockSpec((None, KV_RANK + ONES_ROWS, tm), lambda i: (i // per_seq, 0, i % per_seq)))
    out_shape = [jax.ShapeDtypeStruct((T, n), dt) for n, dt in zip(widths, dts)]
    out_shape.append(jax.ShapeDtypeStruct((T // seq, KV_RANK + ONES_ROWS, seq), BF16))
    return pl.pallas_call(
        _proj_kernel, grid=(T // tm,), in_specs=in_specs, out_specs=out_specs, out_shape=out_shape,
        compiler_params=_cparams(1), name="proj")(x2, *ws, kvg, smb)


def _dsa_kernel(qi_ref, smq_ref, smk_ref, qa_ref, ckv_ref, ckvt_ref, wuk_ref, wuvt_ref, tz_ref,
                oa_ref, qs_ref, sc_ref, am_ref, ql_ref, x_ref, acc_ref, m_ref, *, topk):
    tq, kc = Q_TILE, K_CHUNK
    nh = N_HEADS_A
    qb = pl.program_id(1)
    n_ch = qb + 1
    seq_keys = n_ch * kc
    t0 = qb * tq

    def rows(c):
        return pl.ds(pl.multiple_of(c * kc, kc), kc)

    def lanes(h):
        return slice(h * tq, (h + 1) * tq)

    def chunk_loop(n, body, init, per_trip=2):
        def run(first, count, carry):
            for r in range(count):
                carry = body(first + r, carry)
            return carry

        shift = per_trip.bit_length() - 1
        trips = jnp.right_shift(n, shift)
        carry = lax.fori_loop(0, trips, lambda p, c: run(p * per_trip, per_trip, c), init)
        done = trips * per_trip
        count = per_trip // 2
        while count:
            start = done + jnp.bitwise_and(n - done, -2 * count)
            carry = lax.cond(jnp.bitwise_and(n, count) != 0,
                             functools.partial(run, start, count), lambda c: c, carry)
            count //= 2
        return carry

    for h in range(N_HEADS_IDX):
        qs_ref[h * tq:(h + 1) * tq, :] = qi_ref[:, h * HEAD_DIM_IDX:(h + 1) * HEAD_DIM_IDX]
    w_t = smq_ref[...].T
    q_pos = lax.broadcasted_iota(jnp.int32, (1, tq), 1) + t0
    key_iota = lax.broadcasted_iota(jnp.int32, (kc, tq), 0)

    def score_chunk(c, carry):
        mx, mn = carry
        kk = smk_ref[rows(c), SM_KIDX:SM_KIDX + HEAD_DIM_IDX].astype(BF16)
        dots = lax.dot_general(kk, qs_ref[...], (((1,), (1,)), ((), ())), preferred_element_type=F32)
        sc = jnp.zeros((kc, tq), F32)
        for h in range(N_HEADS_IDX):
            sc = sc + w_t[SM_WIDX + h:SM_WIDX + h + 1, :] * jnp.maximum(dots[:, lanes(h)], 0.0)
        vis = (key_iota + c * kc) <= q_pos
        sc_ref[rows(c), :] = jnp.where(vis, sc, NEG_INF)
        mx = jnp.maximum(mx, jnp.max(jnp.where(vis, sc, NEG_INF), axis=0, keepdims=True))
        mn = jnp.minimum(mn, jnp.min(jnp.where(vis, sc, jnp.inf), axis=0, keepdims=True))
        return mx, mn

    mx, mn = chunk_loop(n_ch, score_chunk,
                        (jnp.full((1, tq), NEG_INF, F32), jnp.full((1, tq), jnp.inf, F32)), MXU_CHUNKS_PER_TRIP)

    n_vis = (q_pos + 1).astype(F32)
    k_row = jnp.minimum(n_vis, float(topk))

    def count(pred):
        def body(c, a):
            hit = pred(sc_ref[rows(c), :]).astype(F32)
            return a + jnp.sum(hit.reshape(kc // SUBLANES, SUBLANES, tq), axis=0)
        a = chunk_loop(n_ch, body, jnp.zeros((SUBLANES, tq), F32))
        return jnp.sum(a, axis=0, keepdims=True)

    def any_lane(flag):
        return jnp.max(jnp.where(flag, 1.0, 0.0)) > 0.0

    def crowded(cnt_lo, c_hi):
        return any_lane(jnp.logical_and(cnt_lo != k_row, cnt_lo - c_hi > PEEL_BRACKET))

    def bisect_cond(carry):
        it, lo, hi, cnt_lo, c_hi = carry
        return jnp.logical_and(it < BISECT_MAX_CHECKS, crowded(cnt_lo, c_hi))

    def bisect_body(carry):
        it, lo, hi, cnt_lo, c_hi = carry
        for _ in range(BISECT_STEPS_PER_CHECK):
            mid = lo * 0.5 + hi * 0.5
            cnt = count(lambda s: s >= mid)
            ge = cnt >= k_row
            lo, cnt_lo = jnp.where(ge, mid, lo), jnp.where(ge, cnt, cnt_lo)
            hi, c_hi = jnp.where(ge, hi, mid), jnp.where(ge, c_hi, cnt)
        return it + 1, lo, hi, cnt_lo, c_hi

    hi0 = mx + jnp.maximum(jnp.abs(mx), 1e-30) * 1e-6
    _, lo, hi, cnt_lo, c_hi = lax.while_loop(
        bisect_cond, bisect_body, (jnp.int32(0), mn, hi0, n_vis, jnp.zeros((1, tq), F32)))

    def peel_cond(carry):
        it, lo, hi, cnt_lo, c_hi, done = carry
        return jnp.logical_and(it < seq_keys, any_lane(done == 0.0))

    def peel_body(carry):
        it, lo, hi, cnt_lo, c_hi, done = carry

        def top_body(c, v):
            s = sc_ref[rows(c), :]
            inside = jnp.logical_and(s >= lo, s < hi)
            return jnp.maximum(v, jnp.max(jnp.where(inside, s, NEG_INF), axis=0, keepdims=True))

        v = chunk_loop(n_ch, top_body, jnp.full((1, tq), NEG_INF, F32))
        c_v = count(lambda s: s >= v)
        reached = c_v >= k_row
        live = done == 0.0
        fin = jnp.logical_and(live, reached)
        cut = jnp.logical_and(live, jnp.logical_not(reached))
        return (it + 1, jnp.where(fin, v, lo), jnp.where(cut, v, hi), jnp.where(fin, c_v, cnt_lo),
                jnp.where(cut, c_v, c_hi), jnp.where(fin, 1.0, done))

    _, lo, hi, cnt_lo, c_hi, _ = lax.while_loop(
        peel_cond, peel_body,
        (jnp.int32(0), lo, hi, cnt_lo, c_hi, jnp.where(cnt_lo == k_row, 1.0, 0.0)))

    def mask_chunk(c, _):
        am_ref[rows(c), :] = jnp.where(sc_ref[rows(c), :] >= lo, 0.0, NEG_INF)
        return 0

    chunk_loop(n_ch, mask_chunk, 0)

    tied = cnt_lo != k_row
    for part in range(tq // LANES):
        ls = slice(part * LANES, (part + 1) * LANES)

        @pl.when(any_lane(tied[:, ls]))
        def _ties():
            need, lo_p, hi_p = (k_row - c_hi)[:, ls], lo[:, ls], hi[:, ls]
            lower = (lax.broadcasted_iota(jnp.int32, (kc, kc), 1)
                     < lax.broadcasted_iota(jnp.int32, (kc, kc), 0)).astype(BF16)

            def tie_chunk(c, before):
                s = sc_ref[rows(c), ls]
                above = s >= hi_p
                tie = jnp.logical_and(s >= lo_p, jnp.logical_not(above))
                tie_f = tie.astype(F32)
                rank = jnp.dot(lower, tie_f.astype(BF16), preferred_element_type=F32) + before
                sel = jnp.logical_or(above, jnp.logical_and(tie, rank < need))
                am_ref[rows(c), ls] = jnp.where(sel, 0.0, NEG_INF)
                return before + jnp.sum(tie_f, axis=0, keepdims=True)

            chunk_loop(n_ch, tie_chunk, jnp.zeros((1, LANES), F32))

    for h in range(nh):
        qh = qa_ref[:, h * HEAD_DIM_A:(h + 1) * HEAD_DIM_A]
        qlt = lax.dot_general(wuk_ref[h], qh, (((1,), (1,)), ((), ())), preferred_element_type=F32)
        ql_ref[:, lanes(h)] = (qlt * (HEAD_DIM_A ** -0.5 * LOG2E)).astype(BF16)

    m_ref[...] = jnp.full(m_ref.shape, NEG_INF, F32)

    def pass_a(c, table):
        raw = jnp.dot(ckv_ref[rows(c), :], ql_ref[...], preferred_element_type=F32)
        am = am_ref[rows(c), :]
        for h in range(nh):
            x = raw[:, lanes(h)] + am
            if table is not None:
                x = x + tz_ref[table, h]
            x_ref[rows(c), lanes(h)] = x
            m_ref[:, lanes(h)] = jnp.maximum(m_ref[:, lanes(h)], jnp.max(x, axis=0, keepdims=True))

    def far_chunk(c, _):
        pass_a(c, None)
        return 0

    chunk_loop(jnp.maximum(qb - 1, 0), far_chunk, 0, MXU_CHUNKS_PER_TRIP)

    @pl.when(qb >= 1)
    def _prev():
        pass_a(qb - 1, 1)

    pass_a(qb, 0)

    def pv(c):
        p = jnp.exp2((x_ref[rows(c), :] - m_ref[...]).astype(BF16))
        return jnp.dot(ckvt_ref[:, rows(c)], p, preferred_element_type=F32)

    def pass_b(c, _):
        acc_ref[...] += pv(c)
        return 0

    acc_ref[...] = pv(qb)
    chunk_loop(qb, pass_b, 0, MXU_CHUNKS_PER_TRIP)
    inv_l = 1.0 / acc_ref[KV_RANK:KV_RANK + 1, :]
    o_lat = (acc_ref[0:KV_RANK, :] * inv_l).astype(BF16)
    outs = [jnp.dot(wuvt_ref[h], o_lat[:, lanes(h)], preferred_element_type=F32) for h in range(nh)]
    oa_ref[...] = jnp.concatenate(outs, axis=0).T.astype(BF16)


def _dsa(qi, sm, qa, ckv, ckv_t, wuk_t, wuv_t, tz, batch, seq):
    tq = Q_TILE
    nq = seq // tq
    topk = min(TOPK_MAX, seq // 4)
    T = batch * seq
    kern = functools.partial(_dsa_kernel, topk=topk)
    blk_q = lambda n: pl.BlockSpec((tq, n), lambda b, q: (b * nq + q, 0))
    blk_s = lambda n: pl.BlockSpec((seq, n), lambda b, q: (b, 0))
    in_specs = [
        blk_q(WIDTH_IDX), blk_q(LANES), blk_s(LANES), blk_q(WIDTH_A), blk_s(KV_RANK),
        pl.BlockSpec((None, KV_RANK + ONES_ROWS, seq), lambda b, q: (b, 0, 0)),
        _full(wuk_t.shape), _full(wuv_t.shape), _full(tz.shape),
    ]
    scratch = [
        pltpu.VMEM((N_HEADS_IDX * tq, HEAD_DIM_IDX), BF16),
        pltpu.VMEM((seq, tq), F32),
        pltpu.VMEM((seq, tq), F32),
        pltpu.VMEM((KV_RANK, N_HEADS_A * tq), BF16),
        pltpu.VMEM((seq, N_HEADS_A * tq), F32),
        pltpu.VMEM((KV_RANK + ONES_ROWS, N_HEADS_A * tq), F32),
        pltpu.VMEM((1, N_HEADS_A * tq), F32),
    ]
    return pl.pallas_call(
        kern, grid=(batch, nq), in_specs=in_specs,
        out_specs=pl.BlockSpec((tq, WIDTH_A), lambda b, q: (b * nq + q, 0)),
        out_shape=jax.ShapeDtypeStruct((T, WIDTH_A), BF16),
        scratch_shapes=scratch, compiler_params=_cparams(2), name="dsa")(
            qi, sm, sm, qa, ckv, ckv_t, wuk_t, wuv_t, tz)


def _t5_bucket(dist):
    max_exact = N_BUCKETS // 2
    d = jnp.maximum(dist, 0)
    ratio = jnp.log(jnp.maximum(d, 1).astype(F32) / max_exact) / math.log(MAX_DISTANCE / max_exact)
    large = jnp.minimum(max_exact + (ratio * (N_BUCKETS - max_exact)).astype(jnp.int32), N_BUCKETS - 1)
    return jnp.where(d < max_exact, d, large)


def _bias_tables(rel_bias):
    tq = Q_TILE
    span = 2 * tq
    assert int(_np_bucket(tq)) == N_BUCKETS - 1
    far = rel_bias[N_BUCKETS - 1]
    by_dist = ((rel_bias[_t5_bucket(jnp.arange(span))] - far).astype(F32) * LOG2E).T
    diag = jnp.concatenate([by_dist[:, :tq], jnp.zeros_like(by_dist[:, :tq])], axis=1)
    prev = jnp.concatenate([by_dist[:, tq:], by_dist[:, :tq]], axis=1)

    def toeplitz(f):
        m = jnp.tile(f, (1, tq))[:, :tq * (span - 1)].reshape(N_HEADS_A, tq, span - 1)
        return m[:, :, :tq]

    return jnp.stack([toeplitz(diag), toeplitz(prev)])


def _np_bucket(d):
    max_exact = N_BUCKETS // 2
    ratio = np.log(np.float32(max(d, 1)) / np.float32(max_exact)) / math.log(MAX_DISTANCE / max_exact)
    return min(max_exact + int(ratio * (N_BUCKETS - max_exact)), N_BUCKETS - 1) if d >= max_exact else d


def _mlstm_kernel(qk_ref, halo_ref, v_ref, sm_ref, o_ref, cw_ref, cb_ref, g_ref,
                  out_ref, c_ref, m_ref):
    c_idx = pl.program_id(1)

    @pl.when(c_idx == 0)
    def _init():
        c_ref[...] = jnp.zeros_like(c_ref)
        m_ref[...] = jnp.zeros_like(m_ref)

    for g in range(M_GROUPS):
        gates = _mlstm_gates(g, c_idx, qk_ref, halo_ref, sm_ref, cw_ref, cb_ref)
        outs = [_mlstm_head(g, h, gates, v_ref, o_ref, g_ref, c_ref, m_ref) for h in range(N_HEADS_M)]
        out_ref[g] = jnp.concatenate(outs, axis=-1).astype(BF16)


def _mlstm_gates(g, c_idx, qk_ref, halo_ref, sm_ref, cw_ref, cb_ref):
    L = M_CHUNK

    hw = halo_ref.shape[1]
    halo = jnp.where(c_idx > 0, halo_ref[g].astype(F32), 0.0)
    ext = jnp.concatenate([halo, qk_ref[g].astype(F32)], axis=0)
    acc = jnp.zeros((L, 2 * WIDTH_M), F32) + cb_ref[...]
    for w in range(CONV_WIDTH):
        off = hw - (CONV_WIDTH - 1) + w
        acc = acc + ext[off:off + L, :] * cw_ref[w:w + 1, :]
    qk = acc * _sigmoid(acc)

    sm = sm_ref[g]
    sm_t = sm.T
    r_i = lax.broadcasted_iota(jnp.int32, (L, L), 0)
    c_i = lax.broadcasted_iota(jnp.int32, (L, L), 1)
    tril = (c_i <= r_i).astype(BF16)
    bcum_c = sum(jnp.dot(tril, part, preferred_element_type=F32)
                 for part in _split3(jax.nn.log_sigmoid(sm)))
    return qk, sm, sm_t, bcum_c, bcum_c.T, c_i <= r_i


def _mlstm_head(g, h, gates, v_ref, o_ref, g_ref, c_ref, m_ref):
    L = M_CHUNK
    dm = HEAD_DIM_M
    qk, sm, sm_t, bcum_c, bcum_r, causal = gates
    ones_col = (lax.broadcasted_iota(jnp.int32, (L, dm), 1) == 0).astype(BF16)
    st = g * N_HEADS_M + h
    q = qk[:, h * dm:(h + 1) * dm].astype(BF16)
    k = (qk[:, WIDTH_M + h * dm:WIDTH_M + (h + 1) * dm] * (dm ** -0.5))
    v_aug = jnp.concatenate([v_ref[g, :, h * dm:(h + 1) * dm], ones_col], axis=-1)
    b_col = bcum_c[:, SM_F + h:SM_F + h + 1]
    g_col = sm[:, SM_I + h:SM_I + h + 1] - b_col
    g_row = sm_t[SM_I + h:SM_I + h + 1, :] - bcum_r[SM_F + h:SM_F + h + 1, :]
    b_last = b_col[L - 1:L, :]
    m_prev = m_ref[st]
    c_prev = c_ref[st]

    log_d = jnp.where(causal, b_col + g_row, NEG_INF)
    m_j = jnp.maximum(b_col + m_prev, jnp.max(log_d, axis=-1, keepdims=True))
    w_inter = jnp.exp(b_col + m_prev - m_j)
    qkt = lax.dot_general(q, k.astype(BF16), (((1,), (1,)), ((), ())), preferred_element_type=F32)
    s = qkt * jnp.exp(log_d - m_j)
    o_aug = jnp.dot(s.astype(BF16), v_aug, preferred_element_type=F32) + \
        w_inter * jnp.dot(q, c_prev.astype(BF16), preferred_element_type=F32)
    num = o_aug[:, :dm]
    den = o_aug[:, dm:dm + 1]
    hh = num / jnp.maximum(jnp.abs(den), jnp.exp(-m_j))

    lwe = b_last + g_col
    m_loc = jnp.max(lwe, axis=0, keepdims=True)
    kw = (k * jnp.exp(lwe - m_loc)).astype(BF16)
    c_loc = lax.dot_general(kw, v_aug, (((0,), (0,)), ((), ())), preferred_element_type=F32)
    m_new = jnp.maximum(b_last + m_prev, m_loc)
    c_ref[st] = jnp.exp(b_last + m_prev - m_new) * c_prev + jnp.exp(m_loc - m_new) * c_loc
    m_ref[st] = m_new

    hn = hh * lax.rsqrt(jnp.mean(hh * hh, axis=-1, keepdims=True) + LN_EPS) * g_ref[:, h * dm:(h + 1) * dm]
    og = o_ref[g, :, h * dm:(h + 1) * dm].astype(F32)
    return _sigmoid(og) * hn


def _mlstm(qk, v, sm, o, conv_w, conv_b, mh_g, batch, seq):
    L = M_CHUNK
    G = M_GROUPS
    nc = seq // L
    T = batch * seq
    hb = L // CONV_HALO
    grouped = lambda a: a.reshape(G, T // G, a.shape[-1])
    blk = lambda n: pl.BlockSpec((G, L, n), lambda b, c: (0, b * nc + c, 0))
    in_specs = [
        blk(2 * WIDTH_M),
        pl.BlockSpec((G, CONV_HALO, 2 * WIDTH_M), lambda b, c: (0, jnp.maximum((b * nc + c) * hb - 1, 0), 0)),
        blk(WIDTH_M), blk(LANES), blk(WIDTH_M),
        _full(conv_w.shape), _full(conv_b.shape), _full(mh_g.shape),
    ]
    scratch = [pltpu.VMEM((G * N_HEADS_M, HEAD_DIM_M, 2 * HEAD_DIM_M), F32),
               pltpu.VMEM((G * N_HEADS_M, 1, 1), F32)]
    qk_g = grouped(qk)
    out = pl.pallas_call(
        _mlstm_kernel, grid=(batch // G, nc), in_specs=in_specs, out_specs=blk(WIDTH_M),
        out_shape=jax.ShapeDtypeStruct((G, T // G, WIDTH_M), BF16), scratch_shapes=scratch,
        compiler_params=_cparams(2), name="mlstm")(
            qk_g, qk_g, grouped(v), grouped(sm), grouped(o), conv_w, conv_b, mh_g)
    return out.reshape(T, WIDTH_M)


def _layer_norm(y, g, b):
    mu = jnp.mean(y, axis=-1, keepdims=True)
    var = jnp.mean(jnp.square(y - mu), axis=-1, keepdims=True)
    return (y - mu) * lax.rsqrt(var + LN_EPS) * g + b


def _router_logits(x1, wr, br):
    x1_hi = x1.astype(BF16)
    x1_lo = (x1 - x1_hi.astype(F32)).astype(BF16)
    return (jnp.dot(x1_hi, wr[0], preferred_element_type=F32)
            + jnp.dot(x1_lo, wr[0], preferred_element_type=F32)
            + jnp.dot(x1_hi, wr[1], preferred_element_type=F32)) + br[...]


def _merge_kernel(x_ref, oa_ref, hm_ref, ga_ref, gm_ref, wua, wum, wout, l1g, l1b, wr, br,
                  x1_ref, route_ref, cnt_ref):
    pa = jnp.dot(oa_ref[...], wua[...], preferred_element_type=F32)
    pm = jnp.dot(hm_ref[...], wum[...], preferred_element_type=F32)
    y = _sigmoid(ga_ref[...]) * pa.astype(BF16) + _sigmoid(gm_ref[...]) * pm.astype(BF16)
    mix = jnp.dot(y.astype(BF16), wout[...], preferred_element_type=F32)
    x1 = _layer_norm(ALPHA * x_ref[...] + mix, l1g[...], l1b[...])
    x1_ref[:, 0:D_MODEL] = x1

    lt = _router_logits(x1, wr, br).T
    tm = lt.shape[1]
    big = jnp.int32(LANES)
    le = lt[0:N_EXPERTS, :]
    lg = lt[N_EXPERTS:N_EXPERTS + SUBLANES, :]
    row_g = lax.broadcasted_iota(jnp.int32, lg.shape, 0)
    row_e = lax.broadcasted_iota(jnp.int32, le.shape, 0)
    is_grp = row_g < N_GROUPS
    gl = jnp.where(is_grp, lg, NEG_INF)
    ge = jnp.exp(gl - jnp.max(gl, axis=0, keepdims=True))
    gp = ge / jnp.sum(ge, axis=0, keepdims=True)
    g_w = jnp.max(gp, axis=0, keepdims=True)
    g_idx = jnp.min(jnp.where(jnp.logical_and(is_grp, gp == g_w), row_g, big), axis=0, keepdims=True)
    el = jnp.where(jnp.right_shift(row_e, EPG_SHIFT) == g_idx, le, NEG_INF)
    m1 = jnp.max(el, axis=0, keepdims=True)
    i1 = jnp.min(jnp.where(el == m1, row_e, big), axis=0, keepdims=True)
    el2 = jnp.where(row_e == i1, NEG_INF, el)
    m2 = jnp.max(el2, axis=0, keepdims=True)
    i2 = jnp.min(jnp.where(el2 == m2, row_e, big), axis=0, keepdims=True)

    e2 = jnp.exp(m2 - m1)
    w1 = g_w / (1.0 + e2)
    w2 = g_w * e2 / (1.0 + e2)
    first_is_a = i1 < i2
    row_w = lax.broadcasted_iota(jnp.int32, (LANES, tm), 0)
    w_rows = jnp.where(row_w == 0, jnp.where(first_is_a, w1, w2),
                       jnp.where(row_w == 1, jnp.where(first_is_a, w2, w1), 0.0))
    x1_ref[:, D_MODEL:D_MODEL + LANES] = w_rows.T

    a = jnp.bitwise_and(jnp.minimum(i1, i2), EXPERTS_PER_GROUP - 1)
    b = jnp.bitwise_and(jnp.maximum(i1, i2), EXPERTS_PER_GROUP - 1)
    pair = jnp.right_shift(a * (2 * EXPERTS_PER_GROUP - 1 - a), 1) + (b - a - 1)
    cls = g_idx * PAIRS_PER_GROUP + pair
    row_c = lax.broadcasted_iota(jnp.int32, (CLASS_ROWS, tm), 0)
    onehot = (row_c == cls).astype(F32)

    @pl.when(pl.program_id(0) == 0)
    def _init():
        cnt_ref[...] = jnp.zeros_like(cnt_ref)

    earlier = (lax.broadcasted_iota(jnp.int32, (tm, tm), 0)
               < lax.broadcasted_iota(jnp.int32, (tm, tm), 1)).astype(BF16)
    prior = jnp.dot(onehot.astype(BF16), earlier, preferred_element_type=F32) + cnt_ref[:, 0:1]
    rank = jnp.sum(prior * onehot, axis=0, keepdims=True)
    cnt_ref[...] += jnp.sum(onehot, axis=1, keepdims=True)
    row_o = lax.broadcasted_iota(jnp.int32, route_ref.shape, 0)
    route_ref[...] = jnp.where(row_o == 0, cls.astype(F32), jnp.where(row_o == 1, rank, 0.0))


def _merge(x2, oa, hm, ga, gm, wua, wum, wout, l1g, l1b, wr, br, tm):
    T = x2.shape[0]
    blk = lambda n: pl.BlockSpec((tm, n), lambda i: (i, 0))
    in_specs = [blk(D_MODEL), blk(WIDTH_A), blk(WIDTH_M), blk(D_MODEL), blk(D_MODEL),
                _full(wua.shape), _full(wum.shape), _full(wout.shape), _full(l1g.shape), _full(l1b.shape),
                _full(wr.shape), _full(br.shape)]
    out_specs = [blk(D_MODEL + LANES), pl.BlockSpec((SUBLANES, tm), lambda i: (0, i)), _full((CLASS_ROWS, LANES))]
    out_shape = [jax.ShapeDtypeStruct((T, D_MODEL + LANES), F32), jax.ShapeDtypeStruct((SUBLANES, T), F32),
                 jax.ShapeDtypeStruct((CLASS_ROWS, LANES), F32)]
    return pl.pallas_call(
        _merge_kernel, grid=(T // tm,), in_specs=in_specs, out_specs=out_specs, out_shape=out_shape,
        compiler_params=_cparams(1), name="merge")(x2, oa, hm, ga, gm, wua, wum, wout, l1g, l1b, wr, br)


def _sc_mesh():
    return plsc.VectorSubcoreMesh(core_axis_name="c", subcore_axis_name="s",
                                  num_cores=SC_CORES, num_subcores=SC_SUBCORES)


def _sc_chunks(n_rows):
    workers = SC_CORES * SC_SUBCORES
    assert n_rows % (workers * SC_ROWS * 2) == 0
    return n_rows // (workers * SC_ROWS)


def _sc_scratch(n_chunks, width, dtype):
    return [pltpu.VMEM((n_chunks, SC_ROWS), jnp.int32),
            pltpu.VMEM((SC_ROWS, width), dtype), pltpu.VMEM((SC_ROWS, width), dtype),
            pltpu.SemaphoreType.DMA, pltpu.SemaphoreType.DMA]


def _sc_scatter_rows(rows, idx, n_out):
    n_in, width = rows.shape
    n_chunks = _sc_chunks(n_in)

    @functools.partial(
        pl.kernel, mesh=_sc_mesh(), out_type=jax.ShapeDtypeStruct((n_out, width), rows.dtype),
        scratch_types=_sc_scratch(n_chunks, width, rows.dtype), name="sc_dispatch")
    def scatter(rows_hbm, idx_hbm, out_hbm, idx_v, rows_a, rows_b, sem_a, sem_b):
        first = (lax.axis_index("s") * SC_CORES + lax.axis_index("c")) * n_chunks
        pltpu.sync_copy(idx_hbm.at[pl.ds(first, n_chunks)], idx_v)

        def load(c, buf):
            pltpu.sync_copy(rows_hbm.at[pl.ds((first + c) * SC_ROWS, SC_ROWS)], buf)

        def put(c, buf, sem):
            return pltpu.make_async_copy(buf, out_hbm.at[idx_v.at[c]], sem)

        load(0, rows_a)
        put(0, rows_a, sem_a).start()

        @pl.loop(0, n_chunks, step=2)
        def _(j):
            load(j + 1, rows_b)
            put(j + 1, rows_b, sem_b).start()
            put(j, rows_a, sem_a).wait()

            @pl.when(j + 2 < n_chunks)
            def _():
                load(j + 2, rows_a)
                put(j + 2, rows_a, sem_a).start()

            put(j + 1, rows_b, sem_b).wait()

    return scatter(rows, idx.reshape(n_in // SC_ROWS, SC_ROWS))


def _sc_gather_rows(table, idx):
    n_out, width = idx.shape[0], table.shape[1]
    n_chunks = _sc_chunks(n_out)

    @functools.partial(
        pl.kernel, mesh=_sc_mesh(), out_type=jax.ShapeDtypeStruct((n_out, width), table.dtype),
        scratch_types=_sc_scratch(n_chunks, width, table.dtype), name="sc_combine")
    def gather(table_hbm, idx_hbm, out_hbm, idx_v, rows_a, rows_b, sem_a, sem_b):
        first = (lax.axis_index("s") * SC_CORES + lax.axis_index("c")) * n_chunks
        pltpu.sync_copy(idx_hbm.at[pl.ds(first, n_chunks)], idx_v)

        def fetch(c, buf, sem):
            return pltpu.make_async_copy(table_hbm.at[idx_v.at[c]], buf, sem)

        def store(c, buf):
            pltpu.sync_copy(buf, out_hbm.at[pl.ds((first + c) * SC_ROWS, SC_ROWS)])

        fetch(0, rows_a, sem_a).start()

        @pl.loop(0, n_chunks, step=2)
        def _(j):
            fetch(j + 1, rows_b, sem_b).start()
            fetch(j, rows_a, sem_a).wait()
            store(j, rows_a)

            @pl.when(j + 2 < n_chunks)
            def _():
                fetch(j + 2, rows_a, sem_a).start()

            fetch(j + 1, rows_b, sem_b).wait()
            store(j + 1, rows_b)

    return gather(table, idx.reshape(n_out // SC_ROWS, SC_ROWS))


def _moe_kernel(ta_ref, tb_ref, nv_ref, xs_ref, wga, wua, wda, wgb, wub, wdb, l2g, l2b, ys_ref):
    i = pl.program_id(0)

    @pl.when(nv_ref[i] > 0)
    def _compute():
        x = xs_ref[:, 0:D_MODEL]
        xb = x.astype(BF16)
        w_a = xs_ref[:, D_MODEL:D_MODEL + 1]
        w_b = xs_ref[:, D_MODEL + 1:D_MODEL + 2]

        def expert(wg, wu, wd):
            g = jnp.dot(xb, wg[...], preferred_element_type=F32)
            u = jnp.dot(xb, wu[...], preferred_element_type=F32)
            hdn = (g * _sigmoid(g) * u).astype(BF16)
            return jnp.dot(hdn, wd[...], preferred_element_type=F32)

        ffn = w_a * expert(wga, wua, wda) + w_b * expert(wgb, wub, wdb)
        ys_ref[...] = _layer_norm(ALPHA * x + ffn, l2g[...], l2b[...])


def _moe(xs, tile_a, tile_b, n_valid, wg, wu, wd, l2g, l2b):
    tm = MOE_TILE
    n_tiles = xs.shape[0] // tm
    rows = lambda n: pl.BlockSpec((tm, n), lambda i, ta, tb, nv: (i, 0))
    w_in = lambda which: pl.BlockSpec((None, D_MODEL, D_EXPERT), lambda i, ta, tb, nv: ((ta, tb)[which][i], 0, 0))
    w_out = lambda which: pl.BlockSpec((None, D_EXPERT, D_MODEL), lambda i, ta, tb, nv: ((ta, tb)[which][i], 0, 0))
    const = lambda shape: pl.BlockSpec(shape, lambda i, ta, tb, nv: (0,) * len(shape))
    grid_spec = pltpu.PrefetchScalarGridSpec(
        num_scalar_prefetch=3, grid=(n_tiles,),
        in_specs=[rows(xs.shape[1]), w_in(0), w_in(0), w_out(0), w_in(1), w_in(1), w_out(1),
                  const(l2g.shape), const(l2b.shape)],
        out_specs=rows(D_MODEL))
    return pl.pallas_call(
        _moe_kernel, grid_spec=grid_spec, out_shape=jax.ShapeDtypeStruct((xs.shape[0], D_MODEL), F32),
        compiler_params=_cparams(1), name="moe")(
            tile_a, tile_b, n_valid, xs, wg, wu, wd, wg, wu, wd, l2g, l2b)


def _route_tables(route, counts, n_tokens):
    tm = MOE_TILE
    n_tiles = n_tokens // tm + N_CLASSES
    cnt = counts[:N_CLASSES, 0].astype(jnp.int32)
    tiles = (cnt + tm - 1) // tm
    tile_end = jnp.cumsum(tiles)
    tile_start = tile_end - tiles
    t_idx = jnp.arange(n_tiles, dtype=jnp.int32)
    cls_of_tile = jnp.minimum(jnp.sum(t_idx[:, None] >= tile_end[None, :], axis=1), N_CLASSES - 1).astype(jnp.int32)
    classes = np.arange(N_CLASSES)
    tile_is = cls_of_tile[:, None] == classes[None, :]
    per_tile = lambda table: jnp.sum(jnp.where(tile_is, jnp.asarray(table, jnp.int32)[None, :], 0), axis=1)
    n_valid = jnp.clip(per_tile(cnt) - (t_idx - per_tile(tile_start)) * tm, 0, tm).astype(jnp.int32)
    first_expert = classes // PAIRS_PER_GROUP * EXPERTS_PER_GROUP
    tile_a = per_tile(first_expert + np.asarray(PAIR_A)[classes % PAIRS_PER_GROUP])
    tile_b = per_tile(first_expert + np.asarray(PAIR_B)[classes % PAIRS_PER_GROUP])
    cls = route[0].astype(jnp.int32)
    rank = route[1].astype(jnp.int32)
    row0 = jnp.sum(jnp.where(cls[:, None] == jnp.arange(N_CLASSES)[None, :], (tile_start * tm)[None, :], 0), axis=1)
    return tile_a, tile_b, n_valid, row0 + rank, n_tiles * tm


def _pick_tile(T, pref):
    t = pref
    while T % t:
        t //= 2
    return t


def kernel(x, w_in, conv_w, conv_b, kv_norm_g, w_uk, w_uv, rel_bias, b_i, b_f, mh_norm_g, w_up_a, w_up_m,
           w_out, ln1_g, ln1_b, w_grp, b_grp, w_rt, b_rt, w_gate, w_up, w_down, ln2_g, ln2_b):
    B, S, _ = x.shape
    T = B * S
    assert S % Q_TILE == 0 and S % M_CHUNK == 0 and T % MOE_TILE == 0 and w_in.shape[0] == DEPTH
    tz = _bias_tables(rel_bias)
    x2 = x.reshape(T, D_MODEL)
    for l in range(DEPTH):
        w = w_in[l]
        o = np.cumsum((WIDTH_A, KV_RANK, WIDTH_IDX, HEAD_DIM_IDX, N_HEADS_IDX, 2 * WIDTH_M, WIDTH_M,
                       N_HEADS_M, N_HEADS_M, WIDTH_M, D_MODEL, D_MODEL)).tolist()
        o = [0] + o
        seg = lambda j: w[:, o[j]:o[j + 1]]
        pad = LANES - (HEAD_DIM_IDX + N_HEADS_IDX + 2 * N_HEADS_M)
        w_small = jnp.concatenate([seg(3), seg(4), seg(7), seg(8), jnp.zeros((D_MODEL, pad), w.dtype)], axis=1)
        ws = [seg(0), seg(1), seg(2), w_small, seg(5), seg(6), seg(9), seg(10), seg(11)]
        ws = [a.astype(BF16) for a in ws]
        smb = jnp.zeros((1, LANES), F32).at[0, SM_I:SM_I + N_HEADS_M].set(b_i[l]) \
            .at[0, SM_F:SM_F + N_HEADS_M].set(b_f[l])
        qa, ckv, qi, sm, qk, v, og, ga, gm, ckv_t = _proj(x2, ws, kv_norm_g[l][None, :], smb,
                                                          _pick_tile(S, PROJ_TILE), S)

        wuk_t = jnp.transpose(w_uk[l], (1, 0, 2)).astype(BF16)
        wuv_t = jnp.transpose(w_uv[l], (1, 2, 0)).astype(BF16)
        oa = _dsa(qi, sm, qa, ckv, ckv_t, wuk_t, wuv_t, tz, B, S)

        hm = _mlstm(qk, v, sm, og, conv_w[l], conv_b[l][None, :], mh_norm_g[l].reshape(1, WIDTH_M), B, S)

        w_router = jnp.concatenate(
            [w_rt[l], w_grp[l], jnp.zeros((D_MODEL, LANES - N_EXPERTS - N_GROUPS), F32)], axis=1)
        b_router = jnp.concatenate(
            [b_rt[l], b_grp[l], jnp.zeros((LANES - N_EXPERTS - N_GROUPS,), F32)])[None, :]
        wr_hi = w_router.astype(BF16)
        wr_split = jnp.stack([wr_hi, (w_router - wr_hi.astype(F32)).astype(BF16)])
        x1, route, counts = _merge(x2, oa, hm, ga, gm, w_up_a[l].astype(BF16), w_up_m[l].astype(BF16),
                                   w_out[l].astype(BF16), ln1_g[l][None, :], ln1_b[l][None, :],
                                   wr_split, b_router, _pick_tile(T, MERGE_TILE))

        tile_a, tile_b, n_valid, pos, n_sorted = _route_tables(route, counts, T)
        xs = _sc_scatter_rows(x1, pos, n_sorted)
        ys = _moe(xs, tile_a, tile_b, n_valid, w_gate[l].astype(BF16), w_up[l].astype(BF16),
                  w_down[l].astype(BF16), ln2_g[l][None, :], ln2_b[l][None, :])
        x2 = _sc_gather_rows(ys, pos)
    return x2.reshape(B, S, D_MODEL)
```

```python
import functools
import math

import jax
import jax.numpy as jnp
import numpy as np
from jax import lax
from jax.experimental import pallas as pl
from jax.experimental.pallas import tpu as pltpu
from jax.experimental.pallas import tpu_sc as plsc

F32 = jnp.float32
BF16 = jnp.bfloat16

D_MODEL = 1024
N_HEADS_A = 8
HEAD_DIM_A = 64
WIDTH_A = N_HEADS_A * HEAD_DIM_A
KV_RANK = 256
N_HEADS_IDX = 8
HEAD_DIM_IDX = 64
WIDTH_IDX = N_HEADS_IDX * HEAD_DIM_IDX
TOPK_MAX = 256
N_BUCKETS = 32
MAX_DISTANCE = 128
N_HEADS_M = 4
HEAD_DIM_M = 128
WIDTH_M = N_HEADS_M * HEAD_DIM_M
CONV_WIDTH = 4
N_GROUPS = 4
EXPERTS_PER_GROUP = 4
N_EXPERTS = N_GROUPS * EXPERTS_PER_GROUP
D_EXPERT = 512
LN_EPS = 1e-5
DEPTH = 1
ALPHA = (2.0 * DEPTH) ** 0.25

LANES = 128
SUBLANES = 8
VMEM_LIMIT = 56 * 1024 * 1024

SM_KIDX = 0
SM_WIDX = HEAD_DIM_IDX
SM_I = SM_WIDX + N_HEADS_IDX
SM_F = SM_I + N_HEADS_M

Q_TILE = 256
K_CHUNK = Q_TILE
MXU_CHUNKS_PER_TRIP = 4
ONES_ROWS = 16
M_CHUNK = 128
M_GROUPS = 2
CONV_HALO = 16
assert CONV_HALO >= CONV_WIDTH - 1
PROJ_TILE = 1024
MERGE_TILE = 256
MOE_TILE = 512
SC_CORES = 2
SC_SUBCORES = 16
SC_ROWS = 32
EPG_SHIFT = EXPERTS_PER_GROUP.bit_length() - 1
assert 1 << EPG_SHIFT == EXPERTS_PER_GROUP
PAIR_A, PAIR_B = zip(*[(a, b) for a in range(EXPERTS_PER_GROUP) for b in range(a + 1, EXPERTS_PER_GROUP)])
PAIRS_PER_GROUP = len(PAIR_A)
N_CLASSES = N_GROUPS * PAIRS_PER_GROUP
CLASS_ROWS = -(-N_CLASSES // SUBLANES) * SUBLANES
BISECT_STEPS_PER_CHECK = 3
BISECT_MAX_CHECKS = 5
PEEL_BRACKET = 2.0
NEG_INF = float("-inf")
LOG2E = math.log2(math.e)


def _cparams(n_grid):
    return pltpu.CompilerParams(dimension_semantics=("arbitrary",) * n_grid,
                                vmem_limit_bytes=VMEM_LIMIT)


def _full(shape):
    nd = len(shape)
    return pl.BlockSpec(shape, lambda *_: (0,) * nd, pipeline_mode=pl.Buffered(1))


def _sigmoid(x):
    return 0.5 * jnp.tanh(0.5 * x) + 0.5


def _split3(x):
    hi = x.astype(BF16)
    r = x - hi.astype(F32)
    mid = r.astype(BF16)
    return hi, mid, (r - mid.astype(F32)).astype(BF16)


def _proj_kernel(x_ref, wqa, wckv, wqi, wsm, wqk, wv, wo, wga, wgm, kvg, smb,
                 qa_o, ckv_o, qi_o, sm_o, qk_o, v_o, o_o, ga_o, gm_o, ckvt_o):
    xb = x_ref[...].astype(BF16)

    def mm(w):
        return jnp.dot(xb, w[...], preferred_element_type=F32)

    qa_o[...] = mm(wqa).astype(BF16)
    c = mm(wckv)
    c = c * lax.rsqrt(jnp.mean(c * c, axis=-1, keepdims=True) + LN_EPS) * kvg[...]
    ckv_o[...] = c.astype(BF16)
    ckvt_o[0:KV_RANK, :] = c.T.astype(BF16)
    ckvt_o[KV_RANK:KV_RANK + ONES_ROWS, :] = jnp.ones((ONES_ROWS, c.shape[0]), BF16)
    qi_o[...] = mm(wqi).astype(BF16)
    sm_o[...] = mm(wsm) + smb[...]
    qk_o[...] = mm(wqk).astype(BF16)
    v_o[...] = mm(wv).astype(BF16)
    o_o[...] = mm(wo).astype(BF16)
    ga_o[...] = mm(wga).astype(BF16)
    gm_o[...] = mm(wgm).astype(BF16)


def _proj(x2, ws, kvg, smb, tm, seq):
    T = x2.shape[0]
    per_seq = seq // tm
    widths = [w.shape[1] for w in ws]
    dts = [BF16, BF16, BF16, F32, BF16, BF16, BF16, BF16, BF16]
    in_specs = [pl.BlockSpec((tm, D_MODEL), lambda i: (i, 0))]
    in_specs += [_full(w.shape) for w in ws]
    in_specs += [_full(kvg.shape), _full(smb.shape)]
    out_specs = [pl.BlockSpec((tm, n), lambda i: (i, 0)) for n in widths]
    out_specs.append(pl.BlockSpec((None, KV_RANK + ONES_ROWS, tm), lambda i: (i // per_seq, 0, i % per_seq)))
    out_shape = [jax.ShapeDtypeStruct((T, n), dt) for n, dt in zip(widths, dts)]
    out_shape.append(jax.ShapeDtypeStruct((T // seq, KV_RANK + ONES_ROWS, seq), BF16))
    return pl.pallas_call(
        _proj_kernel, grid=(T // tm,), in_specs=in_specs, out_specs=out_specs, out_shape=out_shape,
        compiler_params=_cparams(1), name="proj")(x2, *ws, kvg, smb)


def _dsa_kernel(qi_ref, smq_ref, smk_ref, qa_ref, ckv_ref, ckvt_ref, wuk_ref, wuvt_ref, tz_ref,
                oa_ref, qs_ref, sc_ref, am_ref, ql_ref, x_ref, acc_ref, m_ref, *, topk):
    tq, kc = Q_TILE, K_CHUNK
    nh = N_HEADS_A
    qb = pl.program_id(1)
    n_ch = qb + 1
    seq_keys = n_ch * kc
    t0 = qb * tq

    def rows(c):
        return pl.ds(pl.multiple_of(c * kc, kc), kc)

    def lanes(h):
        return slice(h * tq, (h + 1) * tq)

    def chunk_loop(n, body, init, per_trip=2):
        def run(first, count, carry):
            for r in range(count):
                carry = body(first + r, carry)
            return carry

        shift = per_trip.bit_length() - 1
        trips = jnp.right_shift(n, shift)
        carry = lax.fori_loop(0, trips, lambda p, c: run(p * per_trip, per_trip, c), init)
        done = trips * per_trip
        count = per_trip // 2
        while count:
            start = done + jnp.bitwise_and(n - done, -2 * count)
            carry = lax.cond(jnp.bitwise_and(n, count) != 0,
                             functools.partial(run, start, count), lambda c: c, carry)
            count //= 2
        return carry

    for h in range(N_HEADS_IDX):
        qs_ref[h * tq:(h + 1) * tq, :] = qi_ref[:, h * HEAD_DIM_IDX:(h + 1) * HEAD_DIM_IDX]
    w_t = smq_ref[...].T
    q_pos = lax.broadcasted_iota(jnp.int32, (1, tq), 1) + t0
    key_iota = lax.broadcasted_iota(jnp.int32, (kc, tq), 0)

    def score_chunk(c, carry):
        mx, mn = carry
        kk = smk_ref[rows(c), SM_KIDX:SM_KIDX + HEAD_DIM_IDX].astype(BF16)
        dots = lax.dot_general(kk, qs_ref[...], (((1,), (1,)), ((), ())), preferred_element_type=F32)
        sc = jnp.zeros((kc, tq), F32)
        for h in range(N_HEADS_IDX):
            sc = sc + w_t[SM_WIDX + h:SM_WIDX + h + 1, :] * jnp.maximum(dots[:, lanes(h)], 0.0)
        vis = (key_iota + c * kc) <= q_pos
        sc_ref[rows(c), :] = jnp.where(vis, sc, NEG_INF)
        mx = jnp.maximum(mx, jnp.max(jnp.where(vis, sc, NEG_INF), axis=0, keepdims=True))
        mn = jnp.minimum(mn, jnp.min(jnp.where(vis, sc, jnp.inf), axis=0, keepdims=True))
        return mx, mn

    mx, mn = chunk_loop(n_ch, score_chunk,
                        (jnp.full((1, tq), NEG_INF, F32), jnp.full((1, tq), jnp.inf, F32)), MXU_CHUNKS_PER_TRIP)

    n_vis = (q_pos + 1).astype(F32)
    k_row = jnp.minimum(n_vis, float(topk))

    def count(pred):
        def body(c, a):
            hit = pred(sc_ref[rows(c), :]).astype(F32)
            return a + jnp.sum(hit.reshape(kc // SUBLANES, SUBLANES, tq), axis=0)
        a = chunk_loop(n_ch, body, jnp.zeros((SUBLANES, tq), F32))
        return jnp.sum(a, axis=0, keepdims=True)

    def any_lane(flag):
        return jnp.max(jnp.where(flag, 1.0, 0.0)) > 0.0

    def crowded(cnt_lo, c_hi):
        return any_lane(jnp.logical_and(cnt_lo != k_row, cnt_lo - c_hi > PEEL_BRACKET))

    def bisect_cond(carry):
        it, lo, hi, cnt_lo, c_hi = carry
        return jnp.logical_and(it < BISECT_MAX_CHECKS, crowded(cnt_lo, c_hi))

    def bisect_body(carry):
        it, lo, hi, cnt_lo, c_hi = carry
        for _ in range(BISECT_STEPS_PER_CHECK):
            mid = lo * 0.5 + hi * 0.5
            cnt = count(lambda s: s >= mid)
            ge = cnt >= k_row
            lo, cnt_lo = jnp.where(ge, mid, lo), jnp.where(ge, cnt, cnt_lo)
            hi, c_hi = jnp.where(ge, hi, mid), jnp.where(ge, c_hi, cnt)
        return it + 1, lo, hi, cnt_lo, c_hi

    hi0 = mx + jnp.maximum(jnp.abs(mx), 1e-30) * 1e-6
    _, lo, hi, cnt_lo, c_hi = lax.while_loop(
        bisect_cond, bisect_body, (jnp.int32(0), mn, hi0, n_vis, jnp.zeros((1, tq), F32)))

    def peel_cond(carry):
        it, lo, hi, cnt_lo, c_hi, done = carry
        return jnp.logical_and(it < seq_keys, any_lane(done == 0.0))

    def peel_body(carry):
        it, lo, hi, cnt_lo, c_hi, done = carry

        def top_body(c, v):
            s = sc_ref[rows(c), :]
            inside = jnp.logical_and(s >= lo, s < hi)
            return jnp.maximum(v, jnp.max(jnp.where(inside, s, NEG_INF), axis=0, keepdims=True))

        v = chunk_loop(n_ch, top_body, jnp.full((1, tq), NEG_INF, F32))
        c_v = count(lambda s: s >= v)
        reached = c_v >= k_row
        live = done == 0.0
        fin = jnp.logical_and(live, reached)
        cut = jnp.logical_and(live, jnp.logical_not(reached))
        return (it + 1, jnp.where(fin, v, lo), jnp.where(cut, v, hi), jnp.where(fin, c_v, cnt_lo),
                jnp.where(cut, c_v, c_hi), jnp.where(fin, 1.0, done))

    _, lo, hi, cnt_lo, c_hi, _ = lax.while_loop(
        peel_cond, peel_body,
        (jnp.int32(0), lo, hi, cnt_lo, c_hi, jnp.where(cnt_lo == k_row, 1.0, 0.0)))

    def mask_chunk(c, _):
        am_ref[rows(c), :] = jnp.where(sc_ref[rows(c), :] >= lo, 0.0, NEG_INF)
        return 0

    chunk_loop(n_ch, mask_chunk, 0)

    tied = cnt_lo != k_row
    for part in range(tq // LANES):
        ls = slice(part * LANES, (part + 1) * LANES)

        @pl.when(any_lane(tied[:, ls]))
        def _ties():
            need, lo_p, hi_p = (k_row - c_hi)[:, ls], lo[:, ls], hi[:, ls]
            lower = (lax.broadcasted_iota(jnp.int32, (kc, kc), 1)
                     < lax.broadcasted_iota(jnp.int32, (kc, kc), 0)).astype(BF16)

            def tie_chunk(c, before):
                s = sc_ref[rows(c), ls]
                above = s >= hi_p
                tie = jnp.logical_and(s >= lo_p, jnp.logical_not(above))
                tie_f = tie.astype(F32)
                rank = jnp.dot(lower, tie_f.astype(BF16), preferred_element_type=F32) + before
                sel = jnp.logical_or(above, jnp.logical_and(tie, rank < need))
                am_ref[rows(c), ls] = jnp.where(sel, 0.0, NEG_INF)
                return before + jnp.sum(tie_f, axis=0, keepdims=True)

            chunk_loop(n_ch, tie_chunk, jnp.zeros((1, LANES), F32))

    for h in range(nh):
        qh = qa_ref[:, h * HEAD_DIM_A:(h + 1) * HEAD_DIM_A]
        qlt = lax.dot_general(wuk_ref[h], qh, (((1,), (1,)), ((), ())), preferred_element_type=F32)
        ql_ref[:, lanes(h)] = (qlt * (HEAD_DIM_A ** -0.5 * LOG2E)).astype(BF16)

    m_ref[...] = jnp.full(m_ref.shape, NEG_INF, F32)

    def pass_a(c, table):
        raw = jnp.dot(ckv_ref[rows(c), :], ql_ref[...], preferred_element_type=F32)
        am = am_ref[rows(c), :]
        for h in range(nh):
            x = raw[:, lanes(h)] + am
            if table is not None:
                x = x + tz_ref[table, h]
            x_ref[rows(c), lanes(h)] = x
            m_ref[:, lanes(h)] = jnp.maximum(m_ref[:, lanes(h)], jnp.max(x, axis=0, keepdims=True))

    def far_chunk(c, _):
        pass_a(c, None)
        return 0

    chunk_loop(jnp.maximum(qb - 1, 0), far_chunk, 0, MXU_CHUNKS_PER_TRIP)

    @pl.when(qb >= 1)
    def _prev():
        pass_a(qb - 1, 1)

    pass_a(qb, 0)

    def pv(c):
        p = jnp.exp2((x_ref[rows(c), :] - m_ref[...]).astype(BF16))
        return jnp.dot(ckvt_ref[:, rows(c)], p, preferred_element_type=F32)

    def pass_b(c, _):
        acc_ref[...] += pv(c)
        return 0

    acc_ref[...] = pv(qb)
    chunk_loop(qb, pass_b, 0, MXU_CHUNKS_PER_TRIP)
    inv_l = 1.0 / acc_ref[KV_RANK:KV_RANK + 1, :]
    o_lat = (acc_ref[0:KV_RANK, :] * inv_l).astype(BF16)
    outs = [jnp.dot(wuvt_ref[h], o_lat[:, lanes(h)], preferred_element_type=F32) for h in range(nh)]
    oa_ref[...] = jnp.concatenate(outs, axis=0).T.astype(BF16)


def _dsa(qi, sm, qa, ckv, ckv_t, wuk_t, wuv_t, tz, batch, seq):
    tq = Q_TILE
    nq = seq // tq
    topk = min(TOPK_MAX, seq // 4)
    T = batch * seq
    kern = functools.partial(_dsa_kernel, topk=topk)
    blk_q = lambda n: pl.BlockSpec((tq, n), lambda b, q: (b * nq + q, 0))
    blk_s = lambda n: pl.BlockSpec((seq, n), lambda b, q: (b, 0))
    in_specs = [
        blk_q(WIDTH_IDX), blk_q(LANES), blk_s(LANES), blk_q(WIDTH_A), blk_s(KV_RANK),
        pl.BlockSpec((None, KV_RANK + ONES_ROWS, seq), lambda b, q: (b, 0, 0)),
        _full(wuk_t.shape), _full(wuv_t.shape), _full(tz.shape),
    ]
    scratch = [
        pltpu.VMEM((N_HEADS_IDX * tq, HEAD_DIM_IDX), BF16),
        pltpu.VMEM((seq, tq), F32),
        pltpu.VMEM((seq, tq), F32),
        pltpu.VMEM((KV_RANK, N_HEADS_A * tq), BF16),
        pltpu.VMEM((seq, N_HEADS_A * tq), F32),
        pltpu.VMEM((KV_RANK + ONES_ROWS, N_HEADS_A * tq), F32),
        pltpu.VMEM((1, N_HEADS_A * tq), F32),
    ]
    return pl.pallas_call(
        kern, grid=(batch, nq), in_specs=in_specs,
        out_specs=pl.BlockSpec((tq, WIDTH_A), lambda b, q: (b * nq + q, 0)),
        out_shape=jax.ShapeDtypeStruct((T, WIDTH_A), BF16),
        scratch_shapes=scratch, compiler_params=_cparams(2), name="dsa")(
            qi, sm, sm, qa, ckv, ckv_t, wuk_t, wuv_t, tz)


def _t5_bucket(dist):
    max_exact = N_BUCKETS // 2
    d = jnp.maximum(dist, 0)
    ratio = jnp.log(jnp.maximum(d, 1).astype(F32) / max_exact) / math.log(MAX_DISTANCE / max_exact)
    large = jnp.minimum(max_exact + (ratio * (N_BUCKETS - max_exact)).astype(jnp.int32), N_BUCKETS - 1)
    return jnp.where(d < max_exact, d, large)


def _bias_tables(rel_bias):
    tq = Q_TILE
    span = 2 * tq
    assert int(_np_bucket(tq)) == N_BUCKETS - 1
    far = rel_bias[N_BUCKETS - 1]
    by_dist = ((rel_bias[_t5_bucket(jnp.arange(span))] - far).astype(F32) * LOG2E).T
    diag = jnp.concatenate([by_dist[:, :tq], jnp.zeros_like(by_dist[:, :tq])], axis=1)
    prev = jnp.concatenate([by_dist[:, tq:], by_dist[:, :tq]], axis=1)

    def toeplitz(f):
        m = jnp.tile(f, (1, tq))[:, :tq * (span - 1)].reshape(N_HEADS_A, tq, span - 1)
        return m[:, :, :tq]

    return jnp.stack([toeplitz(diag), toeplitz(prev)])


def _np_bucket(d):
    max_exact = N_BUCKETS // 2
    ratio = np.log(np.float32(max(d, 1)) / np.float32(max_exact)) / math.log(MAX_DISTANCE / max_exact)
    return min(max_exact + int(ratio * (N_BUCKETS - max_exact)), N_BUCKETS - 1) if d >= max_exact else d


def _mlstm_kernel(qk_ref, halo_ref, v_ref, sm_ref, o_ref, cw_ref, cb_ref, g_ref,
                  out_ref, c_ref, m_ref):
    c_idx = pl.program_id(1)

    @pl.when(c_idx == 0)
    def _init():
        c_ref[...] = jnp.zeros_like(c_ref)
        m_ref[...] = jnp.zeros_like(m_ref)

    for g in range(M_GROUPS):
        gates = _mlstm_gates(g, c_idx, qk_ref, halo_ref, sm_ref, cw_ref, cb_ref)
        outs = [_mlstm_head(g, h, gates, v_ref, o_ref, g_ref, c_ref, m_ref) for h in range(N_HEADS_M)]
        out_ref[g] = jnp.concatenate(outs, axis=-1).astype(BF16)


def _mlstm_gates(g, c_idx, qk_ref, halo_ref, sm_ref, cw_ref, cb_ref):
    L = M_CHUNK

    hw = halo_ref.shape[1]
    halo = jnp.where(c_idx > 0, halo_ref[g].astype(F32), 0.0)
    ext = jnp.concatenate([halo, qk_ref[g].astype(F32)], axis=0)
    acc = jnp.zeros((L, 2 * WIDTH_M), F32) + cb_ref[...]
    for w in range(CONV_WIDTH):
        off = hw - (CONV_WIDTH - 1) + w
        acc = acc + ext[off:off + L, :] * cw_ref[w:w + 1, :]
    qk = acc * _sigmoid(acc)

    sm = sm_ref[g]
    sm_t = sm.T
    r_i = lax.broadcasted_iota(jnp.int32, (L, L), 0)
    c_i = lax.broadcasted_iota(jnp.int32, (L, L), 1)
    tril = (c_i <= r_i).astype(BF16)
    bcum_c = sum(jnp.dot(tril, part, preferred_element_type=F32)
                 for part in _split3(jax.nn.log_sigmoid(sm)))
    return qk, sm, sm_t, bcum_c, bcum_c.T, c_i <= r_i


def _mlstm_head(g, h, gates, v_ref, o_ref, g_ref, c_ref, m_ref):
    L = M_CHUNK
    dm = HEAD_DIM_M
    qk, sm, sm_t, bcum_c, bcum_r, causal = gates
    ones_col = (lax.broadcasted_iota(jnp.int32, (L, dm), 1) == 0).astype(BF16)
    st = g * N_HEADS_M + h
    q = qk[:, h * dm:(h + 1) * dm].astype(BF16)
    k = (qk[:, WIDTH_M + h * dm:WIDTH_M + (h + 1) * dm] * (dm ** -0.5))
    v_aug = jnp.concatenate([v_ref[g, :, h * dm:(h + 1) * dm], ones_col], axis=-1)
    b_col = bcum_c[:, SM_F + h:SM_F + h + 1]
    g_col = sm[:, SM_I + h:SM_I + h + 1] - b_col
    g_row = sm_t[SM_I + h:SM_I + h + 1, :] - bcum_r[SM_F + h:SM_F + h + 1, :]
    b_last = b_col[L - 1:L, :]
    m_prev = m_ref[st]
    c_prev = c_ref[st]

    log_d = jnp.where(causal, b_col + g_row, NEG_INF)
    m_j = jnp.maximum(b_col + m_prev, jnp.max(log_d, axis=-1, keepdims=True))
    w_inter = jnp.exp(b_col + m_prev - m_j)
    qkt = lax.dot_general(q, k.astype(BF16), (((1,), (1,)), ((), ())), preferred_element_type=F32)
    s = qkt * jnp.exp(log_d - m_j)
    o_aug = jnp.dot(s.astype(BF16), v_aug, preferred_element_type=F32) + \
        w_inter * jnp.dot(q, c_prev.astype(BF16), preferred_element_type=F32)
    num = o_aug[:, :dm]
    den = o_aug[:, dm:dm + 1]
    hh = num / jnp.maximum(jnp.abs(den), jnp.exp(-m_j))

    lwe = b_last + g_col
    m_loc = jnp.max(lwe, axis=0, keepdims=True)
    kw = (k * jnp.exp(lwe - m_loc)).astype(BF16)
    c_loc = lax.dot_general(kw, v_aug, (((0,), (0,)), ((), ())), preferred_element_type=F32)
    m_new = jnp.maximum(b_last + m_prev, m_loc)
    c_ref[st] = jnp.exp(b_last + m_prev - m_new) * c_prev + jnp.exp(m_loc - m_new) * c_loc
    m_ref[st] = m_new

    hn = hh * lax.rsqrt(jnp.mean(hh * hh, axis=-1, keepdims=True) + LN_EPS) * g_ref[:, h * dm:(h + 1) * dm]
    og = o_ref[g, :, h * dm:(h + 1) * dm].astype(F32)
    return _sigmoid(og) * hn


def _mlstm(qk, v, sm, o, conv_w, conv_b, mh_g, batch, seq):
    L = M_CHUNK
    G = M_GROUPS
    nc = seq // L
    T = batch * seq
    hb = L // CONV_HALO
    grouped = lambda a: a.reshape(G, T // G, a.shape[-1])
    blk = lambda n: pl.BlockSpec((G, L, n), lambda b, c: (0, b * nc + c, 0))
    in_specs = [
        blk(2 * WIDTH_M),
        pl.BlockSpec((G, CONV_HALO, 2 * WIDTH_M), lambda b, c: (0, jnp.maximum((b * nc + c) * hb - 1, 0), 0)),
        blk(WIDTH_M), blk(LANES), blk(WIDTH_M),
        _full(conv_w.shape), _full(conv_b.shape), _full(mh_g.shape),
    ]
    scratch = [pltpu.VMEM((G * N_HEADS_M, HEAD_DIM_M, 2 * HEAD_DIM_M), F32),
               pltpu.VMEM((G * N_HEADS_M, 1, 1), F32)]
    qk_g = grouped(qk)
    out = pl.pallas_call(
        _mlstm_kernel, grid=(batch // G, nc), in_specs=in_specs, out_specs=blk(WIDTH_M),
        out_shape=jax.ShapeDtypeStruct((G, T // G, WIDTH_M), BF16), scratch_shapes=scratch,
        compiler_params=_cparams(2), name="mlstm")(
            qk_g, qk_g, grouped(v), grouped(sm), grouped(o), conv_w, conv_b, mh_g)
    return out.reshape(T, WIDTH_M)


def _layer_norm(y, g, b):
    mu = jnp.mean(y, axis=-1, keepdims=True)
    var = jnp.mean(jnp.square(y - mu), axis=-1, keepdims=True)
    return (y - mu) * lax.rsqrt(var + LN_EPS) * g + b


def _router_logits(x1, wr, br):
    x1_hi = x1.astype(BF16)
    x1_lo = (x1 - x1_hi.astype(F32)).astype(BF16)
    return (jnp.dot(x1_hi, wr[0], preferred_element_type=F32)
            + jnp.dot(x1_lo, wr[0], preferred_element_type=F32)
            + jnp.dot(x1_hi, wr[1], preferred_element_type=F32)) + br[...]


def _merge_kernel(x_ref, oa_ref, hm_ref, ga_ref, gm_ref, wua, wum, wout, l1g, l1b, wr, br,
                  x1_ref, route_ref, cnt_ref):
    pa = jnp.dot(oa_ref[...], wua[...], preferred_element_type=F32)
    pm = jnp.dot(hm_ref[...], wum[...], preferred_element_type=F32)
    y = _sigmoid(ga_ref[...]) * pa.astype(BF16) + _sigmoid(gm_ref[...]) * pm.astype(BF16)
    mix = jnp.dot(y.astype(BF16), wout[...], preferred_element_type=F32)
    x1 = _layer_norm(ALPHA * x_ref[...] + mix, l1g[...], l1b[...])
    x1_ref[:, 0:D_MODEL] = x1

    lt = _router_logits(x1, wr, br).T
    tm = lt.shape[1]
    big = jnp.int32(LANES)
    le = lt[0:N_EXPERTS, :]
    lg = lt[N_EXPERTS:N_EXPERTS + SUBLANES, :]
    row_g = lax.broadcasted_iota(jnp.int32, lg.shape, 0)
    row_e = lax.broadcasted_iota(jnp.int32, le.shape, 0)
    is_grp = row_g < N_GROUPS
    gl = jnp.where(is_grp, lg, NEG_INF)
    ge = jnp.exp(gl - jnp.max(gl, axis=0, keepdims=True))
    gp = ge / jnp.sum(ge, axis=0, keepdims=True)
    g_w = jnp.max(gp, axis=0, keepdims=True)
    g_idx = jnp.min(jnp.where(jnp.logical_and(is_grp, gp == g_w), row_g, big), axis=0, keepdims=True)
    el = jnp.where(jnp.right_shift(row_e, EPG_SHIFT) == g_idx, le, NEG_INF)
    m1 = jnp.max(el, axis=0, keepdims=True)
    i1 = jnp.min(jnp.where(el == m1, row_e, big), axis=0, keepdims=True)
    el2 = jnp.where(row_e == i1, NEG_INF, el)
    m2 = jnp.max(el2, axis=0, keepdims=True)
    i2 = jnp.min(jnp.where(el2 == m2, row_e, big), axis=0, keepdims=True)

    e2 = jnp.exp(m2 - m1)
    w1 = g_w / (1.0 + e2)
    w2 = g_w * e2 / (1.0 + e2)
    first_is_a = i1 < i2
    row_w = lax.broadcasted_iota(jnp.int32, (LANES, tm), 0)
    w_rows = jnp.where(row_w == 0, jnp.where(first_is_a, w1, w2),
                       jnp.where(row_w == 1, jnp.where(first_is_a, w2, w1), 0.0))
    x1_ref[:, D_MODEL:D_MODEL + LANES] = w_rows.T

    a = jnp.bitwise_and(jnp.minimum(i1, i2), EXPERTS_PER_GROUP - 1)
    b = jnp.bitwise_and(jnp.maximum(i1, i2), EXPERTS_PER_GROUP - 1)
    pair = jnp.right_shift(a * (2 * EXPERTS_PER_GROUP - 1 - a), 1) + (b - a - 1)
    cls = g_idx * PAIRS_PER_GROUP + pair
    row_c = lax.broadcasted_iota(jnp.int32, (CLASS_ROWS, tm), 0)
    onehot = (row_c == cls).astype(F32)

    @pl.when(pl.program_id(0) == 0)
    def _init():
        cnt_ref[...] = jnp.zeros_like(cnt_ref)

    earlier = (lax.broadcasted_iota(jnp.int32, (tm, tm), 0)
               < lax.broadcasted_iota(jnp.int32, (tm, tm), 1)).astype(BF16)
    prior = jnp.dot(onehot.astype(BF16), earlier, preferred_element_type=F32) + cnt_ref[:, 0:1]
    rank = jnp.sum(prior * onehot, axis=0, keepdims=True)
    cnt_ref[...] += jnp.sum(onehot, axis=1, keepdims=True)
    row_o = lax.broadcasted_iota(jnp.int32, route_ref.shape, 0)
    route_ref[...] = jnp.where(row_o == 0, cls.astype(F32), jnp.where(row_o == 1, rank, 0.0))


def _merge(x2, oa, hm, ga, gm, wua, wum, wout, l1g, l1b, wr, br, tm):
    T = x2.shape[0]
    blk = lambda n: pl.BlockSpec((tm, n), lambda i: (i, 0))
    in_specs = [blk(D_MODEL), blk(WIDTH_A), blk(WIDTH_M), blk(D_MODEL), blk(D_MODEL),
                _full(wua.shape), _full(wum.shape), _full(wout.shape), _full(l1g.shape), _full(l1b.shape),
                _full(wr.shape), _full(br.shape)]
    out_specs = [blk(D_MODEL + LANES), pl.BlockSpec((SUBLANES, tm), lambda i: (0, i)),
                 pl.BlockSpec((CLASS_ROWS, LANES), lambda i: (0, 0))]
    out_shape = [jax.ShapeDtypeStruct((T, D_MODEL + LANES), F32), jax.ShapeDtypeStruct((SUBLANES, T), F32),
                 jax.ShapeDtypeStruct((CLASS_ROWS, LANES), F32)]
    return pl.pallas_call(
        _merge_kernel, grid=(T // tm,), in_specs=in_specs, out_specs=out_specs, out_shape=out_shape,
        compiler_params=_cparams(1), name="merge")(x2, oa, hm, ga, gm, wua, wum, wout, l1g, l1b, wr, br)


def _sc_mesh():
    return plsc.VectorSubcoreMesh(core_axis_name="c", subcore_axis_name="s",
                                  num_cores=SC_CORES, num_subcores=SC_SUBCORES)


def _sc_chunks(n_rows):
    workers = SC_CORES * SC_SUBCORES
    assert n_rows % (workers * SC_ROWS * 2) == 0
    return n_rows // (workers * SC_ROWS)


def _sc_scratch(n_chunks, width, dtype):
    return [pltpu.VMEM((n_chunks, SC_ROWS), jnp.int32),
            pltpu.VMEM((SC_ROWS, width), dtype), pltpu.VMEM((SC_ROWS, width), dtype),
            pltpu.SemaphoreType.DMA, pltpu.SemaphoreType.DMA]


def _sc_scatter_rows(rows, idx, n_out):
    n_in, width = rows.shape
    n_chunks = _sc_chunks(n_in)

    @functools.partial(
        pl.kernel, mesh=_sc_mesh(), out_type=jax.ShapeDtypeStruct((n_out, width), rows.dtype),
        scratch_types=_sc_scratch(n_chunks, width, rows.dtype), name="sc_dispatch")
    def scatter(rows_hbm, idx_hbm, out_hbm, idx_v, rows_a, rows_b, sem_a, sem_b):
        first = (lax.axis_index("s") * SC_CORES + lax.axis_index("c")) * n_chunks
        pltpu.sync_copy(idx_hbm.at[pl.ds(first, n_chunks)], idx_v)

        def load(c, buf):
            pltpu.sync_copy(rows_hbm.at[pl.ds((first + c) * SC_ROWS, SC_ROWS)], buf)

        def put(c, buf, sem):
            return pltpu.make_async_copy(buf, out_hbm.at[idx_v.at[c]], sem)

        load(0, rows_a)
        put(0, rows_a, sem_a).start()

        @pl.loop(0, n_chunks, step=2)
        def _(j):
            load(j + 1, rows_b)
            put(j + 1, rows_b, sem_b).start()
            put(j, rows_a, sem_a).wait()

            @pl.when(j + 2 < n_chunks)
            def _():
                load(j + 2, rows_a)
                put(j + 2, rows_a, sem_a).start()

            put(j + 1, rows_b, sem_b).wait()

    return scatter(rows, idx.reshape(n_in // SC_ROWS, SC_ROWS))


def _sc_gather_rows(table, idx):
    n_out, width = idx.shape[0], table.shape[1]
    n_chunks = _sc_chunks(n_out)

    @functools.partial(
        pl.kernel, mesh=_sc_mesh(), out_type=jax.ShapeDtypeStruct((n_out, width), table.dtype),
        scratch_types=_sc_scratch(n_chunks, width, table.dtype), name="sc_combine")
    def gather(table_hbm, idx_hbm, out_hbm, idx_v, rows_a, rows_b, sem_a, sem_b):
        first = (lax.axis_index("s") * SC_CORES + lax.axis_index("c")) * n_chunks
        pltpu.sync_copy(idx_hbm.at[pl.ds(first, n_chunks)], idx_v)

        def fetch(c, buf, sem):
            return pltpu.make_async_copy(table_hbm.at[idx_v.at[c]], buf, sem)

        def store(c, buf):
            pltpu.sync_copy(buf, out_hbm.at[pl.ds((first + c) * SC_ROWS, SC_ROWS)])

        fetch(0, rows_a, sem_a).start()

        @pl.loop(0, n_chunks, step=2)
        def _(j):
            fetch(j + 1, rows_b, sem_b).start()
            fetch(j, rows_a, sem_a).wait()
            store(j, rows_a)

            @pl.when(j + 2 < n_chunks)
            def _():
                fetch(j + 2, rows_a, sem_a).start()

            fetch(j + 1, rows_b, sem_b).wait()
            store(j + 1, rows_b)

    return gather(table, idx.reshape(n_out // SC_ROWS, SC_ROWS))


def _moe_kernel(ta_ref, tb_ref, nv_ref, xs_ref, wga, wua, wda, wgb, wub, wdb, l2g, l2b, ys_ref):
    i = pl.program_id(0)

    @pl.when(nv_ref[i] > 0)
    def _compute():
        x = xs_ref[:, 0:D_MODEL]
        xb = x.astype(BF16)
        w_a = xs_ref[:, D_MODEL:D_MODEL + 1]
        w_b = xs_ref[:, D_MODEL + 1:D_MODEL + 2]

        def expert(wg, wu, wd):
            g = jnp.dot(xb, wg[...], preferred_element_type=F32)
            u = jnp.dot(xb, wu[...], preferred_element_type=F32)
            hdn = (g * _sigmoid(g) * u).astype(BF16)
            return jnp.dot(hdn, wd[...], preferred_element_type=F32)

        ffn = w_a * expert(wga, wua, wda) + w_b * expert(wgb, wub, wdb)
        ys_ref[...] = _layer_norm(ALPHA * x + ffn, l2g[...], l2b[...])


def _moe(xs, tile_a, tile_b, n_valid, wg, wu, wd, l2g, l2b):
    tm = MOE_TILE
    n_tiles = xs.shape[0] // tm
    rows = lambda n: pl.BlockSpec((tm, n), lambda i, ta, tb, nv: (i, 0))
    w_in = lambda which: pl.BlockSpec((None, D_MODEL, D_EXPERT), lambda i, ta, tb, nv: ((ta, tb)[which][i], 0, 0))
    w_out = lambda which: pl.BlockSpec((None, D_EXPERT, D_MODEL), lambda i, ta, tb, nv: ((ta, tb)[which][i], 0, 0))
    const = lambda shape: pl.BlockSpec(shape, lambda i, ta, tb, nv: (0,) * len(shape))
    grid_spec = pltpu.PrefetchScalarGridSpec(
        num_scalar_prefetch=3, grid=(n_tiles,),
        in_specs=[rows(xs.shape[1]), w_in(0), w_in(0), w_out(0), w_in(1), w_in(1), w_out(1),
                  const(l2g.shape), const(l2b.shape)],
        out_specs=rows(D_MODEL))
    return pl.pallas_call(
        _moe_kernel, grid_spec=grid_spec, out_shape=jax.ShapeDtypeStruct((xs.shape[0], D_MODEL), F32),
        compiler_params=_cparams(1), name="moe")(
            tile_a, tile_b, n_valid, xs, wg, wu, wd, wg, wu, wd, l2g, l2b)


def _route_tables(route, counts, n_tokens):
    tm = MOE_TILE
    n_tiles = n_tokens // tm + N_CLASSES
    cnt = counts[:N_CLASSES, 0].astype(jnp.int32)
    tiles = (cnt + tm - 1) // tm
    tile_end = jnp.cumsum(tiles)
    tile_start = tile_end - tiles
    t_idx = jnp.arange(n_tiles, dtype=jnp.int32)
    cls_of_tile = jnp.minimum(jnp.sum(t_idx[:, None] >= tile_end[None, :], axis=1), N_CLASSES - 1).astype(jnp.int32)
    classes = np.arange(N_CLASSES)
    tile_is = cls_of_tile[:, None] == classes[None, :]
    per_tile = lambda table: jnp.sum(jnp.where(tile_is, jnp.asarray(table, jnp.int32)[None, :], 0), axis=1)
    n_valid = jnp.clip(per_tile(cnt) - (t_idx - per_tile(tile_start)) * tm, 0, tm).astype(jnp.int32)
    first_expert = classes // PAIRS_PER_GROUP * EXPERTS_PER_GROUP
    tile_a = per_tile(first_expert + np.asarray(PAIR_A)[classes % PAIRS_PER_GROUP])
    tile_b = per_tile(first_expert + np.asarray(PAIR_B)[classes % PAIRS_PER_GROUP])
    cls = route[0].astype(jnp.int32)
    rank = route[1].astype(jnp.int32)
    row0 = jnp.sum(jnp.where(cls[:, None] == jnp.arange(N_CLASSES)[None, :], (tile_start * tm)[None, :], 0), axis=1)
    return tile_a, tile_b, n_valid, row0 + rank, n_tiles * tm


def _pick_tile(T, pref):
    t = pref
    while T % t:
        t //= 2
    return t


def kernel(x, w_in, conv_w, conv_b, kv_norm_g, w_uk, w_uv, rel_bias, b_i, b_f, mh_norm_g, w_up_a, w_up_m,
           w_out, ln1_g, ln1_b, w_grp, b_grp, w_rt, b_rt, w_gate, w_up, w_down, ln2_g, ln2_b):
    B, S, _ = x.shape
    T = B * S
    assert S % Q_TILE == 0 and S % M_CHUNK == 0 and T % MOE_TILE == 0 and w_in.shape[0] == DEPTH
    tz = _bias_tables(rel_bias)
    x2 = x.reshape(T, D_MODEL)
    for l in range(DEPTH):
        w = w_in[l]
        o = np.cumsum((WIDTH_A, KV_RANK, WIDTH_IDX, HEAD_DIM_IDX, N_HEADS_IDX, 2 * WIDTH_M, WIDTH_M,
                       N_HEADS_M, N_HEADS_M, WIDTH_M, D_MODEL, D_MODEL)).tolist()
        o = [0] + o
        seg = lambda j: w[:, o[j]:o[j + 1]]
        pad = LANES - (HEAD_DIM_IDX + N_HEADS_IDX + 2 * N_HEADS_M)
        w_small = jnp.concatenate([seg(3), seg(4), seg(7), seg(8), jnp.zeros((D_MODEL, pad), w.dtype)], axis=1)
        ws = [seg(0), seg(1), seg(2), w_small, seg(5), seg(6), seg(9), seg(10), seg(11)]
        ws = [a.astype(BF16) for a in ws]
        smb = jnp.zeros((1, LANES), F32).at[0, SM_I:SM_I + N_HEADS_M].set(b_i[l]) \
            .at[0, SM_F:SM_F + N_HEADS_M].set(b_f[l])
        qa, ckv, qi, sm, qk, v, og, ga, gm, ckv_t = _proj(x2, ws, kv_norm_g[l][None, :], smb,
                                                          _pick_tile(S, PROJ_TILE), S)

        wuk_t = jnp.transpose(w_uk[l], (1, 0, 2)).astype(BF16)
        wuv_t = jnp.transpose(w_uv[l], (1, 2, 0)).astype(BF16)
        oa = _dsa(qi, sm, qa, ckv, ckv_t, wuk_t, wuv_t, tz, B, S)

        hm = _mlstm(qk, v, sm, og, conv_w[l], conv_b[l][None, :], mh_norm_g[l].reshape(1, WIDTH_M), B, S)

        w_router = jnp.concatenate(
            [w_rt[l], w_grp[l], jnp.zeros((D_MODEL, LANES - N_EXPERTS - N_GROUPS), F32)], axis=1)
        b_router = jnp.concatenate(
            [b_rt[l], b_grp[l], jnp.zeros((LANES - N_EXPERTS - N_GROUPS,), F32)])[None, :]
        wr_hi = w_router.astype(BF16)
        wr_split = jnp.stack([wr_hi, (w_router - wr_hi.astype(F32)).astype(BF16)])
        x1, route, counts = _merge(x2, oa, hm, ga, gm, w_up_a[l].astype(BF16), w_up_m[l].astype(BF16),
                                   w_out[l].astype(BF16), ln1_g[l][None, :], ln1_b[l][None, :],
                                   wr_split, b_router, _pick_tile(T, MERGE_TILE))

        tile_a, tile_b, n_valid, pos, n_sorted = _route_tables(route, counts, T)
        xs = _sc_scatter_rows(x1, pos, n_sorted)
        ys = _moe(xs, tile_a, tile_b, n_valid, w_gate[l].astype(BF16), w_up[l].astype(BF16),
                  w_down[l].astype(BF16), ln2_g[l][None, :], ln2_b[l][None, :])
        x2 = _sc_gather_rows(ys, pos)
    return x2.reshape(B, S, D_MODEL)
```

```python
import functools
import math

import jax
import jax.numpy as jnp
import numpy as np
from jax import lax
from jax.experimental import pallas as pl
from jax.experimental.pallas import tpu as pltpu
from jax.experimental.pallas import tpu_sc as plsc

F32 = jnp.float32
BF16 = jnp.bfloat16

D_MODEL = 1024
N_HEADS_A = 8
HEAD_DIM_A = 64
WIDTH_A = N_HEADS_A * HEAD_DIM_A
KV_RANK = 256
N_HEADS_IDX = 8
HEAD_DIM_IDX = 64
WIDTH_IDX = N_HEADS_IDX * HEAD_DIM_IDX
TOPK_MAX = 256
N_BUCKETS = 32
MAX_DISTANCE = 128
N_HEADS_M = 4
HEAD_DIM_M = 128
WIDTH_M = N_HEADS_M * HEAD_DIM_M
CONV_WIDTH = 4
N_GROUPS = 4
EXPERTS_PER_GROUP = 4
N_EXPERTS = N_GROUPS * EXPERTS_PER_GROUP
D_EXPERT = 512
LN_EPS = 1e-5
DEPTH = 1
ALPHA = (2.0 * DEPTH) ** 0.25

LANES = 128
SUBLANES = 8
VMEM_LIMIT = 56 * 1024 * 1024

SM_KIDX = 0
SM_WIDX = HEAD_DIM_IDX
SM_I = SM_WIDX + N_HEADS_IDX
SM_F = SM_I + N_HEADS_M

Q_TILE = 256
K_CHUNK = Q_TILE
MXU_CHUNKS_PER_TRIP = 4
ONES_ROWS = 16
M_CHUNK = 128
M_GROUPS = 2
CONV_HALO = 16
assert CONV_HALO >= CONV_WIDTH - 1
PROJ_TILE = 1024
MERGE_TILE = 1024
MOE_TILE = 512
SC_CORES = 2
SC_SUBCORES = 16
SC_ROWS = 32
EPG_SHIFT = EXPERTS_PER_GROUP.bit_length() - 1
assert 1 << EPG_SHIFT == EXPERTS_PER_GROUP
PAIR_A, PAIR_B = zip(*[(a, b) for a in range(EXPERTS_PER_GROUP) for b in range(a + 1, EXPERTS_PER_GROUP)])
PAIRS_PER_GROUP = len(PAIR_A)
N_CLASSES = N_GROUPS * PAIRS_PER_GROUP
CLASS_ROWS = -(-N_CLASSES // SUBLANES) * SUBLANES
BISECT_STEPS_PER_CHECK = 3
BISECT_MAX_CHECKS = 5
PEEL_BRACKET = 2.0
NEG_INF = float("-inf")
LOG2E = math.log2(math.e)


def _cparams(n_grid):
    return pltpu.CompilerParams(dimension_semantics=("arbitrary",) * n_grid,
                                vmem_limit_bytes=VMEM_LIMIT)


def _full(shape):
    nd = len(shape)
    return pl.BlockSpec(shape, lambda *_: (0,) * nd, pipeline_mode=pl.Buffered(1))


def _sigmoid(x):
    return 0.5 * jnp.tanh(0.5 * x) + 0.5


def _split3(x):
    hi = x.astype(BF16)
    r = x - hi.astype(F32)
    mid = r.astype(BF16)
    return hi, mid, (r - mid.astype(F32)).astype(BF16)


def _proj_kernel(x_ref, wqa, wckv, wqi, wsm, wqk, wv, wo, wga, wgm, kvg, smb,
                 qa_o, ckv_o, qi_o, sm_o, qk_o, v_o, o_o, ga_o, gm_o, ckvt_o):
    xb = x_ref[...].astype(BF16)

    def mm(w):
        return jnp.dot(xb, w[...], preferred_element_type=F32)

    qa_o[...] = mm(wqa).astype(BF16)
    c = mm(wckv)
    c = c * lax.rsqrt(jnp.mean(c * c, axis=-1, keepdims=True) + LN_EPS) * kvg[...]
    ckv_o[...] = c.astype(BF16)
    ckvt_o[0:KV_RANK, :] = c.T.astype(BF16)
    ckvt_o[KV_RANK:KV_RANK + ONES_ROWS, :] = jnp.ones((ONES_ROWS, c.shape[0]), BF16)
    qi_o[...] = mm(wqi).astype(BF16)
    sm_o[...] = mm(wsm) + smb[...]
    qk_o[...] = mm(wqk).astype(BF16)
    v_o[...] = mm(wv).astype(BF16)
    o_o[...] = mm(wo).astype(BF16)
    ga_o[...] = mm(wga).astype(BF16)
    gm_o[...] = mm(wgm).astype(BF16)


def _proj(x2, ws, kvg, smb, tm, seq):
    T = x2.shape[0]
    per_seq = seq // tm
    widths = [w.shape[1] for w in ws]
    dts = [BF16, BF16, BF16, F32, BF16, BF16, BF16, BF16, BF16]
    in_specs = [pl.BlockSpec((tm, D_MODEL), lambda i: (i, 0))]
    in_specs += [_full(w.shape) for w in ws]
    in_specs += [_full(kvg.shape), _full(smb.shape)]
    out_specs = [pl.BlockSpec((tm, n), lambda i: (i, 0)) for n in widths]
    out_specs.append(pl.BlockSpec((None, KV_RANK + ONES_ROWS, tm), lambda i: (i // per_seq, 0, i % per_seq)))
    out_shape = [jax.ShapeDtypeStruct((T, n), dt) for n, dt in zip(widths, dts)]
    out_shape.append(jax.ShapeDtypeStruct((T // seq, KV_RANK + ONES_ROWS, seq), BF16))
    return pl.pallas_call(
        _proj_kernel, grid=(T // tm,), in_specs=in_specs, out_specs=out_specs, out_shape=out_shape,
        compiler_params=_cparams(1), name="proj")(x2, *ws, kvg, smb)


def _dsa_kernel(qi_ref, smq_ref, smk_ref, qa_ref, ckv_ref, ckvt_ref, wuk_ref, wuvt_ref, tz_ref,
                oa_ref, qs_ref, sc_ref, am_ref, ql_ref, x_ref, acc_ref, m_ref, *, topk):
    tq, kc = Q_TILE, K_CHUNK
    nh = N_HEADS_A
    qb = pl.program_id(1)
    n_ch = qb + 1
    seq_keys = n_ch * kc
    t0 = qb * tq

    def rows(c):
        return pl.ds(pl.multiple_of(c * kc, kc), kc)

    def lanes(h):
        return slice(h * tq, (h + 1) * tq)

    def chunk_loop(n, body, init, per_trip=2):
        def run(first, count, carry):
            for r in range(count):
                carry = body(first + r, carry)
            return carry

        shift = per_trip.bit_length() - 1
        trips = jnp.right_shift(n, shift)
        carry = lax.fori_loop(0, trips, lambda p, c: run(p * per_trip, per_trip, c), init)
        done = trips * per_trip
        count = per_trip // 2
        while count:
            start = done + jnp.bitwise_and(n - done, -2 * count)
            carry = lax.cond(jnp.bitwise_and(n, count) != 0,
                             functools.partial(run, start, count), lambda c: c, carry)
            count //= 2
        return carry

    for h in range(N_HEADS_IDX):
        qs_ref[h * tq:(h + 1) * tq, :] = qi_ref[:, h * HEAD_DIM_IDX:(h + 1) * HEAD_DIM_IDX]
    w_t = smq_ref[...].T
    q_pos = lax.broadcasted_iota(jnp.int32, (1, tq), 1) + t0
    key_iota = lax.broadcasted_iota(jnp.int32, (kc, tq), 0)

    def score_chunk(c, carry):
        mx, mn = carry
        kk = smk_ref[rows(c), SM_KIDX:SM_KIDX + HEAD_DIM_IDX].astype(BF16)
        dots = lax.dot_general(kk, qs_ref[...], (((1,), (1,)), ((), ())), preferred_element_type=F32)
        sc = jnp.zeros((kc, tq), F32)
        for h in range(N_HEADS_IDX):
            sc = sc + w_t[SM_WIDX + h:SM_WIDX + h + 1, :] * jnp.maximum(dots[:, lanes(h)], 0.0)
        vis = (key_iota + c * kc) <= q_pos
        sc_ref[rows(c), :] = jnp.where(vis, sc, NEG_INF)
        mx = jnp.maximum(mx, jnp.max(jnp.where(vis, sc, NEG_INF), axis=0, keepdims=True))
        mn = jnp.minimum(mn, jnp.min(jnp.where(vis, sc, jnp.inf), axis=0, keepdims=True))
        return mx, mn

    mx, mn = chunk_loop(n_ch, score_chunk,
                        (jnp.full((1, tq), NEG_INF, F32), jnp.full((1, tq), jnp.inf, F32)), MXU_CHUNKS_PER_TRIP)

    n_vis = (q_pos + 1).astype(F32)
    k_row = jnp.minimum(n_vis, float(topk))

    def count(pred):
        def body(c, a):
            hit = pred(sc_ref[rows(c), :]).astype(F32)
            return a + jnp.sum(hit.reshape(kc // SUBLANES, SUBLANES, tq), axis=0)
        a = chunk_loop(n_ch, body, jnp.zeros((SUBLANES, tq), F32))
        return jnp.sum(a, axis=0, keepdims=True)

    def any_lane(flag):
        return jnp.max(jnp.where(flag, 1.0, 0.0)) > 0.0

    def crowded(cnt_lo, c_hi):
        return any_lane(jnp.logical_and(cnt_lo != k_row, cnt_lo - c_hi > PEEL_BRACKET))

    def bisect_cond(carry):
        it, lo, hi, cnt_lo, c_hi = carry
        return jnp.logical_and(it < BISECT_MAX_CHECKS, crowded(cnt_lo, c_hi))

    def bisect_body(carry):
        it, lo, hi, cnt_lo, c_hi = carry
        for _ in range(BISECT_STEPS_PER_CHECK):
            mid = lo * 0.5 + hi * 0.5
            cnt = count(lambda s: s >= mid)
            ge = cnt >= k_row
            lo, cnt_lo = jnp.where(ge, mid, lo), jnp.where(ge, cnt, cnt_lo)
            hi, c_hi = jnp.where(ge, hi, mid), jnp.where(ge, c_hi, cnt)
        return it + 1, lo, hi, cnt_lo, c_hi

    hi0 = mx + jnp.maximum(jnp.abs(mx), 1e-30) * 1e-6
    _, lo, hi, cnt_lo, c_hi = lax.while_loop(
        bisect_cond, bisect_body, (jnp.int32(0), mn, hi0, n_vis, jnp.zeros((1, tq), F32)))

    def peel_cond(carry):
        it, lo, hi, cnt_lo, c_hi, done = carry
        return jnp.logical_and(it < seq_keys, any_lane(done == 0.0))

    def peel_body(carry):
        it, lo, hi, cnt_lo, c_hi, done = carry

        def top_body(c, v):
            s = sc_ref[rows(c), :]
            inside = jnp.logical_and(s >= lo, s < hi)
            return jnp.maximum(v, jnp.max(jnp.where(inside, s, NEG_INF), axis=0, keepdims=True))

        v = chunk_loop(n_ch, top_body, jnp.full((1, tq), NEG_INF, F32))
        c_v = count(lambda s: s >= v)
        reached = c_v >= k_row
        live = done == 0.0
        fin = jnp.logical_and(live, reached)
        cut = jnp.logical_and(live, jnp.logical_not(reached))
        return (it + 1, jnp.where(fin, v, lo), jnp.where(cut, v, hi), jnp.where(fin, c_v, cnt_lo),
                jnp.where(cut, c_v, c_hi), jnp.where(fin, 1.0, done))

    _, lo, hi, cnt_lo, c_hi, _ = lax.while_loop(
        peel_cond, peel_body,
        (jnp.int32(0), lo, hi, cnt_lo, c_hi, jnp.where(cnt_lo == k_row, 1.0, 0.0)))

    def mask_chunk(c, _):
        am_ref[rows(c), :] = jnp.where(sc_ref[rows(c), :] >= lo, 0.0, NEG_INF)
        return 0

    chunk_loop(n_ch, mask_chunk, 0)

    tied = cnt_lo != k_row
    for part in range(tq // LANES):
        ls = slice(part * LANES, (part + 1) * LANES)

        @pl.when(any_lane(tied[:, ls]))
        def _ties():
            need, lo_p, hi_p = (k_row - c_hi)[:, ls], lo[:, ls], hi[:, ls]
            lower = (lax.broadcasted_iota(jnp.int32, (kc, kc), 1)
                     < lax.broadcasted_iota(jnp.int32, (kc, kc), 0)).astype(BF16)

            def tie_chunk(c, before):
                s = sc_ref[rows(c), ls]
                above = s >= hi_p
                tie = jnp.logical_and(s >= lo_p, jnp.logical_not(above))
                tie_f = tie.astype(F32)
                rank = jnp.dot(lower, tie_f.astype(BF16), preferred_element_type=F32) + before
                sel = jnp.logical_or(above, jnp.logical_and(tie, rank < need))
                am_ref[rows(c), ls] = jnp.where(sel, 0.0, NEG_INF)
                return before + jnp.sum(tie_f, axis=0, keepdims=True)

            chunk_loop(n_ch, tie_chunk, jnp.zeros((1, LANES), F32))

    for h in range(nh):
        qh = qa_ref[:, h * HEAD_DIM_A:(h + 1) * HEAD_DIM_A]
        qlt = lax.dot_general(wuk_ref[h], qh, (((1,), (1,)), ((), ())), preferred_element_type=F32)
        ql_ref[:, lanes(h)] = (qlt * (HEAD_DIM_A ** -0.5 * LOG2E)).astype(BF16)

    m_ref[...] = jnp.full(m_ref.shape, NEG_INF, F32)

    def pass_a(c, table):
        raw = jnp.dot(ckv_ref[rows(c), :], ql_ref[...], preferred_element_type=F32)
        am = am_ref[rows(c), :]
        for h in range(nh):
            x = raw[:, lanes(h)] + am
            if table is not None:
                x = x + tz_ref[table, h]
            x_ref[rows(c), lanes(h)] = x
            m_ref[:, lanes(h)] = jnp.maximum(m_ref[:, lanes(h)], jnp.max(x, axis=0, keepdims=True))

    def far_chunk(c, _):
        pass_a(c, None)
        return 0

    chunk_loop(jnp.maximum(qb - 1, 0), far_chunk, 0, MXU_CHUNKS_PER_TRIP)

    @pl.when(qb >= 1)
    def _prev():
        pass_a(qb - 1, 1)

    pass_a(qb, 0)

    def pv(c):
        p = jnp.exp2((x_ref[rows(c), :] - m_ref[...]).astype(BF16))
        return jnp.dot(ckvt_ref[:, rows(c)], p, preferred_element_type=F32)

    def pass_b(c, _):
        acc_ref[...] += pv(c)
        return 0

    acc_ref[...] = pv(qb)
    chunk_loop(qb, pass_b, 0, MXU_CHUNKS_PER_TRIP)
    inv_l = 1.0 / acc_ref[KV_RANK:KV_RANK + 1, :]
    o_lat = (acc_ref[0:KV_RANK, :] * inv_l).astype(BF16)
    outs = [jnp.dot(wuvt_ref[h], o_lat[:, lanes(h)], preferred_element_type=F32) for h in range(nh)]
    oa_ref[...] = jnp.concatenate(outs, axis=0).T.astype(BF16)


def _dsa(qi, sm, qa, ckv, ckv_t, wuk_t, wuv_t, tz, batch, seq):
    tq = Q_TILE
    nq = seq // tq
    topk = min(TOPK_MAX, seq // 4)
    T = batch * seq
    kern = functools.partial(_dsa_kernel, topk=topk)
    blk_q = lambda n: pl.BlockSpec((tq, n), lambda b, q: (b * nq + q, 0))
    blk_s = lambda n: pl.BlockSpec((seq, n), lambda b, q: (b, 0))
    in_specs = [
        blk_q(WIDTH_IDX), blk_q(LANES), blk_s(LANES), blk_q(WIDTH_A), blk_s(KV_RANK),
        pl.BlockSpec((None, KV_RANK + ONES_ROWS, seq), lambda b, q: (b, 0, 0)),
        _full(wuk_t.shape), _full(wuv_t.shape), _full(tz.shape),
    ]
    scratch = [
        pltpu.VMEM((N_HEADS_IDX * tq, HEAD_DIM_IDX), BF16),
        pltpu.VMEM((seq, tq), F32),
        pltpu.VMEM((seq, tq), F32),
        pltpu.VMEM((KV_RANK, N_HEADS_A * tq), BF16),
        pltpu.VMEM((seq, N_HEADS_A * tq), F32),
        pltpu.VMEM((KV_RANK + ONES_ROWS, N_HEADS_A * tq), F32),
        pltpu.VMEM((1, N_HEADS_A * tq), F32),
    ]
    return pl.pallas_call(
        kern, grid=(batch, nq), in_specs=in_specs,
        out_specs=pl.BlockSpec((tq, WIDTH_A), lambda b, q: (b * nq + q, 0)),
        out_shape=jax.ShapeDtypeStruct((T, WIDTH_A), BF16),
        scratch_shapes=scratch, compiler_params=_cparams(2), name="dsa")(
            qi, sm, sm, qa, ckv, ckv_t, wuk_t, wuv_t, tz)


def _t5_bucket(dist):
    max_exact = N_BUCKETS // 2
    d = jnp.maximum(dist, 0)
    ratio = jnp.log(jnp.maximum(d, 1).astype(F32) / max_exact) / math.log(MAX_DISTANCE / max_exact)
    large = jnp.minimum(max_exact + (ratio * (N_BUCKETS - max_exact)).astype(jnp.int32), N_BUCKETS - 1)
    return jnp.where(d < max_exact, d, large)


def _bias_tables(rel_bias):
    tq = Q_TILE
    span = 2 * tq
    assert int(_np_bucket(tq)) == N_BUCKETS - 1
    far = rel_bias[N_BUCKETS - 1]
    by_dist = ((rel_bias[_t5_bucket(jnp.arange(span))] - far).astype(F32) * LOG2E).T
    diag = jnp.concatenate([by_dist[:, :tq], jnp.zeros_like(by_dist[:, :tq])], axis=1)
    prev = jnp.concatenate([by_dist[:, tq:], by_dist[:, :tq]], axis=1)

    def toeplitz(f):
        m = jnp.tile(f, (1, tq))[:, :tq * (span - 1)].reshape(N_HEADS_A, tq, span - 1)
        return m[:, :, :tq]

    return jnp.stack([toeplitz(diag), toeplitz(prev)])


def _np_bucket(d):
    max_exact = N_BUCKETS // 2
    ratio = np.log(np.float32(max(d, 1)) / np.float32(max_exact)) / math.log(MAX_DISTANCE / max_exact)
    return min(max_exact + int(ratio * (N_BUCKETS - max_exact)), N_BUCKETS - 1) if d >= max_exact else d


def _mlstm_kernel(qk_ref, halo_ref, v_ref, sm_ref, o_ref, cw_ref, cb_ref, g_ref,
                  out_ref, c_ref, m_ref):
    c_idx = pl.program_id(1)

    @pl.when(c_idx == 0)
    def _init():
        c_ref[...] = jnp.zeros_like(c_ref)
        m_ref[...] = jnp.zeros_like(m_ref)

    for g in range(M_GROUPS):
        gates = _mlstm_gates(g, c_idx, qk_ref, halo_ref, sm_ref, cw_ref, cb_ref)
        outs = [_mlstm_head(g, h, gates, v_ref, o_ref, g_ref, c_ref, m_ref) for h in range(N_HEADS_M)]
        out_ref[g] = jnp.concatenate(outs, axis=-1).astype(BF16)


def _mlstm_gates(g, c_idx, qk_ref, halo_ref, sm_ref, cw_ref, cb_ref):
    L = M_CHUNK

    hw = halo_ref.shape[1]
    halo = jnp.where(c_idx > 0, halo_ref[g].astype(F32), 0.0)
    ext = jnp.concatenate([halo, qk_ref[g].astype(F32)], axis=0)
    acc = jnp.zeros((L, 2 * WIDTH_M), F32) + cb_ref[...]
    for w in range(CONV_WIDTH):
        off = hw - (CONV_WIDTH - 1) + w
        acc = acc + ext[off:off + L, :] * cw_ref[w:w + 1, :]
    qk = acc * _sigmoid(acc)

    sm = sm_ref[g]
    sm_t = sm.T
    r_i = lax.broadcasted_iota(jnp.int32, (L, L), 0)
    c_i = lax.broadcasted_iota(jnp.int32, (L, L), 1)
    tril = (c_i <= r_i).astype(BF16)
    bcum_c = sum(jnp.dot(tril, part, preferred_element_type=F32)
                 for part in _split3(jax.nn.log_sigmoid(sm)))
    return qk, sm, sm_t, bcum_c, bcum_c.T, c_i <= r_i


def _mlstm_head(g, h, gates, v_ref, o_ref, g_ref, c_ref, m_ref):
    L = M_CHUNK
    dm = HEAD_DIM_M
    qk, sm, sm_t, bcum_c, bcum_r, causal = gates
    ones_col = (lax.broadcasted_iota(jnp.int32, (L, dm), 1) == 0).astype(BF16)
    st = g * N_HEADS_M + h
    q = qk[:, h * dm:(h + 1) * dm].astype(BF16)
    k = (qk[:, WIDTH_M + h * dm:WIDTH_M + (h + 1) * dm] * (dm ** -0.5))
    v_aug = jnp.concatenate([v_ref[g, :, h * dm:(h + 1) * dm], ones_col], axis=-1)
    b_col = bcum_c[:, SM_F + h:SM_F + h + 1]
    g_col = sm[:, SM_I + h:SM_I + h + 1] - b_col
    g_row = sm_t[SM_I + h:SM_I + h + 1, :] - bcum_r[SM_F + h:SM_F + h + 1, :]
    b_last = b_col[L - 1:L, :]
    m_prev = m_ref[st]
    c_prev = c_ref[st]

    log_d = jnp.where(causal, b_col + g_row, NEG_INF)
    m_j = jnp.maximum(b_col + m_prev, jnp.max(log_d, axis=-1, keepdims=True))
    w_inter = jnp.exp(b_col + m_prev - m_j)
    qkt = lax.dot_general(q, k.astype(BF16), (((1,), (1,)), ((), ())), preferred_element_type=F32)
    s = qkt * jnp.exp(log_d - m_j)
    o_aug = jnp.dot(s.astype(BF16), v_aug, preferred_element_type=F32) + \
        w_inter * jnp.dot(q, c_prev.astype(BF16), preferred_element_type=F32)
    num = o_aug[:, :dm]
    den = o_aug[:, dm:dm + 1]
    hh = num / jnp.maximum(jnp.abs(den), jnp.exp(-m_j))

    lwe = b_last + g_col
    m_loc = jnp.max(lwe, axis=0, keepdims=True)
    kw = (k * jnp.exp(lwe - m_loc)).astype(BF16)
    c_loc = lax.dot_general(kw, v_aug, (((0,), (0,)), ((), ())), preferred_element_type=F32)
    m_new = jnp.maximum(b_last + m_prev, m_loc)
    c_ref[st] = jnp.exp(b_last + m_prev - m_new) * c_prev + jnp.exp(m_loc - m_new) * c_loc
    m_ref[st] = m_new

    hn = hh * lax.rsqrt(jnp.mean(hh * hh, axis=-1, keepdims=True) + LN_EPS) * g_ref[:, h * dm:(h + 1) * dm]
    og = o_ref[g, :, h * dm:(h + 1) * dm].astype(F32)
    return _sigmoid(og) * hn


def _mlstm(qk, v, sm, o, conv_w, conv_b, mh_g, batch, seq):
    L = M_CHUNK
    G = M_GROUPS
    nc = seq // L
    T = batch * seq
    hb = L // CONV_HALO
    grouped = lambda a: a.reshape(G, T // G, a.shape[-1])
    blk = lambda n: pl.BlockSpec((G, L, n), lambda b, c: (0, b * nc + c, 0))
    in_specs = [
        blk(2 * WIDTH_M),
        pl.BlockSpec((G, CONV_HALO, 2 * WIDTH_M), lambda b, c: (0, jnp.maximum((b * nc + c) * hb - 1, 0), 0)),
        blk(WIDTH_M), blk(LANES), blk(WIDTH_M),
        _full(conv_w.shape), _full(conv_b.shape), _full(mh_g.shape),
    ]
    scratch = [pltpu.VMEM((G * N_HEADS_M, HEAD_DIM_M, 2 * HEAD_DIM_M), F32),
               pltpu.VMEM((G * N_HEADS_M, 1, 1), F32)]
    qk_g = grouped(qk)
    out = pl.pallas_call(
        _mlstm_kernel, grid=(batch // G, nc), in_specs=in_specs, out_specs=blk(WIDTH_M),
        out_shape=jax.ShapeDtypeStruct((G, T // G, WIDTH_M), BF16), scratch_shapes=scratch,
        compiler_params=_cparams(2), name="mlstm")(
            qk_g, qk_g, grouped(v), grouped(sm), grouped(o), conv_w, conv_b, mh_g)
    return out.reshape(T, WIDTH_M)


def _layer_norm(y, g, b):
    mu = jnp.mean(y, axis=-1, keepdims=True)
    var = jnp.mean(jnp.square(y - mu), axis=-1, keepdims=True)
    return (y - mu) * lax.rsqrt(var + LN_EPS) * g + b


def _router_logits(x1, wr, br):
    x1_hi = x1.astype(BF16)
    x1_lo = (x1 - x1_hi.astype(F32)).astype(BF16)
    return (jnp.dot(x1_hi, wr[0], preferred_element_type=F32)
            + jnp.dot(x1_lo, wr[0], preferred_element_type=F32)
            + jnp.dot(x1_hi, wr[1], preferred_element_type=F32)) + br[...]


def _merge_kernel(x_ref, oa_ref, hm_ref, ga_ref, gm_ref, wua, wum, wout, l1g, l1b, wr, br,
                  x1_ref, route_ref, cnt_ref):
    pa = jnp.dot(oa_ref[...], wua[...], preferred_element_type=F32)
    pm = jnp.dot(hm_ref[...], wum[...], preferred_element_type=F32)
    y = _sigmoid(ga_ref[...]) * pa.astype(BF16) + _sigmoid(gm_ref[...]) * pm.astype(BF16)
    mix = jnp.dot(y.astype(BF16), wout[...], preferred_element_type=F32)
    x1 = _layer_norm(ALPHA * x_ref[...] + mix, l1g[...], l1b[...])
    x1_ref[:, 0:D_MODEL] = x1

    lt = _router_logits(x1, wr, br).T
    tm = lt.shape[1]
    big = jnp.int32(LANES)
    le = lt[0:N_EXPERTS, :]
    lg = lt[N_EXPERTS:N_EXPERTS + SUBLANES, :]
    row_g = lax.broadcasted_iota(jnp.int32, lg.shape, 0)
    row_e = lax.broadcasted_iota(jnp.int32, le.shape, 0)
    is_grp = row_g < N_GROUPS
    gl = jnp.where(is_grp, lg, NEG_INF)
    ge = jnp.exp(gl - jnp.max(gl, axis=0, keepdims=True))
    gp = ge / jnp.sum(ge, axis=0, keepdims=True)
    g_w = jnp.max(gp, axis=0, keepdims=True)
    g_idx = jnp.min(jnp.where(jnp.logical_and(is_grp, gp == g_w), row_g, big), axis=0, keepdims=True)
    el = jnp.where(jnp.right_shift(row_e, EPG_SHIFT) == g_idx, le, NEG_INF)
    m1 = jnp.max(el, axis=0, keepdims=True)
    i1 = jnp.min(jnp.where(el == m1, row_e, big), axis=0, keepdims=True)
    el2 = jnp.where(row_e == i1, NEG_INF, el)
    m2 = jnp.max(el2, axis=0, keepdims=True)
    i2 = jnp.min(jnp.where(el2 == m2, row_e, big), axis=0, keepdims=True)

    e2 = jnp.exp(m2 - m1)
    w1 = g_w / (1.0 + e2)
    w2 = g_w * e2 / (1.0 + e2)
    first_is_a = i1 < i2
    row_w = lax.broadcasted_iota(jnp.int32, (LANES, tm), 0)
    w_rows = jnp.where(row_w == 0, jnp.where(first_is_a, w1, w2),
                       jnp.where(row_w == 1, jnp.where(first_is_a, w2, w1), 0.0))
    x1_ref[:, D_MODEL:D_MODEL + LANES] = w_rows.T

    a = jnp.bitwise_and(jnp.minimum(i1, i2), EXPERTS_PER_GROUP - 1)
    b = jnp.bitwise_and(jnp.maximum(i1, i2), EXPERTS_PER_GROUP - 1)
    pair = jnp.right_shift(a * (2 * EXPERTS_PER_GROUP - 1 - a), 1) + (b - a - 1)
    cls = g_idx * PAIRS_PER_GROUP + pair
    row_c = lax.broadcasted_iota(jnp.int32, (CLASS_ROWS, tm), 0)
    onehot = (row_c == cls).astype(F32)

    @pl.when(pl.program_id(0) == 0)
    def _init():
        cnt_ref[...] = jnp.zeros_like(cnt_ref)

    earlier = (lax.broadcasted_iota(jnp.int32, (tm, tm), 0)
               < lax.broadcasted_iota(jnp.int32, (tm, tm), 1)).astype(BF16)
    prior = jnp.dot(onehot.astype(BF16), earlier, preferred_element_type=F32) + cnt_ref[:, 0:1]
    rank = jnp.sum(prior * onehot, axis=0, keepdims=True)
    cnt_ref[...] += jnp.sum(onehot, axis=1, keepdims=True)
    row_o = lax.broadcasted_iota(jnp.int32, route_ref.shape, 0)
    route_ref[...] = jnp.where(row_o == 0, cls.astype(F32), jnp.where(row_o == 1, rank, 0.0))


def _merge(x2, oa, hm, ga, gm, wua, wum, wout, l1g, l1b, wr, br, tm):
    T = x2.shape[0]
    blk = lambda n: pl.BlockSpec((tm, n), lambda i: (i, 0))
    in_specs = [blk(D_MODEL), blk(WIDTH_A), blk(WIDTH_M), blk(D_MODEL), blk(D_MODEL),
                _full(wua.shape), _full(wum.shape), _full(wout.shape), _full(l1g.shape), _full(l1b.shape),
                _full(wr.shape), _full(br.shape)]
    out_specs = [blk(D_MODEL + LANES), pl.BlockSpec((SUBLANES, tm), lambda i: (0, i)),
                 pl.BlockSpec((CLASS_ROWS, LANES), lambda i: (0, 0))]
    out_shape = [jax.ShapeDtypeStruct((T, D_MODEL + LANES), F32), jax.ShapeDtypeStruct((SUBLANES, T), F32),
                 jax.ShapeDtypeStruct((CLASS_ROWS, LANES), F32)]
    return pl.pallas_call(
        _merge_kernel, grid=(T // tm,), in_specs=in_specs, out_specs=out_specs, out_shape=out_shape,
        compiler_params=_cparams(1), name="merge")(x2, oa, hm, ga, gm, wua, wum, wout, l1g, l1b, wr, br)


def _sc_mesh():
    return plsc.VectorSubcoreMesh(core_axis_name="c", subcore_axis_name="s",
                                  num_cores=SC_CORES, num_subcores=SC_SUBCORES)


def _sc_chunks(n_rows):
    workers = SC_CORES * SC_SUBCORES
    assert n_rows % (workers * SC_ROWS * 2) == 0
    return n_rows // (workers * SC_ROWS)


def _sc_scratch(n_chunks, width, dtype):
    return [pltpu.VMEM((n_chunks, SC_ROWS), jnp.int32),
            pltpu.VMEM((SC_ROWS, width), dtype), pltpu.VMEM((SC_ROWS, width), dtype),
            pltpu.SemaphoreType.DMA, pltpu.SemaphoreType.DMA]


def _sc_scatter_rows(rows, idx, n_out):
    n_in, width = rows.shape
    n_chunks = _sc_chunks(n_in)

    @functools.partial(
        pl.kernel, mesh=_sc_mesh(), out_type=jax.ShapeDtypeStruct((n_out, width), rows.dtype),
        scratch_types=_sc_scratch(n_chunks, width, rows.dtype), name="sc_dispatch")
    def scatter(rows_hbm, idx_hbm, out_hbm, idx_v, rows_a, rows_b, sem_a, sem_b):
        first = (lax.axis_index("s") * SC_CORES + lax.axis_index("c")) * n_chunks
        pltpu.sync_copy(idx_hbm.at[pl.ds(first, n_chunks)], idx_v)

        def load(c, buf):
            pltpu.sync_copy(rows_hbm.at[pl.ds((first + c) * SC_ROWS, SC_ROWS)], buf)

        def put(c, buf, sem):
            return pltpu.make_async_copy(buf, out_hbm.at[idx_v.at[c]], sem)

        load(0, rows_a)
        put(0, rows_a, sem_a).start()

        @pl.loop(0, n_chunks, step=2)
        def _(j):
            load(j + 1, rows_b)
            put(j + 1, rows_b, sem_b).start()
            put(j, rows_a, sem_a).wait()

            @pl.when(j + 2 < n_chunks)
            def _():
                load(j + 2, rows_a)
                put(j + 2, rows_a, sem_a).start()

            put(j + 1, rows_b, sem_b).wait()

    return scatter(rows, idx.reshape(n_in // SC_ROWS, SC_ROWS))


def _sc_gather_rows(table, idx):
    n_out, width = idx.shape[0], table.shape[1]
    n_chunks = _sc_chunks(n_out)

    @functools.partial(
        pl.kernel, mesh=_sc_mesh(), out_type=jax.ShapeDtypeStruct((n_out, width), table.dtype),
        scratch_types=_sc_scratch(n_chunks, width, table.dtype), name="sc_combine")
    def gather(table_hbm, idx_hbm, out_hbm, idx_v, rows_a, rows_b, sem_a, sem_b):
        first = (lax.axis_index("s") * SC_CORES + lax.axis_index("c")) * n_chunks
        pltpu.sync_copy(idx_hbm.at[pl.ds(first, n_chunks)], idx_v)

        def fetch(c, buf, sem):
            return pltpu.make_async_copy(table_hbm.at[idx_v.at[c]], buf, sem)

        def store(c, buf):
            pltpu.sync_copy(buf, out_hbm.at[pl.ds((first + c) * SC_ROWS, SC_ROWS)])

        fetch(0, rows_a, sem_a).start()

        @pl.loop(0, n_chunks, step=2)
        def _(j):
            fetch(j + 1, rows_b, sem_b).start()
            fetch(j, rows_a, sem_a).wait()
            store(j, rows_a)

            @pl.when(j + 2 < n_chunks)
            def _():
                fetch(j + 2, rows_a, sem_a).start()

            fetch(j + 1, rows_b, sem_b).wait()
            store(j + 1, rows_b)

    return gather(table, idx.reshape(n_out // SC_ROWS, SC_ROWS))


def _moe_kernel(ta_ref, tb_ref, nv_ref, xs_ref, wga, wua, wda, wgb, wub, wdb, l2g, l2b, ys_ref):
    i = pl.program_id(0)

    @pl.when(nv_ref[i] > 0)
    def _compute():
        x = xs_ref[:, 0:D_MODEL]
        xb = x.astype(BF16)
        w_a = xs_ref[:, D_MODEL:D_MODEL + 1]
        w_b = xs_ref[:, D_MODEL + 1:D_MODEL + 2]

        def expert(wg, wu, wd):
            g = jnp.dot(xb, wg[...], preferred_element_type=F32)
            u = jnp.dot(xb, wu[...], preferred_element_type=F32)
            hdn = (g * _sigmoid(g) * u).astype(BF16)
            return jnp.dot(hdn, wd[...], preferred_element_type=F32)

        ffn = w_a * expert(wga, wua, wda) + w_b * expert(wgb, wub, wdb)
        ys_ref[...] = _layer_norm(ALPHA * x + ffn, l2g[...], l2b[...])


def _moe(xs, tile_a, tile_b, n_valid, wg, wu, wd, l2g, l2b):
    tm = MOE_TILE
    n_tiles = xs.shape[0] // tm
    rows = lambda n: pl.BlockSpec((tm, n), lambda i, ta, tb, nv: (i, 0))
    w_in = lambda which: pl.BlockSpec((None, D_MODEL, D_EXPERT), lambda i, ta, tb, nv: ((ta, tb)[which][i], 0, 0))
    w_out = lambda which: pl.BlockSpec((None, D_EXPERT, D_MODEL), lambda i, ta, tb, nv: ((ta, tb)[which][i], 0, 0))
    const = lambda shape: pl.BlockSpec(shape, lambda i, ta, tb, nv: (0,) * len(shape))
    grid_spec = pltpu.PrefetchScalarGridSpec(
        num_scalar_prefetch=3, grid=(n_tiles,),
        in_specs=[rows(xs.shape[1]), w_in(0), w_in(0), w_out(0), w_in(1), w_in(1), w_out(1),
                  const(l2g.shape), const(l2b.shape)],
        out_specs=rows(D_MODEL))
    return pl.pallas_call(
        _moe_kernel, grid_spec=grid_spec, out_shape=jax.ShapeDtypeStruct((xs.shape[0], D_MODEL), F32),
        compiler_params=_cparams(1), name="moe")(
            tile_a, tile_b, n_valid, xs, wg, wu, wd, wg, wu, wd, l2g, l2b)


def _route_tables(route, counts, n_tokens):
    tm = MOE_TILE
    n_tiles = n_tokens // tm + N_CLASSES
    cnt = counts[:N_CLASSES, 0].astype(jnp.int32)
    tiles = (cnt + tm - 1) // tm
    tile_end = jnp.cumsum(tiles)
    tile_start = tile_end - tiles
    t_idx = jnp.arange(n_tiles, dtype=jnp.int32)
    cls_of_tile = jnp.minimum(jnp.sum(t_idx[:, None] >= tile_end[None, :], axis=1), N_CLASSES - 1).astype(jnp.int32)
    classes = np.arange(N_CLASSES)
    tile_is = cls_of_tile[:, None] == classes[None, :]
    per_tile = lambda table: jnp.sum(jnp.where(tile_is, jnp.asarray(table, jnp.int32)[None, :], 0), axis=1)
    n_valid = jnp.clip(per_tile(cnt) - (t_idx - per_tile(tile_start)) * tm, 0, tm).astype(jnp.int32)
    first_expert = classes // PAIRS_PER_GROUP * EXPERTS_PER_GROUP
    tile_a = per_tile(first_expert + np.asarray(PAIR_A)[classes % PAIRS_PER_GROUP])
    tile_b = per_tile(first_expert + np.asarray(PAIR_B)[classes % PAIRS_PER_GROUP])
    cls = route[0].astype(jnp.int32)
    rank = route[1].astype(jnp.int32)
    row0 = jnp.sum(jnp.where(cls[:, None] == jnp.arange(N_CLASSES)[None, :], (tile_start * tm)[None, :], 0), axis=1)
    return tile_a, tile_b, n_valid, row0 + rank, n_tiles * tm


def _pick_tile(T, pref):
    t = pref
    while T % t:
        t //= 2
    return t


def kernel(x, w_in, conv_w, conv_b, kv_norm_g, w_uk, w_uv, rel_bias, b_i, b_f, mh_norm_g, w_up_a, w_up_m,
           w_out, ln1_g, ln1_b, w_grp, b_grp, w_rt, b_rt, w_gate, w_up, w_down, ln2_g, ln2_b):
    B, S, _ = x.shape
    T = B * S
    assert S % Q_TILE == 0 and S % M_CHUNK == 0 and T % MOE_TILE == 0 and w_in.shape[0] == DEPTH
    tz = _bias_tables(rel_bias)
    x2 = x.reshape(T, D_MODEL)
    for l in range(DEPTH):
        w = w_in[l]
        o = np.cumsum((WIDTH_A, KV_RANK, WIDTH_IDX, HEAD_DIM_IDX, N_HEADS_IDX, 2 * WIDTH_M, WIDTH_M,
                       N_HEADS_M, N_HEADS_M, WIDTH_M, D_MODEL, D_MODEL)).tolist()
        o = [0] + o
        seg = lambda j: w[:, o[j]:o[j + 1]]
        pad = LANES - (HEAD_DIM_IDX + N_HEADS_IDX + 2 * N_HEADS_M)
        w_small = jnp.concatenate([seg(3), seg(4), seg(7), seg(8), jnp.zeros((D_MODEL, pad), w.dtype)], axis=1)
        ws = [seg(0), seg(1), seg(2), w_small, seg(5), seg(6), seg(9), seg(10), seg(11)]
        ws = [a.astype(BF16) for a in ws]
        smb = jnp.zeros((1, LANES), F32).at[0, SM_I:SM_I + N_HEADS_M].set(b_i[l]) \
            .at[0, SM_F:SM_F + N_HEADS_M].set(b_f[l])
        qa, ckv, qi, sm, qk, v, og, ga, gm, ckv_t = _proj(x2, ws, kv_norm_g[l][None, :], smb,
                                                          _pick_tile(S, PROJ_TILE), S)

        wuk_t = jnp.transpose(w_uk[l], (1, 0, 2)).astype(BF16)
        wuv_t = jnp.transpose(w_uv[l], (1, 2, 0)).astype(BF16)
        oa = _dsa(qi, sm, qa, ckv, ckv_t, wuk_t, wuv_t, tz, B, S)

        hm = _mlstm(qk, v, sm, og, conv_w[l], conv_b[l][None, :], mh_norm_g[l].reshape(1, WIDTH_M), B, S)

        w_router = jnp.concatenate(
            [w_rt[l], w_grp[l], jnp.zeros((D_MODEL, LANES - N_EXPERTS - N_GROUPS), F32)], axis=1)
        b_router = jnp.concatenate(
            [b_rt[l], b_grp[l], jnp.zeros((LANES - N_EXPERTS - N_GROUPS,), F32)])[None, :]
        wr_hi = w_router.astype(BF16)
        wr_split = jnp.stack([wr_hi, (w_router - wr_hi.astype(F32)).astype(BF16)])
        x1, route, counts = _merge(x2, oa, hm, ga, gm, w_up_a[l].astype(BF16), w_up_m[l].astype(BF16),
                                   w_out[l].astype(BF16), ln1_g[l][None, :], ln1_b[l][None, :],
                                   wr_split, b_router, _pick_tile(T, MERGE_TILE))

        tile_a, tile_b, n_valid, pos, n_sorted = _route_tables(route, counts, T)
        xs = _sc_scatter_rows(x1, pos, n_sorted)
        ys = _moe(xs, tile_a, tile_b, n_valid, w_gate[l].astype(BF16), w_up[l].astype(BF16),
                  w_down[l].astype(BF16), ln2_g[l][None, :], ln2_b[l][None, :])
        x2 = _sc_gather_rows(ys, pos)
    return x2.reshape(B, S, D_MODEL)
```

```python
import functools
import math

import jax
import jax.numpy as jnp
import numpy as np
from jax import lax
from jax.experimental import pallas as pl
from jax.experimental.pallas import tpu as pltpu
from jax.experimental.pallas import tpu_sc as plsc

F32 = jnp.float32
BF16 = jnp.bfloat16

D_MODEL = 1024
N_HEADS_A = 8
HEAD_DIM_A = 64
WIDTH_A = N_HEADS_A * HEAD_DIM_A
KV_RANK = 256
N_HEADS_IDX = 8
HEAD_DIM_IDX = 64
WIDTH_IDX = N_HEADS_IDX * HEAD_DIM_IDX
TOPK_MAX = 256
N_BUCKETS = 32
MAX_DISTANCE = 128
N_HEADS_M = 4
HEAD_DIM_M = 128
WIDTH_M = N_HEADS_M * HEAD_DIM_M
CONV_WIDTH = 4
N_GROUPS = 4
EXPERTS_PER_GROUP = 4
N_EXPERTS = N_GROUPS * EXPERTS_PER_GROUP
D_EXPERT = 512
LN_EPS = 1e-5
DEPTH = 1
ALPHA = (2.0 * DEPTH) ** 0.25

LANES = 128
SUBLANES = 8
VMEM_LIMIT = 56 * 1024 * 1024

SM_KIDX = 0
SM_WIDX = HEAD_DIM_IDX
SM_I = SM_WIDX + N_HEADS_IDX
SM_F = SM_I + N_HEADS_M

Q_TILE = 256
K_CHUNK = Q_TILE
MXU_CHUNKS_PER_TRIP = 4
ONES_ROWS = 16
M_CHUNK = 128
M_GROUPS = 2
CONV_HALO = 16
assert CONV_HALO >= CONV_WIDTH - 1
PROJ_TILE = 1024
MERGE_TILE = 1024
RANK_BLOCK = 256
MOE_TILE = 512
SC_CORES = 2
SC_SUBCORES = 16
SC_ROWS = 32
EPG_SHIFT = EXPERTS_PER_GROUP.bit_length() - 1
assert 1 << EPG_SHIFT == EXPERTS_PER_GROUP
PAIR_A, PAIR_B = zip(*[(a, b) for a in range(EXPERTS_PER_GROUP) for b in range(a + 1, EXPERTS_PER_GROUP)])
PAIRS_PER_GROUP = len(PAIR_A)
N_CLASSES = N_GROUPS * PAIRS_PER_GROUP
CLASS_ROWS = -(-N_CLASSES // SUBLANES) * SUBLANES
BISECT_STEPS_PER_CHECK = 3
BISECT_MAX_CHECKS = 5
PEEL_BRACKET = 2.0
NEG_INF = float("-inf")
LOG2E = math.log2(math.e)


def _cparams(n_grid):
    return pltpu.CompilerParams(dimension_semantics=("arbitrary",) * n_grid,
                                vmem_limit_bytes=VMEM_LIMIT)


def _full(shape):
    nd = len(shape)
    return pl.BlockSpec(shape, lambda *_: (0,) * nd, pipeline_mode=pl.Buffered(1))


def _sigmoid(x):
    return 0.5 * jnp.tanh(0.5 * x) + 0.5


def _split3(x):
    hi = x.astype(BF16)
    r = x - hi.astype(F32)
    mid = r.astype(BF16)
    return hi, mid, (r - mid.astype(F32)).astype(BF16)


def _proj_kernel(x_ref, wqa, wckv, wqi, wsm, wqk, wv, wo, wga, wgm, kvg, smb,
                 qa_o, ckv_o, qi_o, sm_o, qk_o, v_o, o_o, ga_o, gm_o, ckvt_o):
    xb = x_ref[...].astype(BF16)

    def mm(w):
        return jnp.dot(xb, w[...], preferred_element_type=F32)

    qa_o[...] = mm(wqa).astype(BF16)
    c = mm(wckv)
    c = c * lax.rsqrt(jnp.mean(c * c, axis=-1, keepdims=True) + LN_EPS) * kvg[...]
    ckv_o[...] = c.astype(BF16)
    ckvt_o[0:KV_RANK, :] = c.T.astype(BF16)
    ckvt_o[KV_RANK:KV_RANK + ONES_ROWS, :] = jnp.ones((ONES_ROWS, c.shape[0]), BF16)
    qi_o[...] = mm(wqi).astype(BF16)
    sm_o[...] = mm(wsm) + smb[...]
    qk_o[...] = mm(wqk).astype(BF16)
    v_o[...] = mm(wv).astype(BF16)
    o_o[...] = mm(wo).astype(BF16)
    ga_o[...] = mm(wga).astype(BF16)
    gm_o[...] = mm(wgm).astype(BF16)


def _proj(x2, ws, kvg, smb, tm, seq):
    T = x2.shape[0]
    per_seq = seq // tm
    widths = [w.shape[1] for w in ws]
    dts = [BF16, BF16, BF16, F32, BF16, BF16, BF16, BF16, BF16]
    in_specs = [pl.BlockSpec((tm, D_MODEL), lambda i: (i, 0))]
    in_specs += [_full(w.shape) for w in ws]
    in_specs += [_full(kvg.shape), _full(smb.shape)]
    out_specs = [pl.BlockSpec((tm, n), lambda i: (i, 0)) for n in widths]
    out_specs.append(pl.BlockSpec((None, KV_RANK + ONES_ROWS, tm), lambda i: (i // per_seq, 0, i % per_seq)))
    out_shape = [jax.ShapeDtypeStruct((T, n), dt) for n, dt in zip(widths, dts)]
    out_shape.append(jax.ShapeDtypeStruct((T // seq, KV_RANK + ONES_ROWS, seq), BF16))
    return pl.pallas_call(
        _proj_kernel, grid=(T // tm,), in_specs=in_specs, out_specs=out_specs, out_shape=out_shape,
        compiler_params=_cparams(1), name="proj")(x2, *ws, kvg, smb)


def _dsa_kernel(qi_ref, smq_ref, smk_ref, qa_ref, ckv_ref, ckvt_ref, wuk_ref, wuvt_ref, tz_ref,
                oa_ref, qs_ref, sc_ref, am_ref, ql_ref, x_ref, acc_ref, m_ref, *, topk):
    tq, kc = Q_TILE, K_CHUNK
    nh = N_HEADS_A
    qb = pl.program_id(1)
    n_ch = qb + 1
    seq_keys = n_ch * kc
    t0 = qb * tq

    def rows(c):
        return pl.ds(pl.multiple_of(c * kc, kc), kc)

    def lanes(h):
        return slice(h * tq, (h + 1) * tq)

    def chunk_loop(n, body, init, per_trip=2):
        def run(first, count, carry):
            for r in range(count):
                carry = body(first + r, carry)
            return carry

        shift = per_trip.bit_length() - 1
        trips = jnp.right_shift(n, shift)
        carry = lax.fori_loop(0, trips, lambda p, c: run(p * per_trip, per_trip, c), init)
        done = trips * per_trip
        count = per_trip // 2
        while count:
            start = done + jnp.bitwise_and(n - done, -2 * count)
            carry = lax.cond(jnp.bitwise_and(n, count) != 0,
                             functools.partial(run, start, count), lambda c: c, carry)
            count //= 2
        return carry

    for h in range(N_HEADS_IDX):
        qs_ref[h * tq:(h + 1) * tq, :] = qi_ref[:, h * HEAD_DIM_IDX:(h + 1) * HEAD_DIM_IDX]
    w_t = smq_ref[...].T
    q_pos = lax.broadcasted_iota(jnp.int32, (1, tq), 1) + t0
    key_iota = lax.broadcasted_iota(jnp.int32, (kc, tq), 0)

    def score_chunk(c, carry):
        mx, mn = carry
        kk = smk_ref[rows(c), SM_KIDX:SM_KIDX + HEAD_DIM_IDX].astype(BF16)
        dots = lax.dot_general(kk, qs_ref[...], (((1,), (1,)), ((), ())), preferred_element_type=F32)
        sc = jnp.zeros((kc, tq), F32)
        for h in range(N_HEADS_IDX):
            sc = sc + w_t[SM_WIDX + h:SM_WIDX + h + 1, :] * jnp.maximum(dots[:, lanes(h)], 0.0)
        vis = (key_iota + c * kc) <= q_pos
        sc_ref[rows(c), :] = jnp.where(vis, sc, NEG_INF)
        mx = jnp.maximum(mx, jnp.max(jnp.where(vis, sc, NEG_INF), axis=0, keepdims=True))
        mn = jnp.minimum(mn, jnp.min(jnp.where(vis, sc, jnp.inf), axis=0, keepdims=True))
        return mx, mn

    mx, mn = chunk_loop(n_ch, score_chunk,
                        (jnp.full((1, tq), NEG_INF, F32), jnp.full((1, tq), jnp.inf, F32)), MXU_CHUNKS_PER_TRIP)

    n_vis = (q_pos + 1).astype(F32)
    k_row = jnp.minimum(n_vis, float(topk))

    def count(pred):
        def body(c, a):
            hit = pred(sc_ref[rows(c), :]).astype(F32)
            return a + jnp.sum(hit.reshape(kc // SUBLANES, SUBLANES, tq), axis=0)
        a = chunk_loop(n_ch, body, jnp.zeros((SUBLANES, tq), F32))
        return jnp.sum(a, axis=0, keepdims=True)

    def any_lane(flag):
        return jnp.max(jnp.where(flag, 1.0, 0.0)) > 0.0

    def crowded(cnt_lo, c_hi):
        return any_lane(jnp.logical_and(cnt_lo != k_row, cnt_lo - c_hi > PEEL_BRACKET))

    def bisect_cond(carry):
        it, lo, hi, cnt_lo, c_hi = carry
        return jnp.logical_and(it < BISECT_MAX_CHECKS, crowded(cnt_lo, c_hi))

    def bisect_body(carry):
        it, lo, hi, cnt_lo, c_hi = carry
        for _ in range(BISECT_STEPS_PER_CHECK):
            mid = lo * 0.5 + hi * 0.5
            cnt = count(lambda s: s >= mid)
            ge = cnt >= k_row
            lo, cnt_lo = jnp.where(ge, mid, lo), jnp.where(ge, cnt, cnt_lo)
            hi, c_hi = jnp.where(ge, hi, mid), jnp.where(ge, c_hi, cnt)
        return it + 1, lo, hi, cnt_lo, c_hi

    hi0 = mx + jnp.maximum(jnp.abs(mx), 1e-30) * 1e-6
    _, lo, hi, cnt_lo, c_hi = lax.while_loop(
        bisect_cond, bisect_body, (jnp.int32(0), mn, hi0, n_vis, jnp.zeros((1, tq), F32)))

    def peel_cond(carry):
        it, lo, hi, cnt_lo, c_hi, done = carry
        return jnp.logical_and(it < seq_keys, any_lane(done == 0.0))

    def peel_body(carry):
        it, lo, hi, cnt_lo, c_hi, done = carry

        def top_body(c, v):
            s = sc_ref[rows(c), :]
            inside = jnp.logical_and(s >= lo, s < hi)
            return jnp.maximum(v, jnp.max(jnp.where(inside, s, NEG_INF), axis=0, keepdims=True))

        v = chunk_loop(n_ch, top_body, jnp.full((1, tq), NEG_INF, F32))
        c_v = count(lambda s: s >= v)
        reached = c_v >= k_row
        live = done == 0.0
        fin = jnp.logical_and(live, reached)
        cut = jnp.logical_and(live, jnp.logical_not(reached))
        return (it + 1, jnp.where(fin, v, lo), jnp.where(cut, v, hi), jnp.where(fin, c_v, cnt_lo),
                jnp.where(cut, c_v, c_hi), jnp.where(fin, 1.0, done))

    _, lo, hi, cnt_lo, c_hi, _ = lax.while_loop(
        peel_cond, peel_body,
        (jnp.int32(0), lo, hi, cnt_lo, c_hi, jnp.where(cnt_lo == k_row, 1.0, 0.0)))

    def mask_chunk(c, _):
        am_ref[rows(c), :] = jnp.where(sc_ref[rows(c), :] >= lo, 0.0, NEG_INF)
        return 0

    chunk_loop(n_ch, mask_chunk, 0)

    tied = cnt_lo != k_row
    for part in range(tq // LANES):
        ls = slice(part * LANES, (part + 1) * LANES)

        @pl.when(any_lane(tied[:, ls]))
        def _ties():
            need, lo_p, hi_p = (k_row - c_hi)[:, ls], lo[:, ls], hi[:, ls]
            lower = (lax.broadcasted_iota(jnp.int32, (kc, kc), 1)
                     < lax.broadcasted_iota(jnp.int32, (kc, kc), 0)).astype(BF16)

            def tie_chunk(c, before):
                s = sc_ref[rows(c), ls]
                above = s >= hi_p
                tie = jnp.logical_and(s >= lo_p, jnp.logical_not(above))
                tie_f = tie.astype(F32)
                rank = jnp.dot(lower, tie_f.astype(BF16), preferred_element_type=F32) + before
                sel = jnp.logical_or(above, jnp.logical_and(tie, rank < need))
                am_ref[rows(c), ls] = jnp.where(sel, 0.0, NEG_INF)
                return before + jnp.sum(tie_f, axis=0, keepdims=True)

            chunk_loop(n_ch, tie_chunk, jnp.zeros((1, LANES), F32))

    for h in range(nh):
        qh = qa_ref[:, h * HEAD_DIM_A:(h + 1) * HEAD_DIM_A]
        qlt = lax.dot_general(wuk_ref[h], qh, (((1,), (1,)), ((), ())), preferred_element_type=F32)
        ql_ref[:, lanes(h)] = (qlt * (HEAD_DIM_A ** -0.5 * LOG2E)).astype(BF16)

    m_ref[...] = jnp.full(m_ref.shape, NEG_INF, F32)

    def pass_a(c, table):
        raw = jnp.dot(ckv_ref[rows(c), :], ql_ref[...], preferred_element_type=F32)
        am = am_ref[rows(c), :]
        for h in range(nh):
            x = raw[:, lanes(h)] + am
            if table is not None:
                x = x + tz_ref[table, h]
            x_ref[rows(c), lanes(h)] = x
            m_ref[:, lanes(h)] = jnp.maximum(m_ref[:, lanes(h)], jnp.max(x, axis=0, keepdims=True))

    def far_chunk(c, _):
        pass_a(c, None)
        return 0

    chunk_loop(jnp.maximum(qb - 1, 0), far_chunk, 0, MXU_CHUNKS_PER_TRIP)

    @pl.when(qb >= 1)
    def _prev():
        pass_a(qb - 1, 1)

    pass_a(qb, 0)

    def pv(c):
        p = jnp.exp2((x_ref[rows(c), :] - m_ref[...]).astype(BF16))
        return jnp.dot(ckvt_ref[:, rows(c)], p, preferred_element_type=F32)

    def pass_b(c, _):
        acc_ref[...] += pv(c)
        return 0

    acc_ref[...] = pv(qb)
    chunk_loop(qb, pass_b, 0, MXU_CHUNKS_PER_TRIP)
    inv_l = 1.0 / acc_ref[KV_RANK:KV_RANK + 1, :]
    o_lat = (acc_ref[0:KV_RANK, :] * inv_l).astype(BF16)
    outs = [jnp.dot(wuvt_ref[h], o_lat[:, lanes(h)], preferred_element_type=F32) for h in range(nh)]
    oa_ref[...] = jnp.concatenate(outs, axis=0).T.astype(BF16)


def _dsa(qi, sm, qa, ckv, ckv_t, wuk_t, wuv_t, tz, batch, seq):
    tq = Q_TILE
    nq = seq // tq
    topk = min(TOPK_MAX, seq // 4)
    T = batch * seq
    kern = functools.partial(_dsa_kernel, topk=topk)
    blk_q = lambda n: pl.BlockSpec((tq, n), lambda b, q: (b * nq + q, 0))
    blk_s = lambda n: pl.BlockSpec((seq, n), lambda b, q: (b, 0))
    in_specs = [
        blk_q(WIDTH_IDX), blk_q(LANES), blk_s(LANES), blk_q(WIDTH_A), blk_s(KV_RANK),
        pl.BlockSpec((None, KV_RANK + ONES_ROWS, seq), lambda b, q: (b, 0, 0)),
        _full(wuk_t.shape), _full(wuv_t.shape), _full(tz.shape),
    ]
    scratch = [
        pltpu.VMEM((N_HEADS_IDX * tq, HEAD_DIM_IDX), BF16),
        pltpu.VMEM((seq, tq), F32),
        pltpu.VMEM((seq, tq), F32),
        pltpu.VMEM((KV_RANK, N_HEADS_A * tq), BF16),
        pltpu.VMEM((seq, N_HEADS_A * tq), F32),
        pltpu.VMEM((KV_RANK + ONES_ROWS, N_HEADS_A * tq), F32),
        pltpu.VMEM((1, N_HEADS_A * tq), F32),
    ]
    return pl.pallas_call(
        kern, grid=(batch, nq), in_specs=in_specs,
        out_specs=pl.BlockSpec((tq, WIDTH_A), lambda b, q: (b * nq + q, 0)),
        out_shape=jax.ShapeDtypeStruct((T, WIDTH_A), BF16),
        scratch_shapes=scratch, compiler_params=_cparams(2), name="dsa")(
            qi, sm, sm, qa, ckv, ckv_t, wuk_t, wuv_t, tz)


def _t5_bucket(dist):
    max_exact = N_BUCKETS // 2
    d = jnp.maximum(dist, 0)
    ratio = jnp.log(jnp.maximum(d, 1).astype(F32) / max_exact) / math.log(MAX_DISTANCE / max_exact)
    large = jnp.minimum(max_exact + (ratio * (N_BUCKETS - max_exact)).astype(jnp.int32), N_BUCKETS - 1)
    return jnp.where(d < max_exact, d, large)


def _bias_tables(rel_bias):
    tq = Q_TILE
    span = 2 * tq
    assert int(_np_bucket(tq)) == N_BUCKETS - 1
    far = rel_bias[N_BUCKETS - 1]
    by_dist = ((rel_bias[_t5_bucket(jnp.arange(span))] - far).astype(F32) * LOG2E).T
    diag = jnp.concatenate([by_dist[:, :tq], jnp.zeros_like(by_dist[:, :tq])], axis=1)
    prev = jnp.concatenate([by_dist[:, tq:], by_dist[:, :tq]], axis=1)

    def toeplitz(f):
        m = jnp.tile(f, (1, tq))[:, :tq * (span - 1)].reshape(N_HEADS_A, tq, span - 1)
        return m[:, :, :tq]

    return jnp.stack([toeplitz(diag), toeplitz(prev)])


def _np_bucket(d):
    max_exact = N_BUCKETS // 2
    ratio = np.log(np.float32(max(d, 1)) / np.float32(max_exact)) / math.log(MAX_DISTANCE / max_exact)
    return min(max_exact + int(ratio * (N_BUCKETS - max_exact)), N_BUCKETS - 1) if d >= max_exact else d


def _mlstm_kernel(qk_ref, halo_ref, v_ref, sm_ref, o_ref, cw_ref, cb_ref, g_ref,
                  out_ref, c_ref, m_ref):
    c_idx = pl.program_id(1)

    @pl.when(c_idx == 0)
    def _init():
        c_ref[...] = jnp.zeros_like(c_ref)
        m_ref[...] = jnp.zeros_like(m_ref)

    for g in range(M_GROUPS):
        gates = _mlstm_gates(g, c_idx, qk_ref, halo_ref, sm_ref, cw_ref, cb_ref)
        outs = [_mlstm_head(g, h, gates, v_ref, o_ref, g_ref, c_ref, m_ref) for h in range(N_HEADS_M)]
        out_ref[g] = jnp.concatenate(outs, axis=-1).astype(BF16)


def _mlstm_gates(g, c_idx, qk_ref, halo_ref, sm_ref, cw_ref, cb_ref):
    L = M_CHUNK

    hw = halo_ref.shape[1]
    halo = jnp.where(c_idx > 0, halo_ref[g].astype(F32), 0.0)
    ext = jnp.concatenate([halo, qk_ref[g].astype(F32)], axis=0)
    acc = jnp.zeros((L, 2 * WIDTH_M), F32) + cb_ref[...]
    for w in range(CONV_WIDTH):
        off = hw - (CONV_WIDTH - 1) + w
        acc = acc + ext[off:off + L, :] * cw_ref[w:w + 1, :]
    qk = acc * _sigmoid(acc)

    sm = sm_ref[g]
    sm_t = sm.T
    r_i = lax.broadcasted_iota(jnp.int32, (L, L), 0)
    c_i = lax.broadcasted_iota(jnp.int32, (L, L), 1)
    tril = (c_i <= r_i).astype(BF16)
    bcum_c = sum(jnp.dot(tril, part, preferred_element_type=F32)
                 for part in _split3(jax.nn.log_sigmoid(sm)))
    return qk, sm, sm_t, bcum_c, bcum_c.T, c_i <= r_i


def _mlstm_head(g, h, gates, v_ref, o_ref, g_ref, c_ref, m_ref):
    L = M_CHUNK
    dm = HEAD_DIM_M
    qk, sm, sm_t, bcum_c, bcum_r, causal = gates
    ones_col = (lax.broadcasted_iota(jnp.int32, (L, dm), 1) == 0).astype(BF16)
    st = g * N_HEADS_M + h
    q = qk[:, h * dm:(h + 1) * dm].astype(BF16)
    k = (qk[:, WIDTH_M + h * dm:WIDTH_M + (h + 1) * dm] * (dm ** -0.5))
    v_aug = jnp.concatenate([v_ref[g, :, h * dm:(h + 1) * dm], ones_col], axis=-1)
    b_col = bcum_c[:, SM_F + h:SM_F + h + 1]
    g_col = sm[:, SM_I + h:SM_I + h + 1] - b_col
    g_row = sm_t[SM_I + h:SM_I + h + 1, :] - bcum_r[SM_F + h:SM_F + h + 1, :]
    b_last = b_col[L - 1:L, :]
    m_prev = m_ref[st]
    c_prev = c_ref[st]

    log_d = jnp.where(causal, b_col + g_row, NEG_INF)
    m_j = jnp.maximum(b_col + m_prev, jnp.max(log_d, axis=-1, keepdims=True))
    w_inter = jnp.exp(b_col + m_prev - m_j)
    qkt = lax.dot_general(q, k.astype(BF16), (((1,), (1,)), ((), ())), preferred_element_type=F32)
    s = qkt * jnp.exp(log_d - m_j)
    o_aug = jnp.dot(s.astype(BF16), v_aug, preferred_element_type=F32) + \
        w_inter * jnp.dot(q, c_prev.astype(BF16), preferred_element_type=F32)
    num = o_aug[:, :dm]
    den = o_aug[:, dm:dm + 1]
    hh = num / jnp.maximum(jnp.abs(den), jnp.exp(-m_j))

    lwe = b_last + g_col
    m_loc = jnp.max(lwe, axis=0, keepdims=True)
    kw = (k * jnp.exp(lwe - m_loc)).astype(BF16)
    c_loc = lax.dot_general(kw, v_aug, (((0,), (0,)), ((), ())), preferred_element_type=F32)
    m_new = jnp.maximum(b_last + m_prev, m_loc)
    c_ref[st] = jnp.exp(b_last + m_prev - m_new) * c_prev + jnp.exp(m_loc - m_new) * c_loc
    m_ref[st] = m_new

    hn = hh * lax.rsqrt(jnp.mean(hh * hh, axis=-1, keepdims=True) + LN_EPS) * g_ref[:, h * dm:(h + 1) * dm]
    og = o_ref[g, :, h * dm:(h + 1) * dm].astype(F32)
    return _sigmoid(og) * hn


def _mlstm(qk, v, sm, o, conv_w, conv_b, mh_g, batch, seq):
    L = M_CHUNK
    G = M_GROUPS
    nc = seq // L
    T = batch * seq
    hb = L // CONV_HALO
    grouped = lambda a: a.reshape(G, T // G, a.shape[-1])
    blk = lambda n: pl.BlockSpec((G, L, n), lambda b, c: (0, b * nc + c, 0))
    in_specs = [
        blk(2 * WIDTH_M),
        pl.BlockSpec((G, CONV_HALO, 2 * WIDTH_M), lambda b, c: (0, jnp.maximum((b * nc + c) * hb - 1, 0), 0)),
        blk(WIDTH_M), blk(LANES), blk(WIDTH_M),
        _full(conv_w.shape), _full(conv_b.shape), _full(mh_g.shape),
    ]
    scratch = [pltpu.VMEM((G * N_HEADS_M, HEAD_DIM_M, 2 * HEAD_DIM_M), F32),
               pltpu.VMEM((G * N_HEADS_M, 1, 1), F32)]
    qk_g = grouped(qk)
    out = pl.pallas_call(
        _mlstm_kernel, grid=(batch // G, nc), in_specs=in_specs, out_specs=blk(WIDTH_M),
        out_shape=jax.ShapeDtypeStruct((G, T // G, WIDTH_M), BF16), scratch_shapes=scratch,
        compiler_params=_cparams(2), name="mlstm")(
            qk_g, qk_g, grouped(v), grouped(sm), grouped(o), conv_w, conv_b, mh_g)
    return out.reshape(T, WIDTH_M)


def _layer_norm(y, g, b):
    mu = jnp.mean(y, axis=-1, keepdims=True)
    var = jnp.mean(jnp.square(y - mu), axis=-1, keepdims=True)
    return (y - mu) * lax.rsqrt(var + LN_EPS) * g + b


def _router_logits(x1, wr, br):
    x1_hi = x1.astype(BF16)
    x1_lo = (x1 - x1_hi.astype(F32)).astype(BF16)
    both = jnp.dot(x1_hi, wr[...], preferred_element_type=F32)
    return (both[:, 0:LANES]
            + jnp.dot(x1_lo, wr[:, 0:LANES], preferred_element_type=F32)
            + both[:, LANES:2 * LANES]) + br[...]


def _merge_kernel(x_ref, oa_ref, hm_ref, ga_ref, gm_ref, wua, wum, wout, l1g, l1b, wr, br,
                  x1_ref, route_ref, cnt_ref):
    pa = jnp.dot(oa_ref[...], wua[...], preferred_element_type=F32)
    pm = jnp.dot(hm_ref[...], wum[...], preferred_element_type=F32)
    y = _sigmoid(ga_ref[...]) * pa.astype(BF16) + _sigmoid(gm_ref[...]) * pm.astype(BF16)
    mix = jnp.dot(y.astype(BF16), wout[...], preferred_element_type=F32)
    x1 = _layer_norm(ALPHA * x_ref[...] + mix, l1g[...], l1b[...])
    x1_ref[:, 0:D_MODEL] = x1

    lt = _router_logits(x1, wr, br).T
    tm = lt.shape[1]
    big = jnp.int32(LANES)
    le = lt[0:N_EXPERTS, :]
    lg = lt[N_EXPERTS:N_EXPERTS + SUBLANES, :]
    row_g = lax.broadcasted_iota(jnp.int32, lg.shape, 0)
    row_e = lax.broadcasted_iota(jnp.int32, le.shape, 0)
    is_grp = row_g < N_GROUPS
    gl = jnp.where(is_grp, lg, NEG_INF)
    ge = jnp.exp(gl - jnp.max(gl, axis=0, keepdims=True))
    gp = ge / jnp.sum(ge, axis=0, keepdims=True)
    g_w = jnp.max(gp, axis=0, keepdims=True)
    g_idx = jnp.min(jnp.where(jnp.logical_and(is_grp, gp == g_w), row_g, big), axis=0, keepdims=True)
    el = jnp.where(jnp.right_shift(row_e, EPG_SHIFT) == g_idx, le, NEG_INF)
    m1 = jnp.max(el, axis=0, keepdims=True)
    i1 = jnp.min(jnp.where(el == m1, row_e, big), axis=0, keepdims=True)
    el2 = jnp.where(row_e == i1, NEG_INF, el)
    m2 = jnp.max(el2, axis=0, keepdims=True)
    i2 = jnp.min(jnp.where(el2 == m2, row_e, big), axis=0, keepdims=True)

    e2 = jnp.exp(m2 - m1)
    w1 = g_w / (1.0 + e2)
    w2 = g_w * e2 / (1.0 + e2)
    first_is_a = i1 < i2
    row_w = lax.broadcasted_iota(jnp.int32, (LANES, tm), 0)
    w_rows = jnp.where(row_w == 0, jnp.where(first_is_a, w1, w2),
                       jnp.where(row_w == 1, jnp.where(first_is_a, w2, w1), 0.0))
    x1_ref[:, D_MODEL:D_MODEL + LANES] = w_rows.T

    a = jnp.bitwise_and(jnp.minimum(i1, i2), EXPERTS_PER_GROUP - 1)
    b = jnp.bitwise_and(jnp.maximum(i1, i2), EXPERTS_PER_GROUP - 1)
    pair = jnp.right_shift(a * (2 * EXPERTS_PER_GROUP - 1 - a), 1) + (b - a - 1)
    cls = g_idx * PAIRS_PER_GROUP + pair
    row_c = lax.broadcasted_iota(jnp.int32, (CLASS_ROWS, tm), 0)
    onehot = (row_c == cls).astype(F32)

    @pl.when(pl.program_id(0) == 0)
    def _init():
        cnt_ref[...] = jnp.zeros_like(cnt_ref)

    rb = min(RANK_BLOCK, tm)
    blocks = [onehot[:, j * rb:(j + 1) * rb] for j in range(tm // rb)]
    earlier = (lax.broadcasted_iota(jnp.int32, (rb, rb), 0)
               < lax.broadcasted_iota(jnp.int32, (rb, rb), 1)).astype(BF16)
    local = jnp.dot(jnp.concatenate(blocks, axis=0).astype(BF16), earlier, preferred_element_type=F32)
    before = cnt_ref[:, 0:1]
    ranks = []
    for j, blk in enumerate(blocks):
        prior = local[j * CLASS_ROWS:(j + 1) * CLASS_ROWS, :] + before
        ranks.append(jnp.sum(prior * blk, axis=0, keepdims=True))
        before = before + jnp.sum(blk, axis=1, keepdims=True)
    rank = jnp.concatenate(ranks, axis=1)
    cnt_ref[...] = jnp.broadcast_to(before, cnt_ref.shape)
    row_o = lax.broadcasted_iota(jnp.int32, route_ref.shape, 0)
    route_ref[...] = jnp.where(row_o == 0, cls.astype(F32), jnp.where(row_o == 1, rank, 0.0))


def _merge(x2, oa, hm, ga, gm, wua, wum, wout, l1g, l1b, wr, br, tm):
    T = x2.shape[0]
    blk = lambda n: pl.BlockSpec((tm, n), lambda i: (i, 0))
    in_specs = [blk(D_MODEL), blk(WIDTH_A), blk(WIDTH_M), blk(D_MODEL), blk(D_MODEL),
                _full(wua.shape), _full(wum.shape), _full(wout.shape), _full(l1g.shape), _full(l1b.shape),
                _full(wr.shape), _full(br.shape)]
    out_specs = [blk(D_MODEL + LANES), pl.BlockSpec((SUBLANES, tm), lambda i: (0, i)),
                 pl.BlockSpec((CLASS_ROWS, LANES), lambda i: (0, 0))]
    out_shape = [jax.ShapeDtypeStruct((T, D_MODEL + LANES), F32), jax.ShapeDtypeStruct((SUBLANES, T), F32),
                 jax.ShapeDtypeStruct((CLASS_ROWS, LANES), F32)]
    return pl.pallas_call(
        _merge_kernel, grid=(T // tm,), in_specs=in_specs, out_specs=out_specs, out_shape=out_shape,
        compiler_params=_cparams(1), name="merge")(x2, oa, hm, ga, gm, wua, wum, wout, l1g, l1b, wr, br)


def _sc_mesh():
    return plsc.VectorSubcoreMesh(core_axis_name="c", subcore_axis_name="s",
                                  num_cores=SC_CORES, num_subcores=SC_SUBCORES)


def _sc_chunks(n_rows):
    workers = SC_CORES * SC_SUBCORES
    assert n_rows % (workers * SC_ROWS * 2) == 0
    return n_rows // (workers * SC_ROWS)


def _sc_scratch(n_chunks, width, dtype):
    return [pltpu.VMEM((n_chunks, SC_ROWS), jnp.int32),
            pltpu.VMEM((SC_ROWS, width), dtype), pltpu.VMEM((SC_ROWS, width), dtype),
            pltpu.SemaphoreType.DMA, pltpu.SemaphoreType.DMA]


def _sc_scatter_rows(rows, idx, n_out):
    n_in, width = rows.shape
    n_chunks = _sc_chunks(n_in)

    @functools.partial(
        pl.kernel, mesh=_sc_mesh(), out_type=jax.ShapeDtypeStruct((n_out, width), rows.dtype),
        scratch_types=_sc_scratch(n_chunks, width, rows.dtype), name="sc_dispatch")
    def scatter(rows_hbm, idx_hbm, out_hbm, idx_v, rows_a, rows_b, sem_a, sem_b):
        first = (lax.axis_index("s") * SC_CORES + lax.axis_index("c")) * n_chunks
        pltpu.sync_copy(idx_hbm.at[pl.ds(first, n_chunks)], idx_v)

        def load(c, buf):
            pltpu.sync_copy(rows_hbm.at[pl.ds((first + c) * SC_ROWS, SC_ROWS)], buf)

        def put(c, buf, sem):
            return pltpu.make_async_copy(buf, out_hbm.at[idx_v.at[c]], sem)

        load(0, rows_a)
        put(0, rows_a, sem_a).start()

        @pl.loop(0, n_chunks, step=2)
        def _(j):
            load(j + 1, rows_b)
            put(j + 1, rows_b, sem_b).start()
            put(j, rows_a, sem_a).wait()

            @pl.when(j + 2 < n_chunks)
            def _():
                load(j + 2, rows_a)
                put(j + 2, rows_a, sem_a).start()

            put(j + 1, rows_b, sem_b).wait()

    return scatter(rows, idx.reshape(n_in // SC_ROWS, SC_ROWS))


def _sc_gather_rows(table, idx):
    n_out, width = idx.shape[0], table.shape[1]
    n_chunks = _sc_chunks(n_out)

    @functools.partial(
        pl.kernel, mesh=_sc_mesh(), out_type=jax.ShapeDtypeStruct((n_out, width), table.dtype),
        scratch_types=_sc_scratch(n_chunks, width, table.dtype), name="sc_combine")
    def gather(table_hbm, idx_hbm, out_hbm, idx_v, rows_a, rows_b, sem_a, sem_b):
        first = (lax.axis_index("s") * SC_CORES + lax.axis_index("c")) * n_chunks
        pltpu.sync_copy(idx_hbm.at[pl.ds(first, n_chunks)], idx_v)

        def fetch(c, buf, sem):
            return pltpu.make_async_copy(table_hbm.at[idx_v.at[c]], buf, sem)

        def store(c, buf):
            pltpu.sync_copy(buf, out_hbm.at[pl.ds((first + c) * SC_ROWS, SC_ROWS)])

        fetch(0, rows_a, sem_a).start()

        @pl.loop(0, n_chunks, step=2)
        def _(j):
            fetch(j + 1, rows_b, sem_b).start()
            fetch(j, rows_a, sem_a).wait()
            store(j, rows_a)

            @pl.when(j + 2 < n_chunks)
            def _():
                fetch(j + 2, rows_a, sem_a).start()

            fetch(j + 1, rows_b, sem_b).wait()
            store(j + 1, rows_b)

    return gather(table, idx.reshape(n_out // SC_ROWS, SC_ROWS))


def _moe_kernel(ta_ref, tb_ref, nv_ref, xs_ref, wga, wua, wda, wgb, wub, wdb, l2g, l2b, ys_ref):
    i = pl.program_id(0)

    @pl.when(nv_ref[i] > 0)
    def _compute():
        x = xs_ref[:, 0:D_MODEL]
        xb = x.astype(BF16)
        w_a = xs_ref[:, D_MODEL:D_MODEL + 1]
        w_b = xs_ref[:, D_MODEL + 1:D_MODEL + 2]

        def expert(wg, wu, wd):
            g = jnp.dot(xb, wg[...], preferred_element_type=F32)
            u = jnp.dot(xb, wu[...], preferred_element_type=F32)
            hdn = (g * _sigmoid(g) * u).astype(BF16)
            return jnp.dot(hdn, wd[...], preferred_element_type=F32)

        ffn = w_a * expert(wga, wua, wda) + w_b * expert(wgb, wub, wdb)
        ys_ref[...] = _layer_norm(ALPHA * x + ffn, l2g[...], l2b[...])


def _moe(xs, tile_a, tile_b, n_valid, wg, wu, wd, l2g, l2b):
    tm = MOE_TILE
    n_tiles = xs.shape[0] // tm
    rows = lambda n: pl.BlockSpec((tm, n), lambda i, ta, tb, nv: (i, 0))
    w_in = lambda which: pl.BlockSpec((None, D_MODEL, D_EXPERT), lambda i, ta, tb, nv: ((ta, tb)[which][i], 0, 0))
    w_out = lambda which: pl.BlockSpec((None, D_EXPERT, D_MODEL), lambda i, ta, tb, nv: ((ta, tb)[which][i], 0, 0))
    const = lambda shape: pl.BlockSpec(shape, lambda i, ta, tb, nv: (0,) * len(shape))
    grid_spec = pltpu.PrefetchScalarGridSpec(
        num_scalar_prefetch=3, grid=(n_tiles,),
        in_specs=[rows(xs.shape[1]), w_in(0), w_in(0), w_out(0), w_in(1), w_in(1), w_out(1),
                  const(l2g.shape), const(l2b.shape)],
        out_specs=rows(D_MODEL))
    return pl.pallas_call(
        _moe_kernel, grid_spec=grid_spec, out_shape=jax.ShapeDtypeStruct((xs.shape[0], D_MODEL), F32),
        compiler_params=_cparams(1), name="moe")(
            tile_a, tile_b, n_valid, xs, wg, wu, wd, wg, wu, wd, l2g, l2b)


def _route_tables(route, counts, n_tokens):
    tm = MOE_TILE
    n_tiles = n_tokens // tm + N_CLASSES
    cnt = counts[:N_CLASSES, 0].astype(jnp.int32)
    tiles = (cnt + tm - 1) // tm
    tile_end = jnp.cumsum(tiles)
    tile_start = tile_end - tiles
    t_idx = jnp.arange(n_tiles, dtype=jnp.int32)
    cls_of_tile = jnp.minimum(jnp.sum(t_idx[:, None] >= tile_end[None, :], axis=1), N_CLASSES - 1).astype(jnp.int32)
    classes = np.arange(N_CLASSES)
    tile_is = cls_of_tile[:, None] == classes[None, :]
    per_tile = lambda table: jnp.sum(jnp.where(tile_is, jnp.asarray(table, jnp.int32)[None, :], 0), axis=1)
    n_valid = jnp.clip(per_tile(cnt) - (t_idx - per_tile(tile_start)) * tm, 0, tm).astype(jnp.int32)
    first_expert = classes // PAIRS_PER_GROUP * EXPERTS_PER_GROUP
    tile_a = per_tile(first_expert + np.asarray(PAIR_A)[classes % PAIRS_PER_GROUP])
    tile_b = per_tile(first_expert + np.asarray(PAIR_B)[classes % PAIRS_PER_GROUP])
    cls = route[0].astype(jnp.int32)
    rank = route[1].astype(jnp.int32)
    row0 = jnp.sum(jnp.where(cls[:, None] == jnp.arange(N_CLASSES)[None, :], (tile_start * tm)[None, :], 0), axis=1)
    return tile_a, tile_b, n_valid, row0 + rank, n_tiles * tm


def _pick_tile(T, pref):
    t = pref
    while T % t:
        t //= 2
    return t


def kernel(x, w_in, conv_w, conv_b, kv_norm_g, w_uk, w_uv, rel_bias, b_i, b_f, mh_norm_g, w_up_a, w_up_m,
           w_out, ln1_g, ln1_b, w_grp, b_grp, w_rt, b_rt, w_gate, w_up, w_down, ln2_g, ln2_b):
    B, S, _ = x.shape
    T = B * S
    assert S % Q_TILE == 0 and S % M_CHUNK == 0 and T % MOE_TILE == 0 and w_in.shape[0] == DEPTH
    tz = _bias_tables(rel_bias)
    x2 = x.reshape(T, D_MODEL)
    for l in range(DEPTH):
        w = w_in[l]
        o = np.cumsum((WIDTH_A, KV_RANK, WIDTH_IDX, HEAD_DIM_IDX, N_HEADS_IDX, 2 * WIDTH_M, WIDTH_M,
                       N_HEADS_M, N_HEADS_M, WIDTH_M, D_MODEL, D_MODEL)).tolist()
        o = [0] + o
        seg = lambda j: w[:, o[j]:o[j + 1]]
        pad = LANES - (HEAD_DIM_IDX + N_HEADS_IDX + 2 * N_HEADS_M)
        w_small = jnp.concatenate([seg(3), seg(4), seg(7), seg(8), jnp.zeros((D_MODEL, pad), w.dtype)], axis=1)
        ws = [seg(0), seg(1), seg(2), w_small, seg(5), seg(6), seg(9), seg(10), seg(11)]
        ws = [a.astype(BF16) for a in ws]
        smb = jnp.zeros((1, LANES), F32).at[0, SM_I:SM_I + N_HEADS_M].set(b_i[l]) \
            .at[0, SM_F:SM_F + N_HEADS_M].set(b_f[l])
        qa, ckv, qi, sm, qk, v, og, ga, gm, ckv_t = _proj(x2, ws, kv_norm_g[l][None, :], smb,
                                                          _pick_tile(S, PROJ_TILE), S)

        wuk_t = jnp.transpose(w_uk[l], (1, 0, 2)).astype(BF16)
        wuv_t = jnp.transpose(w_uv[l], (1, 2, 0)).astype(BF16)
        oa = _dsa(qi, sm, qa, ckv, ckv_t, wuk_t, wuv_t, tz, B, S)

        hm = _mlstm(qk, v, sm, og, conv_w[l], conv_b[l][None, :], mh_norm_g[l].reshape(1, WIDTH_M), B, S)

        w_router = jnp.concatenate(
            [w_rt[l], w_grp[l], jnp.zeros((D_MODEL, LANES - N_EXPERTS - N_GROUPS), F32)], axis=1)
        b_router = jnp.concatenate(
            [b_rt[l], b_grp[l], jnp.zeros((LANES - N_EXPERTS - N_GROUPS,), F32)])[None, :]
        wr_hi = w_router.astype(BF16)
        wr_split = jnp.concatenate([wr_hi, (w_router - wr_hi.astype(F32)).astype(BF16)], axis=1)
        x1, route, counts = _merge(x2, oa, hm, ga, gm, w_up_a[l].astype(BF16), w_up_m[l].astype(BF16),
                                   w_out[l].astype(BF16), ln1_g[l][None, :], ln1_b[l][None, :],
                                   wr_split, b_router, _pick_tile(T, MERGE_TILE))

        tile_a, tile_b, n_valid, pos, n_sorted = _route_tables(route, counts, T)
        xs = _sc_scatter_rows(x1, pos, n_sorted)
        ys = _moe(xs, tile_a, tile_b, n_valid, w_gate[l].astype(BF16), w_up[l].astype(BF16),
                  w_down[l].astype(BF16), ln2_g[l][None, :], ln2_b[l][None, :])
        x2 = _sc_gather_rows(ys, pos)
    return x2.reshape(B, S, D_MODEL)
```

```python
import functools
import math

import jax
import jax.numpy as jnp
import numpy as np
from jax import lax
from jax.experimental import pallas as pl
from jax.experimental.pallas import tpu as pltpu
from jax.experimental.pallas import tpu_sc as plsc

F32 = jnp.float32
BF16 = jnp.bfloat16

D_MODEL = 1024
N_HEADS_A = 8
HEAD_DIM_A = 64
WIDTH_A = N_HEADS_A * HEAD_DIM_A
KV_RANK = 256
N_HEADS_IDX = 8
HEAD_DIM_IDX = 64
WIDTH_IDX = N_HEADS_IDX * HEAD_DIM_IDX
TOPK_MAX = 256
N_BUCKETS = 32
MAX_DISTANCE = 128
N_HEADS_M = 4
HEAD_DIM_M = 128
WIDTH_M = N_HEADS_M * HEAD_DIM_M
CONV_WIDTH = 4
N_GROUPS = 4
EXPERTS_PER_GROUP = 4
N_EXPERTS = N_GROUPS * EXPERTS_PER_GROUP
D_EXPERT = 512
LN_EPS = 1e-5
DEPTH = 1
ALPHA = (2.0 * DEPTH) ** 0.25

LANES = 128
SUBLANES = 8
VMEM_LIMIT = 56 * 1024 * 1024

SM_KIDX = 0
SM_WIDX = HEAD_DIM_IDX
SM_I = SM_WIDX + N_HEADS_IDX
SM_F = SM_I + N_HEADS_M

Q_TILE = 256
K_CHUNK = Q_TILE
MXU_CHUNKS_PER_TRIP = 4
ONES_ROWS = 16
M_CHUNK = 128
M_GROUPS = 2
CONV_HALO = 16
assert CONV_HALO >= CONV_WIDTH - 1
PROJ_TILE = 1024
MERGE_TILE = 1024
RANK_BLOCK = 256
MOE_TILE = 512
MOE_DOT_COLS = 256
MOE_NORM_PIECES = 2 * (2 * D_EXPERT + D_MODEL) // MOE_DOT_COLS
SC_CORES = 2
SC_SUBCORES = 16
SC_ROWS = 32
EPG_SHIFT = EXPERTS_PER_GROUP.bit_length() - 1
assert 1 << EPG_SHIFT == EXPERTS_PER_GROUP
PAIR_A, PAIR_B = zip(*[(a, b) for a in range(EXPERTS_PER_GROUP) for b in range(a + 1, EXPERTS_PER_GROUP)])
PAIRS_PER_GROUP = len(PAIR_A)
N_CLASSES = N_GROUPS * PAIRS_PER_GROUP
CLASS_ROWS = -(-N_CLASSES // SUBLANES) * SUBLANES
BISECT_STEPS_PER_CHECK = 3
BISECT_MAX_CHECKS = 5
PEEL_BRACKET = 2.0
NEG_INF = float("-inf")
LOG2E = math.log2(math.e)


def _cparams(n_grid):
    return pltpu.CompilerParams(dimension_semantics=("arbitrary",) * n_grid,
                                vmem_limit_bytes=VMEM_LIMIT)


def _full(shape):
    nd = len(shape)
    return pl.BlockSpec(shape, lambda *_: (0,) * nd, pipeline_mode=pl.Buffered(1))


def _sigmoid(x):
    return 0.5 * jnp.tanh(0.5 * x) + 0.5


def _split3(x):
    hi = x.astype(BF16)
    r = x - hi.astype(F32)
    mid = r.astype(BF16)
    return hi, mid, (r - mid.astype(F32)).astype(BF16)


def _proj_kernel(x_ref, wqa, wckv, wqi, wsm, wqk, wv, wo, wga, wgm, kvg, smb,
                 qa_o, ckv_o, qi_o, sm_o, qk_o, v_o, o_o, ga_o, gm_o, ckvt_o):
    xb = x_ref[...].astype(BF16)

    def mm(w):
        return jnp.dot(xb, w[...], preferred_element_type=F32)

    qa_o[...] = mm(wqa).astype(BF16)
    c = mm(wckv)
    c = c * lax.rsqrt(jnp.mean(c * c, axis=-1, keepdims=True) + LN_EPS) * kvg[...]
    ckv_o[...] = c.astype(BF16)
    ckvt_o[0:KV_RANK, :] = c.T.astype(BF16)
    ckvt_o[KV_RANK:KV_RANK + ONES_ROWS, :] = jnp.ones((ONES_ROWS, c.shape[0]), BF16)
    qi_o[...] = mm(wqi).astype(BF16)
    sm_o[...] = mm(wsm) + smb[...]
    qk_o[...] = mm(wqk).astype(BF16)
    v_o[...] = mm(wv).astype(BF16)
    o_o[...] = mm(wo).astype(BF16)
    ga_o[...] = mm(wga).astype(BF16)
    gm_o[...] = mm(wgm).astype(BF16)


def _proj(x2, ws, kvg, smb, tm, seq):
    T = x2.shape[0]
    per_seq = seq // tm
    widths = [w.shape[1] for w in ws]
    dts = [BF16, BF16, BF16, F32, BF16, BF16, BF16, BF16, BF16]
    in_specs = [pl.BlockSpec((tm, D_MODEL), lambda i: (i, 0))]
    in_specs += [_full(w.shape) for w in ws]
    in_specs += [_full(kvg.shape), _full(smb.shape)]
    out_specs = [pl.BlockSpec((tm, n), lambda i: (i, 0)) for n in widths]
    out_specs.append(pl.BlockSpec((None, KV_RANK + ONES_ROWS, tm), lambda i: (i // per_seq, 0, i % per_seq)))
    out_shape = [jax.ShapeDtypeStruct((T, n), dt) for n, dt in zip(widths, dts)]
    out_shape.append(jax.ShapeDtypeStruct((T // seq, KV_RANK + ONES_ROWS, seq), BF16))
    return pl.pallas_call(
        _proj_kernel, grid=(T // tm,), in_specs=in_specs, out_specs=out_specs, out_shape=out_shape,
        compiler_params=_cparams(1), name="proj")(x2, *ws, kvg, smb)


def _dsa_kernel(qi_ref, smq_ref, smk_ref, qa_ref, ckv_ref, ckvt_ref, wuk_ref, wuvt_ref, tz_ref,
                oa_ref, qs_ref, sc_ref, am_ref, ql_ref, x_ref, acc_ref, m_ref, *, topk):
    tq, kc = Q_TILE, K_CHUNK
    nh = N_HEADS_A
    qb = pl.program_id(1)
    n_ch = qb + 1
    seq_keys = n_ch * kc
    t0 = qb * tq

    def rows(c):
        return pl.ds(pl.multiple_of(c * kc, kc), kc)

    def lanes(h):
        return slice(h * tq, (h + 1) * tq)

    def chunk_loop(n, body, init, per_trip=2):
        def run(first, count, carry):
            for r in range(count):
                carry = body(first + r, carry)
            return carry

        shift = per_trip.bit_length() - 1
        trips = jnp.right_shift(n, shift)
        carry = lax.fori_loop(0, trips, lambda p, c: run(p * per_trip, per_trip, c), init)
        done = trips * per_trip
        count = per_trip // 2
        while count:
            start = done + jnp.bitwise_and(n - done, -2 * count)
            carry = lax.cond(jnp.bitwise_and(n, count) != 0,
                             functools.partial(run, start, count), lambda c: c, carry)
            count //= 2
        return carry

    for h in range(N_HEADS_IDX):
        qs_ref[h * tq:(h + 1) * tq, :] = qi_ref[:, h * HEAD_DIM_IDX:(h + 1) * HEAD_DIM_IDX]
    w_t = smq_ref[...].T
    q_pos = lax.broadcasted_iota(jnp.int32, (1, tq), 1) + t0
    key_iota = lax.broadcasted_iota(jnp.int32, (kc, tq), 0)

    def score_chunk(c, carry):
        mx, mn = carry
        kk = smk_ref[rows(c), SM_KIDX:SM_KIDX + HEAD_DIM_IDX].astype(BF16)
        dots = lax.dot_general(kk, qs_ref[...], (((1,), (1,)), ((), ())), preferred_element_type=F32)
        sc = jnp.zeros((kc, tq), F32)
        for h in range(N_HEADS_IDX):
            sc = sc + w_t[SM_WIDX + h:SM_WIDX + h + 1, :] * jnp.maximum(dots[:, lanes(h)], 0.0)
        vis = (key_iota + c * kc) <= q_pos
        sc_ref[rows(c), :] = jnp.where(vis, sc, NEG_INF)
        mx = jnp.maximum(mx, jnp.max(jnp.where(vis, sc, NEG_INF), axis=0, keepdims=True))
        mn = jnp.minimum(mn, jnp.min(jnp.where(vis, sc, jnp.inf), axis=0, keepdims=True))
        return mx, mn

    mx, mn = chunk_loop(n_ch, score_chunk,
                        (jnp.full((1, tq), NEG_INF, F32), jnp.full((1, tq), jnp.inf, F32)), MXU_CHUNKS_PER_TRIP)

    n_vis = (q_pos + 1).astype(F32)
    k_row = jnp.minimum(n_vis, float(topk))

    def count(pred):
        def body(c, a):
            hit = pred(sc_ref[rows(c), :]).astype(F32)
            return a + jnp.sum(hit.reshape(kc // SUBLANES, SUBLANES, tq), axis=0)
        a = chunk_loop(n_ch, body, jnp.zeros((SUBLANES, tq), F32))
        return jnp.sum(a, axis=0, keepdims=True)

    def any_lane(flag):
        return jnp.max(jnp.where(flag, 1.0, 0.0)) > 0.0

    def crowded(cnt_lo, c_hi):
        return any_lane(jnp.logical_and(cnt_lo != k_row, cnt_lo - c_hi > PEEL_BRACKET))

    def bisect_cond(carry):
        it, lo, hi, cnt_lo, c_hi = carry
        return jnp.logical_and(it < BISECT_MAX_CHECKS, crowded(cnt_lo, c_hi))

    def bisect_body(carry):
        it, lo, hi, cnt_lo, c_hi = carry
        for _ in range(BISECT_STEPS_PER_CHECK):
            mid = lo * 0.5 + hi * 0.5
            cnt = count(lambda s: s >= mid)
            ge = cnt >= k_row
            lo, cnt_lo = jnp.where(ge, mid, lo), jnp.where(ge, cnt, cnt_lo)
            hi, c_hi = jnp.where(ge, hi, mid), jnp.where(ge, c_hi, cnt)
        return it + 1, lo, hi, cnt_lo, c_hi

    hi0 = mx + jnp.maximum(jnp.abs(mx), 1e-30) * 1e-6
    _, lo, hi, cnt_lo, c_hi = lax.while_loop(
        bisect_cond, bisect_body, (jnp.int32(0), mn, hi0, n_vis, jnp.zeros((1, tq), F32)))

    def peel_cond(carry):
        it, lo, hi, cnt_lo, c_hi, done = carry
        return jnp.logical_and(it < seq_keys, any_lane(done == 0.0))

    def peel_body(carry):
        it, lo, hi, cnt_lo, c_hi, done = carry

        def top_body(c, v):
            s = sc_ref[rows(c), :]
            inside = jnp.logical_and(s >= lo, s < hi)
            return jnp.maximum(v, jnp.max(jnp.where(inside, s, NEG_INF), axis=0, keepdims=True))

        v = chunk_loop(n_ch, top_body, jnp.full((1, tq), NEG_INF, F32))
        c_v = count(lambda s: s >= v)
        reached = c_v >= k_row
        live = done == 0.0
        fin = jnp.logical_and(live, reached)
        cut = jnp.logical_and(live, jnp.logical_not(reached))
        return (it + 1, jnp.where(fin, v, lo), jnp.where(cut, v, hi), jnp.where(fin, c_v, cnt_lo),
                jnp.where(cut, c_v, c_hi), jnp.where(fin, 1.0, done))

    _, lo, hi, cnt_lo, c_hi, _ = lax.while_loop(
        peel_cond, peel_body,
        (jnp.int32(0), lo, hi, cnt_lo, c_hi, jnp.where(cnt_lo == k_row, 1.0, 0.0)))

    def mask_chunk(c, _):
        am_ref[rows(c), :] = jnp.where(sc_ref[rows(c), :] >= lo, 0.0, NEG_INF)
        return 0

    chunk_loop(n_ch, mask_chunk, 0)

    tied = cnt_lo != k_row
    for part in range(tq // LANES):
        ls = slice(part * LANES, (part + 1) * LANES)

        @pl.when(any_lane(tied[:, ls]))
        def _ties():
            need, lo_p, hi_p = (k_row - c_hi)[:, ls], lo[:, ls], hi[:, ls]
            lower = (lax.broadcasted_iota(jnp.int32, (kc, kc), 1)
                     < lax.broadcasted_iota(jnp.int32, (kc, kc), 0)).astype(BF16)

            def tie_chunk(c, before):
                s = sc_ref[rows(c), ls]
                above = s >= hi_p
                tie = jnp.logical_and(s >= lo_p, jnp.logical_not(above))
                tie_f = tie.astype(F32)
                rank = jnp.dot(lower, tie_f.astype(BF16), preferred_element_type=F32) + before
                sel = jnp.logical_or(above, jnp.logical_and(tie, rank < need))
                am_ref[rows(c), ls] = jnp.where(sel, 0.0, NEG_INF)
                return before + jnp.sum(tie_f, axis=0, keepdims=True)

            chunk_loop(n_ch, tie_chunk, jnp.zeros((1, LANES), F32))

    for h in range(nh):
        qh = qa_ref[:, h * HEAD_DIM_A:(h + 1) * HEAD_DIM_A]
        qlt = lax.dot_general(wuk_ref[h], qh, (((1,), (1,)), ((), ())), preferred_element_type=F32)
        ql_ref[:, lanes(h)] = (qlt * (HEAD_DIM_A ** -0.5 * LOG2E)).astype(BF16)

    m_ref[...] = jnp.full(m_ref.shape, NEG_INF, F32)

    def pass_a(c, table):
        raw = jnp.dot(ckv_ref[rows(c), :], ql_ref[...], preferred_element_type=F32)
        am = am_ref[rows(c), :]
        for h in range(nh):
            x = raw[:, lanes(h)] + am
            if table is not None:
                x = x + tz_ref[table, h]
            x_ref[rows(c), lanes(h)] = x
            m_ref[:, lanes(h)] = jnp.maximum(m_ref[:, lanes(h)], jnp.max(x, axis=0, keepdims=True))

    def far_chunk(c, _):
        pass_a(c, None)
        return 0

    chunk_loop(jnp.maximum(qb - 1, 0), far_chunk, 0, MXU_CHUNKS_PER_TRIP)

    @pl.when(qb >= 1)
    def _prev():
        pass_a(qb - 1, 1)

    pass_a(qb, 0)

    def pv(c):
        p = jnp.exp2((x_ref[rows(c), :] - m_ref[...]).astype(BF16))
        return jnp.dot(ckvt_ref[:, rows(c)], p, preferred_element_type=F32)

    def pass_b(c, _):
        acc_ref[...] += pv(c)
        return 0

    acc_ref[...] = pv(qb)
    chunk_loop(qb, pass_b, 0, MXU_CHUNKS_PER_TRIP)
    inv_l = 1.0 / acc_ref[KV_RANK:KV_RANK + 1, :]
    o_lat = (acc_ref[0:KV_RANK, :] * inv_l).astype(BF16)
    outs = [jnp.dot(wuvt_ref[h], o_lat[:, lanes(h)], preferred_element_type=F32) for h in range(nh)]
    oa_ref[...] = jnp.concatenate(outs, axis=0).T.astype(BF16)


def _dsa(qi, sm, qa, ckv, ckv_t, wuk_t, wuv_t, tz, batch, seq):
    tq = Q_TILE
    nq = seq // tq
    topk = min(TOPK_MAX, seq // 4)
    T = batch * seq
    kern = functools.partial(_dsa_kernel, topk=topk)
    blk_q = lambda n: pl.BlockSpec((tq, n), lambda b, q: (b * nq + q, 0))
    blk_s = lambda n: pl.BlockSpec((seq, n), lambda b, q: (b, 0))
    in_specs = [
        blk_q(WIDTH_IDX), blk_q(LANES), blk_s(LANES), blk_q(WIDTH_A), blk_s(KV_RANK),
        pl.BlockSpec((None, KV_RANK + ONES_ROWS, seq), lambda b, q: (b, 0, 0)),
        _full(wuk_t.shape), _full(wuv_t.shape), _full(tz.shape),
    ]
    scratch = [
        pltpu.VMEM((N_HEADS_IDX * tq, HEAD_DIM_IDX), BF16),
        pltpu.VMEM((seq, tq), F32),
        pltpu.VMEM((seq, tq), F32),
        pltpu.VMEM((KV_RANK, N_HEADS_A * tq), BF16),
        pltpu.VMEM((seq, N_HEADS_A * tq), F32),
        pltpu.VMEM((KV_RANK + ONES_ROWS, N_HEADS_A * tq), F32),
        pltpu.VMEM((1, N_HEADS_A * tq), F32),
    ]
    return pl.pallas_call(
        kern, grid=(batch, nq), in_specs=in_specs,
        out_specs=pl.BlockSpec((tq, WIDTH_A), lambda b, q: (b * nq + q, 0)),
        out_shape=jax.ShapeDtypeStruct((T, WIDTH_A), BF16),
        scratch_shapes=scratch, compiler_params=_cparams(2), name="dsa")(
            qi, sm, sm, qa, ckv, ckv_t, wuk_t, wuv_t, tz)


def _t5_bucket(dist):
    max_exact = N_BUCKETS // 2
    d = jnp.maximum(dist, 0)
    ratio = jnp.log(jnp.maximum(d, 1).astype(F32) / max_exact) / math.log(MAX_DISTANCE / max_exact)
    large = jnp.minimum(max_exact + (ratio * (N_BUCKETS - max_exact)).astype(jnp.int32), N_BUCKETS - 1)
    return jnp.where(d < max_exact, d, large)


def _bias_tables(rel_bias):
    tq = Q_TILE
    span = 2 * tq
    assert int(_np_bucket(tq)) == N_BUCKETS - 1
    far = rel_bias[N_BUCKETS - 1]
    by_dist = ((rel_bias[_t5_bucket(jnp.arange(span))] - far).astype(F32) * LOG2E).T
    diag = jnp.concatenate([by_dist[:, :tq], jnp.zeros_like(by_dist[:, :tq])], axis=1)
    prev = jnp.concatenate([by_dist[:, tq:], by_dist[:, :tq]], axis=1)

    def toeplitz(f):
        m = jnp.tile(f, (1, tq))[:, :tq * (span - 1)].reshape(N_HEADS_A, tq, span - 1)
        return m[:, :, :tq]

    return jnp.stack([toeplitz(diag), toeplitz(prev)])


def _np_bucket(d):
    max_exact = N_BUCKETS // 2
    ratio = np.log(np.float32(max(d, 1)) / np.float32(max_exact)) / math.log(MAX_DISTANCE / max_exact)
    return min(max_exact + int(ratio * (N_BUCKETS - max_exact)), N_BUCKETS - 1) if d >= max_exact else d


def _mlstm_kernel(qk_ref, halo_ref, v_ref, sm_ref, o_ref, cw_ref, cb_ref, g_ref,
                  out_ref, c_ref, m_ref):
    c_idx = pl.program_id(1)

    @pl.when(c_idx == 0)
    def _init():
        c_ref[...] = jnp.zeros_like(c_ref)
        m_ref[...] = jnp.zeros_like(m_ref)

    for g in range(M_GROUPS):
        gates = _mlstm_gates(g, c_idx, qk_ref, halo_ref, sm_ref, cw_ref, cb_ref)
        outs = [_mlstm_head(g, h, gates, v_ref, o_ref, g_ref, c_ref, m_ref) for h in range(N_HEADS_M)]
        out_ref[g] = jnp.concatenate(outs, axis=-1).astype(BF16)


def _mlstm_gates(g, c_idx, qk_ref, halo_ref, sm_ref, cw_ref, cb_ref):
    L = M_CHUNK

    hw = halo_ref.shape[1]
    halo = jnp.where(c_idx > 0, halo_ref[g].astype(F32), 0.0)
    ext = jnp.concatenate([halo, qk_ref[g].astype(F32)], axis=0)
    acc = jnp.zeros((L, 2 * WIDTH_M), F32) + cb_ref[...]
    for w in range(CONV_WIDTH):
        off = hw - (CONV_WIDTH - 1) + w
        acc = acc + ext[off:off + L, :] * cw_ref[w:w + 1, :]
    qk = acc * _sigmoid(acc)

    sm = sm_ref[g]
    sm_t = sm.T
    r_i = lax.broadcasted_iota(jnp.int32, (L, L), 0)
    c_i = lax.broadcasted_iota(jnp.int32, (L, L), 1)
    tril = (c_i <= r_i).astype(BF16)
    bcum_c = sum(jnp.dot(tril, part, preferred_element_type=F32)
                 for part in _split3(jax.nn.log_sigmoid(sm)))
    return qk, sm, sm_t, bcum_c, bcum_c.T, c_i <= r_i


def _mlstm_head(g, h, gates, v_ref, o_ref, g_ref, c_ref, m_ref):
    L = M_CHUNK
    dm = HEAD_DIM_M
    qk, sm, sm_t, bcum_c, bcum_r, causal = gates
    ones_col = (lax.broadcasted_iota(jnp.int32, (L, dm), 1) == 0).astype(BF16)
    st = g * N_HEADS_M + h
    q = qk[:, h * dm:(h + 1) * dm].astype(BF16)
    k = (qk[:, WIDTH_M + h * dm:WIDTH_M + (h + 1) * dm] * (dm ** -0.5))
    v_aug = jnp.concatenate([v_ref[g, :, h * dm:(h + 1) * dm], ones_col], axis=-1)
    b_col = bcum_c[:, SM_F + h:SM_F + h + 1]
    g_col = sm[:, SM_I + h:SM_I + h + 1] - b_col
    g_row = sm_t[SM_I + h:SM_I + h + 1, :] - bcum_r[SM_F + h:SM_F + h + 1, :]
    b_last = b_col[L - 1:L, :]
    m_prev = m_ref[st]
    c_prev = c_ref[st]

    log_d = jnp.where(causal, b_col + g_row, NEG_INF)
    m_j = jnp.maximum(b_col + m_prev, jnp.max(log_d, axis=-1, keepdims=True))
    w_inter = jnp.exp(b_col + m_prev - m_j)
    qkt = lax.dot_general(q, k.astype(BF16), (((1,), (1,)), ((), ())), preferred_element_type=F32)
    s = qkt * jnp.exp(log_d - m_j)
    o_aug = jnp.dot(s.astype(BF16), v_aug, preferred_element_type=F32) + \
        w_inter * jnp.dot(q, c_prev.astype(BF16), preferred_element_type=F32)
    num = o_aug[:, :dm]
    den = o_aug[:, dm:dm + 1]
    hh = num / jnp.maximum(jnp.abs(den), jnp.exp(-m_j))

    lwe = b_last + g_col
    m_loc = jnp.max(lwe, axis=0, keepdims=True)
    kw = (k * jnp.exp(lwe - m_loc)).astype(BF16)
    c_loc = lax.dot_general(kw, v_aug, (((0,), (0,)), ((), ())), preferred_element_type=F32)
    m_new = jnp.maximum(b_last + m_prev, m_loc)
    c_ref[st] = jnp.exp(b_last + m_prev - m_new) * c_prev + jnp.exp(m_loc - m_new) * c_loc
    m_ref[st] = m_new

    hn = hh * lax.rsqrt(jnp.mean(hh * hh, axis=-1, keepdims=True) + LN_EPS) * g_ref[:, h * dm:(h + 1) * dm]
    og = o_ref[g, :, h * dm:(h + 1) * dm].astype(F32)
    return _sigmoid(og) * hn


def _mlstm(qk, v, sm, o, conv_w, conv_b, mh_g, batch, seq):
    L = M_CHUNK
    G = M_GROUPS
    nc = seq // L
    T = batch * seq
    hb = L // CONV_HALO
    grouped = lambda a: a.reshape(G, T // G, a.shape[-1])
    blk = lambda n: pl.BlockSpec((G, L, n), lambda b, c: (0, b * nc + c, 0))
    in_specs = [
        blk(2 * WIDTH_M),
        pl.BlockSpec((G, CONV_HALO, 2 * WIDTH_M), lambda b, c: (0, jnp.maximum((b * nc + c) * hb - 1, 0), 0)),
        blk(WIDTH_M), blk(LANES), blk(WIDTH_M),
        _full(conv_w.shape), _full(conv_b.shape), _full(mh_g.shape),
    ]
    scratch = [pltpu.VMEM((G * N_HEADS_M, HEAD_DIM_M, 2 * HEAD_DIM_M), F32),
               pltpu.VMEM((G * N_HEADS_M, 1, 1), F32)]
    qk_g = grouped(qk)
    out = pl.pallas_call(
        _mlstm_kernel, grid=(batch // G, nc), in_specs=in_specs, out_specs=blk(WIDTH_M),
        out_shape=jax.ShapeDtypeStruct((G, T // G, WIDTH_M), BF16), scratch_shapes=scratch,
        compiler_params=_cparams(2), name="mlstm")(
            qk_g, qk_g, grouped(v), grouped(sm), grouped(o), conv_w, conv_b, mh_g)
    return out.reshape(T, WIDTH_M)


def _layer_norm(y, g, b):
    mu = jnp.mean(y, axis=-1, keepdims=True)
    var = jnp.mean(jnp.square(y - mu), axis=-1, keepdims=True)
    return (y - mu) * lax.rsqrt(var + LN_EPS) * g + b


def _router_logits(x1, wr, br):
    x1_hi = x1.astype(BF16)
    x1_lo = (x1 - x1_hi.astype(F32)).astype(BF16)
    both = jnp.dot(x1_hi, wr[...], preferred_element_type=F32)
    return (both[:, 0:LANES]
            + jnp.dot(x1_lo, wr[:, 0:LANES], preferred_element_type=F32)
            + both[:, LANES:2 * LANES]) + br[...]


def _merge_kernel(x_ref, oa_ref, hm_ref, ga_ref, gm_ref, wua, wum, wout, l1g, l1b, wr, br,
                  x1_ref, route_ref, cnt_ref):
    pa = jnp.dot(oa_ref[...], wua[...], preferred_element_type=F32)
    pm = jnp.dot(hm_ref[...], wum[...], preferred_element_type=F32)
    y = _sigmoid(ga_ref[...]) * pa.astype(BF16) + _sigmoid(gm_ref[...]) * pm.astype(BF16)
    mix = jnp.dot(y.astype(BF16), wout[...], preferred_element_type=F32)
    x1 = _layer_norm(ALPHA * x_ref[...] + mix, l1g[...], l1b[...])
    x1_ref[:, 0:D_MODEL] = x1

    lt = _router_logits(x1, wr, br).T
    tm = lt.shape[1]
    big = jnp.int32(LANES)
    le = lt[0:N_EXPERTS, :]
    lg = lt[N_EXPERTS:N_EXPERTS + SUBLANES, :]
    row_g = lax.broadcasted_iota(jnp.int32, lg.shape, 0)
    row_e = lax.broadcasted_iota(jnp.int32, le.shape, 0)
    is_grp = row_g < N_GROUPS
    gl = jnp.where(is_grp, lg, NEG_INF)
    ge = jnp.exp(gl - jnp.max(gl, axis=0, keepdims=True))
    gp = ge / jnp.sum(ge, axis=0, keepdims=True)
    g_w = jnp.max(gp, axis=0, keepdims=True)
    g_idx = jnp.min(jnp.where(jnp.logical_and(is_grp, gp == g_w), row_g, big), axis=0, keepdims=True)
    el = jnp.where(jnp.right_shift(row_e, EPG_SHIFT) == g_idx, le, NEG_INF)
    m1 = jnp.max(el, axis=0, keepdims=True)
    i1 = jnp.min(jnp.where(el == m1, row_e, big), axis=0, keepdims=True)
    el2 = jnp.where(row_e == i1, NEG_INF, el)
    m2 = jnp.max(el2, axis=0, keepdims=True)
    i2 = jnp.min(jnp.where(el2 == m2, row_e, big), axis=0, keepdims=True)

    e2 = jnp.exp(m2 - m1)
    w1 = g_w / (1.0 + e2)
    w2 = g_w * e2 / (1.0 + e2)
    first_is_a = i1 < i2
    row_w = lax.broadcasted_iota(jnp.int32, (LANES, tm), 0)
    w_rows = jnp.where(row_w == 0, jnp.where(first_is_a, w1, w2),
                       jnp.where(row_w == 1, jnp.where(first_is_a, w2, w1), 0.0))
    x1_ref[:, D_MODEL:D_MODEL + LANES] = w_rows.T

    a = jnp.bitwise_and(jnp.minimum(i1, i2), EXPERTS_PER_GROUP - 1)
    b = jnp.bitwise_and(jnp.maximum(i1, i2), EXPERTS_PER_GROUP - 1)
    pair = jnp.right_shift(a * (2 * EXPERTS_PER_GROUP - 1 - a), 1) + (b - a - 1)
    cls = g_idx * PAIRS_PER_GROUP + pair
    row_c = lax.broadcasted_iota(jnp.int32, (CLASS_ROWS, tm), 0)
    onehot = (row_c == cls).astype(F32)

    @pl.when(pl.program_id(0) == 0)
    def _init():
        cnt_ref[...] = jnp.zeros_like(cnt_ref)

    rb = min(RANK_BLOCK, tm)
    blocks = [onehot[:, j * rb:(j + 1) * rb] for j in range(tm // rb)]
    earlier = (lax.broadcasted_iota(jnp.int32, (rb, rb), 0)
               < lax.broadcasted_iota(jnp.int32, (rb, rb), 1)).astype(BF16)
    local = jnp.dot(jnp.concatenate(blocks, axis=0).astype(BF16), earlier, preferred_element_type=F32)
    before = cnt_ref[:, 0:1]
    ranks = []
    for j, blk in enumerate(blocks):
        prior = local[j * CLASS_ROWS:(j + 1) * CLASS_ROWS, :] + before
        ranks.append(jnp.sum(prior * blk, axis=0, keepdims=True))
        before = before + jnp.sum(blk, axis=1, keepdims=True)
    rank = jnp.concatenate(ranks, axis=1)
    cnt_ref[...] = jnp.broadcast_to(before, cnt_ref.shape)
    row_o = lax.broadcasted_iota(jnp.int32, route_ref.shape, 0)
    route_ref[...] = jnp.where(row_o == 0, cls.astype(F32), jnp.where(row_o == 1, rank, 0.0))


def _merge(x2, oa, hm, ga, gm, wua, wum, wout, l1g, l1b, wr, br, tm):
    T = x2.shape[0]
    blk = lambda n: pl.BlockSpec((tm, n), lambda i: (i, 0))
    in_specs = [blk(D_MODEL), blk(WIDTH_A), blk(WIDTH_M), blk(D_MODEL), blk(D_MODEL),
                _full(wua.shape), _full(wum.shape), _full(wout.shape), _full(l1g.shape), _full(l1b.shape),
                _full(wr.shape), _full(br.shape)]
    out_specs = [blk(D_MODEL + LANES), pl.BlockSpec((SUBLANES, tm), lambda i: (0, i)),
                 pl.BlockSpec((CLASS_ROWS, LANES), lambda i: (0, 0))]
    out_shape = [jax.ShapeDtypeStruct((T, D_MODEL + LANES), F32), jax.ShapeDtypeStruct((SUBLANES, T), F32),
                 jax.ShapeDtypeStruct((CLASS_ROWS, LANES), F32)]
    return pl.pallas_call(
        _merge_kernel, grid=(T // tm,), in_specs=in_specs, out_specs=out_specs, out_shape=out_shape,
        compiler_params=_cparams(1), name="merge")(x2, oa, hm, ga, gm, wua, wum, wout, l1g, l1b, wr, br)


def _sc_mesh():
    return plsc.VectorSubcoreMesh(core_axis_name="c", subcore_axis_name="s",
                                  num_cores=SC_CORES, num_subcores=SC_SUBCORES)


def _sc_chunks(n_rows):
    workers = SC_CORES * SC_SUBCORES
    assert n_rows % (workers * SC_ROWS * 2) == 0
    return n_rows // (workers * SC_ROWS)


def _sc_scratch(n_chunks, width, dtype):
    return [pltpu.VMEM((n_chunks, SC_ROWS), jnp.int32),
            pltpu.VMEM((SC_ROWS, width), dtype), pltpu.VMEM((SC_ROWS, width), dtype),
            pltpu.SemaphoreType.DMA, pltpu.SemaphoreType.DMA]


def _sc_scatter_rows(rows, idx, n_out):
    n_in, width = rows.shape
    n_chunks = _sc_chunks(n_in)

    @functools.partial(
        pl.kernel, mesh=_sc_mesh(), out_type=jax.ShapeDtypeStruct((n_out, width), rows.dtype),
        scratch_types=_sc_scratch(n_chunks, width, rows.dtype), name="sc_dispatch")
    def scatter(rows_hbm, idx_hbm, out_hbm, idx_v, rows_a, rows_b, sem_a, sem_b):
        first = (lax.axis_index("s") * SC_CORES + lax.axis_index("c")) * n_chunks
        pltpu.sync_copy(idx_hbm.at[pl.ds(first, n_chunks)], idx_v)

        def load(c, buf):
            pltpu.sync_copy(rows_hbm.at[pl.ds((first + c) * SC_ROWS, SC_ROWS)], buf)

        def put(c, buf, sem):
            return pltpu.make_async_copy(buf, out_hbm.at[idx_v.at[c]], sem)

        load(0, rows_a)
        put(0, rows_a, sem_a).start()

        @pl.loop(0, n_chunks, step=2)
        def _(j):
            load(j + 1, rows_b)
            put(j + 1, rows_b, sem_b).start()
            put(j, rows_a, sem_a).wait()

            @pl.when(j + 2 < n_chunks)
            def _():
                load(j + 2, rows_a)
                put(j + 2, rows_a, sem_a).start()

            put(j + 1, rows_b, sem_b).wait()

    return scatter(rows, idx.reshape(n_in // SC_ROWS, SC_ROWS))


def _sc_gather_rows(table, idx):
    n_out, width = idx.shape[0], table.shape[1]
    n_chunks = _sc_chunks(n_out)

    @functools.partial(
        pl.kernel, mesh=_sc_mesh(), out_type=jax.ShapeDtypeStruct((n_out, width), table.dtype),
        scratch_types=_sc_scratch(n_chunks, width, table.dtype), name="sc_combine")
    def gather(table_hbm, idx_hbm, out_hbm, idx_v, rows_a, rows_b, sem_a, sem_b):
        first = (lax.axis_index("s") * SC_CORES + lax.axis_index("c")) * n_chunks
        pltpu.sync_copy(idx_hbm.at[pl.ds(first, n_chunks)], idx_v)

        def fetch(c, buf, sem):
            return pltpu.make_async_copy(table_hbm.at[idx_v.at[c]], buf, sem)

        def store(c, buf):
            pltpu.sync_copy(buf, out_hbm.at[pl.ds((first + c) * SC_ROWS, SC_ROWS)])

        fetch(0, rows_a, sem_a).start()

        @pl.loop(0, n_chunks, step=2)
        def _(j):
            fetch(j + 1, rows_b, sem_b).start()
            fetch(j, rows_a, sem_a).wait()
            store(j, rows_a)

            @pl.when(j + 2 < n_chunks)
            def _():
                fetch(j + 2, rows_a, sem_a).start()

            fetch(j + 1, rows_b, sem_b).wait()
            store(j + 1, rows_b)

    return gather(table, idx.reshape(n_out // SC_ROWS, SC_ROWS))


def _moe_kernel(ta_ref, tb_ref, nu_ref, xs_ref, wga, wua, wda, wgb, wub, wdb, l2g, l2b, ys_ref,
                pre_ref):
    i = pl.program_id(0)
    n_used = nu_ref[0]
    tm = pre_ref.shape[0]
    pr = tm // MOE_NORM_PIECES

    def norm_rows(piece):
        rs = slice(piece * pr, (piece + 1) * pr)
        ys_ref[rs, :] = _layer_norm(pre_ref[rs, :], l2g[...], l2b[...])

    @pl.when(i == 0)
    def _first():
        pre_ref[...] = jnp.zeros_like(pre_ref)

    @pl.when(i < n_used)
    def _compute():
        x = xs_ref[:, 0:D_MODEL]
        xb = x.astype(BF16)
        w_a = xs_ref[:, D_MODEL:D_MODEL + 1]
        w_b = xs_ref[:, D_MODEL + 1:D_MODEL + 2]
        piece = iter(range(MOE_NORM_PIECES))

        def dot_pieces(lhs, w):
            outs = []
            for n in range(w.shape[1] // MOE_DOT_COLS):
                outs.append(jnp.dot(lhs, w[:, n * MOE_DOT_COLS:(n + 1) * MOE_DOT_COLS],
                                    preferred_element_type=F32))
                norm_rows(next(piece))
            return jnp.concatenate(outs, axis=1)

        def expert(wg, wu, wd):
            g = dot_pieces(xb, wg)
            u = dot_pieces(xb, wu)
            hdn = (g * _sigmoid(g) * u).astype(BF16)
            return dot_pieces(hdn, wd)

        ffn = w_a * expert(wga, wua, wda) + w_b * expert(wgb, wub, wdb)
        pre_ref[...] = ALPHA * x + ffn

    @pl.when(i == n_used)
    def _last():
        for piece in range(MOE_NORM_PIECES):
            norm_rows(piece)


def _moe(xs, tile_a, tile_b, n_used, wg, wu, wd, l2g, l2b):
    tm = MOE_TILE
    n_tiles = xs.shape[0] // tm
    used = lambda i, nu: jnp.minimum(i, nu[0] - 1)
    rows = lambda n: pl.BlockSpec((tm, n), lambda i, ta, tb, nu: (used(i, nu), 0))
    prev_rows = pl.BlockSpec((tm, D_MODEL), lambda i, ta, tb, nu: (used(jnp.maximum(i - 1, 0), nu), 0))
    w_in = lambda which: pl.BlockSpec(
        (None, D_MODEL, D_EXPERT), lambda i, ta, tb, nu: ((ta, tb)[which][used(i, nu)], 0, 0))
    w_out = lambda which: pl.BlockSpec(
        (None, D_EXPERT, D_MODEL), lambda i, ta, tb, nu: ((ta, tb)[which][used(i, nu)], 0, 0))
    const = lambda shape: pl.BlockSpec(shape, lambda i, ta, tb, nu: (0,) * len(shape))
    grid_spec = pltpu.PrefetchScalarGridSpec(
        num_scalar_prefetch=3, grid=(n_tiles + 1,),
        in_specs=[rows(xs.shape[1]), w_in(0), w_in(0), w_out(0), w_in(1), w_in(1), w_out(1),
                  const(l2g.shape), const(l2b.shape)],
        out_specs=prev_rows,
        scratch_shapes=[pltpu.VMEM((tm, D_MODEL), F32)])
    return pl.pallas_call(
        _moe_kernel, grid_spec=grid_spec, out_shape=jax.ShapeDtypeStruct((xs.shape[0], D_MODEL), F32),
        compiler_params=_cparams(1), name="moe")(
            tile_a, tile_b, n_used, xs, wg, wu, wd, wg, wu, wd, l2g, l2b)


def _route_tables(route, counts, n_tokens):
    tm = MOE_TILE
    n_tiles = n_tokens // tm + N_CLASSES
    cnt = counts[:N_CLASSES, 0].astype(jnp.int32)
    tiles = (cnt + tm - 1) // tm
    tile_end = jnp.cumsum(tiles)
    tile_start = tile_end - tiles
    t_idx = jnp.arange(n_tiles, dtype=jnp.int32)
    cls_of_tile = jnp.minimum(jnp.sum(t_idx[:, None] >= tile_end[None, :], axis=1), N_CLASSES - 1).astype(jnp.int32)
    classes = np.arange(N_CLASSES)
    tile_is = cls_of_tile[:, None] == classes[None, :]
    per_tile = lambda table: jnp.sum(jnp.where(tile_is, jnp.asarray(table, jnp.int32)[None, :], 0), axis=1)
    first_expert = classes // PAIRS_PER_GROUP * EXPERTS_PER_GROUP
    tile_a = per_tile(first_expert + np.asarray(PAIR_A)[classes % PAIRS_PER_GROUP])
    tile_b = per_tile(first_expert + np.asarray(PAIR_B)[classes % PAIRS_PER_GROUP])
    cls = route[0].astype(jnp.int32)
    rank = route[1].astype(jnp.int32)
    row0 = jnp.sum(jnp.where(cls[:, None] == jnp.arange(N_CLASSES)[None, :], (tile_start * tm)[None, :], 0), axis=1)
    n_used = jnp.maximum(tile_end[-1:], 1).astype(jnp.int32)
    return tile_a, tile_b, n_used, row0 + rank, n_tiles * tm


def _pick_tile(T, pref):
    t = pref
    while T % t:
        t //= 2
    return t


def kernel(x, w_in, conv_w, conv_b, kv_norm_g, w_uk, w_uv, rel_bias, b_i, b_f, mh_norm_g, w_up_a, w_up_m,
           w_out, ln1_g, ln1_b, w_grp, b_grp, w_rt, b_rt, w_gate, w_up, w_down, ln2_g, ln2_b):
    B, S, _ = x.shape
    T = B * S
    assert S % Q_TILE == 0 and S % M_CHUNK == 0 and T % MOE_TILE == 0 and w_in.shape[0] == DEPTH
    tz = _bias_tables(rel_bias)
    x2 = x.reshape(T, D_MODEL)
    for l in range(DEPTH):
        w = w_in[l]
        o = np.cumsum((WIDTH_A, KV_RANK, WIDTH_IDX, HEAD_DIM_IDX, N_HEADS_IDX, 2 * WIDTH_M, WIDTH_M,
                       N_HEADS_M, N_HEADS_M, WIDTH_M, D_MODEL, D_MODEL)).tolist()
        o = [0] + o
        seg = lambda j: w[:, o[j]:o[j + 1]]
        pad = LANES - (HEAD_DIM_IDX + N_HEADS_IDX + 2 * N_HEADS_M)
        w_small = jnp.concatenate([seg(3), seg(4), seg(7), seg(8), jnp.zeros((D_MODEL, pad), w.dtype)], axis=1)
        ws = [seg(0), seg(1), seg(2), w_small, seg(5), seg(6), seg(9), seg(10), seg(11)]
        ws = [a.astype(BF16) for a in ws]
        smb = jnp.zeros((1, LANES), F32).at[0, SM_I:SM_I + N_HEADS_M].set(b_i[l]) \
            .at[0, SM_F:SM_F + N_HEADS_M].set(b_f[l])
        qa, ckv, qi, sm, qk, v, og, ga, gm, ckv_t = _proj(x2, ws, kv_norm_g[l][None, :], smb,
                                                          _pick_tile(S, PROJ_TILE), S)

        wuk_t = jnp.transpose(w_uk[l], (1, 0, 2)).astype(BF16)
        wuv_t = jnp.transpose(w_uv[l], (1, 2, 0)).astype(BF16)
        oa = _dsa(qi, sm, qa, ckv, ckv_t, wuk_t, wuv_t, tz, B, S)

        hm = _mlstm(qk, v, sm, og, conv_w[l], conv_b[l][None, :], mh_norm_g[l].reshape(1, WIDTH_M), B, S)

        w_router = jnp.concatenate(
            [w_rt[l], w_grp[l], jnp.zeros((D_MODEL, LANES - N_EXPERTS - N_GROUPS), F32)], axis=1)
        b_router = jnp.concatenate(
            [b_rt[l], b_grp[l], jnp.zeros((LANES - N_EXPERTS - N_GROUPS,), F32)])[None, :]
        wr_hi = w_router.astype(BF16)
        wr_split = jnp.concatenate([wr_hi, (w_router - wr_hi.astype(F32)).astype(BF16)], axis=1)
        x1, route, counts = _merge(x2, oa, hm, ga, gm, w_up_a[l].astype(BF16), w_up_m[l].astype(BF16),
                                   w_out[l].astype(BF16), ln1_g[l][None, :], ln1_b[l][None, :],
                                   wr_split, b_router, _pick_tile(T, MERGE_TILE))

        tile_a, tile_b, n_used, pos, n_sorted = _route_tables(route, counts, T)
        xs = _sc_scatter_rows(x1, pos, n_sorted)
        ys = _moe(xs, tile_a, tile_b, n_used, w_gate[l].astype(BF16), w_up[l].astype(BF16),
                  w_down[l].astype(BF16), ln2_g[l][None, :], ln2_b[l][None, :])
        x2 = _sc_gather_rows(ys, pos)
    return x2.reshape(B, S, D_MODEL)
```

```python
import functools
import math

import jax
import jax.numpy as jnp
import numpy as np
from jax import lax
from jax.experimental import pallas as pl
from jax.experimental.pallas import tpu as pltpu
from jax.experimental.pallas import tpu_sc as plsc

F32 = jnp.float32
BF16 = jnp.bfloat16

D_MODEL = 1024
N_HEADS_A = 8
HEAD_DIM_A = 64
WIDTH_A = N_HEADS_A * HEAD_DIM_A
KV_RANK = 256
N_HEADS_IDX = 8
HEAD_DIM_IDX = 64
WIDTH_IDX = N_HEADS_IDX * HEAD_DIM_IDX
TOPK_MAX = 256
N_BUCKETS = 32
MAX_DISTANCE = 128
N_HEADS_M = 4
HEAD_DIM_M = 128
WIDTH_M = N_HEADS_M * HEAD_DIM_M
CONV_WIDTH = 4
N_GROUPS = 4
EXPERTS_PER_GROUP = 4
N_EXPERTS = N_GROUPS * EXPERTS_PER_GROUP
D_EXPERT = 512
LN_EPS = 1e-5
DEPTH = 1
ALPHA = (2.0 * DEPTH) ** 0.25

LANES = 128
SUBLANES = 8
VMEM_LIMIT = 56 * 1024 * 1024

SM_KIDX = 0
SM_WIDX = HEAD_DIM_IDX
SM_I = SM_WIDX + N_HEADS_IDX
SM_F = SM_I + N_HEADS_M

HEADS_PER_DOT = LANES // HEAD_DIM_A
assert HEADS_PER_DOT * HEAD_DIM_A == LANES and N_HEADS_A % HEADS_PER_DOT == 0
Q_TILE = 256
K_CHUNK = Q_TILE
MXU_CHUNKS_PER_TRIP = 4
ONES_ROWS = 16
M_CHUNK = 128
M_GROUPS = 2
CONV_HALO = 16
assert CONV_HALO >= CONV_WIDTH - 1
PROJ_TILE = 1024
MERGE_TILE = 1024
RANK_BLOCK = 256
MOE_TILE = 512
MOE_DOT_COLS = 256
MOE_NORM_PIECES = 2 * (2 * D_EXPERT + D_MODEL) // MOE_DOT_COLS
SC_CORES = 2
SC_SUBCORES = 16
SC_ROWS = 32
EPG_SHIFT = EXPERTS_PER_GROUP.bit_length() - 1
assert 1 << EPG_SHIFT == EXPERTS_PER_GROUP
PAIR_A, PAIR_B = zip(*[(a, b) for a in range(EXPERTS_PER_GROUP) for b in range(a + 1, EXPERTS_PER_GROUP)])
PAIRS_PER_GROUP = len(PAIR_A)
N_CLASSES = N_GROUPS * PAIRS_PER_GROUP
CLASS_ROWS = -(-N_CLASSES // SUBLANES) * SUBLANES
BISECT_STEPS_PER_CHECK = 3
BISECT_MAX_CHECKS = 5
PEEL_BRACKET = 2.0
NEG_INF = float("-inf")
LOG2E = math.log2(math.e)


def _cparams(n_grid):
    return pltpu.CompilerParams(dimension_semantics=("arbitrary",) * n_grid,
                                vmem_limit_bytes=VMEM_LIMIT)


def _full(shape):
    nd = len(shape)
    return pl.BlockSpec(shape, lambda *_: (0,) * nd, pipeline_mode=pl.Buffered(1))


def _sigmoid(x):
    return 0.5 * jnp.tanh(0.5 * x) + 0.5


def _split3(x):
    hi = x.astype(BF16)
    r = x - hi.astype(F32)
    mid = r.astype(BF16)
    return hi, mid, (r - mid.astype(F32)).astype(BF16)


def _proj_kernel(x_ref, wqa, wckv, wqi, wsm, wqk, wv, wo, wga, wgm, wuk, kvg, smb,
                 qa_o, ka_o, qi_o, sm_o, qk_o, v_o, o_o, ga_o, gm_o, ckvt_o):
    xb = x_ref[...].astype(BF16)

    def mm(w):
        return jnp.dot(xb, w[...], preferred_element_type=F32)

    c = mm(wckv)
    qa_o[...] = (mm(wqa) * (HEAD_DIM_A ** -0.5 * LOG2E)).astype(BF16)
    c = c * lax.rsqrt(jnp.mean(c * c, axis=-1, keepdims=True) + LN_EPS) * kvg[...]
    qi_o[...] = mm(wqi).astype(BF16)
    ka_o[...] = jnp.dot(c.astype(BF16), wuk[...], preferred_element_type=F32).astype(BF16)
    ckvt_o[0:KV_RANK, :] = c.T.astype(BF16)
    ckvt_o[KV_RANK:KV_RANK + ONES_ROWS, :] = jnp.ones((ONES_ROWS, c.shape[0]), BF16)
    sm_o[...] = mm(wsm) + smb[...]
    qk_o[...] = mm(wqk).astype(BF16)
    v_o[...] = mm(wv).astype(BF16)
    o_o[...] = mm(wo).astype(BF16)
    ga_o[...] = mm(wga).astype(BF16)
    gm_o[...] = mm(wgm).astype(BF16)


def _proj(x2, ws, wuk, kvg, smb, tm, seq):
    T = x2.shape[0]
    per_seq = seq // tm

    def by_token(n, dt=BF16):
        return pl.BlockSpec((tm, n), lambda i: (i, 0)), jax.ShapeDtypeStruct((T, n), dt)

    def by_lane(n):
        return (pl.BlockSpec((None, n, tm), lambda i: (i // per_seq, 0, i % per_seq)),
                jax.ShapeDtypeStruct((T // seq, n, seq), BF16))

    outs = [by_token(WIDTH_A), by_token(WIDTH_A), by_token(WIDTH_IDX), by_token(LANES, F32)]
    outs += [by_token(w.shape[1]) for w in ws[4:]]
    outs.append(by_lane(KV_RANK + ONES_ROWS))
    in_specs = [pl.BlockSpec((tm, D_MODEL), lambda i: (i, 0))]
    in_specs += [_full(a.shape) for a in (*ws, wuk, kvg, smb)]
    return pl.pallas_call(
        _proj_kernel, grid=(T // tm,), in_specs=in_specs, out_specs=[o[0] for o in outs],
        out_shape=[o[1] for o in outs], compiler_params=_cparams(1), name="proj")(x2, *ws, wuk, kvg, smb)


def _dsa_kernel(qi_ref, smq_ref, smk_ref, qa_ref, ka_ref, ckvt_ref, wuvt_ref, tz_ref,
                oa_ref, qs_ref, sc_ref, am_ref, ql_ref, x_ref, acc_ref, m_ref, *, topk):
    tq, kc = Q_TILE, K_CHUNK
    nh = N_HEADS_A
    qb = pl.program_id(1)
    n_ch = qb + 1
    seq_keys = n_ch * kc
    t0 = qb * tq

    def rows(c):
        return pl.ds(pl.multiple_of(c * kc, kc), kc)

    def lanes(h):
        return slice(h * tq, (h + 1) * tq)

    def chunk_loop(n, body, init, per_trip=2):
        def run(first, count, carry):
            for r in range(count):
                carry = body(first + r, carry)
            return carry

        shift = per_trip.bit_length() - 1
        trips = jnp.right_shift(n, shift)
        carry = lax.fori_loop(0, trips, lambda p, c: run(p * per_trip, per_trip, c), init)
        done = trips * per_trip
        count = per_trip // 2
        while count:
            start = done + jnp.bitwise_and(n - done, -2 * count)
            carry = lax.cond(jnp.bitwise_and(n, count) != 0,
                             functools.partial(run, start, count), lambda c: c, carry)
            count //= 2
        return carry

    for h in range(N_HEADS_IDX):
        qs_ref[h * tq:(h + 1) * tq, :] = qi_ref[:, h * HEAD_DIM_IDX:(h + 1) * HEAD_DIM_IDX]
    w_t = smq_ref[...].T
    q_pos = lax.broadcasted_iota(jnp.int32, (1, tq), 1) + t0
    key_iota = lax.broadcasted_iota(jnp.int32, (kc, tq), 0)

    def score_chunk(c, carry):
        mx, mn = carry
        kk = smk_ref[rows(c), SM_KIDX:SM_KIDX + HEAD_DIM_IDX].astype(BF16)
        dots = lax.dot_general(kk, qs_ref[...], (((1,), (1,)), ((), ())), preferred_element_type=F32)
        sc = jnp.zeros((kc, tq), F32)
        for h in range(N_HEADS_IDX):
            sc = sc + w_t[SM_WIDX + h:SM_WIDX + h + 1, :] * jnp.maximum(dots[:, lanes(h)], 0.0)
        vis = (key_iota + c * kc) <= q_pos
        sc_ref[rows(c), :] = jnp.where(vis, sc, NEG_INF)
        mx = jnp.maximum(mx, jnp.max(jnp.where(vis, sc, NEG_INF), axis=0, keepdims=True))
        mn = jnp.minimum(mn, jnp.min(jnp.where(vis, sc, jnp.inf), axis=0, keepdims=True))
        return mx, mn

    mx, mn = chunk_loop(n_ch, score_chunk,
                        (jnp.full((1, tq), NEG_INF, F32), jnp.full((1, tq), jnp.inf, F32)), MXU_CHUNKS_PER_TRIP)

    n_vis = (q_pos + 1).astype(F32)
    k_row = jnp.minimum(n_vis, float(topk))

    def count(pred):
        def body(c, a):
            hit = pred(sc_ref[rows(c), :]).astype(F32)
            return a + jnp.sum(hit.reshape(kc // SUBLANES, SUBLANES, tq), axis=0)
        a = chunk_loop(n_ch, body, jnp.zeros((SUBLANES, tq), F32))
        return jnp.sum(a, axis=0, keepdims=True)

    def any_lane(flag):
        return jnp.max(jnp.where(flag, 1.0, 0.0)) > 0.0

    def crowded(cnt_lo, c_hi):
        return any_lane(jnp.logical_and(cnt_lo != k_row, cnt_lo - c_hi > PEEL_BRACKET))

    def bisect_cond(carry):
        it, lo, hi, cnt_lo, c_hi = carry
        return jnp.logical_and(it < BISECT_MAX_CHECKS, crowded(cnt_lo, c_hi))

    def bisect_body(carry):
        it, lo, hi, cnt_lo, c_hi = carry
        for _ in range(BISECT_STEPS_PER_CHECK):
            mid = lo * 0.5 + hi * 0.5
            cnt = count(lambda s: s >= mid)
            ge = cnt >= k_row
            lo, cnt_lo = jnp.where(ge, mid, lo), jnp.where(ge, cnt, cnt_lo)
            hi, c_hi = jnp.where(ge, hi, mid), jnp.where(ge, c_hi, cnt)
        return it + 1, lo, hi, cnt_lo, c_hi

    hi0 = mx + jnp.maximum(jnp.abs(mx), 1e-30) * 1e-6
    _, lo, hi, cnt_lo, c_hi = lax.while_loop(
        bisect_cond, bisect_body, (jnp.int32(0), mn, hi0, n_vis, jnp.zeros((1, tq), F32)))

    def peel_cond(carry):
        it, lo, hi, cnt_lo, c_hi, done = carry
        return jnp.logical_and(it < seq_keys, any_lane(done == 0.0))

    def peel_body(carry):
        it, lo, hi, cnt_lo, c_hi, done = carry

        def top_body(c, v):
            s = sc_ref[rows(c), :]
            inside = jnp.logical_and(s >= lo, s < hi)
            return jnp.maximum(v, jnp.max(jnp.where(inside, s, NEG_INF), axis=0, keepdims=True))

        v = chunk_loop(n_ch, top_body, jnp.full((1, tq), NEG_INF, F32))
        c_v = count(lambda s: s >= v)
        reached = c_v >= k_row
        live = done == 0.0
        fin = jnp.logical_and(live, reached)
        cut = jnp.logical_and(live, jnp.logical_not(reached))
        return (it + 1, jnp.where(fin, v, lo), jnp.where(cut, v, hi), jnp.where(fin, c_v, cnt_lo),
                jnp.where(cut, c_v, c_hi), jnp.where(fin, 1.0, done))

    _, lo, hi, cnt_lo, c_hi, _ = lax.while_loop(
        peel_cond, peel_body,
        (jnp.int32(0), lo, hi, cnt_lo, c_hi, jnp.where(cnt_lo == k_row, 1.0, 0.0)))

    def mask_chunk(c, _):
        am_ref[rows(c), :] = jnp.where(sc_ref[rows(c), :] >= lo, 0.0, NEG_INF)
        return 0

    chunk_loop(n_ch, mask_chunk, 0)

    tied = cnt_lo != k_row
    for part in range(tq // LANES):
        ls = slice(part * LANES, (part + 1) * LANES)

        @pl.when(any_lane(tied[:, ls]))
        def _ties():
            need, lo_p, hi_p = (k_row - c_hi)[:, ls], lo[:, ls], hi[:, ls]
            lower = (lax.broadcasted_iota(jnp.int32, (kc, kc), 1)
                     < lax.broadcasted_iota(jnp.int32, (kc, kc), 0)).astype(BF16)

            def tie_chunk(c, before):
                s = sc_ref[rows(c), ls]
                above = s >= hi_p
                tie = jnp.logical_and(s >= lo_p, jnp.logical_not(above))
                tie_f = tie.astype(F32)
                rank = jnp.dot(lower, tie_f.astype(BF16), preferred_element_type=F32) + before
                sel = jnp.logical_or(above, jnp.logical_and(tie, rank < need))
                am_ref[rows(c), ls] = jnp.where(sel, 0.0, NEG_INF)
                return before + jnp.sum(tie_f, axis=0, keepdims=True)

            chunk_loop(n_ch, tie_chunk, jnp.zeros((1, LANES), F32))

    head_of_lane = lax.broadcasted_iota(jnp.int32, (tq, LANES), 1) // HEAD_DIM_A
    for pair in range(nh // HEADS_PER_DOT):
        q_tile = qa_ref[:, pair * LANES:(pair + 1) * LANES]
        for slot in range(HEADS_PER_DOT):
            ql_ref[pair, slot * tq:(slot + 1) * tq, :] = jnp.where(head_of_lane == slot, q_tile, 0)

    m_ref[...] = jnp.full(m_ref.shape, NEG_INF, F32)

    def pass_a(c, table):
        am = am_ref[rows(c), :]
        for pair in range(nh // HEADS_PER_DOT):
            raw = lax.dot_general(ka_ref[rows(c), pair * LANES:(pair + 1) * LANES], ql_ref[pair],
                                  (((1,), (1,)), ((), ())), preferred_element_type=F32)
            for slot in range(HEADS_PER_DOT):
                h = pair * HEADS_PER_DOT + slot
                x = raw[:, slot * tq:(slot + 1) * tq] + am
                if table is not None:
                    x = x + tz_ref[table, h]
                x_ref[rows(c), lanes(h)] = x
                m_ref[:, lanes(h)] = jnp.maximum(m_ref[:, lanes(h)], jnp.max(x, axis=0, keepdims=True))

    def far_chunk(c, _):
        pass_a(c, None)
        return 0

    chunk_loop(jnp.maximum(qb - 1, 0), far_chunk, 0, MXU_CHUNKS_PER_TRIP)

    @pl.when(qb >= 1)
    def _prev():
        pass_a(qb - 1, 1)

    pass_a(qb, 0)

    def pv(c):
        p = jnp.exp2((x_ref[rows(c), :] - m_ref[...]).astype(BF16))
        return jnp.dot(ckvt_ref[:, rows(c)], p, preferred_element_type=F32)

    def pass_b(c, _):
        acc_ref[...] += pv(c)
        return 0

    acc_ref[...] = pv(qb)
    chunk_loop(qb, pass_b, 0, MXU_CHUNKS_PER_TRIP)
    inv_l = 1.0 / acc_ref[KV_RANK:KV_RANK + 1, :]
    o_lat = (acc_ref[0:KV_RANK, :] * inv_l).astype(BF16)
    outs = [jnp.dot(wuvt_ref[h], o_lat[:, lanes(h)], preferred_element_type=F32) for h in range(nh)]
    oa_ref[...] = jnp.concatenate(outs, axis=0).T.astype(BF16)


def _dsa(qi, sm, qa, ka, ckv_t, wuv_t, tz, batch, seq):
    tq = Q_TILE
    nq = seq // tq
    topk = min(TOPK_MAX, seq // 4)
    T = batch * seq
    kern = functools.partial(_dsa_kernel, topk=topk)
    blk_q = lambda n: pl.BlockSpec((tq, n), lambda b, q: (b * nq + q, 0))
    blk_s = lambda n: pl.BlockSpec((seq, n), lambda b, q: (b, 0))
    in_specs = [
        blk_q(WIDTH_IDX), blk_q(LANES), blk_s(LANES),
        blk_q(WIDTH_A), blk_s(WIDTH_A),
        pl.BlockSpec((None, KV_RANK + ONES_ROWS, seq), lambda b, q: (b, 0, 0)),
        _full(wuv_t.shape), _full(tz.shape),
    ]
    scratch = [
        pltpu.VMEM((N_HEADS_IDX * tq, HEAD_DIM_IDX), BF16),
        pltpu.VMEM((seq, tq), F32),
        pltpu.VMEM((seq, tq), F32),
        pltpu.VMEM((N_HEADS_A // HEADS_PER_DOT, HEADS_PER_DOT * tq, LANES), BF16),
        pltpu.VMEM((seq, N_HEADS_A * tq), F32),
        pltpu.VMEM((KV_RANK + ONES_ROWS, N_HEADS_A * tq), F32),
        pltpu.VMEM((1, N_HEADS_A * tq), F32),
    ]
    return pl.pallas_call(
        kern, grid=(batch, nq), in_specs=in_specs,
        out_specs=pl.BlockSpec((tq, WIDTH_A), lambda b, q: (b * nq + q, 0)),
        out_shape=jax.ShapeDtypeStruct((T, WIDTH_A), BF16),
        scratch_shapes=scratch, compiler_params=_cparams(2), name="dsa")(
            qi, sm, sm, qa, ka, ckv_t, wuv_t, tz)


def _t5_bucket(dist):
    max_exact = N_BUCKETS // 2
    d = jnp.maximum(dist, 0)
    ratio = jnp.log(jnp.maximum(d, 1).astype(F32) / max_exact) / math.log(MAX_DISTANCE / max_exact)
    large = jnp.minimum(max_exact + (ratio * (N_BUCKETS - max_exact)).astype(jnp.int32), N_BUCKETS - 1)
    return jnp.where(d < max_exact, d, large)


def _bias_tables(rel_bias):
    tq = Q_TILE
    span = 2 * tq
    assert int(_np_bucket(tq)) == N_BUCKETS - 1
    far = rel_bias[N_BUCKETS - 1]
    by_dist = ((rel_bias[_t5_bucket(jnp.arange(span))] - far).astype(F32) * LOG2E).T
    diag = jnp.concatenate([by_dist[:, :tq], jnp.zeros_like(by_dist[:, :tq])], axis=1)
    prev = jnp.concatenate([by_dist[:, tq:], by_dist[:, :tq]], axis=1)

    def toeplitz(f):
        m = jnp.tile(f, (1, tq))[:, :tq * (span - 1)].reshape(N_HEADS_A, tq, span - 1)
        return m[:, :, :tq]

    return jnp.stack([toeplitz(diag), toeplitz(prev)])


def _np_bucket(d):
    max_exact = N_BUCKETS // 2
    ratio = np.log(np.float32(max(d, 1)) / np.float32(max_exact)) / math.log(MAX_DISTANCE / max_exact)
    return min(max_exact + int(ratio * (N_BUCKETS - max_exact)), N_BUCKETS - 1) if d >= max_exact else d


def _mlstm_kernel(qk_ref, halo_ref, v_ref, sm_ref, o_ref, cw_ref, cb_ref, g_ref,
                  out_ref, c_ref, m_ref):
    c_idx = pl.program_id(1)

    @pl.when(c_idx == 0)
    def _init():
        c_ref[...] = jnp.zeros_like(c_ref)
        m_ref[...] = jnp.zeros_like(m_ref)

    for g in range(M_GROUPS):
        gates = _mlstm_gates(g, c_idx, qk_ref, halo_ref, sm_ref, cw_ref, cb_ref)
        outs = [_mlstm_head(g, h, gates, v_ref, o_ref, g_ref, c_ref, m_ref) for h in range(N_HEADS_M)]
        out_ref[g] = jnp.concatenate(outs, axis=-1).astype(BF16)


def _mlstm_gates(g, c_idx, qk_ref, halo_ref, sm_ref, cw_ref, cb_ref):
    L = M_CHUNK

    hw = halo_ref.shape[1]
    halo = jnp.where(c_idx > 0, halo_ref[g].astype(F32), 0.0)
    ext = jnp.concatenate([halo, qk_ref[g].astype(F32)], axis=0)
    acc = jnp.zeros((L, 2 * WIDTH_M), F32) + cb_ref[...]
    for w in range(CONV_WIDTH):
        off = hw - (CONV_WIDTH - 1) + w
        acc = acc + ext[off:off + L, :] * cw_ref[w:w + 1, :]
    qk = acc * _sigmoid(acc)

    sm = sm_ref[g]
    sm_t = sm.T
    r_i = lax.broadcasted_iota(jnp.int32, (L, L), 0)
    c_i = lax.broadcasted_iota(jnp.int32, (L, L), 1)
    tril = (c_i <= r_i).astype(BF16)
    bcum_c = sum(jnp.dot(tril, part, preferred_element_type=F32)
                 for part in _split3(jax.nn.log_sigmoid(sm)))
    return qk, sm, sm_t, bcum_c, bcum_c.T, c_i <= r_i


def _mlstm_head(g, h, gates, v_ref, o_ref, g_ref, c_ref, m_ref):
    L = M_CHUNK
    dm = HEAD_DIM_M
    qk, sm, sm_t, bcum_c, bcum_r, causal = gates
    ones_col = (lax.broadcasted_iota(jnp.int32, (L, dm), 1) == 0).astype(BF16)
    st = g * N_HEADS_M + h
    q = qk[:, h * dm:(h + 1) * dm].astype(BF16)
    k = (qk[:, WIDTH_M + h * dm:WIDTH_M + (h + 1) * dm] * (dm ** -0.5))
    v_aug = jnp.concatenate([v_ref[g, :, h * dm:(h + 1) * dm], ones_col], axis=-1)
    b_col = bcum_c[:, SM_F + h:SM_F + h + 1]
    g_col = sm[:, SM_I + h:SM_I + h + 1] - b_col
    g_row = sm_t[SM_I + h:SM_I + h + 1, :] - bcum_r[SM_F + h:SM_F + h + 1, :]
    b_last = b_col[L - 1:L, :]
    m_prev = m_ref[st]
    c_prev = c_ref[st]

    log_d = jnp.where(causal, b_col + g_row, NEG_INF)
    m_j = jnp.maximum(b_col + m_prev, jnp.max(log_d, axis=-1, keepdims=True))
    w_inter = jnp.exp(b_col + m_prev - m_j)
    qkt = lax.dot_general(q, k.astype(BF16), (((1,), (1,)), ((), ())), preferred_element_type=F32)
    s = qkt * jnp.exp(log_d - m_j)
    o_aug = jnp.dot(s.astype(BF16), v_aug, preferred_element_type=F32) + \
        w_inter * jnp.dot(q, c_prev.astype(BF16), preferred_element_type=F32)
    num = o_aug[:, :dm]
    den = o_aug[:, dm:dm + 1]
    hh = num / jnp.maximum(jnp.abs(den), jnp.exp(-m_j))

    lwe = b_last + g_col
    m_loc = jnp.max(lwe, axis=0, keepdims=True)
    kw = (k * jnp.exp(lwe - m_loc)).astype(BF16)
    c_loc = lax.dot_general(kw, v_aug, (((0,), (0,)), ((), ())), preferred_element_type=F32)
    m_new = jnp.maximum(b_last + m_prev, m_loc)
    c_ref[st] = jnp.exp(b_last + m_prev - m_new) * c_prev + jnp.exp(m_loc - m_new) * c_loc
    m_ref[st] = m_new

    hn = hh * lax.rsqrt(jnp.mean(hh * hh, axis=-1, keepdims=True) + LN_EPS) * g_ref[:, h * dm:(h + 1) * dm]
    og = o_ref[g, :, h * dm:(h + 1) * dm].astype(F32)
    return _sigmoid(og) * hn


def _mlstm(qk, v, sm, o, conv_w, conv_b, mh_g, batch, seq):
    L = M_CHUNK
    G = M_GROUPS
    nc = seq // L
    T = batch * seq
    hb = L // CONV_HALO
    grouped = lambda a: a.reshape(G, T // G, a.shape[-1])
    blk = lambda n: pl.BlockSpec((G, L, n), lambda b, c: (0, b * nc + c, 0))
    in_specs = [
        blk(2 * WIDTH_M),
        pl.BlockSpec((G, CONV_HALO, 2 * WIDTH_M), lambda b, c: (0, jnp.maximum((b * nc + c) * hb - 1, 0), 0)),
        blk(WIDTH_M), blk(LANES), blk(WIDTH_M),
        _full(conv_w.shape), _full(conv_b.shape), _full(mh_g.shape),
    ]
    scratch = [pltpu.VMEM((G * N_HEADS_M, HEAD_DIM_M, 2 * HEAD_DIM_M), F32),
               pltpu.VMEM((G * N_HEADS_M, 1, 1), F32)]
    qk_g = grouped(qk)
    out = pl.pallas_call(
        _mlstm_kernel, grid=(batch // G, nc), in_specs=in_specs, out_specs=blk(WIDTH_M),
        out_shape=jax.ShapeDtypeStruct((G, T // G, WIDTH_M), BF16), scratch_shapes=scratch,
        compiler_params=_cparams(2), name="mlstm")(
            qk_g, qk_g, grouped(v), grouped(sm), grouped(o), conv_w, conv_b, mh_g)
    return out.reshape(T, WIDTH_M)


def _layer_norm(y, g, b):
    mu = jnp.mean(y, axis=-1, keepdims=True)
    var = jnp.mean(jnp.square(y - mu), axis=-1, keepdims=True)
    return (y - mu) * lax.rsqrt(var + LN_EPS) * g + b


def _router_logits(x1, wr, br):
    x1_hi = x1.astype(BF16)
    x1_lo = (x1 - x1_hi.astype(F32)).astype(BF16)
    both = jnp.dot(x1_hi, wr[...], preferred_element_type=F32)
    return (both[:, 0:LANES]
            + jnp.dot(x1_lo, wr[:, 0:LANES], preferred_element_type=F32)
            + both[:, LANES:2 * LANES]) + br[...]


def _merge_kernel(x_ref, oa_ref, hm_ref, ga_ref, gm_ref, wua, wum, wout, l1g, l1b, wr, br,
                  x1_ref, route_ref, cnt_ref):
    pa = jnp.dot(oa_ref[...], wua[...], preferred_element_type=F32)
    pm = jnp.dot(hm_ref[...], wum[...], preferred_element_type=F32)
    y = _sigmoid(ga_ref[...]) * pa.astype(BF16) + _sigmoid(gm_ref[...]) * pm.astype(BF16)
    mix = jnp.dot(y.astype(BF16), wout[...], preferred_element_type=F32)
    x1 = _layer_norm(ALPHA * x_ref[...] + mix, l1g[...], l1b[...])
    x1_ref[:, 0:D_MODEL] = x1

    lt = _router_logits(x1, wr, br).T
    tm = lt.shape[1]
    big = jnp.int32(LANES)
    le = lt[0:N_EXPERTS, :]
    lg = lt[N_EXPERTS:N_EXPERTS + SUBLANES, :]
    row_g = lax.broadcasted_iota(jnp.int32, lg.shape, 0)
    row_e = lax.broadcasted_iota(jnp.int32, le.shape, 0)
    is_grp = row_g < N_GROUPS
    gl = jnp.where(is_grp, lg, NEG_INF)
    ge = jnp.exp(gl - jnp.max(gl, axis=0, keepdims=True))
    gp = ge / jnp.sum(ge, axis=0, keepdims=True)
    g_w = jnp.max(gp, axis=0, keepdims=True)
    g_idx = jnp.min(jnp.where(jnp.logical_and(is_grp, gp == g_w), row_g, big), axis=0, keepdims=True)
    el = jnp.where(jnp.right_shift(row_e, EPG_SHIFT) == g_idx, le, NEG_INF)
    m1 = jnp.max(el, axis=0, keepdims=True)
    i1 = jnp.min(jnp.where(el == m1, row_e, big), axis=0, keepdims=True)
    el2 = jnp.where(row_e == i1, NEG_INF, el)
    m2 = jnp.max(el2, axis=0, keepdims=True)
    i2 = jnp.min(jnp.where(el2 == m2, row_e, big), axis=0, keepdims=True)

    e2 = jnp.exp(m2 - m1)
    w1 = g_w / (1.0 + e2)
    w2 = g_w * e2 / (1.0 + e2)
    first_is_a = i1 < i2
    row_w = lax.broadcasted_iota(jnp.int32, (LANES, tm), 0)
    w_rows = jnp.where(row_w == 0, jnp.where(first_is_a, w1, w2),
                       jnp.where(row_w == 1, jnp.where(first_is_a, w2, w1), 0.0))
    x1_ref[:, D_MODEL:D_MODEL + LANES] = w_rows.T

    a = jnp.bitwise_and(jnp.minimum(i1, i2), EXPERTS_PER_GROUP - 1)
    b = jnp.bitwise_and(jnp.maximum(i1, i2), EXPERTS_PER_GROUP - 1)
    pair = jnp.right_shift(a * (2 * EXPERTS_PER_GROUP - 1 - a), 1) + (b - a - 1)
    cls = g_idx * PAIRS_PER_GROUP + pair
    row_c = lax.broadcasted_iota(jnp.int32, (CLASS_ROWS, tm), 0)
    onehot = (row_c == cls).astype(F32)

    @pl.when(pl.program_id(0) == 0)
    def _init():
        cnt_ref[...] = jnp.zeros_like(cnt_ref)

    rb = min(RANK_BLOCK, tm)
    blocks = [onehot[:, j * rb:(j + 1) * rb] for j in range(tm // rb)]
    earlier = (lax.broadcasted_iota(jnp.int32, (rb, rb), 0)
               < lax.broadcasted_iota(jnp.int32, (rb, rb), 1)).astype(BF16)
    local = jnp.dot(jnp.concatenate(blocks, axis=0).astype(BF16), earlier, preferred_element_type=F32)
    before = cnt_ref[:, 0:1]
    ranks = []
    for j, blk in enumerate(blocks):
        prior = local[j * CLASS_ROWS:(j + 1) * CLASS_ROWS, :] + before
        ranks.append(jnp.sum(prior * blk, axis=0, keepdims=True))
        before = before + jnp.sum(blk, axis=1, keepdims=True)
    rank = jnp.concatenate(ranks, axis=1)
    cnt_ref[...] = jnp.broadcast_to(before, cnt_ref.shape)
    row_o = lax.broadcasted_iota(jnp.int32, route_ref.shape, 0)
    route_ref[...] = jnp.where(row_o == 0, cls.astype(F32), jnp.where(row_o == 1, rank, 0.0))


def _merge(x2, oa, hm, ga, gm, wua, wum, wout, l1g, l1b, wr, br, tm):
    T = x2.shape[0]
    blk = lambda n: pl.BlockSpec((tm, n), lambda i: (i, 0))
    in_specs = [blk(D_MODEL), blk(WIDTH_A), blk(WIDTH_M), blk(D_MODEL), blk(D_MODEL),
                _full(wua.shape), _full(wum.shape), _full(wout.shape), _full(l1g.shape), _full(l1b.shape),
                _full(wr.shape), _full(br.shape)]
    out_specs = [blk(D_MODEL + LANES), pl.BlockSpec((SUBLANES, tm), lambda i: (0, i)),
                 pl.BlockSpec((CLASS_ROWS, LANES), lambda i: (0, 0))]
    out_shape = [jax.ShapeDtypeStruct((T, D_MODEL + LANES), F32), jax.ShapeDtypeStruct((SUBLANES, T), F32),
                 jax.ShapeDtypeStruct((CLASS_ROWS, LANES), F32)]
    return pl.pallas_call(
        _merge_kernel, grid=(T // tm,), in_specs=in_specs, out_specs=out_specs, out_shape=out_shape,
        compiler_params=_cparams(1), name="merge")(x2, oa, hm, ga, gm, wua, wum, wout, l1g, l1b, wr, br)


def _sc_mesh():
    return plsc.VectorSubcoreMesh(core_axis_name="c", subcore_axis_name="s",
                                  num_cores=SC_CORES, num_subcores=SC_SUBCORES)


def _sc_chunks(n_rows):
    workers = SC_CORES * SC_SUBCORES
    assert n_rows % (workers * SC_ROWS * 2) == 0
    return n_rows // (workers * SC_ROWS)


def _sc_scratch(n_chunks, width, dtype):
    return [pltpu.VMEM((n_chunks, SC_ROWS), jnp.int32),
            pltpu.VMEM((SC_ROWS, width), dtype), pltpu.VMEM((SC_ROWS, width), dtype),
            pltpu.SemaphoreType.DMA, pltpu.SemaphoreType.DMA]


def _sc_scatter_rows(rows, idx, n_out):
    n_in, width = rows.shape
    n_chunks = _sc_chunks(n_in)

    @functools.partial(
        pl.kernel, mesh=_sc_mesh(), out_type=jax.ShapeDtypeStruct((n_out, width), rows.dtype),
        scratch_types=_sc_scratch(n_chunks, width, rows.dtype), name="sc_dispatch")
    def scatter(rows_hbm, idx_hbm, out_hbm, idx_v, rows_a, rows_b, sem_a, sem_b):
        first = (lax.axis_index("s") * SC_CORES + lax.axis_index("c")) * n_chunks
        pltpu.sync_copy(idx_hbm.at[pl.ds(first, n_chunks)], idx_v)

        def load(c, buf):
            pltpu.sync_copy(rows_hbm.at[pl.ds((first + c) * SC_ROWS, SC_ROWS)], buf)

        def put(c, buf, sem):
            return pltpu.make_async_copy(buf, out_hbm.at[idx_v.at[c]], sem)

        load(0, rows_a)
        put(0, rows_a, sem_a).start()

        @pl.loop(0, n_chunks, step=2)
        def _(j):
            load(j + 1, rows_b)
            put(j + 1, rows_b, sem_b).start()
            put(j, rows_a, sem_a).wait()

            @pl.when(j + 2 < n_chunks)
            def _():
                load(j + 2, rows_a)
                put(j + 2, rows_a, sem_a).start()

            put(j + 1, rows_b, sem_b).wait()

    return scatter(rows, idx.reshape(n_in // SC_ROWS, SC_ROWS))


def _sc_gather_rows(table, idx):
    n_out, width = idx.shape[0], table.shape[1]
    n_chunks = _sc_chunks(n_out)

    @functools.partial(
        pl.kernel, mesh=_sc_mesh(), out_type=jax.ShapeDtypeStruct((n_out, width), table.dtype),
        scratch_types=_sc_scratch(n_chunks, width, table.dtype), name="sc_combine")
    def gather(table_hbm, idx_hbm, out_hbm, idx_v, rows_a, rows_b, sem_a, sem_b):
        first = (lax.axis_index("s") * SC_CORES + lax.axis_index("c")) * n_chunks
        pltpu.sync_copy(idx_hbm.at[pl.ds(first, n_chunks)], idx_v)

        def fetch(c, buf, sem):
            return pltpu.make_async_copy(table_hbm.at[idx_v.at[c]], buf, sem)

        def store(c, buf):
            pltpu.sync_copy(buf, out_hbm.at[pl.ds((first + c) * SC_ROWS, SC_ROWS)])

        fetch(0, rows_a, sem_a).start()

        @pl.loop(0, n_chunks, step=2)
        def _(j):
            fetch(j + 1, rows_b, sem_b).start()
            fetch(j, rows_a, sem_a).wait()
            store(j, rows_a)

            @pl.when(j + 2 < n_chunks)
            def _():
                fetch(j + 2, rows_a, sem_a).start()

            fetch(j + 1, rows_b, sem_b).wait()
            store(j + 1, rows_b)

    return gather(table, idx.reshape(n_out // SC_ROWS, SC_ROWS))


def _moe_kernel(ta_ref, tb_ref, nu_ref, xs_ref, wga, wua, wda, wgb, wub, wdb, l2g, l2b, ys_ref,
                pre_ref):
    i = pl.program_id(0)
    n_used = nu_ref[0]
    tm = pre_ref.shape[0]
    pr = tm // MOE_NORM_PIECES

    def norm_rows(piece):
        rs = slice(piece * pr, (piece + 1) * pr)
        ys_ref[rs, :] = _layer_norm(pre_ref[rs, :], l2g[...], l2b[...])

    @pl.when(i == 0)
    def _first():
        pre_ref[...] = jnp.zeros_like(pre_ref)

    @pl.when(i < n_used)
    def _compute():
        x = xs_ref[:, 0:D_MODEL]
        xb = x.astype(BF16)
        w_a = xs_ref[:, D_MODEL:D_MODEL + 1]
        w_b = xs_ref[:, D_MODEL + 1:D_MODEL + 2]
        piece = iter(range(MOE_NORM_PIECES))

        def dot_pieces(lhs, w):
            outs = []
            for n in range(w.shape[1] // MOE_DOT_COLS):
                outs.append(jnp.dot(lhs, w[:, n * MOE_DOT_COLS:(n + 1) * MOE_DOT_COLS],
                                    preferred_element_type=F32))
                norm_rows(next(piece))
            return jnp.concatenate(outs, axis=1)

        def expert(wg, wu, wd):
            g = dot_pieces(xb, wg)
            u = dot_pieces(xb, wu)
            hdn = (g * _sigmoid(g) * u).astype(BF16)
            return dot_pieces(hdn, wd)

        ffn = w_a * expert(wga, wua, wda) + w_b * expert(wgb, wub, wdb)
        pre_ref[...] = ALPHA * x + ffn

    @pl.when(i == n_used)
    def _last():
        for piece in range(MOE_NORM_PIECES):
            norm_rows(piece)


def _moe(xs, tile_a, tile_b, n_used, wg, wu, wd, l2g, l2b):
    tm = MOE_TILE
    n_tiles = xs.shape[0] // tm
    used = lambda i, nu: jnp.minimum(i, nu[0] - 1)
    rows = lambda n: pl.BlockSpec((tm, n), lambda i, ta, tb, nu: (used(i, nu), 0))
    prev_rows = pl.BlockSpec((tm, D_MODEL), lambda i, ta, tb, nu: (used(jnp.maximum(i - 1, 0), nu), 0))
    w_in = lambda which: pl.BlockSpec(
        (None, D_MODEL, D_EXPERT), lambda i, ta, tb, nu: ((ta, tb)[which][used(i, nu)], 0, 0))
    w_out = lambda which: pl.BlockSpec(
        (None, D_EXPERT, D_MODEL), lambda i, ta, tb, nu: ((ta, tb)[which][used(i, nu)], 0, 0))
    const = lambda shape: pl.BlockSpec(shape, lambda i, ta, tb, nu: (0,) * len(shape))
    grid_spec = pltpu.PrefetchScalarGridSpec(
        num_scalar_prefetch=3, grid=(n_tiles + 1,),
        in_specs=[rows(xs.shape[1]), w_in(0), w_in(0), w_out(0), w_in(1), w_in(1), w_out(1),
                  const(l2g.shape), const(l2b.shape)],
        out_specs=prev_rows,
        scratch_shapes=[pltpu.VMEM((tm, D_MODEL), F32)])
    return pl.pallas_call(
        _moe_kernel, grid_spec=grid_spec, out_shape=jax.ShapeDtypeStruct((xs.shape[0], D_MODEL), F32),
        compiler_params=_cparams(1), name="moe")(
            tile_a, tile_b, n_used, xs, wg, wu, wd, wg, wu, wd, l2g, l2b)


def _route_tables(route, counts, n_tokens):
    tm = MOE_TILE
    n_tiles = n_tokens // tm + N_CLASSES
    cnt = counts[:N_CLASSES, 0].astype(jnp.int32)
    tiles = (cnt + tm - 1) // tm
    tile_end = jnp.cumsum(tiles)
    tile_start = tile_end - tiles
    t_idx = jnp.arange(n_tiles, dtype=jnp.int32)
    cls_of_tile = jnp.minimum(jnp.sum(t_idx[:, None] >= tile_end[None, :], axis=1), N_CLASSES - 1).astype(jnp.int32)
    classes = np.arange(N_CLASSES)
    tile_is = cls_of_tile[:, None] == classes[None, :]
    per_tile = lambda table: jnp.sum(jnp.where(tile_is, jnp.asarray(table, jnp.int32)[None, :], 0), axis=1)
    first_expert = classes // PAIRS_PER_GROUP * EXPERTS_PER_GROUP
    tile_a = per_tile(first_expert + np.asarray(PAIR_A)[classes % PAIRS_PER_GROUP])
    tile_b = per_tile(first_expert + np.asarray(PAIR_B)[classes % PAIRS_PER_GROUP])
    cls = route[0].astype(jnp.int32)
    rank = route[1].astype(jnp.int32)
    row0 = jnp.sum(jnp.where(cls[:, None] == jnp.arange(N_CLASSES)[None, :], (tile_start * tm)[None, :], 0), axis=1)
    n_used = jnp.maximum(tile_end[-1:], 1).astype(jnp.int32)
    return tile_a, tile_b, n_used, row0 + rank, n_tiles * tm


def _pick_tile(T, pref):
    t = pref
    while T % t:
        t //= 2
    return t


def kernel(x, w_in, conv_w, conv_b, kv_norm_g, w_uk, w_uv, rel_bias, b_i, b_f, mh_norm_g, w_up_a, w_up_m,
           w_out, ln1_g, ln1_b, w_grp, b_grp, w_rt, b_rt, w_gate, w_up, w_down, ln2_g, ln2_b):
    B, S, _ = x.shape
    T = B * S
    assert S % Q_TILE == 0 and S % M_CHUNK == 0 and T % MOE_TILE == 0 and w_in.shape[0] == DEPTH
    tz = _bias_tables(rel_bias)
    x2 = x.reshape(T, D_MODEL)
    for l in range(DEPTH):
        w = w_in[l]
        o = np.cumsum((WIDTH_A, KV_RANK, WIDTH_IDX, HEAD_DIM_IDX, N_HEADS_IDX, 2 * WIDTH_M, WIDTH_M,
                       N_HEADS_M, N_HEADS_M, WIDTH_M, D_MODEL, D_MODEL)).tolist()
        o = [0] + o
        seg = lambda j: w[:, o[j]:o[j + 1]]
        pad = LANES - (HEAD_DIM_IDX + N_HEADS_IDX + 2 * N_HEADS_M)
        w_small = jnp.concatenate([seg(3), seg(4), seg(7), seg(8), jnp.zeros((D_MODEL, pad), w.dtype)], axis=1)
        ws = [seg(0), seg(1), seg(2), w_small, seg(5), seg(6), seg(9), seg(10), seg(11)]
        ws = [a.astype(BF16) for a in ws]
        smb = jnp.zeros((1, LANES), F32).at[0, SM_I:SM_I + N_HEADS_M].set(b_i[l]) \
            .at[0, SM_F:SM_F + N_HEADS_M].set(b_f[l])
        wuk = w_uk[l].reshape(KV_RANK, WIDTH_A).astype(BF16)
        qa, ka, qi, sm, qk, v, og, ga, gm, ckv_t = _proj(x2, ws, wuk, kv_norm_g[l][None, :], smb,
                                                           _pick_tile(S, PROJ_TILE), S)

        wuv_t = jnp.transpose(w_uv[l], (1, 2, 0)).astype(BF16)
        oa = _dsa(qi, sm, qa, ka, ckv_t, wuv_t, tz, B, S)

        hm = _mlstm(qk, v, sm, og, conv_w[l], conv_b[l][None, :], mh_norm_g[l].reshape(1, WIDTH_M), B, S)

        w_router = jnp.concatenate(
            [w_rt[l], w_grp[l], jnp.zeros((D_MODEL, LANES - N_EXPERTS - N_GROUPS), F32)], axis=1)
        b_router = jnp.concatenate(
            [b_rt[l], b_grp[l], jnp.zeros((LANES - N_EXPERTS - N_GROUPS,), F32)])[None, :]
        wr_hi = w_router.astype(BF16)
        wr_split = jnp.concatenate([wr_hi, (w_router - wr_hi.astype(F32)).astype(BF16)], axis=1)
        x1, route, counts = _merge(x2, oa, hm, ga, gm, w_up_a[l].astype(BF16), w_up_m[l].astype(BF16),
                                   w_out[l].astype(BF16), ln1_g[l][None, :], ln1_b[l][None, :],
                                   wr_split, b_router, _pick_tile(T, MERGE_TILE))

        tile_a, tile_b, n_used, pos, n_sorted = _route_tables(route, counts, T)
        xs = _sc_scatter_rows(x1, pos, n_sorted)
        ys = _moe(xs, tile_a, tile_b, n_used, w_gate[l].astype(BF16), w_up[l].astype(BF16),
                  w_down[l].astype(BF16), ln2_g[l][None, :], ln2_b[l][None, :])
        x2 = _sc_gather_rows(ys, pos)
    return x2.reshape(B, S, D_MODEL)
```

```python
import functools
import math

import jax
import jax.numpy as jnp
import numpy as np
from jax import lax
from jax.experimental import pallas as pl
from jax.experimental.pallas import tpu as pltpu
from jax.experimental.pallas import tpu_sc as plsc

F32 = jnp.float32
BF16 = jnp.bfloat16

D_MODEL = 1024
N_HEADS_A = 8
HEAD_DIM_A = 64
WIDTH_A = N_HEADS_A * HEAD_DIM_A
KV_RANK = 256
N_HEADS_IDX = 8
HEAD_DIM_IDX = 64
WIDTH_IDX = N_HEADS_IDX * HEAD_DIM_IDX
TOPK_MAX = 256
N_BUCKETS = 32
MAX_DISTANCE = 128
N_HEADS_M = 4
HEAD_DIM_M = 128
WIDTH_M = N_HEADS_M * HEAD_DIM_M
CONV_WIDTH = 4
N_GROUPS = 4
EXPERTS_PER_GROUP = 4
N_EXPERTS = N_GROUPS * EXPERTS_PER_GROUP
D_EXPERT = 512
LN_EPS = 1e-5
DEPTH = 1
ALPHA = (2.0 * DEPTH) ** 0.25

LANES = 128
SUBLANES = 8
VMEM_LIMIT = 56 * 1024 * 1024

SM_KIDX = 0
SM_WIDX = HEAD_DIM_IDX
SM_I = SM_WIDX + N_HEADS_IDX
SM_F = SM_I + N_HEADS_M

HEADS_PER_DOT = LANES // HEAD_DIM_A
assert HEADS_PER_DOT * HEAD_DIM_A == LANES and N_HEADS_A % HEADS_PER_DOT == 0
Q_TILE = 256
K_CHUNK = Q_TILE
MXU_CHUNKS_PER_TRIP = 4
ONES_ROWS = 16
V_ROWS = HEAD_DIM_A + ONES_ROWS
M_CHUNK = 128
M_GROUPS = 2
CONV_HALO = 16
assert CONV_HALO >= CONV_WIDTH - 1
PROJ_TILE = 1024
MERGE_TILE = 1024
RANK_BLOCK = 256
MOE_TILE = 512
MOE_DOT_COLS = 256
MOE_NORM_PIECES = 2 * (2 * D_EXPERT + D_MODEL) // MOE_DOT_COLS
SC_CORES = 2
SC_SUBCORES = 16
SC_ROWS = 32
EPG_SHIFT = EXPERTS_PER_GROUP.bit_length() - 1
assert 1 << EPG_SHIFT == EXPERTS_PER_GROUP
PAIR_A, PAIR_B = zip(*[(a, b) for a in range(EXPERTS_PER_GROUP) for b in range(a + 1, EXPERTS_PER_GROUP)])
PAIRS_PER_GROUP = len(PAIR_A)
N_CLASSES = N_GROUPS * PAIRS_PER_GROUP
CLASS_ROWS = -(-N_CLASSES // SUBLANES) * SUBLANES
BISECT_STEPS_PER_CHECK = 3
BISECT_MAX_CHECKS = 5
PEEL_BRACKET = 2.0
NEG_INF = float("-inf")
LOG2E = math.log2(math.e)


def _cparams(n_grid):
    return pltpu.CompilerParams(dimension_semantics=("arbitrary",) * n_grid,
                                vmem_limit_bytes=VMEM_LIMIT)


def _full(shape):
    nd = len(shape)
    return pl.BlockSpec(shape, lambda *_: (0,) * nd, pipeline_mode=pl.Buffered(1))


def _sigmoid(x):
    return 0.5 * jnp.tanh(0.5 * x) + 0.5


def _split3(x):
    hi = x.astype(BF16)
    r = x - hi.astype(F32)
    mid = r.astype(BF16)
    return hi, mid, (r - mid.astype(F32)).astype(BF16)


def _proj_kernel(x_ref, wqa, wckv, wqi, wsm, wqk, wv, wo, wga, wgm, wukv, kvg, smb,
                 qa_o, ka_o, qi_o, sm_o, qk_o, v_o, o_o, ga_o, gm_o, vat_o):
    xb = x_ref[...].astype(BF16)

    def mm(w):
        return jnp.dot(xb, w[...], preferred_element_type=F32)

    c = mm(wckv)
    qa_o[...] = (mm(wqa) * (HEAD_DIM_A ** -0.5 * LOG2E)).astype(BF16)
    c = c * lax.rsqrt(jnp.mean(c * c, axis=-1, keepdims=True) + LN_EPS) * kvg[...]
    qi_o[...] = mm(wqi).astype(BF16)
    kv = jnp.dot(c.astype(BF16), wukv[...], preferred_element_type=F32)
    ka_o[...] = kv[:, 0:WIDTH_A].astype(BF16)
    va_t = kv[:, WIDTH_A:2 * WIDTH_A].T.astype(BF16)
    for h in range(N_HEADS_A):
        r0 = h * V_ROWS
        vat_o[r0:r0 + HEAD_DIM_A, :] = va_t[h * HEAD_DIM_A:(h + 1) * HEAD_DIM_A, :]
        vat_o[r0 + HEAD_DIM_A:r0 + V_ROWS, :] = jnp.ones((ONES_ROWS, va_t.shape[1]), BF16)
    sm_o[...] = mm(wsm) + smb[...]
    qk_o[...] = mm(wqk).astype(BF16)
    v_o[...] = mm(wv).astype(BF16)
    o_o[...] = mm(wo).astype(BF16)
    ga_o[...] = mm(wga).astype(BF16)
    gm_o[...] = mm(wgm).astype(BF16)


def _proj(x2, ws, wukv, kvg, smb, tm, seq):
    T = x2.shape[0]
    per_seq = seq // tm

    def by_token(n, dt=BF16):
        return pl.BlockSpec((tm, n), lambda i: (i, 0)), jax.ShapeDtypeStruct((T, n), dt)

    def by_lane(n):
        return (pl.BlockSpec((None, n, tm), lambda i: (i // per_seq, 0, i % per_seq)),
                jax.ShapeDtypeStruct((T // seq, n, seq), BF16))

    outs = [by_token(WIDTH_A), by_token(WIDTH_A), by_token(WIDTH_IDX), by_token(LANES, F32)]
    outs += [by_token(w.shape[1]) for w in ws[4:]]
    outs.append(by_lane(N_HEADS_A * V_ROWS))
    in_specs = [pl.BlockSpec((tm, D_MODEL), lambda i: (i, 0))]
    in_specs += [_full(a.shape) for a in (*ws, wukv, kvg, smb)]
    return pl.pallas_call(
        _proj_kernel, grid=(T // tm,), in_specs=in_specs, out_specs=[o[0] for o in outs],
        out_shape=[o[1] for o in outs], compiler_params=_cparams(1), name="proj")(x2, *ws, wukv, kvg, smb)


def _dsa_kernel(qi_ref, smq_ref, smk_ref, qa_ref, ka_ref, vat_ref, tz_ref,
                oa_ref, qs_ref, sc_ref, am_ref, ql_ref, x_ref, acc_ref, m_ref, *, topk):
    tq, kc = Q_TILE, K_CHUNK
    nh = N_HEADS_A
    qb = pl.program_id(1)
    n_ch = qb + 1
    seq_keys = n_ch * kc
    t0 = qb * tq

    def rows(c):
        return pl.ds(pl.multiple_of(c * kc, kc), kc)

    def lanes(h):
        return slice(h * tq, (h + 1) * tq)

    def chunk_loop(n, body, init, per_trip=2):
        def run(first, count, carry):
            for r in range(count):
                carry = body(first + r, carry)
            return carry

        shift = per_trip.bit_length() - 1
        trips = jnp.right_shift(n, shift)
        carry = lax.fori_loop(0, trips, lambda p, c: run(p * per_trip, per_trip, c), init)
        done = trips * per_trip
        count = per_trip // 2
        while count:
            start = done + jnp.bitwise_and(n - done, -2 * count)
            carry = lax.cond(jnp.bitwise_and(n, count) != 0,
                             functools.partial(run, start, count), lambda c: c, carry)
            count //= 2
        return carry

    for h in range(N_HEADS_IDX):
        qs_ref[h * tq:(h + 1) * tq, :] = qi_ref[:, h * HEAD_DIM_IDX:(h + 1) * HEAD_DIM_IDX]
    w_t = smq_ref[...].T
    q_pos = lax.broadcasted_iota(jnp.int32, (1, tq), 1) + t0
    key_iota = lax.broadcasted_iota(jnp.int32, (kc, tq), 0)

    def score_chunk(c, carry):
        mx, mn = carry
        kk = smk_ref[rows(c), SM_KIDX:SM_KIDX + HEAD_DIM_IDX].astype(BF16)
        dots = lax.dot_general(kk, qs_ref[...], (((1,), (1,)), ((), ())), preferred_element_type=F32)
        sc = jnp.zeros((kc, tq), F32)
        for h in range(N_HEADS_IDX):
            sc = sc + w_t[SM_WIDX + h:SM_WIDX + h + 1, :] * jnp.maximum(dots[:, lanes(h)], 0.0)
        vis = (key_iota + c * kc) <= q_pos
        sc_ref[rows(c), :] = jnp.where(vis, sc, NEG_INF)
        mx = jnp.maximum(mx, jnp.max(jnp.where(vis, sc, NEG_INF), axis=0, keepdims=True))
        mn = jnp.minimum(mn, jnp.min(jnp.where(vis, sc, jnp.inf), axis=0, keepdims=True))
        return mx, mn

    mx, mn = chunk_loop(n_ch, score_chunk,
                        (jnp.full((1, tq), NEG_INF, F32), jnp.full((1, tq), jnp.inf, F32)), MXU_CHUNKS_PER_TRIP)

    n_vis = (q_pos + 1).astype(F32)
    k_row = jnp.minimum(n_vis, float(topk))

    def count(pred):
        def body(c, a):
            hit = pred(sc_ref[rows(c), :]).astype(F32)
            return a + jnp.sum(hit.reshape(kc // SUBLANES, SUBLANES, tq), axis=0)
        a = chunk_loop(n_ch, body, jnp.zeros((SUBLANES, tq), F32))
        return jnp.sum(a, axis=0, keepdims=True)

    def any_lane(flag):
        return jnp.max(jnp.where(flag, 1.0, 0.0)) > 0.0

    def crowded(cnt_lo, c_hi):
        return any_lane(jnp.logical_and(cnt_lo != k_row, cnt_lo - c_hi > PEEL_BRACKET))

    def bisect_cond(carry):
        it, lo, hi, cnt_lo, c_hi = carry
        return jnp.logical_and(it < BISECT_MAX_CHECKS, crowded(cnt_lo, c_hi))

    def bisect_body(carry):
        it, lo, hi, cnt_lo, c_hi = carry
        for _ in range(BISECT_STEPS_PER_CHECK):
            mid = lo * 0.5 + hi * 0.5
            cnt = count(lambda s: s >= mid)
            ge = cnt >= k_row
            lo, cnt_lo = jnp.where(ge, mid, lo), jnp.where(ge, cnt, cnt_lo)
            hi, c_hi = jnp.where(ge, hi, mid), jnp.where(ge, c_hi, cnt)
        return it + 1, lo, hi, cnt_lo, c_hi

    hi0 = mx + jnp.maximum(jnp.abs(mx), 1e-30) * 1e-6
    _, lo, hi, cnt_lo, c_hi = lax.while_loop(
        bisect_cond, bisect_body, (jnp.int32(0), mn, hi0, n_vis, jnp.zeros((1, tq), F32)))

    def peel_cond(carry):
        it, lo, hi, cnt_lo, c_hi, done = carry
        return jnp.logical_and(it < seq_keys, any_lane(done == 0.0))

    def peel_body(carry):
        it, lo, hi, cnt_lo, c_hi, done = carry

        def top_body(c, v):
            s = sc_ref[rows(c), :]
            inside = jnp.logical_and(s >= lo, s < hi)
            return jnp.maximum(v, jnp.max(jnp.where(inside, s, NEG_INF), axis=0, keepdims=True))

        v = chunk_loop(n_ch, top_body, jnp.full((1, tq), NEG_INF, F32))
        c_v = count(lambda s: s >= v)
        reached = c_v >= k_row
        live = done == 0.0
        fin = jnp.logical_and(live, reached)
        cut = jnp.logical_and(live, jnp.logical_not(reached))
        return (it + 1, jnp.where(fin, v, lo), jnp.where(cut, v, hi), jnp.where(fin, c_v, cnt_lo),
                jnp.where(cut, c_v, c_hi), jnp.where(fin, 1.0, done))

    _, lo, hi, cnt_lo, c_hi, _ = lax.while_loop(
        peel_cond, peel_body,
        (jnp.int32(0), lo, hi, cnt_lo, c_hi, jnp.where(cnt_lo == k_row, 1.0, 0.0)))

    def mask_chunk(c, _):
        am_ref[rows(c), :] = jnp.where(sc_ref[rows(c), :] >= lo, 0.0, NEG_INF)
        return 0

    chunk_loop(n_ch, mask_chunk, 0)

    tied = cnt_lo != k_row
    for part in range(tq // LANES):
        ls = slice(part * LANES, (part + 1) * LANES)

        @pl.when(any_lane(tied[:, ls]))
        def _ties():
            need, lo_p, hi_p = (k_row - c_hi)[:, ls], lo[:, ls], hi[:, ls]
            lower = (lax.broadcasted_iota(jnp.int32, (kc, kc), 1)
                     < lax.broadcasted_iota(jnp.int32, (kc, kc), 0)).astype(BF16)

            def tie_chunk(c, before):
                s = sc_ref[rows(c), ls]
                above = s >= hi_p
                tie = jnp.logical_and(s >= lo_p, jnp.logical_not(above))
                tie_f = tie.astype(F32)
                rank = jnp.dot(lower, tie_f.astype(BF16), preferred_element_type=F32) + before
                sel = jnp.logical_or(above, jnp.logical_and(tie, rank < need))
                am_ref[rows(c), ls] = jnp.where(sel, 0.0, NEG_INF)
                return before + jnp.sum(tie_f, axis=0, keepdims=True)

            chunk_loop(n_ch, tie_chunk, jnp.zeros((1, LANES), F32))

    head_of_lane = lax.broadcasted_iota(jnp.int32, (tq, LANES), 1) // HEAD_DIM_A
    for pair in range(nh // HEADS_PER_DOT):
        q_tile = qa_ref[:, pair * LANES:(pair + 1) * LANES]
        for slot in range(HEADS_PER_DOT):
            ql_ref[pair, slot * tq:(slot + 1) * tq, :] = jnp.where(head_of_lane == slot, q_tile, 0)

    m_ref[...] = jnp.full(m_ref.shape, NEG_INF, F32)

    def pass_a(c, table):
        am = am_ref[rows(c), :]
        for pair in range(nh // HEADS_PER_DOT):
            raw = lax.dot_general(ka_ref[rows(c), pair * LANES:(pair + 1) * LANES], ql_ref[pair],
                                  (((1,), (1,)), ((), ())), preferred_element_type=F32)
            for slot in range(HEADS_PER_DOT):
                h = pair * HEADS_PER_DOT + slot
                x = raw[:, slot * tq:(slot + 1) * tq] + am
                if table is not None:
                    x = x + tz_ref[table, h]
                x_ref[rows(c), lanes(h)] = x
                m_ref[:, lanes(h)] = jnp.maximum(m_ref[:, lanes(h)], jnp.max(x, axis=0, keepdims=True))

    def far_chunk(c, _):
        pass_a(c, None)
        return 0

    chunk_loop(jnp.maximum(qb - 1, 0), far_chunk, 0, MXU_CHUNKS_PER_TRIP)

    @pl.when(qb >= 1)
    def _prev():
        pass_a(qb - 1, 1)

    pass_a(qb, 0)

    def pv(c):
        p = jnp.exp2((x_ref[rows(c), :] - m_ref[...]).astype(BF16))
        return jnp.concatenate(
            [jnp.dot(vat_ref[h * V_ROWS:(h + 1) * V_ROWS, rows(c)], p[:, lanes(h)], preferred_element_type=F32)
             for h in range(nh)], axis=0)

    def pass_b(c, _):
        acc_ref[...] += pv(c)
        return 0

    acc_ref[...] = pv(qb)
    chunk_loop(qb, pass_b, 0, MXU_CHUNKS_PER_TRIP)
    outs = []
    for h in range(nh):
        blk = acc_ref[h * V_ROWS:(h + 1) * V_ROWS, :]
        outs.append(blk[0:HEAD_DIM_A, :] * (1.0 / blk[HEAD_DIM_A:HEAD_DIM_A + 1, :]))
    oa_ref[...] = jnp.concatenate(outs, axis=0).T.astype(BF16)


def _dsa(qi, sm, qa, ka, va_t, tz, batch, seq):
    tq = Q_TILE
    nq = seq // tq
    topk = min(TOPK_MAX, seq // 4)
    T = batch * seq
    kern = functools.partial(_dsa_kernel, topk=topk)
    blk_q = lambda n: pl.BlockSpec((tq, n), lambda b, q: (b * nq + q, 0))
    blk_s = lambda n: pl.BlockSpec((seq, n), lambda b, q: (b, 0))
    in_specs = [
        blk_q(WIDTH_IDX), blk_q(LANES), blk_s(LANES),
        blk_q(WIDTH_A), blk_s(WIDTH_A),
        pl.BlockSpec((None, N_HEADS_A * V_ROWS, seq), lambda b, q: (b, 0, 0)),
        _full(tz.shape),
    ]
    scratch = [
        pltpu.VMEM((N_HEADS_IDX * tq, HEAD_DIM_IDX), BF16),
        pltpu.VMEM((seq, tq), F32),
        pltpu.VMEM((seq, tq), F32),
        pltpu.VMEM((N_HEADS_A // HEADS_PER_DOT, HEADS_PER_DOT * tq, LANES), BF16),
        pltpu.VMEM((seq, N_HEADS_A * tq), F32),
        pltpu.VMEM((N_HEADS_A * V_ROWS, tq), F32),
        pltpu.VMEM((1, N_HEADS_A * tq), F32),
    ]
    return pl.pallas_call(
        kern, grid=(batch, nq), in_specs=in_specs,
        out_specs=pl.BlockSpec((tq, WIDTH_A), lambda b, q: (b * nq + q, 0)),
        out_shape=jax.ShapeDtypeStruct((T, WIDTH_A), BF16),
        scratch_shapes=scratch, compiler_params=_cparams(2), name="dsa")(
            qi, sm, sm, qa, ka, va_t, tz)


def _t5_bucket(dist):
    max_exact = N_BUCKETS // 2
    d = jnp.maximum(dist, 0)
    ratio = jnp.log(jnp.maximum(d, 1).astype(F32) / max_exact) / math.log(MAX_DISTANCE / max_exact)
    large = jnp.minimum(max_exact + (ratio * (N_BUCKETS - max_exact)).astype(jnp.int32), N_BUCKETS - 1)
    return jnp.where(d < max_exact, d, large)


def _bias_tables(rel_bias):
    tq = Q_TILE
    span = 2 * tq
    assert int(_np_bucket(tq)) == N_BUCKETS - 1
    far = rel_bias[N_BUCKETS - 1]
    by_dist = ((rel_bias[_t5_bucket(jnp.arange(span))] - far).astype(F32) * LOG2E).T
    diag = jnp.concatenate([by_dist[:, :tq], jnp.zeros_like(by_dist[:, :tq])], axis=1)
    prev = jnp.concatenate([by_dist[:, tq:], by_dist[:, :tq]], axis=1)

    def toeplitz(f):
        m = jnp.tile(f, (1, tq))[:, :tq * (span - 1)].reshape(N_HEADS_A, tq, span - 1)
        return m[:, :, :tq]

    return jnp.stack([toeplitz(diag), toeplitz(prev)])


def _np_bucket(d):
    max_exact = N_BUCKETS // 2
    ratio = np.log(np.float32(max(d, 1)) / np.float32(max_exact)) / math.log(MAX_DISTANCE / max_exact)
    return min(max_exact + int(ratio * (N_BUCKETS - max_exact)), N_BUCKETS - 1) if d >= max_exact else d


def _mlstm_kernel(qk_ref, halo_ref, v_ref, sm_ref, o_ref, cw_ref, cb_ref, g_ref,
                  out_ref, c_ref, m_ref):
    c_idx = pl.program_id(1)

    @pl.when(c_idx == 0)
    def _init():
        c_ref[...] = jnp.zeros_like(c_ref)
        m_ref[...] = jnp.zeros_like(m_ref)

    for g in range(M_GROUPS):
        gates = _mlstm_gates(g, c_idx, qk_ref, halo_ref, sm_ref, cw_ref, cb_ref)
        outs = [_mlstm_head(g, h, gates, v_ref, o_ref, g_ref, c_ref, m_ref) for h in range(N_HEADS_M)]
        out_ref[g] = jnp.concatenate(outs, axis=-1).astype(BF16)


def _mlstm_gates(g, c_idx, qk_ref, halo_ref, sm_ref, cw_ref, cb_ref):
    L = M_CHUNK

    hw = halo_ref.shape[1]
    halo = jnp.where(c_idx > 0, halo_ref[g].astype(F32), 0.0)
    ext = jnp.concatenate([halo, qk_ref[g].astype(F32)], axis=0)
    acc = jnp.zeros((L, 2 * WIDTH_M), F32) + cb_ref[...]
    for w in range(CONV_WIDTH):
        off = hw - (CONV_WIDTH - 1) + w
        acc = acc + ext[off:off + L, :] * cw_ref[w:w + 1, :]
    qk = acc * _sigmoid(acc)

    sm = sm_ref[g]
    sm_t = sm.T
    r_i = lax.broadcasted_iota(jnp.int32, (L, L), 0)
    c_i = lax.broadcasted_iota(jnp.int32, (L, L), 1)
    tril = (c_i <= r_i).astype(BF16)
    bcum_c = sum(jnp.dot(tril, part, preferred_element_type=F32)
                 for part in _split3(jax.nn.log_sigmoid(sm)))
    return qk, sm, sm_t, bcum_c, bcum_c.T, c_i <= r_i


def _mlstm_head(g, h, gates, v_ref, o_ref, g_ref, c_ref, m_ref):
    L = M_CHUNK
    dm = HEAD_DIM_M
    qk, sm, sm_t, bcum_c, bcum_r, causal = gates
    ones_col = (lax.broadcasted_iota(jnp.int32, (L, dm), 1) == 0).astype(BF16)
    st = g * N_HEADS_M + h
    q = qk[:, h * dm:(h + 1) * dm].astype(BF16)
    k = (qk[:, WIDTH_M + h * dm:WIDTH_M + (h + 1) * dm] * (dm ** -0.5))
    v_aug = jnp.concatenate([v_ref[g, :, h * dm:(h + 1) * dm], ones_col], axis=-1)
    b_col = bcum_c[:, SM_F + h:SM_F + h + 1]
    g_col = sm[:, SM_I + h:SM_I + h + 1] - b_col
    g_row = sm_t[SM_I + h:SM_I + h + 1, :] - bcum_r[SM_F + h:SM_F + h + 1, :]
    b_last = b_col[L - 1:L, :]
    m_prev = m_ref[st]
    c_prev = c_ref[st]

    log_d = jnp.where(causal, b_col + g_row, NEG_INF)
    m_j = jnp.maximum(b_col + m_prev, jnp.max(log_d, axis=-1, keepdims=True))
    w_inter = jnp.exp(b_col + m_prev - m_j)
    qkt = lax.dot_general(q, k.astype(BF16), (((1,), (1,)), ((), ())), preferred_element_type=F32)
    s = qkt * jnp.exp(log_d - m_j)
    o_aug = jnp.dot(s.astype(BF16), v_aug, preferred_element_type=F32) + \
        w_inter * jnp.dot(q, c_prev.astype(BF16), preferred_element_type=F32)
    num = o_aug[:, :dm]
    den = o_aug[:, dm:dm + 1]
    hh = num / jnp.maximum(jnp.abs(den), jnp.exp(-m_j))

    lwe = b_last + g_col
    m_loc = jnp.max(lwe, axis=0, keepdims=True)
    kw = (k * jnp.exp(lwe - m_loc)).astype(BF16)
    c_loc = lax.dot_general(kw, v_aug, (((0,), (0,)), ((), ())), preferred_element_type=F32)
    m_new = jnp.maximum(b_last + m_prev, m_loc)
    c_ref[st] = jnp.exp(b_last + m_prev - m_new) * c_prev + jnp.exp(m_loc - m_new) * c_loc
    m_ref[st] = m_new

    hn = hh * lax.rsqrt(jnp.mean(hh * hh, axis=-1, keepdims=True) + LN_EPS) * g_ref[:, h * dm:(h + 1) * dm]
    og = o_ref[g, :, h * dm:(h + 1) * dm].astype(F32)
    return _sigmoid(og) * hn


def _mlstm(qk, v, sm, o, conv_w, conv_b, mh_g, batch, seq):
    L = M_CHUNK
    G = M_GROUPS
    nc = seq // L
    T = batch * seq
    hb = L // CONV_HALO
    grouped = lambda a: a.reshape(G, T // G, a.shape[-1])
    blk = lambda n: pl.BlockSpec((G, L, n), lambda b, c: (0, b * nc + c, 0))
    in_specs = [
        blk(2 * WIDTH_M),
        pl.BlockSpec((G, CONV_HALO, 2 * WIDTH_M), lambda b, c: (0, jnp.maximum((b * nc + c) * hb - 1, 0), 0)),
        blk(WIDTH_M), blk(LANES), blk(WIDTH_M),
        _full(conv_w.shape), _full(conv_b.shape), _full(mh_g.shape),
    ]
    scratch = [pltpu.VMEM((G * N_HEADS_M, HEAD_DIM_M, 2 * HEAD_DIM_M), F32),
               pltpu.VMEM((G * N_HEADS_M, 1, 1), F32)]
    qk_g = grouped(qk)
    out = pl.pallas_call(
        _mlstm_kernel, grid=(batch // G, nc), in_specs=in_specs, out_specs=blk(WIDTH_M),
        out_shape=jax.ShapeDtypeStruct((G, T // G, WIDTH_M), BF16), scratch_shapes=scratch,
        compiler_params=_cparams(2), name="mlstm")(
            qk_g, qk_g, grouped(v), grouped(sm), grouped(o), conv_w, conv_b, mh_g)
    return out.reshape(T, WIDTH_M)


def _layer_norm(y, g, b):
    mu = jnp.mean(y, axis=-1, keepdims=True)
    var = jnp.mean(jnp.square(y - mu), axis=-1, keepdims=True)
    return (y - mu) * lax.rsqrt(var + LN_EPS) * g + b


def _router_logits(x1, wr, br):
    x1_hi = x1.astype(BF16)
    x1_lo = (x1 - x1_hi.astype(F32)).astype(BF16)
    both = jnp.dot(x1_hi, wr[...], preferred_element_type=F32)
    return (both[:, 0:LANES]
            + jnp.dot(x1_lo, wr[:, 0:LANES], preferred_element_type=F32)
            + both[:, LANES:2 * LANES]) + br[...]


def _merge_kernel(x_ref, oa_ref, hm_ref, ga_ref, gm_ref, wua, wum, wout, l1g, l1b, wr, br,
                  x1_ref, route_ref, cnt_ref):
    pa = jnp.dot(oa_ref[...], wua[...], preferred_element_type=F32)
    pm = jnp.dot(hm_ref[...], wum[...], preferred_element_type=F32)
    y = _sigmoid(ga_ref[...]) * pa.astype(BF16) + _sigmoid(gm_ref[...]) * pm.astype(BF16)
    mix = jnp.dot(y.astype(BF16), wout[...], preferred_element_type=F32)
    x1 = _layer_norm(ALPHA * x_ref[...] + mix, l1g[...], l1b[...])
    x1_ref[:, 0:D_MODEL] = x1

    lt = _router_logits(x1, wr, br).T
    tm = lt.shape[1]
    big = jnp.int32(LANES)
    le = lt[0:N_EXPERTS, :]
    lg = lt[N_EXPERTS:N_EXPERTS + SUBLANES, :]
    row_g = lax.broadcasted_iota(jnp.int32, lg.shape, 0)
    row_e = lax.broadcasted_iota(jnp.int32, le.shape, 0)
    is_grp = row_g < N_GROUPS
    gl = jnp.where(is_grp, lg, NEG_INF)
    ge = jnp.exp(gl - jnp.max(gl, axis=0, keepdims=True))
    gp = ge / jnp.sum(ge, axis=0, keepdims=True)
    g_w = jnp.max(gp, axis=0, keepdims=True)
    g_idx = jnp.min(jnp.where(jnp.logical_and(is_grp, gp == g_w), row_g, big), axis=0, keepdims=True)
    el = jnp.where(jnp.right_shift(row_e, EPG_SHIFT) == g_idx, le, NEG_INF)
    m1 = jnp.max(el, axis=0, keepdims=True)
    i1 = jnp.min(jnp.where(el == m1, row_e, big), axis=0, keepdims=True)
    el2 = jnp.where(row_e == i1, NEG_INF, el)
    m2 = jnp.max(el2, axis=0, keepdims=True)
    i2 = jnp.min(jnp.where(el2 == m2, row_e, big), axis=0, keepdims=True)

    e2 = jnp.exp(m2 - m1)
    w1 = g_w / (1.0 + e2)
    w2 = g_w * e2 / (1.0 + e2)
    first_is_a = i1 < i2
    row_w = lax.broadcasted_iota(jnp.int32, (LANES, tm), 0)
    w_rows = jnp.where(row_w == 0, jnp.where(first_is_a, w1, w2),
                       jnp.where(row_w == 1, jnp.where(first_is_a, w2, w1), 0.0))
    x1_ref[:, D_MODEL:D_MODEL + LANES] = w_rows.T

    a = jnp.bitwise_and(jnp.minimum(i1, i2), EXPERTS_PER_GROUP - 1)
    b = jnp.bitwise_and(jnp.maximum(i1, i2), EXPERTS_PER_GROUP - 1)
    pair = jnp.right_shift(a * (2 * EXPERTS_PER_GROUP - 1 - a), 1) + (b - a - 1)
    cls = g_idx * PAIRS_PER_GROUP + pair
    row_c = lax.broadcasted_iota(jnp.int32, (CLASS_ROWS, tm), 0)
    onehot = (row_c == cls).astype(F32)

    @pl.when(pl.program_id(0) == 0)
    def _init():
        cnt_ref[...] = jnp.zeros_like(cnt_ref)

    rb = min(RANK_BLOCK, tm)
    blocks = [onehot[:, j * rb:(j + 1) * rb] for j in range(tm // rb)]
    earlier = (lax.broadcasted_iota(jnp.int32, (rb, rb), 0)
               < lax.broadcasted_iota(jnp.int32, (rb, rb), 1)).astype(BF16)
    local = jnp.dot(jnp.concatenate(blocks, axis=0).astype(BF16), earlier, preferred_element_type=F32)
    before = cnt_ref[:, 0:1]
    ranks = []
    for j, blk in enumerate(blocks):
        prior = local[j * CLASS_ROWS:(j + 1) * CLASS_ROWS, :] + before
        ranks.append(jnp.sum(prior * blk, axis=0, keepdims=True))
        before = before + jnp.sum(blk, axis=1, keepdims=True)
    rank = jnp.concatenate(ranks, axis=1)
    cnt_ref[...] = jnp.broadcast_to(before, cnt_ref.shape)
    row_o = lax.broadcasted_iota(jnp.int32, route_ref.shape, 0)
    route_ref[...] = jnp.where(row_o == 0, cls.astype(F32), jnp.where(row_o == 1, rank, 0.0))


def _merge(x2, oa, hm, ga, gm, wua, wum, wout, l1g, l1b, wr, br, tm):
    T = x2.shape[0]
    blk = lambda n: pl.BlockSpec((tm, n), lambda i: (i, 0))
    in_specs = [blk(D_MODEL), blk(WIDTH_A), blk(WIDTH_M), blk(D_MODEL), blk(D_MODEL),
                _full(wua.shape), _full(wum.shape), _full(wout.shape), _full(l1g.shape), _full(l1b.shape),
                _full(wr.shape), _full(br.shape)]
    out_specs = [blk(D_MODEL + LANES), pl.BlockSpec((SUBLANES, tm), lambda i: (0, i)),
                 pl.BlockSpec((CLASS_ROWS, LANES), lambda i: (0, 0))]
    out_shape = [jax.ShapeDtypeStruct((T, D_MODEL + LANES), F32), jax.ShapeDtypeStruct((SUBLANES, T), F32),
                 jax.ShapeDtypeStruct((CLASS_ROWS, LANES), F32)]
    return pl.pallas_call(
        _merge_kernel, grid=(T // tm,), in_specs=in_specs, out_specs=out_specs, out_shape=out_shape,
        compiler_params=_cparams(1), name="merge")(x2, oa, hm, ga, gm, wua, wum, wout, l1g, l1b, wr, br)


def _sc_mesh():
    return plsc.VectorSubcoreMesh(core_axis_name="c", subcore_axis_name="s",
                                  num_cores=SC_CORES, num_subcores=SC_SUBCORES)


def _sc_chunks(n_rows):
    workers = SC_CORES * SC_SUBCORES
    assert n_rows % (workers * SC_ROWS * 2) == 0
    return n_rows // (workers * SC_ROWS)


def _sc_scratch(n_chunks, width, dtype):
    return [pltpu.VMEM((n_chunks, SC_ROWS), jnp.int32),
            pltpu.VMEM((SC_ROWS, width), dtype), pltpu.VMEM((SC_ROWS, width), dtype),
            pltpu.SemaphoreType.DMA, pltpu.SemaphoreType.DMA]


def _sc_scatter_rows(rows, idx, n_out):
    n_in, width = rows.shape
    n_chunks = _sc_chunks(n_in)

    @functools.partial(
        pl.kernel, mesh=_sc_mesh(), out_type=jax.ShapeDtypeStruct((n_out, width), rows.dtype),
        scratch_types=_sc_scratch(n_chunks, width, rows.dtype), name="sc_dispatch")
    def scatter(rows_hbm, idx_hbm, out_hbm, idx_v, rows_a, rows_b, sem_a, sem_b):
        first = (lax.axis_index("s") * SC_CORES + lax.axis_index("c")) * n_chunks
        pltpu.sync_copy(idx_hbm.at[pl.ds(first, n_chunks)], idx_v)

        def load(c, buf):
            pltpu.sync_copy(rows_hbm.at[pl.ds((first + c) * SC_ROWS, SC_ROWS)], buf)

        def put(c, buf, sem):
            return pltpu.make_async_copy(buf, out_hbm.at[idx_v.at[c]], sem)

        load(0, rows_a)
        put(0, rows_a, sem_a).start()

        @pl.loop(0, n_chunks, step=2)
        def _(j):
            load(j + 1, rows_b)
            put(j + 1, rows_b, sem_b).start()
            put(j, rows_a, sem_a).wait()

            @pl.when(j + 2 < n_chunks)
            def _():
                load(j + 2, rows_a)
                put(j + 2, rows_a, sem_a).start()

            put(j + 1, rows_b, sem_b).wait()

    return scatter(rows, idx.reshape(n_in // SC_ROWS, SC_ROWS))


def _sc_gather_rows(table, idx):
    n_out, width = idx.shape[0], table.shape[1]
    n_chunks = _sc_chunks(n_out)

    @functools.partial(
        pl.kernel, mesh=_sc_mesh(), out_type=jax.ShapeDtypeStruct((n_out, width), table.dtype),
        scratch_types=_sc_scratch(n_chunks, width, table.dtype), name="sc_combine")
    def gather(table_hbm, idx_hbm, out_hbm, idx_v, rows_a, rows_b, sem_a, sem_b):
        first = (lax.axis_index("s") * SC_CORES + lax.axis_index("c")) * n_chunks
        pltpu.sync_copy(idx_hbm.at[pl.ds(first, n_chunks)], idx_v)

        def fetch(c, buf, sem):
            return pltpu.make_async_copy(table_hbm.at[idx_v.at[c]], buf, sem)

        def store(c, buf):
            pltpu.sync_copy(buf, out_hbm.at[pl.ds((first + c) * SC_ROWS, SC_ROWS)])

        fetch(0, rows_a, sem_a).start()

        @pl.loop(0, n_chunks, step=2)
        def _(j):
            fetch(j + 1, rows_b, sem_b).start()
            fetch(j, rows_a, sem_a).wait()
            store(j, rows_a)

            @pl.when(j + 2 < n_chunks)
            def _():
                fetch(j + 2, rows_a, sem_a).start()

            fetch(j + 1, rows_b, sem_b).wait()
            store(j + 1, rows_b)

    return gather(table, idx.reshape(n_out // SC_ROWS, SC_ROWS))


def _moe_kernel(ta_ref, tb_ref, nu_ref, xs_ref, wga, wua, wda, wgb, wub, wdb, l2g, l2b, ys_ref,
                pre_ref):
    i = pl.program_id(0)
    n_used = nu_ref[0]
    tm = pre_ref.shape[0]
    pr = tm // MOE_NORM_PIECES

    def norm_rows(piece):
        rs = slice(piece * pr, (piece + 1) * pr)
        ys_ref[rs, :] = _layer_norm(pre_ref[rs, :], l2g[...], l2b[...])

    @pl.when(i == 0)
    def _first():
        pre_ref[...] = jnp.zeros_like(pre_ref)

    @pl.when(i < n_used)
    def _compute():
        x = xs_ref[:, 0:D_MODEL]
        xb = x.astype(BF16)
        w_a = xs_ref[:, D_MODEL:D_MODEL + 1]
        w_b = xs_ref[:, D_MODEL + 1:D_MODEL + 2]
        piece = iter(range(MOE_NORM_PIECES))

        def dot_pieces(lhs, w):
            outs = []
            for n in range(w.shape[1] // MOE_DOT_COLS):
                outs.append(jnp.dot(lhs, w[:, n * MOE_DOT_COLS:(n + 1) * MOE_DOT_COLS],
                                    preferred_element_type=F32))
                norm_rows(next(piece))
            return jnp.concatenate(outs, axis=1)

        def expert(wg, wu, wd):
            g = dot_pieces(xb, wg)
            u = dot_pieces(xb, wu)
            hdn = (g * _sigmoid(g) * u).astype(BF16)
            return dot_pieces(hdn, wd)

        ffn = w_a * expert(wga, wua, wda) + w_b * expert(wgb, wub, wdb)
        pre_ref[...] = ALPHA * x + ffn

    @pl.when(i == n_used)
    def _last():
        for piece in range(MOE_NORM_PIECES):
            norm_rows(piece)


def _moe(xs, tile_a, tile_b, n_used, wg, wu, wd, l2g, l2b):
    tm = MOE_TILE
    n_tiles = xs.shape[0] // tm
    used = lambda i, nu: jnp.minimum(i, nu[0] - 1)
    rows = lambda n: pl.BlockSpec((tm, n), lambda i, ta, tb, nu: (used(i, nu), 0))
    prev_rows = pl.BlockSpec((tm, D_MODEL), lambda i, ta, tb, nu: (used(jnp.maximum(i - 1, 0), nu), 0))
    w_in = lambda which: pl.BlockSpec(
        (None, D_MODEL, D_EXPERT), lambda i, ta, tb, nu: ((ta, tb)[which][used(i, nu)], 0, 0))
    w_out = lambda which: pl.BlockSpec(
        (None, D_EXPERT, D_MODEL), lambda i, ta, tb, nu: ((ta, tb)[which][used(i, nu)], 0, 0))
    const = lambda shape: pl.BlockSpec(shape, lambda i, ta, tb, nu: (0,) * len(shape))
    grid_spec = pltpu.PrefetchScalarGridSpec(
        num_scalar_prefetch=3, grid=(n_tiles + 1,),
        in_specs=[rows(xs.shape[1]), w_in(0), w_in(0), w_out(0), w_in(1), w_in(1), w_out(1),
                  const(l2g.shape), const(l2b.shape)],
        out_specs=prev_rows,
        scratch_shapes=[pltpu.VMEM((tm, D_MODEL), F32)])
    return pl.pallas_call(
        _moe_kernel, grid_spec=grid_spec, out_shape=jax.ShapeDtypeStruct((xs.shape[0], D_MODEL), F32),
        compiler_params=_cparams(1), name="moe")(
            tile_a, tile_b, n_used, xs, wg, wu, wd, wg, wu, wd, l2g, l2b)


def _route_tables(route, counts, n_tokens):
    tm = MOE_TILE
    n_tiles = n_tokens // tm + N_CLASSES
    cnt = counts[:N_CLASSES, 0].astype(jnp.int32)
    tiles = (cnt + tm - 1) // tm
    tile_end = jnp.cumsum(tiles)
    tile_start = tile_end - tiles
    t_idx = jnp.arange(n_tiles, dtype=jnp.int32)
    cls_of_tile = jnp.minimum(jnp.sum(t_idx[:, None] >= tile_end[None, :], axis=1), N_CLASSES - 1).astype(jnp.int32)
    classes = np.arange(N_CLASSES)
    tile_is = cls_of_tile[:, None] == classes[None, :]
    per_tile = lambda table: jnp.sum(jnp.where(tile_is, jnp.asarray(table, jnp.int32)[None, :], 0), axis=1)
    first_expert = classes // PAIRS_PER_GROUP * EXPERTS_PER_GROUP
    tile_a = per_tile(first_expert + np.asarray(PAIR_A)[classes % PAIRS_PER_GROUP])
    tile_b = per_tile(first_expert + np.asarray(PAIR_B)[classes % PAIRS_PER_GROUP])
    cls = route[0].astype(jnp.int32)
    rank = route[1].astype(jnp.int32)
    row0 = jnp.sum(jnp.where(cls[:, None] == jnp.arange(N_CLASSES)[None, :], (tile_start * tm)[None, :], 0), axis=1)
    n_used = jnp.maximum(tile_end[-1:], 1).astype(jnp.int32)
    return tile_a, tile_b, n_used, row0 + rank, n_tiles * tm


def _pick_tile(T, pref):
    t = pref
    while T % t:
        t //= 2
    return t


def kernel(x, w_in, conv_w, conv_b, kv_norm_g, w_uk, w_uv, rel_bias, b_i, b_f, mh_norm_g, w_up_a, w_up_m,
           w_out, ln1_g, ln1_b, w_grp, b_grp, w_rt, b_rt, w_gate, w_up, w_down, ln2_g, ln2_b):
    B, S, _ = x.shape
    T = B * S
    assert S % Q_TILE == 0 and S % M_CHUNK == 0 and T % MOE_TILE == 0 and w_in.shape[0] == DEPTH
    tz = _bias_tables(rel_bias)
    x2 = x.reshape(T, D_MODEL)
    for l in range(DEPTH):
        w = w_in[l]
        o = np.cumsum((WIDTH_A, KV_RANK, WIDTH_IDX, HEAD_DIM_IDX, N_HEADS_IDX, 2 * WIDTH_M, WIDTH_M,
                       N_HEADS_M, N_HEADS_M, WIDTH_M, D_MODEL, D_MODEL)).tolist()
        o = [0] + o
        seg = lambda j: w[:, o[j]:o[j + 1]]
        pad = LANES - (HEAD_DIM_IDX + N_HEADS_IDX + 2 * N_HEADS_M)
        w_small = jnp.concatenate([seg(3), seg(4), seg(7), seg(8), jnp.zeros((D_MODEL, pad), w.dtype)], axis=1)
        ws = [seg(0), seg(1), seg(2), w_small, seg(5), seg(6), seg(9), seg(10), seg(11)]
        ws = [a.astype(BF16) for a in ws]
        smb = jnp.zeros((1, LANES), F32).at[0, SM_I:SM_I + N_HEADS_M].set(b_i[l]) \
            .at[0, SM_F:SM_F + N_HEADS_M].set(b_f[l])
        wukv = jnp.concatenate([w_uk[l].reshape(KV_RANK, WIDTH_A), w_uv[l].reshape(KV_RANK, WIDTH_A)],
                               axis=1).astype(BF16)
        qa, ka, qi, sm, qk, v, og, ga, gm, va_t = _proj(x2, ws, wukv, kv_norm_g[l][None, :], smb,
                                                        _pick_tile(S, PROJ_TILE), S)
        oa = _dsa(qi, sm, qa, ka, va_t, tz, B, S)

        hm = _mlstm(qk, v, sm, og, conv_w[l], conv_b[l][None, :], mh_norm_g[l].reshape(1, WIDTH_M), B, S)

        w_router = jnp.concatenate(
            [w_rt[l], w_grp[l], jnp.zeros((D_MODEL, LANES - N_EXPERTS - N_GROUPS), F32)], axis=1)
        b_router = jnp.concatenate(
            [b_rt[l], b_grp[l], jnp.zeros((LANES - N_EXPERTS - N_GROUPS,), F32)])[None, :]
        wr_hi = w_router.astype(BF16)
        wr_split = jnp.concatenate([wr_hi, (w_router - wr_hi.astype(F32)).astype(BF16)], axis=1)
        x1, route, counts = _merge(x2, oa, hm, ga, gm, w_up_a[l].astype(BF16), w_up_m[l].astype(BF16),
                                   w_out[l].astype(BF16), ln1_g[l][None, :], ln1_b[l][None, :],
                                   wr_split, b_router, _pick_tile(T, MERGE_TILE))

        tile_a, tile_b, n_used, pos, n_sorted = _route_tables(route, counts, T)
        xs = _sc_scatter_rows(x1, pos, n_sorted)
        ys = _moe(xs, tile_a, tile_b, n_used, w_gate[l].astype(BF16), w_up[l].astype(BF16),
                  w_down[l].astype(BF16), ln2_g[l][None, :], ln2_b[l][None, :])
        x2 = _sc_gather_rows(ys, pos)
    return x2.reshape(B, S, D_MODEL)
```

```python
import functools
import math

import jax
import jax.numpy as jnp
import numpy as np
from jax import lax
from jax.experimental import pallas as pl
from jax.experimental.pallas import tpu as pltpu
from jax.experimental.pallas import tpu_sc as plsc

F32 = jnp.float32
BF16 = jnp.bfloat16

D_MODEL = 1024
N_HEADS_A = 8
HEAD_DIM_A = 64
WIDTH_A = N_HEADS_A * HEAD_DIM_A
KV_RANK = 256
N_HEADS_IDX = 8
HEAD_DIM_IDX = 64
WIDTH_IDX = N_HEADS_IDX * HEAD_DIM_IDX
TOPK_MAX = 256
N_BUCKETS = 32
MAX_DISTANCE = 128
N_HEADS_M = 4
HEAD_DIM_M = 128
WIDTH_M = N_HEADS_M * HEAD_DIM_M
CONV_WIDTH = 4
N_GROUPS = 4
EXPERTS_PER_GROUP = 4
N_EXPERTS = N_GROUPS * EXPERTS_PER_GROUP
D_EXPERT = 512
LN_EPS = 1e-5
DEPTH = 1
ALPHA = (2.0 * DEPTH) ** 0.25

LANES = 128
SUBLANES = 8
VMEM_LIMIT = 56 * 1024 * 1024

SM_KIDX = 0
SM_WIDX = HEAD_DIM_IDX
SM_I = SM_WIDX + N_HEADS_IDX
SM_F = SM_I + N_HEADS_M

HEADS_PER_DOT = LANES // HEAD_DIM_A
assert HEADS_PER_DOT * HEAD_DIM_A == LANES and N_HEADS_A % HEADS_PER_DOT == 0
Q_TILE = 256
K_CHUNK = Q_TILE
MXU_CHUNKS_PER_TRIP = 4
ONES_ROWS = 16
V_ROWS = HEAD_DIM_A + ONES_ROWS
M_CHUNK = 128
M_GROUPS = 2
CONV_HALO = 16
assert CONV_HALO >= CONV_WIDTH - 1
PROJ_TILE = 1024
MERGE_TILE = 1024
RANK_BLOCK = 256
MOE_TILE = 512
MOE_DOT_COLS = 256
MOE_NORM_PIECES = 2 * (2 * D_EXPERT + D_MODEL) // MOE_DOT_COLS
SC_CORES = 2
SC_SUBCORES = 16
SC_ROWS = 32
EPG_SHIFT = EXPERTS_PER_GROUP.bit_length() - 1
assert 1 << EPG_SHIFT == EXPERTS_PER_GROUP
PAIR_A, PAIR_B = zip(*[(a, b) for a in range(EXPERTS_PER_GROUP) for b in range(a + 1, EXPERTS_PER_GROUP)])
PAIRS_PER_GROUP = len(PAIR_A)
N_CLASSES = N_GROUPS * PAIRS_PER_GROUP
CLASS_ROWS = -(-N_CLASSES // SUBLANES) * SUBLANES
BISECT_STEPS = 15
NEG_INF = float("-inf")
LOG2E = math.log2(math.e)


def _cparams(n_grid):
    return pltpu.CompilerParams(dimension_semantics=("arbitrary",) * n_grid,
                                vmem_limit_bytes=VMEM_LIMIT)


def _full(shape):
    nd = len(shape)
    return pl.BlockSpec(shape, lambda *_: (0,) * nd, pipeline_mode=pl.Buffered(1))


def _sigmoid(x):
    return 0.5 * jnp.tanh(0.5 * x) + 0.5


def _split3(x):
    hi = x.astype(BF16)
    r = x - hi.astype(F32)
    mid = r.astype(BF16)
    return hi, mid, (r - mid.astype(F32)).astype(BF16)


def _proj_kernel(x_ref, wqa, wckv, wqi, wsm, wqk, wv, wo, wga, wgm, wukv, kvg, smb,
                 qa_o, ka_o, qi_o, sm_o, qk_o, v_o, o_o, ga_o, gm_o, vat_o):
    xb = x_ref[...].astype(BF16)

    def mm(w):
        return jnp.dot(xb, w[...], preferred_element_type=F32)

    c = mm(wckv)
    qa_o[...] = (mm(wqa) * (HEAD_DIM_A ** -0.5 * LOG2E)).astype(BF16)
    c = c * lax.rsqrt(jnp.mean(c * c, axis=-1, keepdims=True) + LN_EPS) * kvg[...]
    qi_o[...] = mm(wqi).astype(BF16)
    kv = jnp.dot(c.astype(BF16), wukv[...], preferred_element_type=F32)
    ka_o[...] = kv[:, 0:WIDTH_A].astype(BF16)
    va_t = kv[:, WIDTH_A:2 * WIDTH_A].T.astype(BF16)
    for h in range(N_HEADS_A):
        r0 = h * V_ROWS
        vat_o[r0:r0 + HEAD_DIM_A, :] = va_t[h * HEAD_DIM_A:(h + 1) * HEAD_DIM_A, :]
        vat_o[r0 + HEAD_DIM_A:r0 + V_ROWS, :] = jnp.ones((ONES_ROWS, va_t.shape[1]), BF16)
    sm_o[...] = mm(wsm) + smb[...]
    qk_o[...] = mm(wqk).astype(BF16)
    v_o[...] = mm(wv).astype(BF16)
    o_o[...] = mm(wo).astype(BF16)
    ga_o[...] = mm(wga).astype(BF16)
    gm_o[...] = mm(wgm).astype(BF16)


def _proj(x2, ws, wukv, kvg, smb, tm, seq):
    T = x2.shape[0]
    per_seq = seq // tm

    def by_token(n, dt=BF16):
        return pl.BlockSpec((tm, n), lambda i: (i, 0)), jax.ShapeDtypeStruct((T, n), dt)

    def by_lane(n):
        return (pl.BlockSpec((None, n, tm), lambda i: (i // per_seq, 0, i % per_seq)),
                jax.ShapeDtypeStruct((T // seq, n, seq), BF16))

    outs = [by_token(WIDTH_A), by_token(WIDTH_A), by_token(WIDTH_IDX), by_token(LANES, F32)]
    outs += [by_token(w.shape[1]) for w in ws[4:]]
    outs.append(by_lane(N_HEADS_A * V_ROWS))
    in_specs = [pl.BlockSpec((tm, D_MODEL), lambda i: (i, 0))]
    in_specs += [_full(a.shape) for a in (*ws, wukv, kvg, smb)]
    return pl.pallas_call(
        _proj_kernel, grid=(T // tm,), in_specs=in_specs, out_specs=[o[0] for o in outs],
        out_shape=[o[1] for o in outs], compiler_params=_cparams(1), name="proj")(x2, *ws, wukv, kvg, smb)


def _dsa_kernel(qi_ref, smq_ref, smk_ref, qa_ref, ka_ref, vat_ref, tz_ref,
                oa_ref, qs_ref, sc_ref, am_ref, ql_ref, x_ref, acc_ref, m_ref, *, topk):
    tq, kc = Q_TILE, K_CHUNK
    nh = N_HEADS_A
    qb = pl.program_id(1)
    n_ch = qb + 1
    seq_keys = n_ch * kc
    t0 = qb * tq

    def rows(c):
        return pl.ds(pl.multiple_of(c * kc, kc), kc)

    def lanes(h):
        return slice(h * tq, (h + 1) * tq)

    def chunk_loop(n, body, init, per_trip=2):
        def run(first, count, carry):
            for r in range(count):
                carry = body(first + r, carry)
            return carry

        shift = per_trip.bit_length() - 1
        trips = jnp.right_shift(n, shift)
        carry = lax.fori_loop(0, trips, lambda p, c: run(p * per_trip, per_trip, c), init)
        done = trips * per_trip
        count = per_trip // 2
        while count:
            start = done + jnp.bitwise_and(n - done, -2 * count)
            carry = lax.cond(jnp.bitwise_and(n, count) != 0,
                             functools.partial(run, start, count), lambda c: c, carry)
            count //= 2
        return carry

    for h in range(N_HEADS_IDX):
        qs_ref[h * tq:(h + 1) * tq, :] = qi_ref[:, h * HEAD_DIM_IDX:(h + 1) * HEAD_DIM_IDX]
    w_t = smq_ref[...].T
    q_pos = lax.broadcasted_iota(jnp.int32, (1, tq), 1) + t0
    key_iota = lax.broadcasted_iota(jnp.int32, (kc, tq), 0)

    def score_chunk(c, carry):
        mx, mn = carry
        kk = smk_ref[rows(c), SM_KIDX:SM_KIDX + HEAD_DIM_IDX].astype(BF16)
        dots = lax.dot_general(kk, qs_ref[...], (((1,), (1,)), ((), ())), preferred_element_type=F32)
        sc = jnp.zeros((kc, tq), F32)
        for h in range(N_HEADS_IDX):
            sc = sc + w_t[SM_WIDX + h:SM_WIDX + h + 1, :] * jnp.maximum(dots[:, lanes(h)], 0.0)
        vis = (key_iota + c * kc) <= q_pos
        sc_ref[rows(c), :] = jnp.where(vis, sc, NEG_INF)
        mx = jnp.maximum(mx, jnp.max(jnp.where(vis, sc, NEG_INF), axis=0, keepdims=True))
        mn = jnp.minimum(mn, jnp.min(jnp.where(vis, sc, jnp.inf), axis=0, keepdims=True))
        return mx, mn

    mx, mn = chunk_loop(n_ch, score_chunk,
                        (jnp.full((1, tq), NEG_INF, F32), jnp.full((1, tq), jnp.inf, F32)), MXU_CHUNKS_PER_TRIP)

    n_vis = (q_pos + 1).astype(F32)
    k_row = jnp.minimum(n_vis, float(topk))

    def count(pred):
        def body(c, a):
            hit = pred(sc_ref[rows(c), :]).astype(F32)
            return a + jnp.sum(hit.reshape(kc // SUBLANES, SUBLANES, tq), axis=0)
        a = chunk_loop(n_ch, body, jnp.zeros((SUBLANES, tq), F32))
        return jnp.sum(a, axis=0, keepdims=True)

    def any_lane(flag):
        return jnp.max(jnp.where(flag, 1.0, 0.0)) > 0.0

    def bisect_step(_, carry):
        lo, hi, cnt_lo, c_hi = carry
        mid = lo * 0.5 + hi * 0.5
        cnt = count(lambda s: s >= mid)
        ge = cnt >= k_row
        return jnp.where(ge, mid, lo), jnp.where(ge, hi, mid), jnp.where(ge, cnt, cnt_lo), jnp.where(ge, c_hi, cnt)

    hi0 = mx + jnp.maximum(jnp.abs(mx), 1e-30) * 1e-6
    lo, hi, cnt_lo, c_hi = lax.cond(
        t0 + tq > topk,
        lambda carry: lax.fori_loop(0, BISECT_STEPS, bisect_step, carry),
        lambda carry: carry,
        (mn, hi0, n_vis, jnp.zeros((1, tq), F32)))

    def peel_cond(carry):
        it, lo, hi, cnt_lo, c_hi, done = carry
        return jnp.logical_and(it < seq_keys, any_lane(done == 0.0))

    def peel_body(carry):
        it, lo, hi, cnt_lo, c_hi, done = carry

        def top_body(c, v):
            s = sc_ref[rows(c), :]
            inside = jnp.logical_and(s >= lo, s < hi)
            return jnp.maximum(v, jnp.max(jnp.where(inside, s, NEG_INF), axis=0, keepdims=True))

        v = chunk_loop(n_ch, top_body, jnp.full((1, tq), NEG_INF, F32))
        c_v = count(lambda s: s >= v)
        reached = c_v >= k_row
        live = done == 0.0
        fin = jnp.logical_and(live, reached)
        cut = jnp.logical_and(live, jnp.logical_not(reached))
        return (it + 1, jnp.where(fin, v, lo), jnp.where(cut, v, hi), jnp.where(fin, c_v, cnt_lo),
                jnp.where(cut, c_v, c_hi), jnp.where(fin, 1.0, done))

    _, lo, hi, cnt_lo, c_hi, _ = lax.while_loop(
        peel_cond, peel_body,
        (jnp.int32(0), lo, hi, cnt_lo, c_hi, jnp.where(cnt_lo == k_row, 1.0, 0.0)))

    def mask_chunk(c, _):
        am_ref[rows(c), :] = jnp.where(sc_ref[rows(c), :] >= lo, 0.0, NEG_INF)
        return 0

    chunk_loop(n_ch, mask_chunk, 0)

    tied = cnt_lo != k_row
    lane_tiles = [slice(part * LANES, (part + 1) * LANES) for part in range(tq // LANES)]
    any_tied = [any_lane(tied[:, ls]) for ls in lane_tiles]
    for ls, flag in zip(lane_tiles, any_tied):

        @pl.when(flag)
        def _ties():
            need, lo_p, hi_p = (k_row - c_hi)[:, ls], lo[:, ls], hi[:, ls]
            lower = (lax.broadcasted_iota(jnp.int32, (kc, kc), 1)
                     < lax.broadcasted_iota(jnp.int32, (kc, kc), 0)).astype(BF16)

            def tie_chunk(c, before):
                s = sc_ref[rows(c), ls]
                above = s >= hi_p
                tie = jnp.logical_and(s >= lo_p, jnp.logical_not(above))
                tie_f = tie.astype(F32)
                rank = jnp.dot(lower, tie_f.astype(BF16), preferred_element_type=F32) + before
                sel = jnp.logical_or(above, jnp.logical_and(tie, rank < need))
                am_ref[rows(c), ls] = jnp.where(sel, 0.0, NEG_INF)
                return before + jnp.sum(tie_f, axis=0, keepdims=True)

            chunk_loop(n_ch, tie_chunk, jnp.zeros((1, LANES), F32), MXU_CHUNKS_PER_TRIP)

    head_of_lane = lax.broadcasted_iota(jnp.int32, (tq, LANES), 1) // HEAD_DIM_A
    for pair in range(nh // HEADS_PER_DOT):
        q_tile = qa_ref[:, pair * LANES:(pair + 1) * LANES]
        for slot in range(HEADS_PER_DOT):
            ql_ref[pair, slot * tq:(slot + 1) * tq, :] = jnp.where(head_of_lane == slot, q_tile, 0)

    m_ref[...] = jnp.full(m_ref.shape, NEG_INF, F32)

    def pass_a(c, table):
        am = am_ref[rows(c), :]
        for pair in range(nh // HEADS_PER_DOT):
            raw = lax.dot_general(ka_ref[rows(c), pair * LANES:(pair + 1) * LANES], ql_ref[pair],
                                  (((1,), (1,)), ((), ())), preferred_element_type=F32)
            for slot in range(HEADS_PER_DOT):
                h = pair * HEADS_PER_DOT + slot
                x = raw[:, slot * tq:(slot + 1) * tq] + am
                if table is not None:
                    x = x + tz_ref[table, h]
                x_ref[rows(c), lanes(h)] = x
                m_ref[:, lanes(h)] = jnp.maximum(m_ref[:, lanes(h)], jnp.max(x, axis=0, keepdims=True))

    def far_chunk(c, _):
        pass_a(c, None)
        return 0

    chunk_loop(jnp.maximum(qb - 1, 0), far_chunk, 0, MXU_CHUNKS_PER_TRIP)

    @pl.when(qb >= 1)
    def _prev():
        pass_a(qb - 1, 1)

    pass_a(qb, 0)

    def pv(c):
        p = jnp.exp2((x_ref[rows(c), :] - m_ref[...]).astype(BF16))
        return jnp.concatenate(
            [jnp.dot(vat_ref[h * V_ROWS:(h + 1) * V_ROWS, rows(c)], p[:, lanes(h)], preferred_element_type=F32)
             for h in range(nh)], axis=0)

    def pass_b(c, _):
        acc_ref[...] += pv(c)
        return 0

    acc_ref[...] = pv(qb)
    chunk_loop(qb, pass_b, 0, MXU_CHUNKS_PER_TRIP)
    outs = []
    for h in range(nh):
        blk = acc_ref[h * V_ROWS:(h + 1) * V_ROWS, :]
        outs.append(blk[0:HEAD_DIM_A, :] * (1.0 / blk[HEAD_DIM_A:HEAD_DIM_A + 1, :]))
    oa_ref[...] = jnp.concatenate(outs, axis=0).T.astype(BF16)


def _dsa(qi, sm, qa, ka, va_t, tz, batch, seq):
    tq = Q_TILE
    nq = seq // tq
    topk = min(TOPK_MAX, seq // 4)
    T = batch * seq
    kern = functools.partial(_dsa_kernel, topk=topk)
    blk_q = lambda n: pl.BlockSpec((tq, n), lambda b, q: (b * nq + q, 0))
    blk_s = lambda n: pl.BlockSpec((seq, n), lambda b, q: (b, 0))
    in_specs = [
        blk_q(WIDTH_IDX), blk_q(LANES), blk_s(LANES),
        blk_q(WIDTH_A), blk_s(WIDTH_A),
        pl.BlockSpec((None, N_HEADS_A * V_ROWS, seq), lambda b, q: (b, 0, 0)),
        _full(tz.shape),
    ]
    scratch = [
        pltpu.VMEM((N_HEADS_IDX * tq, HEAD_DIM_IDX), BF16),
        pltpu.VMEM((seq, tq), F32),
        pltpu.VMEM((seq, tq), F32),
        pltpu.VMEM((N_HEADS_A // HEADS_PER_DOT, HEADS_PER_DOT * tq, LANES), BF16),
        pltpu.VMEM((seq, N_HEADS_A * tq), F32),
        pltpu.VMEM((N_HEADS_A * V_ROWS, tq), F32),
        pltpu.VMEM((1, N_HEADS_A * tq), F32),
    ]
    return pl.pallas_call(
        kern, grid=(batch, nq), in_specs=in_specs,
        out_specs=pl.BlockSpec((tq, WIDTH_A), lambda b, q: (b * nq + q, 0)),
        out_shape=jax.ShapeDtypeStruct((T, WIDTH_A), BF16),
        scratch_shapes=scratch, compiler_params=_cparams(2), name="dsa")(
            qi, sm, sm, qa, ka, va_t, tz)


def _t5_bucket(dist):
    max_exact = N_BUCKETS // 2
    d = jnp.maximum(dist, 0)
    ratio = jnp.log(jnp.maximum(d, 1).astype(F32) / max_exact) / math.log(MAX_DISTANCE / max_exact)
    large = jnp.minimum(max_exact + (ratio * (N_BUCKETS - max_exact)).astype(jnp.int32), N_BUCKETS - 1)
    return jnp.where(d < max_exact, d, large)


def _bias_tables(rel_bias):
    tq = Q_TILE
    span = 2 * tq
    assert int(_np_bucket(tq)) == N_BUCKETS - 1
    far = rel_bias[N_BUCKETS - 1]
    by_dist = ((rel_bias[_t5_bucket(jnp.arange(span))] - far).astype(F32) * LOG2E).T
    diag = jnp.concatenate([by_dist[:, :tq], jnp.zeros_like(by_dist[:, :tq])], axis=1)
    prev = jnp.concatenate([by_dist[:, tq:], by_dist[:, :tq]], axis=1)

    def toeplitz(f):
        m = jnp.tile(f, (1, tq))[:, :tq * (span - 1)].reshape(N_HEADS_A, tq, span - 1)
        return m[:, :, :tq]

    return jnp.stack([toeplitz(diag), toeplitz(prev)])


def _np_bucket(d):
    max_exact = N_BUCKETS // 2
    ratio = np.log(np.float32(max(d, 1)) / np.float32(max_exact)) / math.log(MAX_DISTANCE / max_exact)
    return min(max_exact + int(ratio * (N_BUCKETS - max_exact)), N_BUCKETS - 1) if d >= max_exact else d


def _mlstm_kernel(qk_ref, halo_ref, v_ref, sm_ref, o_ref, cw_ref, cb_ref, g_ref,
                  out_ref, c_ref, m_ref):
    c_idx = pl.program_id(1)

    @pl.when(c_idx == 0)
    def _init():
        c_ref[...] = jnp.zeros_like(c_ref)
        m_ref[...] = jnp.zeros_like(m_ref)

    for g in range(M_GROUPS):
        gates = _mlstm_gates(g, c_idx, qk_ref, halo_ref, sm_ref, cw_ref, cb_ref)
        outs = [_mlstm_head(g, h, gates, v_ref, o_ref, g_ref, c_ref, m_ref) for h in range(N_HEADS_M)]
        out_ref[g] = jnp.concatenate(outs, axis=-1).astype(BF16)


def _mlstm_gates(g, c_idx, qk_ref, halo_ref, sm_ref, cw_ref, cb_ref):
    L = M_CHUNK

    hw = halo_ref.shape[1]
    halo = jnp.where(c_idx > 0, halo_ref[g].astype(F32), 0.0)
    ext = jnp.concatenate([halo, qk_ref[g].astype(F32)], axis=0)
    acc = jnp.zeros((L, 2 * WIDTH_M), F32) + cb_ref[...]
    for w in range(CONV_WIDTH):
        off = hw - (CONV_WIDTH - 1) + w
        acc = acc + ext[off:off + L, :] * cw_ref[w:w + 1, :]
    qk = acc * _sigmoid(acc)

    sm = sm_ref[g]
    sm_t = sm.T
    r_i = lax.broadcasted_iota(jnp.int32, (L, L), 0)
    c_i = lax.broadcasted_iota(jnp.int32, (L, L), 1)
    tril = (c_i <= r_i).astype(BF16)
    bcum_c = sum(jnp.dot(tril, part, preferred_element_type=F32)
                 for part in _split3(jax.nn.log_sigmoid(sm)))
    return qk, sm, sm_t, bcum_c, bcum_c.T, c_i <= r_i


def _mlstm_head(g, h, gates, v_ref, o_ref, g_ref, c_ref, m_ref):
    L = M_CHUNK
    dm = HEAD_DIM_M
    qk, sm, sm_t, bcum_c, bcum_r, causal = gates
    ones_col = (lax.broadcasted_iota(jnp.int32, (L, dm), 1) == 0).astype(BF16)
    st = g * N_HEADS_M + h
    q = qk[:, h * dm:(h + 1) * dm].astype(BF16)
    k = (qk[:, WIDTH_M + h * dm:WIDTH_M + (h + 1) * dm] * (dm ** -0.5))
    v_aug = jnp.concatenate([v_ref[g, :, h * dm:(h + 1) * dm], ones_col], axis=-1)
    b_col = bcum_c[:, SM_F + h:SM_F + h + 1]
    g_col = sm[:, SM_I + h:SM_I + h + 1] - b_col
    g_row = sm_t[SM_I + h:SM_I + h + 1, :] - bcum_r[SM_F + h:SM_F + h + 1, :]
    b_last = b_col[L - 1:L, :]
    m_prev = m_ref[st]
    c_prev = c_ref[st]

    log_d = jnp.where(causal, b_col + g_row, NEG_INF)
    m_j = jnp.maximum(b_col + m_prev, jnp.max(log_d, axis=-1, keepdims=True))
    w_inter = jnp.exp(b_col + m_prev - m_j)
    qkt = lax.dot_general(q, k.astype(BF16), (((1,), (1,)), ((), ())), preferred_element_type=F32)
    s = qkt * jnp.exp(log_d - m_j)
    o_aug = jnp.dot(s.astype(BF16), v_aug, preferred_element_type=F32) + \
        w_inter * jnp.dot(q, c_prev.astype(BF16), preferred_element_type=F32)
    num = o_aug[:, :dm]
    den = o_aug[:, dm:dm + 1]
    hh = num / jnp.maximum(jnp.abs(den), jnp.exp(-m_j))

    lwe = b_last + g_col
    m_loc = jnp.max(lwe, axis=0, keepdims=True)
    kw = (k * jnp.exp(lwe - m_loc)).astype(BF16)
    c_loc = lax.dot_general(kw, v_aug, (((0,), (0,)), ((), ())), preferred_element_type=F32)
    m_new = jnp.maximum(b_last + m_prev, m_loc)
    c_ref[st] = jnp.exp(b_last + m_prev - m_new) * c_prev + jnp.exp(m_loc - m_new) * c_loc
    m_ref[st] = m_new

    hn = hh * lax.rsqrt(jnp.mean(hh * hh, axis=-1, keepdims=True) + LN_EPS) * g_ref[:, h * dm:(h + 1) * dm]
    og = o_ref[g, :, h * dm:(h + 1) * dm].astype(F32)
    return _sigmoid(og) * hn


def _mlstm(qk, v, sm, o, conv_w, conv_b, mh_g, batch, seq):
    L = M_CHUNK
    G = M_GROUPS
    nc = seq // L
    T = batch * seq
    hb = L // CONV_HALO
    grouped = lambda a: a.reshape(G, T // G, a.shape[-1])
    blk = lambda n: pl.BlockSpec((G, L, n), lambda b, c: (0, b * nc + c, 0))
    in_specs = [
        blk(2 * WIDTH_M),
        pl.BlockSpec((G, CONV_HALO, 2 * WIDTH_M), lambda b, c: (0, jnp.maximum((b * nc + c) * hb - 1, 0), 0)),
        blk(WIDTH_M), blk(LANES), blk(WIDTH_M),
        _full(conv_w.shape), _full(conv_b.shape), _full(mh_g.shape),
    ]
    scratch = [pltpu.VMEM((G * N_HEADS_M, HEAD_DIM_M, 2 * HEAD_DIM_M), F32),
               pltpu.VMEM((G * N_HEADS_M, 1, 1), F32)]
    qk_g = grouped(qk)
    out = pl.pallas_call(
        _mlstm_kernel, grid=(batch // G, nc), in_specs=in_specs, out_specs=blk(WIDTH_M),
        out_shape=jax.ShapeDtypeStruct((G, T // G, WIDTH_M), BF16), scratch_shapes=scratch,
        compiler_params=_cparams(2), name="mlstm")(
            qk_g, qk_g, grouped(v), grouped(sm), grouped(o), conv_w, conv_b, mh_g)
    return out.reshape(T, WIDTH_M)


def _layer_norm(y, g, b):
    mu = jnp.mean(y, axis=-1, keepdims=True)
    var = jnp.mean(jnp.square(y - mu), axis=-1, keepdims=True)
    return (y - mu) * lax.rsqrt(var + LN_EPS) * g + b


def _router_logits(x1, wr, br):
    x1_hi = x1.astype(BF16)
    x1_lo = (x1 - x1_hi.astype(F32)).astype(BF16)
    both = jnp.dot(x1_hi, wr[...], preferred_element_type=F32)
    return (both[:, 0:LANES]
            + jnp.dot(x1_lo, wr[:, 0:LANES], preferred_element_type=F32)
            + both[:, LANES:2 * LANES]) + br[...]


def _merge_kernel(x_ref, oa_ref, hm_ref, ga_ref, gm_ref, wua, wum, wout, l1g, l1b, wr, br,
                  x1_ref, route_ref, cnt_ref):
    pa = jnp.dot(oa_ref[...], wua[...], preferred_element_type=F32)
    pm = jnp.dot(hm_ref[...], wum[...], preferred_element_type=F32)
    y = _sigmoid(ga_ref[...]) * pa.astype(BF16) + _sigmoid(gm_ref[...]) * pm.astype(BF16)
    mix = jnp.dot(y.astype(BF16), wout[...], preferred_element_type=F32)
    x1 = _layer_norm(ALPHA * x_ref[...] + mix, l1g[...], l1b[...])
    x1_ref[:, 0:D_MODEL] = x1

    lt = _router_logits(x1, wr, br).T
    tm = lt.shape[1]
    big = jnp.int32(LANES)
    le = lt[0:N_EXPERTS, :]
    lg = lt[N_EXPERTS:N_EXPERTS + SUBLANES, :]
    row_g = lax.broadcasted_iota(jnp.int32, lg.shape, 0)
    row_e = lax.broadcasted_iota(jnp.int32, le.shape, 0)
    is_grp = row_g < N_GROUPS
    gl = jnp.where(is_grp, lg, NEG_INF)
    ge = jnp.exp(gl - jnp.max(gl, axis=0, keepdims=True))
    gp = ge / jnp.sum(ge, axis=0, keepdims=True)
    g_w = jnp.max(gp, axis=0, keepdims=True)
    g_idx = jnp.min(jnp.where(jnp.logical_and(is_grp, gp == g_w), row_g, big), axis=0, keepdims=True)
    el = jnp.where(jnp.right_shift(row_e, EPG_SHIFT) == g_idx, le, NEG_INF)
    m1 = jnp.max(el, axis=0, keepdims=True)
    i1 = jnp.min(jnp.where(el == m1, row_e, big), axis=0, keepdims=True)
    el2 = jnp.where(row_e == i1, NEG_INF, el)
    m2 = jnp.max(el2, axis=0, keepdims=True)
    i2 = jnp.min(jnp.where(el2 == m2, row_e, big), axis=0, keepdims=True)

    e2 = jnp.exp(m2 - m1)
    w1 = g_w / (1.0 + e2)
    w2 = g_w * e2 / (1.0 + e2)
    first_is_a = i1 < i2
    row_w = lax.broadcasted_iota(jnp.int32, (LANES, tm), 0)
    w_rows = jnp.where(row_w == 0, jnp.where(first_is_a, w1, w2),
                       jnp.where(row_w == 1, jnp.where(first_is_a, w2, w1), 0.0))
    x1_ref[:, D_MODEL:D_MODEL + LANES] = w_rows.T

    a = jnp.bitwise_and(jnp.minimum(i1, i2), EXPERTS_PER_GROUP - 1)
    b = jnp.bitwise_and(jnp.maximum(i1, i2), EXPERTS_PER_GROUP - 1)
    pair = jnp.right_shift(a * (2 * EXPERTS_PER_GROUP - 1 - a), 1) + (b - a - 1)
    cls = g_idx * PAIRS_PER_GROUP + pair
    row_c = lax.broadcasted_iota(jnp.int32, (CLASS_ROWS, tm), 0)
    onehot = (row_c == cls).astype(F32)

    @pl.when(pl.program_id(0) == 0)
    def _init():
        cnt_ref[...] = jnp.zeros_like(cnt_ref)

    rb = min(RANK_BLOCK, tm)
    blocks = [onehot[:, j * rb:(j + 1) * rb] for j in range(tm // rb)]
    earlier = (lax.broadcasted_iota(jnp.int32, (rb, rb), 0)
               < lax.broadcasted_iota(jnp.int32, (rb, rb), 1)).astype(BF16)
    local = jnp.dot(jnp.concatenate(blocks, axis=0).astype(BF16), earlier, preferred_element_type=F32)
    before = cnt_ref[:, 0:1]
    ranks = []
    for j, blk in enumerate(blocks):
        prior = local[j * CLASS_ROWS:(j + 1) * CLASS_ROWS, :] + before
        ranks.append(jnp.sum(prior * blk, axis=0, keepdims=True))
        before = before + jnp.sum(blk, axis=1, keepdims=True)
    rank = jnp.concatenate(ranks, axis=1)
    cnt_ref[...] = jnp.broadcast_to(before, cnt_ref.shape)
    row_o = lax.broadcasted_iota(jnp.int32, route_ref.shape, 0)
    route_ref[...] = jnp.where(row_o == 0, cls.astype(F32), jnp.where(row_o == 1, rank, 0.0))


def _merge(x2, oa, hm, ga, gm, wua, wum, wout, l1g, l1b, wr, br, tm):
    T = x2.shape[0]
    blk = lambda n: pl.BlockSpec((tm, n), lambda i: (i, 0))
    in_specs = [blk(D_MODEL), blk(WIDTH_A), blk(WIDTH_M), blk(D_MODEL), blk(D_MODEL),
                _full(wua.shape), _full(wum.shape), _full(wout.shape), _full(l1g.shape), _full(l1b.shape),
                _full(wr.shape), _full(br.shape)]
    out_specs = [blk(D_MODEL + LANES), pl.BlockSpec((SUBLANES, tm), lambda i: (0, i)),
                 pl.BlockSpec((CLASS_ROWS, LANES), lambda i: (0, 0))]
    out_shape = [jax.ShapeDtypeStruct((T, D_MODEL + LANES), F32), jax.ShapeDtypeStruct((SUBLANES, T), F32),
                 jax.ShapeDtypeStruct((CLASS_ROWS, LANES), F32)]
    return pl.pallas_call(
        _merge_kernel, grid=(T // tm,), in_specs=in_specs, out_specs=out_specs, out_shape=out_shape,
        compiler_params=_cparams(1), name="merge")(x2, oa, hm, ga, gm, wua, wum, wout, l1g, l1b, wr, br)


def _sc_mesh():
    return plsc.VectorSubcoreMesh(core_axis_name="c", subcore_axis_name="s",
                                  num_cores=SC_CORES, num_subcores=SC_SUBCORES)


def _sc_chunks(n_rows):
    workers = SC_CORES * SC_SUBCORES
    assert n_rows % (workers * SC_ROWS * 2) == 0
    return n_rows // (workers * SC_ROWS)


def _sc_scratch(n_chunks, width, dtype):
    return [pltpu.VMEM((n_chunks, SC_ROWS), jnp.int32),
            pltpu.VMEM((SC_ROWS, width), dtype), pltpu.VMEM((SC_ROWS, width), dtype),
            pltpu.SemaphoreType.DMA, pltpu.SemaphoreType.DMA]


def _sc_scatter_rows(rows, idx, n_out):
    n_in, width = rows.shape
    n_chunks = _sc_chunks(n_in)

    @functools.partial(
        pl.kernel, mesh=_sc_mesh(), out_type=jax.ShapeDtypeStruct((n_out, width), rows.dtype),
        scratch_types=_sc_scratch(n_chunks, width, rows.dtype), name="sc_dispatch")
    def scatter(rows_hbm, idx_hbm, out_hbm, idx_v, rows_a, rows_b, sem_a, sem_b):
        first = (lax.axis_index("s") * SC_CORES + lax.axis_index("c")) * n_chunks
        pltpu.sync_copy(idx_hbm.at[pl.ds(first, n_chunks)], idx_v)

        def load(c, buf):
            pltpu.sync_copy(rows_hbm.at[pl.ds((first + c) * SC_ROWS, SC_ROWS)], buf)

        def put(c, buf, sem):
            return pltpu.make_async_copy(buf, out_hbm.at[idx_v.at[c]], sem)

        load(0, rows_a)
        put(0, rows_a, sem_a).start()

        @pl.loop(0, n_chunks, step=2)
        def _(j):
            load(j + 1, rows_b)
            put(j + 1, rows_b, sem_b).start()
            put(j, rows_a, sem_a).wait()

            @pl.when(j + 2 < n_chunks)
            def _():
                load(j + 2, rows_a)
                put(j + 2, rows_a, sem_a).start()

            put(j + 1, rows_b, sem_b).wait()

    return scatter(rows, idx.reshape(n_in // SC_ROWS, SC_ROWS))


def _sc_gather_rows(table, idx):
    n_out, width = idx.shape[0], table.shape[1]
    n_chunks = _sc_chunks(n_out)

    @functools.partial(
        pl.kernel, mesh=_sc_mesh(), out_type=jax.ShapeDtypeStruct((n_out, width), table.dtype),
        scratch_types=_sc_scratch(n_chunks, width, table.dtype), name="sc_combine")
    def gather(table_hbm, idx_hbm, out_hbm, idx_v, rows_a, rows_b, sem_a, sem_b):
        first = (lax.axis_index("s") * SC_CORES + lax.axis_index("c")) * n_chunks
        pltpu.sync_copy(idx_hbm.at[pl.ds(first, n_chunks)], idx_v)

        def fetch(c, buf, sem):
            return pltpu.make_async_copy(table_hbm.at[idx_v.at[c]], buf, sem)

        def store(c, buf):
            pltpu.sync_copy(buf, out_hbm.at[pl.ds((first + c) * SC_ROWS, SC_ROWS)])

        fetch(0, rows_a, sem_a).start()

        @pl.loop(0, n_chunks, step=2)
        def _(j):
            fetch(j + 1, rows_b, sem_b).start()
            fetch(j, rows_a, sem_a).wait()
            store(j, rows_a)

            @pl.when(j + 2 < n_chunks)
            def _():
                fetch(j + 2, rows_a, sem_a).start()

            fetch(j + 1, rows_b, sem_b).wait()
            store(j + 1, rows_b)

    return gather(table, idx.reshape(n_out // SC_ROWS, SC_ROWS))


def _moe_kernel(ta_ref, tb_ref, nu_ref, xs_ref, wga, wua, wda, wgb, wub, wdb, l2g, l2b, ys_ref,
                pre_ref):
    i = pl.program_id(0)
    n_used = nu_ref[0]
    tm = pre_ref.shape[0]
    pr = tm // MOE_NORM_PIECES

    def norm_rows(piece):
        rs = slice(piece * pr, (piece + 1) * pr)
        ys_ref[rs, :] = _layer_norm(pre_ref[rs, :], l2g[...], l2b[...])

    @pl.when(i == 0)
    def _first():
        pre_ref[...] = jnp.zeros_like(pre_ref)

    @pl.when(i < n_used)
    def _compute():
        x = xs_ref[:, 0:D_MODEL]
        xb = x.astype(BF16)
        w_a = xs_ref[:, D_MODEL:D_MODEL + 1]
        w_b = xs_ref[:, D_MODEL + 1:D_MODEL + 2]
        piece = iter(range(MOE_NORM_PIECES))

        def dot_pieces(lhs, w):
            outs = []
            for n in range(w.shape[1] // MOE_DOT_COLS):
                outs.append(jnp.dot(lhs, w[:, n * MOE_DOT_COLS:(n + 1) * MOE_DOT_COLS],
                                    preferred_element_type=F32))
                norm_rows(next(piece))
            return jnp.concatenate(outs, axis=1)

        def expert(wg, wu, wd):
            g = dot_pieces(xb, wg)
            u = dot_pieces(xb, wu)
            hdn = (g * _sigmoid(g) * u).astype(BF16)
            return dot_pieces(hdn, wd)

        ffn = w_a * expert(wga, wua, wda) + w_b * expert(wgb, wub, wdb)
        pre_ref[...] = ALPHA * x + ffn

    @pl.when(i == n_used)
    def _last():
        for piece in range(MOE_NORM_PIECES):
            norm_rows(piece)


def _moe(xs, tile_a, tile_b, n_used, wg, wu, wd, l2g, l2b):
    tm = MOE_TILE
    n_tiles = xs.shape[0] // tm
    used = lambda i, nu: jnp.minimum(i, nu[0] - 1)
    rows = lambda n: pl.BlockSpec((tm, n), lambda i, ta, tb, nu: (used(i, nu), 0))
    prev_rows = pl.BlockSpec((tm, D_MODEL), lambda i, ta, tb, nu: (used(jnp.maximum(i - 1, 0), nu), 0))
    w_in = lambda which: pl.BlockSpec(
        (None, D_MODEL, D_EXPERT), lambda i, ta, tb, nu: ((ta, tb)[which][used(i, nu)], 0, 0))
    w_out = lambda which: pl.BlockSpec(
        (None, D_EXPERT, D_MODEL), lambda i, ta, tb, nu: ((ta, tb)[which][used(i, nu)], 0, 0))
    const = lambda shape: pl.BlockSpec(shape, lambda i, ta, tb, nu: (0,) * len(shape))
    grid_spec = pltpu.PrefetchScalarGridSpec(
        num_scalar_prefetch=3, grid=(n_tiles + 1,),
        in_specs=[rows(xs.shape[1]), w_in(0), w_in(0), w_out(0), w_in(1), w_in(1), w_out(1),
                  const(l2g.shape), const(l2b.shape)],
        out_specs=prev_rows,
        scratch_shapes=[pltpu.VMEM((tm, D_MODEL), F32)])
    return pl.pallas_call(
        _moe_kernel, grid_spec=grid_spec, out_shape=jax.ShapeDtypeStruct((xs.shape[0], D_MODEL), F32),
        compiler_params=_cparams(1), name="moe")(
            tile_a, tile_b, n_used, xs, wg, wu, wd, wg, wu, wd, l2g, l2b)


def _route_tables(route, counts, n_tokens):
    tm = MOE_TILE
    n_tiles = n_tokens // tm + N_CLASSES
    cnt = counts[:N_CLASSES, 0].astype(jnp.int32)
    tiles = (cnt + tm - 1) // tm
    tile_end = jnp.cumsum(tiles)
    tile_start = tile_end - tiles
    t_idx = jnp.arange(n_tiles, dtype=jnp.int32)
    cls_of_tile = jnp.minimum(jnp.sum(t_idx[:, None] >= tile_end[None, :], axis=1), N_CLASSES - 1).astype(jnp.int32)
    classes = np.arange(N_CLASSES)
    tile_is = cls_of_tile[:, None] == classes[None, :]
    per_tile = lambda table: jnp.sum(jnp.where(tile_is, jnp.asarray(table, jnp.int32)[None, :], 0), axis=1)
    first_expert = classes // PAIRS_PER_GROUP * EXPERTS_PER_GROUP
    tile_a = per_tile(first_expert + np.asarray(PAIR_A)[classes % PAIRS_PER_GROUP])
    tile_b = per_tile(first_expert + np.asarray(PAIR_B)[classes % PAIRS_PER_GROUP])
    cls = route[0].astype(jnp.int32)
    rank = route[1].astype(jnp.int32)
    row0 = jnp.sum(jnp.where(cls[:, None] == jnp.arange(N_CLASSES)[None, :], (tile_start * tm)[None, :], 0), axis=1)
    n_used = jnp.maximum(tile_end[-1:], 1).astype(jnp.int32)
    return tile_a, tile_b, n_used, row0 + rank, n_tiles * tm


def _pick_tile(T, pref):
    t = pref
    while T % t:
        t //= 2
    return t


def kernel(x, w_in, conv_w, conv_b, kv_norm_g, w_uk, w_uv, rel_bias, b_i, b_f, mh_norm_g, w_up_a, w_up_m,
           w_out, ln1_g, ln1_b, w_grp, b_grp, w_rt, b_rt, w_gate, w_up, w_down, ln2_g, ln2_b):
    B, S, _ = x.shape
    T = B * S
    assert S % Q_TILE == 0 and S % M_CHUNK == 0 and T % MOE_TILE == 0 and w_in.shape[0] == DEPTH
    tz = _bias_tables(rel_bias)
    x2 = x.reshape(T, D_MODEL)
    for l in range(DEPTH):
        w = w_in[l]
        o = np.cumsum((WIDTH_A, KV_RANK, WIDTH_IDX, HEAD_DIM_IDX, N_HEADS_IDX, 2 * WIDTH_M, WIDTH_M,
                       N_HEADS_M, N_HEADS_M, WIDTH_M, D_MODEL, D_MODEL)).tolist()
        o = [0] + o
        seg = lambda j: w[:, o[j]:o[j + 1]]
        pad = LANES - (HEAD_DIM_IDX + N_HEADS_IDX + 2 * N_HEADS_M)
        w_small = jnp.concatenate([seg(3), seg(4), seg(7), seg(8), jnp.zeros((D_MODEL, pad), w.dtype)], axis=1)
        ws = [seg(0), seg(1), seg(2), w_small, seg(5), seg(6), seg(9), seg(10), seg(11)]
        ws = [a.astype(BF16) for a in ws]
        smb = jnp.zeros((1, LANES), F32).at[0, SM_I:SM_I + N_HEADS_M].set(b_i[l]) \
            .at[0, SM_F:SM_F + N_HEADS_M].set(b_f[l])
        wukv = jnp.concatenate([w_uk[l].reshape(KV_RANK, WIDTH_A), w_uv[l].reshape(KV_RANK, WIDTH_A)],
                               axis=1).astype(BF16)
        qa, ka, qi, sm, qk, v, og, ga, gm, va_t = _proj(x2, ws, wukv, kv_norm_g[l][None, :], smb,
                                                        _pick_tile(S, PROJ_TILE), S)
        oa = _dsa(qi, sm, qa, ka, va_t, tz, B, S)

        hm = _mlstm(qk, v, sm, og, conv_w[l], conv_b[l][None, :], mh_norm_g[l].reshape(1, WIDTH_M), B, S)

        w_router = jnp.concatenate(
            [w_rt[l], w_grp[l], jnp.zeros((D_MODEL, LANES - N_EXPERTS - N_GROUPS), F32)], axis=1)
        b_router = jnp.concatenate(
            [b_rt[l], b_grp[l], jnp.zeros((LANES - N_EXPERTS - N_GROUPS,), F32)])[None, :]
        wr_hi = w_router.astype(BF16)
        wr_split = jnp.concatenate([wr_hi, (w_router - wr_hi.astype(F32)).astype(BF16)], axis=1)
        x1, route, counts = _merge(x2, oa, hm, ga, gm, w_up_a[l].astype(BF16), w_up_m[l].astype(BF16),
                                   w_out[l].astype(BF16), ln1_g[l][None, :], ln1_b[l][None, :],
                                   wr_split, b_router, _pick_tile(T, MERGE_TILE))

        tile_a, tile_b, n_used, pos, n_sorted = _route_tables(route, counts, T)
        xs = _sc_scatter_rows(x1, pos, n_sorted)
        ys = _moe(xs, tile_a, tile_b, n_used, w_gate[l].astype(BF16), w_up[l].astype(BF16),
                  w_down[l].astype(BF16), ln2_g[l][None, :], ln2_b[l][None, :])
        x2 = _sc_gather_rows(ys, pos)
    return x2.reshape(B, S, D_MODEL)
```

```python
import functools
import math

import jax
import jax.numpy as jnp
import numpy as np
from jax import lax
from jax.experimental import pallas as pl
from jax.experimental.pallas import tpu as pltpu
from jax.experimental.pallas import tpu_sc as plsc

F32 = jnp.float32
BF16 = jnp.bfloat16

D_MODEL = 1024
N_HEADS_A = 8
HEAD_DIM_A = 64
WIDTH_A = N_HEADS_A * HEAD_DIM_A
KV_RANK = 256
N_HEADS_IDX = 8
HEAD_DIM_IDX = 64
WIDTH_IDX = N_HEADS_IDX * HEAD_DIM_IDX
TOPK_MAX = 256
N_BUCKETS = 32
MAX_DISTANCE = 128
N_HEADS_M = 4
HEAD_DIM_M = 128
WIDTH_M = N_HEADS_M * HEAD_DIM_M
CONV_WIDTH = 4
N_GROUPS = 4
EXPERTS_PER_GROUP = 4
N_EXPERTS = N_GROUPS * EXPERTS_PER_GROUP
D_EXPERT = 512
LN_EPS = 1e-5
DEPTH = 1
ALPHA = (2.0 * DEPTH) ** 0.25

LANES = 128
SUBLANES = 8
VMEM_LIMIT = 56 * 1024 * 1024

SM_KIDX = 0
SM_WIDX = HEAD_DIM_IDX
SM_I = SM_WIDX + N_HEADS_IDX
SM_F = SM_I + N_HEADS_M

HEADS_PER_DOT = LANES // HEAD_DIM_A
assert HEADS_PER_DOT * HEAD_DIM_A == LANES and N_HEADS_A % HEADS_PER_DOT == 0
Q_TILE = 256
K_CHUNK = Q_TILE
MXU_CHUNKS_PER_TRIP = 4
ONES_ROWS = 16
V_ROWS = HEAD_DIM_A + ONES_ROWS
M_CHUNK = 128
M_GROUPS = 2
CONV_HALO = 16
assert CONV_HALO >= CONV_WIDTH - 1
PROJ_TILE = 1024
MERGE_TILE = 1024
RANK_BLOCK = 256
MOE_TILE = 512
MOE_DOT_COLS = 256
MOE_NORM_PIECES = 2 * (2 * D_EXPERT + D_MODEL) // MOE_DOT_COLS
SC_CORES = 2
SC_SUBCORES = 16
SC_ROWS = 32
EPG_SHIFT = EXPERTS_PER_GROUP.bit_length() - 1
assert 1 << EPG_SHIFT == EXPERTS_PER_GROUP
PAIR_A, PAIR_B = zip(*[(a, b) for a in range(EXPERTS_PER_GROUP) for b in range(a + 1, EXPERTS_PER_GROUP)])
PAIRS_PER_GROUP = len(PAIR_A)
N_CLASSES = N_GROUPS * PAIRS_PER_GROUP
CLASS_ROWS = -(-N_CLASSES // SUBLANES) * SUBLANES
BISECT_STEPS = 15
NEG_INF = float("-inf")
LOG2E = math.log2(math.e)


def _cparams(n_grid):
    return pltpu.CompilerParams(dimension_semantics=("arbitrary",) * n_grid,
                                vmem_limit_bytes=VMEM_LIMIT)


def _full(shape):
    nd = len(shape)
    return pl.BlockSpec(shape, lambda *_: (0,) * nd, pipeline_mode=pl.Buffered(1))


def _sigmoid(x):
    return 0.5 * jnp.tanh(0.5 * x) + 0.5


def _split3(x):
    hi = x.astype(BF16)
    r = x - hi.astype(F32)
    mid = r.astype(BF16)
    return hi, mid, (r - mid.astype(F32)).astype(BF16)


def _proj_kernel(x_ref, wqa, wckv, wqi, wsm, wqk, wv, wo, wga, wgm, wukv, kvg, smb,
                 qa_o, ka_o, qi_o, sm_o, qk_o, v_o, o_o, ga_o, gm_o, vat_o):
    xb = x_ref[...].astype(BF16)

    def mm(w):
        return jnp.dot(xb, w[...], preferred_element_type=F32)

    c = mm(wckv)
    qa_o[...] = (mm(wqa) * (HEAD_DIM_A ** -0.5 * LOG2E)).astype(BF16)
    c = c * lax.rsqrt(jnp.mean(c * c, axis=-1, keepdims=True) + LN_EPS) * kvg[...]
    qi_o[...] = mm(wqi).astype(BF16)
    kv = jnp.dot(c.astype(BF16), wukv[...], preferred_element_type=F32)
    ka_o[...] = kv[:, 0:WIDTH_A].astype(BF16)
    va_t = kv[:, WIDTH_A:2 * WIDTH_A].T.astype(BF16)
    for h in range(N_HEADS_A):
        r0 = h * V_ROWS
        vat_o[r0:r0 + HEAD_DIM_A, :] = va_t[h * HEAD_DIM_A:(h + 1) * HEAD_DIM_A, :]
        vat_o[r0 + HEAD_DIM_A:r0 + V_ROWS, :] = jnp.ones((ONES_ROWS, va_t.shape[1]), BF16)
    sm_o[...] = mm(wsm) + smb[...]
    qk_o[...] = mm(wqk).astype(BF16)
    v_o[...] = mm(wv).astype(BF16)
    o_o[...] = mm(wo).astype(BF16)
    ga_o[...] = mm(wga).astype(BF16)
    gm_o[...] = mm(wgm).astype(BF16)


def _proj(x2, ws, wukv, kvg, smb, tm, seq):
    T = x2.shape[0]
    per_seq = seq // tm

    def by_token(n, dt=BF16):
        return pl.BlockSpec((tm, n), lambda i: (i, 0)), jax.ShapeDtypeStruct((T, n), dt)

    def by_lane(n):
        return (pl.BlockSpec((None, n, tm), lambda i: (i // per_seq, 0, i % per_seq)),
                jax.ShapeDtypeStruct((T // seq, n, seq), BF16))

    outs = [by_token(WIDTH_A), by_token(WIDTH_A), by_token(WIDTH_IDX), by_token(LANES, F32)]
    outs += [by_token(w.shape[1]) for w in ws[4:]]
    outs.append(by_lane(N_HEADS_A * V_ROWS))
    in_specs = [pl.BlockSpec((tm, D_MODEL), lambda i: (i, 0))]
    in_specs += [_full(a.shape) for a in (*ws, wukv, kvg, smb)]
    return pl.pallas_call(
        _proj_kernel, grid=(T // tm,), in_specs=in_specs, out_specs=[o[0] for o in outs],
        out_shape=[o[1] for o in outs], compiler_params=_cparams(1), name="proj")(x2, *ws, wukv, kvg, smb)


def _dsa_kernel(qi_ref, smq_ref, smk_ref, qa_ref, ka_ref, vat_ref, tz_ref,
                oa_ref, qs_ref, sc_ref, am_ref, ql_ref, x_ref, acc_ref, m_ref, *, topk):
    tq, kc = Q_TILE, K_CHUNK
    nh = N_HEADS_A
    qb = pl.program_id(1)
    n_ch = qb + 1
    seq_keys = n_ch * kc
    t0 = qb * tq

    def rows(c):
        return pl.ds(pl.multiple_of(c * kc, kc), kc)

    def lanes(h):
        return slice(h * tq, (h + 1) * tq)

    def chunk_loop(n, body, init, per_trip=2):
        def run(first, count, carry):
            for r in range(count):
                carry = body(first + r, carry)
            return carry

        shift = per_trip.bit_length() - 1
        trips = jnp.right_shift(n, shift)
        carry = lax.fori_loop(0, trips, lambda p, c: run(p * per_trip, per_trip, c), init)
        done = trips * per_trip
        count = per_trip // 2
        while count:
            start = done + jnp.bitwise_and(n - done, -2 * count)
            carry = lax.cond(jnp.bitwise_and(n, count) != 0,
                             functools.partial(run, start, count), lambda c: c, carry)
            count //= 2
        return carry

    for h in range(N_HEADS_IDX):
        qs_ref[h * tq:(h + 1) * tq, :] = qi_ref[:, h * HEAD_DIM_IDX:(h + 1) * HEAD_DIM_IDX]
    w_t = smq_ref[...].T
    q_pos = lax.broadcasted_iota(jnp.int32, (1, tq), 1) + t0
    key_iota = lax.broadcasted_iota(jnp.int32, (kc, tq), 0)

    def score_chunk(c, carry):
        mx, mn = carry
        kk = smk_ref[rows(c), SM_KIDX:SM_KIDX + HEAD_DIM_IDX].astype(BF16)
        dots = lax.dot_general(kk, qs_ref[...], (((1,), (1,)), ((), ())), preferred_element_type=F32)
        sc = jnp.zeros((kc, tq), F32)
        for h in range(N_HEADS_IDX):
            sc = sc + w_t[SM_WIDX + h:SM_WIDX + h + 1, :] * jnp.maximum(dots[:, lanes(h)], 0.0)
        vis = (key_iota + c * kc) <= q_pos
        sc_ref[rows(c), :] = jnp.where(vis, sc, NEG_INF)
        mx = jnp.maximum(mx, jnp.max(jnp.where(vis, sc, NEG_INF), axis=0, keepdims=True))
        mn = jnp.minimum(mn, jnp.min(jnp.where(vis, sc, jnp.inf), axis=0, keepdims=True))
        return mx, mn

    mx, mn = chunk_loop(n_ch, score_chunk,
                        (jnp.full((1, tq), NEG_INF, F32), jnp.full((1, tq), jnp.inf, F32)), MXU_CHUNKS_PER_TRIP)

    n_vis = (q_pos + 1).astype(F32)
    k_row = jnp.minimum(n_vis, float(topk))

    def count(pred):
        def body(c, a):
            hit = pred(sc_ref[rows(c), :]).astype(F32)
            return a + jnp.sum(hit.reshape(kc // SUBLANES, SUBLANES, tq), axis=0)
        a = chunk_loop(n_ch, body, jnp.zeros((SUBLANES, tq), F32))
        return jnp.sum(a, axis=0, keepdims=True)

    def any_lane(flag):
        return jnp.max(jnp.where(flag, 1.0, 0.0)) > 0.0

    def bisect_step(_, carry):
        lo, hi, cnt_lo, c_hi = carry
        mid = lo * 0.5 + hi * 0.5
        cnt = count(lambda s: s >= mid)
        ge = cnt >= k_row
        return jnp.where(ge, mid, lo), jnp.where(ge, hi, mid), jnp.where(ge, cnt, cnt_lo), jnp.where(ge, c_hi, cnt)

    def peel_cond(carry):
        it, lo, hi, cnt_lo, c_hi, done = carry
        return jnp.logical_and(it < seq_keys, any_lane(done == 0.0))

    def peel_body(carry):
        it, lo, hi, cnt_lo, c_hi, done = carry

        def top_body(c, v):
            s = sc_ref[rows(c), :]
            inside = jnp.logical_and(s >= lo, s < hi)
            return jnp.maximum(v, jnp.max(jnp.where(inside, s, NEG_INF), axis=0, keepdims=True))

        v = chunk_loop(n_ch, top_body, jnp.full((1, tq), NEG_INF, F32))
        c_v = count(lambda s: s >= v)
        reached = c_v >= k_row
        live = done == 0.0
        fin = jnp.logical_and(live, reached)
        cut = jnp.logical_and(live, jnp.logical_not(reached))
        return (it + 1, jnp.where(fin, v, lo), jnp.where(cut, v, hi), jnp.where(fin, c_v, cnt_lo),
                jnp.where(cut, c_v, c_hi), jnp.where(fin, 1.0, done))

    lane_tiles = [slice(part * LANES, (part + 1) * LANES) for part in range(tq // LANES)]

    def flags(carry):
        _, _, _, cnt_lo, _, done = carry
        tied = cnt_lo != k_row
        return any_lane(done == 0.0), [any_lane(tied[:, ls]) for ls in lane_tiles]

    def search(carry):
        lo, hi, cnt_lo, c_hi = lax.fori_loop(0, BISECT_STEPS, bisect_step, carry)
        carry = peel_body((jnp.int32(0), lo, hi, cnt_lo, c_hi, jnp.where(cnt_lo == k_row, 1.0, 0.0)))
        still_open, any_tied = flags(carry)

        def keep_peeling(carry):
            carry = lax.while_loop(peel_cond, peel_body, carry)
            return carry, flags(carry)[1]

        carry, any_tied = lax.cond(still_open, keep_peeling, lambda carry: (carry, any_tied), carry)
        return carry[1:5], any_tied

    hi0 = mx + jnp.maximum(jnp.abs(mx), 1e-30) * 1e-6
    (lo, hi, cnt_lo, c_hi), any_tied = lax.cond(
        t0 + tq > topk, search, lambda carry: (carry, [jnp.zeros((), jnp.bool_)] * len(lane_tiles)),
        (mn, hi0, n_vis, jnp.zeros((1, tq), F32)))

    def mask_chunk(c, _):
        am_ref[rows(c), :] = jnp.where(sc_ref[rows(c), :] >= lo, 0.0, NEG_INF)
        return 0

    chunk_loop(n_ch, mask_chunk, 0)

    for ls, flag in zip(lane_tiles, any_tied):

        @pl.when(flag)
        def _ties():
            need, lo_p, hi_p = (k_row - c_hi)[:, ls], lo[:, ls], hi[:, ls]
            lower = (lax.broadcasted_iota(jnp.int32, (kc, kc), 1)
                     < lax.broadcasted_iota(jnp.int32, (kc, kc), 0)).astype(BF16)

            def tie_chunk(c, before):
                s = sc_ref[rows(c), ls]
                above = s >= hi_p
                tie = jnp.logical_and(s >= lo_p, jnp.logical_not(above))
                tie_f = tie.astype(F32)
                rank = jnp.dot(lower, tie_f.astype(BF16), preferred_element_type=F32) + before
                sel = jnp.logical_or(above, jnp.logical_and(tie, rank < need))
                am_ref[rows(c), ls] = jnp.where(sel, 0.0, NEG_INF)
                return before + jnp.sum(tie_f, axis=0, keepdims=True)

            chunk_loop(n_ch, tie_chunk, jnp.zeros((1, LANES), F32), MXU_CHUNKS_PER_TRIP)

    head_of_lane = lax.broadcasted_iota(jnp.int32, (tq, LANES), 1) // HEAD_DIM_A
    for pair in range(nh // HEADS_PER_DOT):
        q_tile = qa_ref[:, pair * LANES:(pair + 1) * LANES]
        for slot in range(HEADS_PER_DOT):
            ql_ref[pair, slot * tq:(slot + 1) * tq, :] = jnp.where(head_of_lane == slot, q_tile, 0)

    m_ref[...] = jnp.full(m_ref.shape, NEG_INF, F32)

    def pass_a(c, table):
        am = am_ref[rows(c), :]
        for pair in range(nh // HEADS_PER_DOT):
            raw = lax.dot_general(ka_ref[rows(c), pair * LANES:(pair + 1) * LANES], ql_ref[pair],
                                  (((1,), (1,)), ((), ())), preferred_element_type=F32)
            for slot in range(HEADS_PER_DOT):
                h = pair * HEADS_PER_DOT + slot
                x = raw[:, slot * tq:(slot + 1) * tq] + am
                if table is not None:
                    x = x + tz_ref[table, h]
                x_ref[rows(c), lanes(h)] = x
                m_ref[:, lanes(h)] = jnp.maximum(m_ref[:, lanes(h)], jnp.max(x, axis=0, keepdims=True))

    def far_chunk(c, _):
        pass_a(c, None)
        return 0

    chunk_loop(jnp.maximum(qb - 1, 0), far_chunk, 0, MXU_CHUNKS_PER_TRIP)

    @pl.when(qb >= 1)
    def _prev():
        pass_a(qb - 1, 1)

    pass_a(qb, 0)

    def pv(c):
        p = jnp.exp2((x_ref[rows(c), :] - m_ref[...]).astype(BF16))
        return jnp.concatenate(
            [jnp.dot(vat_ref[h * V_ROWS:(h + 1) * V_ROWS, rows(c)], p[:, lanes(h)], preferred_element_type=F32)
             for h in range(nh)], axis=0)

    def pass_b(c, _):
        acc_ref[...] += pv(c)
        return 0

    acc_ref[...] = pv(qb)
    chunk_loop(qb, pass_b, 0, MXU_CHUNKS_PER_TRIP)
    outs = []
    for h in range(nh):
        blk = acc_ref[h * V_ROWS:(h + 1) * V_ROWS, :]
        outs.append(blk[0:HEAD_DIM_A, :] * (1.0 / blk[HEAD_DIM_A:HEAD_DIM_A + 1, :]))
    oa_ref[...] = jnp.concatenate(outs, axis=0).T.astype(BF16)


def _dsa(qi, sm, qa, ka, va_t, tz, batch, seq):
    tq = Q_TILE
    nq = seq // tq
    topk = min(TOPK_MAX, seq // 4)
    T = batch * seq
    kern = functools.partial(_dsa_kernel, topk=topk)
    blk_q = lambda n: pl.BlockSpec((tq, n), lambda b, q: (b * nq + q, 0))
    blk_s = lambda n: pl.BlockSpec((seq, n), lambda b, q: (b, 0))
    in_specs = [
        blk_q(WIDTH_IDX), blk_q(LANES), blk_s(LANES),
        blk_q(WIDTH_A), blk_s(WIDTH_A),
        pl.BlockSpec((None, N_HEADS_A * V_ROWS, seq), lambda b, q: (b, 0, 0)),
        _full(tz.shape),
    ]
    scratch = [
        pltpu.VMEM((N_HEADS_IDX * tq, HEAD_DIM_IDX), BF16),
        pltpu.VMEM((seq, tq), F32),
        pltpu.VMEM((seq, tq), F32),
        pltpu.VMEM((N_HEADS_A // HEADS_PER_DOT, HEADS_PER_DOT * tq, LANES), BF16),
        pltpu.VMEM((seq, N_HEADS_A * tq), F32),
        pltpu.VMEM((N_HEADS_A * V_ROWS, tq), F32),
        pltpu.VMEM((1, N_HEADS_A * tq), F32),
    ]
    return pl.pallas_call(
        kern, grid=(batch, nq), in_specs=in_specs,
        out_specs=pl.BlockSpec((tq, WIDTH_A), lambda b, q: (b * nq + q, 0)),
        out_shape=jax.ShapeDtypeStruct((T, WIDTH_A), BF16),
        scratch_shapes=scratch, compiler_params=_cparams(2), name="dsa")(
            qi, sm, sm, qa, ka, va_t, tz)


def _t5_bucket(dist):
    max_exact = N_BUCKETS // 2
    d = jnp.maximum(dist, 0)
    ratio = jnp.log(jnp.maximum(d, 1).astype(F32) / max_exact) / math.log(MAX_DISTANCE / max_exact)
    large = jnp.minimum(max_exact + (ratio * (N_BUCKETS - max_exact)).astype(jnp.int32), N_BUCKETS - 1)
    return jnp.where(d < max_exact, d, large)


def _bias_tables(rel_bias):
    tq = Q_TILE
    span = 2 * tq
    assert int(_np_bucket(tq)) == N_BUCKETS - 1
    far = rel_bias[N_BUCKETS - 1]
    by_dist = ((rel_bias[_t5_bucket(jnp.arange(span))] - far).astype(F32) * LOG2E).T
    diag = jnp.concatenate([by_dist[:, :tq], jnp.zeros_like(by_dist[:, :tq])], axis=1)
    prev = jnp.concatenate([by_dist[:, tq:], by_dist[:, :tq]], axis=1)

    def toeplitz(f):
        m = jnp.tile(f, (1, tq))[:, :tq * (span - 1)].reshape(N_HEADS_A, tq, span - 1)
        return m[:, :, :tq]

    return jnp.stack([toeplitz(diag), toeplitz(prev)])


def _np_bucket(d):
    max_exact = N_BUCKETS // 2
    ratio = np.log(np.float32(max(d, 1)) / np.float32(max_exact)) / math.log(MAX_DISTANCE / max_exact)
    return min(max_exact + int(ratio * (N_BUCKETS - max_exact)), N_BUCKETS - 1) if d >= max_exact else d


def _mlstm_kernel(qk_ref, halo_ref, v_ref, sm_ref, o_ref, cw_ref, cb_ref, g_ref,
                  out_ref, c_ref, m_ref):
    c_idx = pl.program_id(1)

    @pl.when(c_idx == 0)
    def _init():
        c_ref[...] = jnp.zeros_like(c_ref)
        m_ref[...] = jnp.zeros_like(m_ref)

    for g in range(M_GROUPS):
        gates = _mlstm_gates(g, c_idx, qk_ref, halo_ref, sm_ref, cw_ref, cb_ref)
        outs = [_mlstm_head(g, h, gates, v_ref, o_ref, g_ref, c_ref, m_ref) for h in range(N_HEADS_M)]
        out_ref[g] = jnp.concatenate(outs, axis=-1).astype(BF16)


def _mlstm_gates(g, c_idx, qk_ref, halo_ref, sm_ref, cw_ref, cb_ref):
    L = M_CHUNK

    hw = halo_ref.shape[1]
    halo = jnp.where(c_idx > 0, halo_ref[g].astype(F32), 0.0)
    ext = jnp.concatenate([halo, qk_ref[g].astype(F32)], axis=0)
    acc = jnp.zeros((L, 2 * WIDTH_M), F32) + cb_ref[...]
    for w in range(CONV_WIDTH):
        off = hw - (CONV_WIDTH - 1) + w
        acc = acc + ext[off:off + L, :] * cw_ref[w:w + 1, :]
    qk = acc * _sigmoid(acc)

    sm = sm_ref[g]
    sm_t = sm.T
    r_i = lax.broadcasted_iota(jnp.int32, (L, L), 0)
    c_i = lax.broadcasted_iota(jnp.int32, (L, L), 1)
    tril = (c_i <= r_i).astype(BF16)
    bcum_c = sum(jnp.dot(tril, part, preferred_element_type=F32)
                 for part in _split3(jax.nn.log_sigmoid(sm)))
    return qk, sm, sm_t, bcum_c, bcum_c.T, c_i <= r_i


def _mlstm_head(g, h, gates, v_ref, o_ref, g_ref, c_ref, m_ref):
    L = M_CHUNK
    dm = HEAD_DIM_M
    qk, sm, sm_t, bcum_c, bcum_r, causal = gates
    ones_col = (lax.broadcasted_iota(jnp.int32, (L, dm), 1) == 0).astype(BF16)
    st = g * N_HEADS_M + h
    q = qk[:, h * dm:(h + 1) * dm].astype(BF16)
    k = (qk[:, WIDTH_M + h * dm:WIDTH_M + (h + 1) * dm] * (dm ** -0.5))
    v_aug = jnp.concatenate([v_ref[g, :, h * dm:(h + 1) * dm], ones_col], axis=-1)
    b_col = bcum_c[:, SM_F + h:SM_F + h + 1]
    g_col = sm[:, SM_I + h:SM_I + h + 1] - b_col
    g_row = sm_t[SM_I + h:SM_I + h + 1, :] - bcum_r[SM_F + h:SM_F + h + 1, :]
    b_last = b_col[L - 1:L, :]
    m_prev = m_ref[st]
    c_prev = c_ref[st]

    log_d = jnp.where(causal, b_col + g_row, NEG_INF)
    m_j = jnp.maximum(b_col + m_prev, jnp.max(log_d, axis=-1, keepdims=True))
    w_inter = jnp.exp(b_col + m_prev - m_j)
    qkt = lax.dot_general(q, k.astype(BF16), (((1,), (1,)), ((), ())), preferred_element_type=F32)
    s = qkt * jnp.exp(log_d - m_j)
    o_aug = jnp.dot(s.astype(BF16), v_aug, preferred_element_type=F32) + \
        w_inter * jnp.dot(q, c_prev.astype(BF16), preferred_element_type=F32)
    num = o_aug[:, :dm]
    den = o_aug[:, dm:dm + 1]
    hh = num / jnp.maximum(jnp.abs(den), jnp.exp(-m_j))

    lwe = b_last + g_col
    m_loc = jnp.max(lwe, axis=0, keepdims=True)
    kw = (k * jnp.exp(lwe - m_loc)).astype(BF16)
    c_loc = lax.dot_general(kw, v_aug, (((0,), (0,)), ((), ())), preferred_element_type=F32)
    m_new = jnp.maximum(b_last + m_prev, m_loc)
    c_ref[st] = jnp.exp(b_last + m_prev - m_new) * c_prev + jnp.exp(m_loc - m_new) * c_loc
    m_ref[st] = m_new

    hn = hh * lax.rsqrt(jnp.mean(hh * hh, axis=-1, keepdims=True) + LN_EPS) * g_ref[:, h * dm:(h + 1) * dm]
    og = o_ref[g, :, h * dm:(h + 1) * dm].astype(F32)
    return _sigmoid(og) * hn


def _mlstm(qk, v, sm, o, conv_w, conv_b, mh_g, batch, seq):
    L = M_CHUNK
    G = M_GROUPS
    nc = seq // L
    T = batch * seq
    hb = L // CONV_HALO
    grouped = lambda a: a.reshape(G, T // G, a.shape[-1])
    blk = lambda n: pl.BlockSpec((G, L, n), lambda b, c: (0, b * nc + c, 0))
    in_specs = [
        blk(2 * WIDTH_M),
        pl.BlockSpec((G, CONV_HALO, 2 * WIDTH_M), lambda b, c: (0, jnp.maximum((b * nc + c) * hb - 1, 0), 0)),
        blk(WIDTH_M), blk(LANES), blk(WIDTH_M),
        _full(conv_w.shape), _full(conv_b.shape), _full(mh_g.shape),
    ]
    scratch = [pltpu.VMEM((G * N_HEADS_M, HEAD_DIM_M, 2 * HEAD_DIM_M), F32),
               pltpu.VMEM((G * N_HEADS_M, 1, 1), F32)]
    qk_g = grouped(qk)
    out = pl.pallas_call(
        _mlstm_kernel, grid=(batch // G, nc), in_specs=in_specs, out_specs=blk(WIDTH_M),
        out_shape=jax.ShapeDtypeStruct((G, T // G, WIDTH_M), BF16), scratch_shapes=scratch,
        compiler_params=_cparams(2), name="mlstm")(
            qk_g, qk_g, grouped(v), grouped(sm), grouped(o), conv_w, conv_b, mh_g)
    return out.reshape(T, WIDTH_M)


def _layer_norm(y, g, b):
    mu = jnp.mean(y, axis=-1, keepdims=True)
    var = jnp.mean(jnp.square(y - mu), axis=-1, keepdims=True)
    return (y - mu) * lax.rsqrt(var + LN_EPS) * g + b


def _router_logits(x1, wr, br):
    x1_hi = x1.astype(BF16)
    x1_lo = (x1 - x1_hi.astype(F32)).astype(BF16)
    both = jnp.dot(x1_hi, wr[...], preferred_element_type=F32)
    return (both[:, 0:LANES]
            + jnp.dot(x1_lo, wr[:, 0:LANES], preferred_element_type=F32)
            + both[:, LANES:2 * LANES]) + br[...]


def _merge_kernel(x_ref, oa_ref, hm_ref, ga_ref, gm_ref, wua, wum, wout, l1g, l1b, wr, br,
                  x1_ref, route_ref, cnt_ref):
    pa = jnp.dot(oa_ref[...], wua[...], preferred_element_type=F32)
    pm = jnp.dot(hm_ref[...], wum[...], preferred_element_type=F32)
    y = _sigmoid(ga_ref[...]) * pa.astype(BF16) + _sigmoid(gm_ref[...]) * pm.astype(BF16)
    mix = jnp.dot(y.astype(BF16), wout[...], preferred_element_type=F32)
    x1 = _layer_norm(ALPHA * x_ref[...] + mix, l1g[...], l1b[...])
    x1_ref[:, 0:D_MODEL] = x1

    lt = _router_logits(x1, wr, br).T
    tm = lt.shape[1]
    big = jnp.int32(LANES)
    le = lt[0:N_EXPERTS, :]
    lg = lt[N_EXPERTS:N_EXPERTS + SUBLANES, :]
    row_g = lax.broadcasted_iota(jnp.int32, lg.shape, 0)
    row_e = lax.broadcasted_iota(jnp.int32, le.shape, 0)
    is_grp = row_g < N_GROUPS
    gl = jnp.where(is_grp, lg, NEG_INF)
    ge = jnp.exp(gl - jnp.max(gl, axis=0, keepdims=True))
    gp = ge / jnp.sum(ge, axis=0, keepdims=True)
    g_w = jnp.max(gp, axis=0, keepdims=True)
    g_idx = jnp.min(jnp.where(jnp.logical_and(is_grp, gp == g_w), row_g, big), axis=0, keepdims=True)
    el = jnp.where(jnp.right_shift(row_e, EPG_SHIFT) == g_idx, le, NEG_INF)
    m1 = jnp.max(el, axis=0, keepdims=True)
    i1 = jnp.min(jnp.where(el == m1, row_e, big), axis=0, keepdims=True)
    el2 = jnp.where(row_e == i1, NEG_INF, el)
    m2 = jnp.max(el2, axis=0, keepdims=True)
    i2 = jnp.min(jnp.where(el2 == m2, row_e, big), axis=0, keepdims=True)

    e2 = jnp.exp(m2 - m1)
    w1 = g_w / (1.0 + e2)
    w2 = g_w * e2 / (1.0 + e2)
    first_is_a = i1 < i2
    row_w = lax.broadcasted_iota(jnp.int32, (LANES, tm), 0)
    w_rows = jnp.where(row_w == 0, jnp.where(first_is_a, w1, w2),
                       jnp.where(row_w == 1, jnp.where(first_is_a, w2, w1), 0.0))
    x1_ref[:, D_MODEL:D_MODEL + LANES] = w_rows.T

    a = jnp.bitwise_and(jnp.minimum(i1, i2), EXPERTS_PER_GROUP - 1)
    b = jnp.bitwise_and(jnp.maximum(i1, i2), EXPERTS_PER_GROUP - 1)
    pair = jnp.right_shift(a * (2 * EXPERTS_PER_GROUP - 1 - a), 1) + (b - a - 1)
    cls = g_idx * PAIRS_PER_GROUP + pair
    row_c = lax.broadcasted_iota(jnp.int32, (CLASS_ROWS, tm), 0)
    onehot = (row_c == cls).astype(F32)

    @pl.when(pl.program_id(0) == 0)
    def _init():
        cnt_ref[...] = jnp.zeros_like(cnt_ref)

    rb = min(RANK_BLOCK, tm)
    blocks = [onehot[:, j * rb:(j + 1) * rb] for j in range(tm // rb)]
    earlier = (lax.broadcasted_iota(jnp.int32, (rb, rb), 0)
               < lax.broadcasted_iota(jnp.int32, (rb, rb), 1)).astype(BF16)
    local = jnp.dot(jnp.concatenate(blocks, axis=0).astype(BF16), earlier, preferred_element_type=F32)
    before = cnt_ref[:, 0:1]
    ranks = []
    for j, blk in enumerate(blocks):
        prior = local[j * CLASS_ROWS:(j + 1) * CLASS_ROWS, :] + before
        ranks.append(jnp.sum(prior * blk, axis=0, keepdims=True))
        before = before + jnp.sum(blk, axis=1, keepdims=True)
    rank = jnp.concatenate(ranks, axis=1)
    cnt_ref[...] = jnp.broadcast_to(before, cnt_ref.shape)
    row_o = lax.broadcasted_iota(jnp.int32, route_ref.shape, 0)
    route_ref[...] = jnp.where(row_o == 0, cls.astype(F32), jnp.where(row_o == 1, rank, 0.0))


def _merge(x2, oa, hm, ga, gm, wua, wum, wout, l1g, l1b, wr, br, tm):
    T = x2.shape[0]
    blk = lambda n: pl.BlockSpec((tm, n), lambda i: (i, 0))
    in_specs = [blk(D_MODEL), blk(WIDTH_A), blk(WIDTH_M), blk(D_MODEL), blk(D_MODEL),
                _full(wua.shape), _full(wum.shape), _full(wout.shape), _full(l1g.shape), _full(l1b.shape),
                _full(wr.shape), _full(br.shape)]
    out_specs = [blk(D_MODEL + LANES), pl.BlockSpec((SUBLANES, tm), lambda i: (0, i)),
                 pl.BlockSpec((CLASS_ROWS, LANES), lambda i: (0, 0))]
    out_shape = [jax.ShapeDtypeStruct((T, D_MODEL + LANES), F32), jax.ShapeDtypeStruct((SUBLANES, T), F32),
                 jax.ShapeDtypeStruct((CLASS_ROWS, LANES), F32)]
    return pl.pallas_call(
        _merge_kernel, grid=(T // tm,), in_specs=in_specs, out_specs=out_specs, out_shape=out_shape,
        compiler_params=_cparams(1), name="merge")(x2, oa, hm, ga, gm, wua, wum, wout, l1g, l1b, wr, br)


def _sc_mesh():
    return plsc.VectorSubcoreMesh(core_axis_name="c", subcore_axis_name="s",
                                  num_cores=SC_CORES, num_subcores=SC_SUBCORES)


def _sc_chunks(n_rows):
    workers = SC_CORES * SC_SUBCORES
    assert n_rows % (workers * SC_ROWS * 2) == 0
    return n_rows // (workers * SC_ROWS)


def _sc_scratch(n_chunks, width, dtype):
    return [pltpu.VMEM((n_chunks, SC_ROWS), jnp.int32),
            pltpu.VMEM((SC_ROWS, width), dtype), pltpu.VMEM((SC_ROWS, width), dtype),
            pltpu.SemaphoreType.DMA, pltpu.SemaphoreType.DMA]


def _sc_scatter_rows(rows, idx, n_out):
    n_in, width = rows.shape
    n_chunks = _sc_chunks(n_in)

    @functools.partial(
        pl.kernel, mesh=_sc_mesh(), out_type=jax.ShapeDtypeStruct((n_out, width), rows.dtype),
        scratch_types=_sc_scratch(n_chunks, width, rows.dtype), name="sc_dispatch")
    def scatter(rows_hbm, idx_hbm, out_hbm, idx_v, rows_a, rows_b, sem_a, sem_b):
        first = (lax.axis_index("s") * SC_CORES + lax.axis_index("c")) * n_chunks
        pltpu.sync_copy(idx_hbm.at[pl.ds(first, n_chunks)], idx_v)

        def load(c, buf):
            pltpu.sync_copy(rows_hbm.at[pl.ds((first + c) * SC_ROWS, SC_ROWS)], buf)

        def put(c, buf, sem):
            return pltpu.make_async_copy(buf, out_hbm.at[idx_v.at[c]], sem)

        load(0, rows_a)
        put(0, rows_a, sem_a).start()

        @pl.loop(0, n_chunks, step=2)
        def _(j):
            load(j + 1, rows_b)
            put(j + 1, rows_b, sem_b).start()
            put(j, rows_a, sem_a).wait()

            @pl.when(j + 2 < n_chunks)
            def _():
                load(j + 2, rows_a)
                put(j + 2, rows_a, sem_a).start()

            put(j + 1, rows_b, sem_b).wait()

    return scatter(rows, idx.reshape(n_in // SC_ROWS, SC_ROWS))


def _sc_gather_rows(table, idx):
    n_out, width = idx.shape[0], table.shape[1]
    n_chunks = _sc_chunks(n_out)

    @functools.partial(
        pl.kernel, mesh=_sc_mesh(), out_type=jax.ShapeDtypeStruct((n_out, width), table.dtype),
        scratch_types=_sc_scratch(n_chunks, width, table.dtype), name="sc_combine")
    def gather(table_hbm, idx_hbm, out_hbm, idx_v, rows_a, rows_b, sem_a, sem_b):
        first = (lax.axis_index("s") * SC_CORES + lax.axis_index("c")) * n_chunks
        pltpu.sync_copy(idx_hbm.at[pl.ds(first, n_chunks)], idx_v)

        def fetch(c, buf, sem):
            return pltpu.make_async_copy(table_hbm.at[idx_v.at[c]], buf, sem)

        def store(c, buf):
            pltpu.sync_copy(buf, out_hbm.at[pl.ds((first + c) * SC_ROWS, SC_ROWS)])

        fetch(0, rows_a, sem_a).start()

        @pl.loop(0, n_chunks, step=2)
        def _(j):
            fetch(j + 1, rows_b, sem_b).start()
            fetch(j, rows_a, sem_a).wait()
            store(j, rows_a)

            @pl.when(j + 2 < n_chunks)
            def _():
                fetch(j + 2, rows_a, sem_a).start()

            fetch(j + 1, rows_b, sem_b).wait()
            store(j + 1, rows_b)

    return gather(table, idx.reshape(n_out // SC_ROWS, SC_ROWS))


def _moe_kernel(ta_ref, tb_ref, nu_ref, xs_ref, wga, wua, wda, wgb, wub, wdb, l2g, l2b, ys_ref,
                pre_ref):
    i = pl.program_id(0)
    n_used = nu_ref[0]
    tm = pre_ref.shape[0]
    pr = tm // MOE_NORM_PIECES

    def norm_rows(piece):
        rs = slice(piece * pr, (piece + 1) * pr)
        ys_ref[rs, :] = _layer_norm(pre_ref[rs, :], l2g[...], l2b[...])

    @pl.when(i == 0)
    def _first():
        pre_ref[...] = jnp.zeros_like(pre_ref)

    @pl.when(i < n_used)
    def _compute():
        x = xs_ref[:, 0:D_MODEL]
        xb = x.astype(BF16)
        w_a = xs_ref[:, D_MODEL:D_MODEL + 1]
        w_b = xs_ref[:, D_MODEL + 1:D_MODEL + 2]
        piece = iter(range(MOE_NORM_PIECES))

        def dot_pieces(lhs, w):
            outs = []
            for n in range(w.shape[1] // MOE_DOT_COLS):
                outs.append(jnp.dot(lhs, w[:, n * MOE_DOT_COLS:(n + 1) * MOE_DOT_COLS],
                                    preferred_element_type=F32))
                norm_rows(next(piece))
            return jnp.concatenate(outs, axis=1)

        def expert(wg, wu, wd):
            g = dot_pieces(xb, wg)
            u = dot_pieces(xb, wu)
            hdn = (g * _sigmoid(g) * u).astype(BF16)
            return dot_pieces(hdn, wd)

        ffn = w_a * expert(wga, wua, wda) + w_b * expert(wgb, wub, wdb)
        pre_ref[...] = ALPHA * x + ffn

    @pl.when(i == n_used)
    def _last():
        for piece in range(MOE_NORM_PIECES):
            norm_rows(piece)


def _moe(xs, tile_a, tile_b, n_used, wg, wu, wd, l2g, l2b):
    tm = MOE_TILE
    n_tiles = xs.shape[0] // tm
    used = lambda i, nu: jnp.minimum(i, nu[0] - 1)
    rows = lambda n: pl.BlockSpec((tm, n), lambda i, ta, tb, nu: (used(i, nu), 0))
    prev_rows = pl.BlockSpec((tm, D_MODEL), lambda i, ta, tb, nu: (used(jnp.maximum(i - 1, 0), nu), 0))
    w_in = lambda which: pl.BlockSpec(
        (None, D_MODEL, D_EXPERT), lambda i, ta, tb, nu: ((ta, tb)[which][used(i, nu)], 0, 0))
    w_out = lambda which: pl.BlockSpec(
        (None, D_EXPERT, D_MODEL), lambda i, ta, tb, nu: ((ta, tb)[which][used(i, nu)], 0, 0))
    const = lambda shape: pl.BlockSpec(shape, lambda i, ta, tb, nu: (0,) * len(shape))
    grid_spec = pltpu.PrefetchScalarGridSpec(
        num_scalar_prefetch=3, grid=(n_tiles + 1,),
        in_specs=[rows(xs.shape[1]), w_in(0), w_in(0), w_out(0), w_in(1), w_in(1), w_out(1),
                  const(l2g.shape), const(l2b.shape)],
        out_specs=prev_rows,
        scratch_shapes=[pltpu.VMEM((tm, D_MODEL), F32)])
    return pl.pallas_call(
        _moe_kernel, grid_spec=grid_spec, out_shape=jax.ShapeDtypeStruct((xs.shape[0], D_MODEL), F32),
        compiler_params=_cparams(1), name="moe")(
            tile_a, tile_b, n_used, xs, wg, wu, wd, wg, wu, wd, l2g, l2b)


def _route_tables(route, counts, n_tokens):
    tm = MOE_TILE
    n_tiles = n_tokens // tm + N_CLASSES
    cnt = counts[:N_CLASSES, 0].astype(jnp.int32)
    tiles = (cnt + tm - 1) // tm
    tile_end = jnp.cumsum(tiles)
    tile_start = tile_end - tiles
    t_idx = jnp.arange(n_tiles, dtype=jnp.int32)
    cls_of_tile = jnp.minimum(jnp.sum(t_idx[:, None] >= tile_end[None, :], axis=1), N_CLASSES - 1).astype(jnp.int32)
    classes = np.arange(N_CLASSES)
    tile_is = cls_of_tile[:, None] == classes[None, :]
    per_tile = lambda table: jnp.sum(jnp.where(tile_is, jnp.asarray(table, jnp.int32)[None, :], 0), axis=1)
    first_expert = classes // PAIRS_PER_GROUP * EXPERTS_PER_GROUP
    tile_a = per_tile(first_expert + np.asarray(PAIR_A)[classes % PAIRS_PER_GROUP])
    tile_b = per_tile(first_expert + np.asarray(PAIR_B)[classes % PAIRS_PER_GROUP])
    cls = route[0].astype(jnp.int32)
    rank = route[1].astype(jnp.int32)
    row0 = jnp.sum(jnp.where(cls[:, None] == jnp.arange(N_CLASSES)[None, :], (tile_start * tm)[None, :], 0), axis=1)
    n_used = jnp.maximum(tile_end[-1:], 1).astype(jnp.int32)
    return tile_a, tile_b, n_used, row0 + rank, n_tiles * tm


def _pick_tile(T, pref):
    t = pref
    while T % t:
        t //= 2
    return t


def kernel(x, w_in, conv_w, conv_b, kv_norm_g, w_uk, w_uv, rel_bias, b_i, b_f, mh_norm_g, w_up_a, w_up_m,
           w_out, ln1_g, ln1_b, w_grp, b_grp, w_rt, b_rt, w_gate, w_up, w_down, ln2_g, ln2_b):
    B, S, _ = x.shape
    T = B * S
    assert S % Q_TILE == 0 and S % M_CHUNK == 0 and T % MOE_TILE == 0 and w_in.shape[0] == DEPTH
    tz = _bias_tables(rel_bias)
    x2 = x.reshape(T, D_MODEL)
    for l in range(DEPTH):
        w = w_in[l]
        o = np.cumsum((WIDTH_A, KV_RANK, WIDTH_IDX, HEAD_DIM_IDX, N_HEADS_IDX, 2 * WIDTH_M, WIDTH_M,
                       N_HEADS_M, N_HEADS_M, WIDTH_M, D_MODEL, D_MODEL)).tolist()
        o = [0] + o
        seg = lambda j: w[:, o[j]:o[j + 1]]
        pad = LANES - (HEAD_DIM_IDX + N_HEADS_IDX + 2 * N_HEADS_M)
        w_small = jnp.concatenate([seg(3), seg(4), seg(7), seg(8), jnp.zeros((D_MODEL, pad), w.dtype)], axis=1)
        ws = [seg(0), seg(1), seg(2), w_small, seg(5), seg(6), seg(9), seg(10), seg(11)]
        ws = [a.astype(BF16) for a in ws]
        smb = jnp.zeros((1, LANES), F32).at[0, SM_I:SM_I + N_HEADS_M].set(b_i[l]) \
            .at[0, SM_F:SM_F + N_HEADS_M].set(b_f[l])
        wukv = jnp.concatenate([w_uk[l].reshape(KV_RANK, WIDTH_A), w_uv[l].reshape(KV_RANK, WIDTH_A)],
                               axis=1).astype(BF16)
        qa, ka, qi, sm, qk, v, og, ga, gm, va_t = _proj(x2, ws, wukv, kv_norm_g[l][None, :], smb,
                                                        _pick_tile(S, PROJ_TILE), S)
        oa = _dsa(qi, sm, qa, ka, va_t, tz, B, S)

        hm = _mlstm(qk, v, sm, og, conv_w[l], conv_b[l][None, :], mh_norm_g[l].reshape(1, WIDTH_M), B, S)

        w_router = jnp.concatenate(
            [w_rt[l], w_grp[l], jnp.zeros((D_MODEL, LANES - N_EXPERTS - N_GROUPS), F32)], axis=1)
        b_router = jnp.concatenate(
            [b_rt[l], b_grp[l], jnp.zeros((LANES - N_EXPERTS - N_GROUPS,), F32)])[None, :]
        wr_hi = w_router.astype(BF16)
        wr_split = jnp.concatenate([wr_hi, (w_router - wr_hi.astype(F32)).astype(BF16)], axis=1)
        x1, route, counts = _merge(x2, oa, hm, ga, gm, w_up_a[l].astype(BF16), w_up_m[l].astype(BF16),
                                   w_out[l].astype(BF16), ln1_g[l][None, :], ln1_b[l][None, :],
                                   wr_split, b_router, _pick_tile(T, MERGE_TILE))

        tile_a, tile_b, n_used, pos, n_sorted = _route_tables(route, counts, T)
        xs = _sc_scatter_rows(x1, pos, n_sorted)
        ys = _moe(xs, tile_a, tile_b, n_used, w_gate[l].astype(BF16), w_up[l].astype(BF16),
                  w_down[l].astype(BF16), ln2_g[l][None, :], ln2_b[l][None, :])
        x2 = _sc_gather_rows(ys, pos)
    return x2.reshape(B, S, D_MODEL)
```
